```python
import math
import jax, jax.numpy as jnp
from jax import lax
import numpy as np

D_MODEL = 1024
BATCH = 4
SEQ = 8192
DEPTH = 4

N_MIXERS = 3
N_A = (DEPTH + 2) // 3
N_B = (DEPTH + 1) // 3
N_C = DEPTH // 3
BLOCK = 128
N_META = 16
PAD = BLOCK - N_META
NEG = -1e30

DEEPNORM_ALPHA = (2.0 * DEPTH) ** 0.25
DEEPNORM_BETA = (8.0 * DEPTH) ** -0.25
LN_EPS = 1e-5
RMS_EPS = 1e-6

FOX_HEADS = 16
FOX_HEAD_DIM = 64
FOX_GATE_BIAS = 4.0

SWA_Q_HEADS = 16
SWA_KV_HEADS = 2
SWA_HEAD_DIM = 64
WINDOW = 128
ROPE_THETA = 500000.0
ROPE_DIM = SWA_HEAD_DIM // 4

MLA_HEADS = 16
MLA_Q_LORA = 384
MLA_KV_LORA = 256
MLA_NOPE = 64
MLA_ROPE = 32
MLA_V = 64
MLA_ROPE_THETA = 10000.0

D_FF = 2816
CONV_W = 3

kernel_name = "hybrid_fox_swa_mla_convffn_trunk"


def layer_norm(x, g, b):
    xf = x.astype(jnp.float32)
    mu = jnp.mean(xf, axis=-1, keepdims=True)
    var = jnp.mean(jnp.square(xf - mu), axis=-1, keepdims=True)
    return ((xf - mu) * lax.rsqrt(var + LN_EPS) * g + b).astype(x.dtype)


def rms_norm(x, g):
    xf = x.astype(jnp.float32)
    return (xf * lax.rsqrt(jnp.mean(jnp.square(xf), axis=-1, keepdims=True) + RMS_EPS) * g).astype(x.dtype)


def rope_angles(pos, dim, theta):
    inv = theta ** (-jnp.arange(0, dim, 2, dtype=jnp.float32) / dim)
    ang = pos.astype(jnp.float32)[:, None] * inv[None, :]
    return jnp.cos(ang), jnp.sin(ang)


def apply_rope(x, cos, sin):
    x1, x2 = jnp.split(x.astype(jnp.float32), 2, axis=-1)
    c = cos[None, :, None, :]
    s = sin[None, :, None, :]
    return jnp.concatenate([x1 * c - x2 * s, x1 * s + x2 * c], axis=-1).astype(x.dtype)


def partial_rope(x, cos, sin):
    return jnp.concatenate([apply_rope(x[..., :ROPE_DIM], cos, sin), x[..., ROPE_DIM:]], axis=-1)


def dense_causal_attention(q, k, v, decay=None):
    B, L, H, dk = q.shape
    nblk = L // BLOCK
    scale = dk ** -0.5
    kpos = jnp.arange(L)
    kvalid = kpos >= PAD
    qb = q.reshape(B, nblk, BLOCK, H, dk).transpose(1, 0, 2, 3, 4)
    xs = (jnp.arange(nblk), qb)
    if decay is not None:
        ck = decay.transpose(0, 2, 1)
        cb = decay.reshape(B, nblk, BLOCK, H).transpose(1, 0, 3, 2)
        xs = xs + (cb,)

    def one_block(args):
        i, qi = args[0], args[1]
        s = jnp.einsum('bqhd,bkhd->bhqk', qi, k, preferred_element_type=jnp.float32) * scale
        if decay is not None:
            s = s + args[2][..., None] - ck[:, :, None, :]
        qpos = i * BLOCK + jnp.arange(BLOCK)
        mask = (kpos[None, :] <= qpos[:, None]) & kvalid[None, :]
        s = jnp.where(mask[None, None], s, NEG)
        p = jax.nn.softmax(s, axis=-1)
        return jnp.einsum('bhqk,bkhd->bqhd', p.astype(v.dtype), v)

    out = lax.map(one_block, xs)
    return out.transpose(1, 0, 2, 3, 4).reshape(B, L, H, v.shape[-1])


def fox_mixer(h, w_in, b_f, w_o):
    B, L, _ = h.shape
    hd = FOX_HEADS * FOX_HEAD_DIM
    proj = h @ w_in
    q, k, v, fg = jnp.split(proj, [hd, 2 * hd, 3 * hd], axis=-1)
    q = q.reshape(B, L, FOX_HEADS, FOX_HEAD_DIM)
    k = k.reshape(B, L, FOX_HEADS, FOX_HEAD_DIM)
    v = v.reshape(B, L, FOX_HEADS, FOX_HEAD_DIM)
    log_f = jax.nn.log_sigmoid((fg + b_f).astype(jnp.float32))
    c = jnp.cumsum(log_f, axis=1)
    o = dense_causal_attention(q, k, v, c)
    return o.reshape(B, L, hd) @ w_o


def swa_mixer(h, w_in, sinks, w_o, cos, sin):
    B, L, _ = h.shape
    G = SWA_Q_HEADS // SWA_KV_HEADS
    d = SWA_HEAD_DIM
    nblk = L // BLOCK
    qd, kd = SWA_Q_HEADS * d, SWA_KV_HEADS * d
    q, k, v = jnp.split(h @ w_in, [qd, qd + kd], axis=-1)
    q = partial_rope(q.reshape(B, L, SWA_Q_HEADS, d), cos, sin)
    k = partial_rope(k.reshape(B, L, SWA_KV_HEADS, d), cos, sin)
    v = v.reshape(B, L, SWA_KV_HEADS, d)

    qb = q.reshape(B, nblk, BLOCK, SWA_KV_HEADS, G, d)
    kb = k.reshape(B, nblk, BLOCK, SWA_KV_HEADS, d)
    vb = v.reshape(B, nblk, BLOCK, SWA_KV_HEADS, d)
    pad_blk = ((0, 0), (1, 0), (0, 0), (0, 0), (0, 0))
    kband = jnp.concatenate([jnp.pad(kb, pad_blk)[:, :-1], kb], axis=2)
    vband = jnp.concatenate([jnp.pad(vb, pad_blk)[:, :-1], vb], axis=2)
    kmeta = k[:, PAD:PAD + N_META]
    vmeta = v[:, PAD:PAD + N_META]

    scale = d ** -0.5
    s_band = jnp.einsum('bnqhgd,bnkhd->bnhgqk', qb, kband, preferred_element_type=jnp.float32) * scale
    s_meta = jnp.einsum('bnqhgd,bmhd->bnhgqm', qb, kmeta, preferred_element_type=jnp.float32) * scale

    qpos = jnp.arange(L).reshape(nblk, BLOCK)
    kpos = (jnp.arange(nblk)[:, None] - 1) * BLOCK + jnp.arange(2 * BLOCK)[None, :]
    diff = qpos[:, :, None] - kpos[:, None, :]
    band_mask = (diff >= 0) & (diff < WINDOW) & (kpos[:, None, :] >= PAD + N_META)
    meta_pos = PAD + jnp.arange(N_META)
    meta_mask = meta_pos[None, None, :] <= qpos[:, :, None]
    s_band = jnp.where(band_mask[None, :, None, None], s_band, NEG)
    s_meta = jnp.where(meta_mask[None, :, None, None], s_meta, NEG)

    sink = sinks.astype(jnp.float32).reshape(SWA_KV_HEADS, G)[None, None, :, :, None, None]
    m = jnp.maximum(jnp.maximum(s_band.max(-1, keepdims=True), s_meta.max(-1, keepdims=True)), sink)
    p_band = jnp.exp(s_band - m)
    p_meta = jnp.exp(s_meta - m)
    denom = p_band.sum(-1, keepdims=True) + p_meta.sum(-1, keepdims=True) + jnp.exp(sink - m)
    p_band = (p_band / denom).astype(v.dtype)
    p_meta = (p_meta / denom).astype(v.dtype)
    o = (jnp.einsum('bnhgqk,bnkhd->bnqhgd', p_band, vband)
         + jnp.einsum('bnhgqm,bmhd->bnqhgd', p_meta, vmeta))
    return o.reshape(B, L, qd) @ w_o


def mla_mixer(h, w_a, g_q, g_kv, w_uq, w_ukv, w_o, cos, sin):
    B, L, _ = h.shape
    cq, ckv, kr = jnp.split(h @ w_a, [MLA_Q_LORA, MLA_Q_LORA + MLA_KV_LORA], axis=-1)
    cq = rms_norm(cq, g_q)
    ckv = rms_norm(ckv, g_kv)
    q = (cq @ w_uq).reshape(B, L, MLA_HEADS, MLA_NOPE + MLA_ROPE)
    q_nope, q_rope = jnp.split(q, [MLA_NOPE], axis=-1)
    q_rope = apply_rope(q_rope, cos, sin)
    k_rope = apply_rope(kr[:, :, None, :], cos, sin)
    kv = (ckv @ w_ukv).reshape(B, L, MLA_HEADS, MLA_NOPE + MLA_V)
    k_nope, v = jnp.split(kv, [MLA_NOPE], axis=-1)
    qf = jnp.concatenate([q_nope, q_rope], axis=-1)
    kf = jnp.concatenate([k_nope, jnp.broadcast_to(k_rope, (B, L, MLA_HEADS, MLA_ROPE))], axis=-1)
    o = dense_causal_attention(qf, kf, v)
    return o.reshape(B, L, MLA_HEADS * MLA_V) @ w_o


def conv_glu_ffn(h, w_in, conv_w, conv_b, w_out, valid):
    L = h.shape[1]
    u = (h @ w_in) * valid.astype(h.dtype)[None, :, None]
    up = jnp.pad(u, ((0, 0), (CONV_W - 1, 0), (0, 0)))
    y = conv_b + up[:, 0:L] * conv_w[0]
    for j in range(1, CONV_W):
        y = y + up[:, j:j + L] * conv_w[j]
    g, val = jnp.split(y, 2, axis=-1)
    return (jax.nn.silu(g) * val) @ w_out


def _dense(key, shape, fan_in, gain=1.0):
    return jax.random.normal(key, shape, jnp.float32) * (gain * fan_in ** -0.5)


def setup_inputs(seed: int = 0) -> dict:
    key = jax.random.key(seed)
    ks = jax.random.split(key, 24)
    D = D_MODEL
    beta = DEEPNORM_BETA
    nrm = lambda k, s: jax.random.normal(k, s, jnp.float32)

    x = nrm(ks[0], (BATCH, SEQ, D))
    meta_tokens = nrm(ks[1], (N_META, D))
    ln1_g = 1.0 + 0.01 * nrm(ks[2], (DEPTH, D))
    ln1_b = 0.01 * nrm(ks[3], (DEPTH, D))
    ln2_g = 1.0 + 0.01 * nrm(ks[4], (DEPTH, D))
    ln2_b = 0.01 * nrm(ks[5], (DEPTH, D))

    fhd = FOX_HEADS * FOX_HEAD_DIM
    fox_cols = jnp.concatenate([jnp.ones((2 * fhd,), jnp.float32), jnp.full((fhd,), beta, jnp.float32),
                                jnp.ones((FOX_HEADS,), jnp.float32)])
    fox_w_in = _dense(ks[6], (N_A, D, 3 * fhd + FOX_HEADS), D) * fox_cols
    fox_b_f = FOX_GATE_BIAS + 0.5 * nrm(ks[7], (N_A, FOX_HEADS))
    fox_w_o = _dense(ks[8], (N_A, fhd, D), fhd, beta)

    sqd, skd = SWA_Q_HEADS * SWA_HEAD_DIM, SWA_KV_HEADS * SWA_HEAD_DIM
    swa_cols = jnp.concatenate([jnp.ones((sqd + skd,), jnp.float32), jnp.full((skd,), beta, jnp.float32)])
    swa_w_in = _dense(ks[9], (N_B, D, sqd + 2 * skd), D) * swa_cols
    swa_sinks = 0.5 * nrm(ks[10], (N_B, SWA_Q_HEADS))
    swa_w_o = _dense(ks[11], (N_B, sqd, D), sqd, beta)

    mla_w_a = _dense(ks[12], (N_C, D, MLA_Q_LORA + MLA_KV_LORA + MLA_ROPE), D)
    mla_g_q = 1.0 + 0.01 * nrm(ks[13], (N_C, MLA_Q_LORA))
    mla_g_kv = 1.0 + 0.01 * nrm(ks[14], (N_C, MLA_KV_LORA))
    mla_w_uq = _dense(ks[15], (N_C, MLA_Q_LORA, MLA_HEADS * (MLA_NOPE + MLA_ROPE)), MLA_Q_LORA)
    ukv_cols = jnp.tile(jnp.concatenate([jnp.ones((MLA_NOPE,), jnp.float32),
                                         jnp.full((MLA_V,), beta, jnp.float32)]), MLA_HEADS)
    mla_w_ukv = _dense(ks[16], (N_C, MLA_KV_LORA, MLA_HEADS * (MLA_NOPE + MLA_V)), MLA_KV_LORA) * ukv_cols
    mla_w_o = _dense(ks[17], (N_C, MLA_HEADS * MLA_V, D), MLA_HEADS * MLA_V, beta)

    ffn_w_in = _dense(ks[18], (DEPTH, D, 2 * D_FF), D, beta)
    ffn_conv_w = _dense(ks[19], (DEPTH, CONV_W, 2 * D_FF), CONV_W)
    ffn_conv_b = 0.01 * nrm(ks[20], (DEPTH, 2 * D_FF))
    ffn_w_out = _dense(ks[21], (DEPTH, D_FF, D), D_FF, beta)

    return {"x": x, "meta_tokens": meta_tokens, "ln1_g": ln1_g, "ln1_b": ln1_b, "ln2_g": ln2_g, "ln2_b": ln2_b,
            "fox_w_in": fox_w_in, "fox_b_f": fox_b_f, "fox_w_o": fox_w_o,
            "swa_w_in": swa_w_in, "swa_sinks": swa_sinks, "swa_w_o": swa_w_o,
            "mla_w_a": mla_w_a, "mla_g_q": mla_g_q, "mla_g_kv": mla_g_kv, "mla_w_uq": mla_w_uq,
            "mla_w_ukv": mla_w_ukv, "mla_w_o": mla_w_o,
            "ffn_w_in": ffn_w_in, "ffn_conv_w": ffn_conv_w, "ffn_conv_b": ffn_conv_b, "ffn_w_out": ffn_w_out}


def reference(x, meta_tokens, ln1_g, ln1_b, ln2_g, ln2_b,
              fox_w_in, fox_b_f, fox_w_o,
              swa_w_in, swa_sinks, swa_w_o,
              mla_w_a, mla_g_q, mla_g_kv, mla_w_uq, mla_w_ukv, mla_w_o,
              ffn_w_in, ffn_conv_w, ffn_conv_b, ffn_w_out):
    B, S, D = x.shape
    h = jnp.concatenate([jnp.zeros((B, PAD, D), x.dtype),
                         jnp.broadcast_to(meta_tokens.astype(x.dtype)[None], (B, N_META, D)), x], axis=1)
    L = h.shape[1]
    idx = jnp.arange(L)
    pos = idx - PAD
    valid = idx >= PAD
    cos_p, sin_p = rope_angles(pos, ROPE_DIM, ROPE_THETA)
    cos_m, sin_m = rope_angles(pos, MLA_ROPE, MLA_ROPE_THETA)

    for i in range(DEPTH):
        kind, j = i % N_MIXERS, i // N_MIXERS
        if kind == 0:
            mix = fox_mixer(h, fox_w_in[j], fox_b_f[j], fox_w_o[j])
        elif kind == 1:
            mix = swa_mixer(h, swa_w_in[j], swa_sinks[j], swa_w_o[j], cos_p, sin_p)
        else:
            mix = mla_mixer(h, mla_w_a[j], mla_g_q[j], mla_g_kv[j], mla_w_uq[j], mla_w_ukv[j], mla_w_o[j],
                            cos_m, sin_m)
        h = layer_norm(DEEPNORM_ALPHA * h + mix, ln1_g[i], ln1_b[i])
        ffn = conv_glu_ffn(h, ffn_w_in[i], ffn_conv_w[i], ffn_conv_b[i], ffn_w_out[i], valid)
        h = layer_norm(DEEPNORM_ALPHA * h + ffn, ln2_g[i], ln2_b[i])
    return h[:, BLOCK:]
```

```python
import functools

import jax
import jax.numpy as jnp
from jax import lax
from jax.experimental import pallas as pl
from jax.experimental.pallas import tpu as pltpu

F32 = jnp.float32
BF16 = jnp.bfloat16

D_MODEL = 1024
DEPTH = 4
N_META = 16
LEAD = 240
FIRST_REAL = LEAD + N_META
NEG = -1e30
DEEPNORM_ALPHA = (2.0 * DEPTH) ** 0.25
LN_EPS = 1e-5
RMS_EPS = 1e-6
HEADS = 16
HEAD_DIM = 64
PAIRS = HEADS // 2
SWA_KV_HEADS = 2
WINDOW = 128
ROPE_THETA = 500000.0
ROPE_DIM = 16
MLA_Q_LORA = 384
MLA_KV_LORA = 256
MLA_NOPE = 64
MLA_ROPE = 32
MLA_ROPE_THETA = 10000.0
D_FF = 2816
CONV_W = 3

LANES = 128
BF16_SUBLANES = 16
VMEM_LIMIT = 56 * 1024 * 1024


def _params(*sem):
    return pltpu.CompilerParams(dimension_semantics=sem, vmem_limit_bytes=VMEM_LIMIT)


def _tile(n, pref, mult):
    best = mult
    t = mult
    while t <= min(n, pref):
        if n % t == 0:
            best = t
        t += mult
    assert n % best == 0
    return best


def _dot(a, b):
    return jnp.dot(a, b, preferred_element_type=F32)


def _dot_nt(a, b):
    return lax.dot_general(a, b, (((1,), (1,)), ((), ())), preferred_element_type=F32)


def _layer_norm(x, g, b):
    mu = jnp.mean(x, axis=-1, keepdims=True)
    xc = x - mu
    var = jnp.mean(xc * xc, axis=-1, keepdims=True)
    return xc * lax.rsqrt(var + LN_EPS) * g + b


def _mm_kernel(x_ref, w_ref, o_ref):
    o_ref[...] = _dot(x_ref[...], w_ref[...]).astype(o_ref.dtype)


def _matmul(x, w, out_dtype, tm, tn):
    n, k = x.shape
    m = w.shape[1]
    return pl.pallas_call(
        _mm_kernel,
        grid=(n // tm, m // tn),
        in_specs=[pl.BlockSpec((tm, k), lambda i, j: (i, 0)),
                  pl.BlockSpec((k, tn), lambda i, j: (0, j))],
        out_specs=pl.BlockSpec((tm, tn), lambda i, j: (i, j)),
        out_shape=jax.ShapeDtypeStruct((n, m), out_dtype),
        compiler_params=_params("parallel", "parallel"),
        name="matmul",
    )(x, w)


def _fox_gate_kernel(x_ref, w_ref, b_ref, ccol_ref, crow_ref, carry_ref, *, tm, tiles_per_batch):
    i = pl.program_id(0)

    @pl.when(i % tiles_per_batch == 0)
    def _():
        carry_ref[...] = jnp.zeros_like(carry_ref)

    fg = _dot(x_ref[...], w_ref[...]) + b_ref[...]
    logf = jnp.minimum(fg, 0.0) - jnp.log(1.0 + jnp.exp(-jnp.abs(fg)))
    row = lax.broadcasted_iota(jnp.int32, (tm, tm), 0)
    col = lax.broadcasted_iota(jnp.int32, (tm, tm), 1)
    tri = jnp.where(col <= row, 1.0, 0.0).astype(BF16)
    hi = logf.astype(BF16)
    r1 = logf - hi.astype(F32)
    mid = r1.astype(BF16)
    lo = (r1 - mid.astype(F32)).astype(BF16)
    cs = _dot(tri, hi) + _dot(tri, mid) + _dot(tri, lo) + carry_ref[...]
    carry_ref[...] = cs[tm - 1:tm, :]
    ccol_ref[...] = cs[:, :HEADS]
    crow_ref[0] = cs.T[:HEADS, :]


def _fox_gate(hb, w_fg, b_fg, batch, lp, tm):
    n = hb.shape[0]
    tpb = lp // tm
    return pl.pallas_call(
        functools.partial(_fox_gate_kernel, tm=tm, tiles_per_batch=tpb),
        grid=(n // tm,),
        in_specs=[pl.BlockSpec((tm, D_MODEL), lambda i: (i, 0)),
                  pl.BlockSpec((D_MODEL, LANES), lambda i: (0, 0)),
                  pl.BlockSpec((1, LANES), lambda i: (0, 0))],
        out_specs=[pl.BlockSpec((tm, HEADS), lambda i: (i, 0)),
                   pl.BlockSpec((1, HEADS, tm), lambda i: (i // tpb, 0, i % tpb))],
        out_shape=[jax.ShapeDtypeStruct((n, HEADS), F32),
                   jax.ShapeDtypeStruct((batch, HEADS, lp), F32)],
        scratch_shapes=[pltpu.VMEM((1, LANES), F32)],
        compiler_params=_params("arbitrary"),
        name="fox_gate",
    )(hb, w_fg, b_fg)


def _softmax_step(s_pair, v2, lo, m_ref, l_ref, acc_ref):
    ps, alphas = [], []
    for a, s in enumerate(s_pair):
        m_old = m_ref[a]
        m_new = jnp.maximum(m_old, jnp.max(s, axis=1, keepdims=True))
        alpha = jnp.exp(m_old - m_new)
        p = jnp.exp(s - m_new)
        l_ref[a] = alpha * l_ref[a] + jnp.sum(p, axis=1, keepdims=True)
        m_ref[a] = m_new
        ps.append(p.astype(BF16))
        alphas.append(alpha)
    v2f = v2.astype(F32)
    v_lo = jnp.where(lo, v2f, 0.0).astype(BF16)
    v_hi = jnp.where(lo, 0.0, v2f).astype(BF16)
    pv = _dot(ps[0], v_lo) + _dot(ps[1], v_hi)
    acc_ref[...] = jnp.where(lo, alphas[0], alphas[1]) * acc_ref[...] + pv


def _causal_sweep(i, tq, tk, step):
    r = tq // tk
    jdiag = jnp.maximum(i * r, 1)
    n_masked = 1 + (i + 1) * r - jdiag

    def masked_body(t, c):
        step(jnp.where(t == 0, 0, jdiag + t - 1), True)
        return c

    def plain_body(j, c):
        step(j, False)
        return c

    lax.fori_loop(0, n_masked, masked_body, 0)
    lax.fori_loop(1, jnp.maximum(i * r, 1), plain_body, 0)


def _causal_mask(i, j, tq, tk):
    qpos = i * tq + lax.broadcasted_iota(jnp.int32, (tq, tk), 0)
    kpos = j * tk + lax.broadcasted_iota(jnp.int32, (tq, tk), 1)
    return (kpos <= qpos) & (kpos >= LEAD)


def _init_softmax_state(m_ref, l_ref, acc_ref):
    m_ref[...] = jnp.full_like(m_ref, NEG)
    l_ref[...] = jnp.zeros_like(l_ref)
    acc_ref[...] = jnp.zeros_like(acc_ref)


def _finish_softmax(o_ref, lo, l_ref, acc_ref):
    o_ref[0] = (acc_ref[...] / jnp.where(lo, l_ref[0], l_ref[1])).astype(o_ref.dtype)


def _fox_attn_kernel(q_ref, k_ref, v_ref, ccol_ref, crow_ref, o_ref, m_ref, l_ref, acc_ref, *, tq, tk):
    p = pl.program_id(1)
    i = pl.program_id(2)
    lo = lax.broadcasted_iota(jnp.int32, (1, LANES), 1) < HEAD_DIM
    qf = q_ref[0].astype(F32)
    q_pair = (jnp.where(lo, qf, 0.0).astype(BF16), jnp.where(lo, 0.0, qf).astype(BF16))
    head_lane = lax.broadcasted_iota(jnp.int32, (1, HEADS), 1)
    ccol = ccol_ref[0]
    ct_pair = tuple(jnp.sum(jnp.where(head_lane == 2 * p + a, ccol, 0.0), axis=1, keepdims=True)
                    for a in range(2))
    _init_softmax_state(m_ref, l_ref, acc_ref)

    def step(j, masked):
        off = pl.multiple_of(j * tk, tk)
        kc = k_ref[0, pl.ds(off, tk), :]
        vc = v_ref[0, pl.ds(off, tk), :]
        s_pair = []
        for a in range(2):
            cs = crow_ref[0, 2 * p + a, pl.ds(j, 1), :]
            s = _dot_nt(q_pair[a], kc) + ct_pair[a] - cs
            if masked:
                s = jnp.where(_causal_mask(i, j, tq, tk), s, NEG)
            s_pair.append(s)
        _softmax_step(s_pair, vc, lo, m_ref, l_ref, acc_ref)

    _causal_sweep(i, tq, tk, step)
    _finish_softmax(o_ref, lo, l_ref, acc_ref)


def _fox_attention(qkv, ccol, crow, batch, lp, tq, tk):
    nk = lp // tk
    crow4 = crow.reshape(batch, HEADS, nk, tk)
    ccol3 = ccol.reshape(batch, lp, HEADS)
    return pl.pallas_call(
        functools.partial(_fox_attn_kernel, tq=tq, tk=tk),
        grid=(batch, PAIRS, lp // tq),
        in_specs=[pl.BlockSpec((1, tq, LANES), lambda b, p, i: (b, i, p)),
                  pl.BlockSpec((1, lp, LANES), lambda b, p, i: (b, 0, PAIRS + p)),
                  pl.BlockSpec((1, lp, LANES), lambda b, p, i: (b, 0, 2 * PAIRS + p)),
                  pl.BlockSpec((1, tq, HEADS), lambda b, p, i: (b, i, 0)),
                  pl.BlockSpec((1, HEADS, nk, tk), lambda b, p, i: (b, 0, 0, 0))],
        out_specs=pl.BlockSpec((1, tq, LANES), lambda b, p, i: (b, i, p)),
        out_shape=jax.ShapeDtypeStruct((batch, lp, HEADS * HEAD_DIM), BF16),
        scratch_shapes=[pltpu.VMEM((2, tq, 1), F32), pltpu.VMEM((2, tq, 1), F32),
                        pltpu.VMEM((tq, LANES), F32)],
        compiler_params=_params("parallel", "parallel", "parallel"),
        name="fox_attention",
    )(qkv, qkv, qkv, ccol3, crow4)


def _mla_attn_kernel(q_ref, kn_ref, kr_ref, v_ref, o_ref, m_ref, l_ref, acc_ref, *, tq, tk):
    i = pl.program_id(2)
    lo = lax.broadcasted_iota(jnp.int32, (1, LANES), 1) < HEAD_DIM
    lane = lax.broadcasted_iota(jnp.int32, (1, 2 * LANES), 1)
    in_a = (lane < MLA_NOPE) | ((lane >= LANES) & (lane < LANES + MLA_ROPE))
    in_b = ((lane >= MLA_NOPE) & (lane < LANES)) | ((lane >= LANES + MLA_ROPE) & (lane < LANES + 2 * MLA_ROPE))
    qf = q_ref[0].astype(F32)
    q_pair = (jnp.where(in_a, qf, 0.0).astype(BF16), jnp.where(in_b, qf, 0.0).astype(BF16))
    _init_softmax_state(m_ref, l_ref, acc_ref)

    def step(j, masked):
        off = pl.multiple_of(j * tk, tk)
        kc = jnp.concatenate([kn_ref[0, pl.ds(off, tk), :], kr_ref[0, pl.ds(off, tk), :]], axis=1)
        vc = v_ref[0, pl.ds(off, tk), :]
        s_pair = []
        for a in range(2):
            s = _dot_nt(q_pair[a], kc)
            if masked:
                s = jnp.where(_causal_mask(i, j, tq, tk), s, NEG)
            s_pair.append(s)
        _softmax_step(s_pair, vc, lo, m_ref, l_ref, acc_ref)

    _causal_sweep(i, tq, tk, step)
    _finish_softmax(o_ref, lo, l_ref, acc_ref)


def _mla_attention(qcat, kv, kr, batch, lp, tq, tk):
    return pl.pallas_call(
        functools.partial(_mla_attn_kernel, tq=tq, tk=tk),
        grid=(batch, PAIRS, lp // tq),
        in_specs=[pl.BlockSpec((1, tq, 2 * LANES), lambda b, p, i: (b, i, p)),
                  pl.BlockSpec((1, lp, LANES), lambda b, p, i: (b, 0, p)),
                  pl.BlockSpec((1, lp, LANES), lambda b, p, i: (b, 0, 0)),
                  pl.BlockSpec((1, lp, LANES), lambda b, p, i: (b, 0, PAIRS + p))],
        out_specs=pl.BlockSpec((1, tq, LANES), lambda b, p, i: (b, i, p)),
        out_shape=jax.ShapeDtypeStruct((batch, lp, HEADS * HEAD_DIM), BF16),
        scratch_shapes=[pltpu.VMEM((2, tq, 1), F32), pltpu.VMEM((2, tq, 1), F32),
                        pltpu.VMEM((tq, LANES), F32)],
        compiler_params=_params("parallel", "parallel", "parallel"),
        name="mla_attention",
    )(qcat, kv, kr, kv)


SWA_TQ = 128


def _swa_attn_kernel(sink_ref, q_ref, km_ref, kp_ref, kc_ref, vm_ref, vp_ref, vc_ref, o_ref):
    i = pl.program_id(1)
    t = SWA_TQ
    lo = lax.broadcasted_iota(jnp.int32, (1, LANES), 1) < HEAD_DIM
    row = lax.broadcasted_iota(jnp.int32, (t, 3 * t), 0)
    col = lax.broadcasted_iota(jnp.int32, (t, 3 * t), 1)
    qpos = i * t + row
    kpos = jnp.where(col < t, t + col, (i - 2) * t + col)
    d = qpos - kpos
    valid = (d >= 0) & (((col < t) & (kpos >= LEAD)) |
                        ((col >= t) & (d < WINDOW) & (kpos >= FIRST_REAL)))
    for g in range(SWA_KV_HEADS):
        sl = slice(g * LANES, (g + 1) * LANES)
        kcat = jnp.concatenate([km_ref[0, :, sl], kp_ref[0, :, sl], kc_ref[0, :, sl]], axis=0)
        vf = jnp.concatenate([vm_ref[0, :, sl], vp_ref[0, :, sl], vc_ref[0, :, sl]], axis=0).astype(F32)
        v_lo = jnp.where(lo, vf, 0.0).astype(BF16)
        v_hi = jnp.where(lo, 0.0, vf).astype(BF16)
        for pp in range(PAIRS // SWA_KV_HEADS):
            p = g * (PAIRS // SWA_KV_HEADS) + pp
            qf = q_ref[0, :, p * LANES:(p + 1) * LANES].astype(F32)
            q_pair = (jnp.where(lo, qf, 0.0).astype(BF16), jnp.where(lo, 0.0, qf).astype(BF16))
            ps, inv = [], []
            for a in range(2):
                sink = sink_ref[2 * p + a]
                s = jnp.where(valid, _dot_nt(q_pair[a], kcat), NEG)
                m = jnp.maximum(jnp.max(s, axis=1, keepdims=True), sink)
                e = jnp.exp(s - m)
                den = jnp.sum(e, axis=1, keepdims=True) + jnp.exp(sink - m)
                ps.append(e.astype(BF16))
                inv.append(1.0 / den)
            o = (_dot(ps[0], v_lo) + _dot(ps[1], v_hi)) * jnp.where(lo, inv[0], inv[1])
            o_ref[0, :, p * LANES:(p + 1) * LANES] = o.astype(o_ref.dtype)


def _swa_attention(qkv, sinks, batch, lp):
    t = SWA_TQ
    kblk, vblk = 4, 5
    kv_spec = lambda col, row_of: pl.BlockSpec((1, t, 2 * LANES), lambda b, i: (b, row_of(i), col))
    meta = lambda i: 1
    prev = lambda i: jnp.maximum(i - 1, 0)
    cur = lambda i: i
    return pl.pallas_call(
        _swa_attn_kernel,
        grid=(batch, lp // t),
        in_specs=[pl.BlockSpec(memory_space=pltpu.SMEM),
                  pl.BlockSpec((1, t, HEADS * HEAD_DIM), lambda b, i: (b, i, 0)),
                  kv_spec(kblk, meta), kv_spec(kblk, prev), kv_spec(kblk, cur),
                  kv_spec(vblk, meta), kv_spec(vblk, prev), kv_spec(vblk, cur)],
        out_specs=pl.BlockSpec((1, t, HEADS * HEAD_DIM), lambda b, i: (b, i, 0)),
        out_shape=jax.ShapeDtypeStruct((batch, lp, HEADS * HEAD_DIM), BF16),
        compiler_params=_params("parallel", "parallel"),
        name="swa_attention",
    )(sinks, qkv, qkv, qkv, qkv, qkv, qkv, qkv)


def _swa_proj_kernel(x_ref, w_ref, cos_ref, sin_ref, o_ref, *, tn, n_rope_blocks):
    j = pl.program_id(1)
    y = _dot(x_ref[...], w_ref[...])

    @pl.when(j < n_rope_blocks)
    def _():
        reps = tn // LANES
        cos = jnp.concatenate([cos_ref[...]] * reps, axis=1)
        sin = jnp.concatenate([sin_ref[...]] * reps, axis=1)
        lane = lax.broadcasted_iota(jnp.int32, (1, tn), 1)
        half = ROPE_DIM // 2
        partner = jnp.where((lane & (HEAD_DIM - 1)) < half,
                            pltpu.roll(y, tn - half, 1),
                            pltpu.roll(y, half, 1))
        o_ref[...] = (y * cos + partner * sin).astype(o_ref.dtype)

    @pl.when(j >= n_rope_blocks)
    def _():
        o_ref[...] = y.astype(o_ref.dtype)


def _swa_proj(hb, w, cos, sin, lp, tm, tn, n_rope_blocks):
    n = hb.shape[0]
    m = w.shape[1]
    tpb = lp // tm
    return pl.pallas_call(
        functools.partial(_swa_proj_kernel, tn=tn, n_rope_blocks=n_rope_blocks),
        grid=(n // tm, m // tn),
        in_specs=[pl.BlockSpec((tm, D_MODEL), lambda i, j: (i, 0)),
                  pl.BlockSpec((D_MODEL, tn), lambda i, j: (0, j)),
                  pl.BlockSpec((tm, LANES), lambda i, j: (i % tpb, 0)),
                  pl.BlockSpec((tm, LANES), lambda i, j: (i % tpb, 0))],
        out_specs=pl.BlockSpec((tm, tn), lambda i, j: (i, j)),
        out_shape=jax.ShapeDtypeStruct((n, m), BF16),
        compiler_params=_params("parallel", "parallel"),
        name="swa_proj",
    )(hb, w, cos, sin)


MLA_A_COLS = MLA_Q_LORA + MLA_KV_LORA + 2 * LANES


def _mla_a_kernel(x_ref, w_ref, gq_ref, gkv_ref, cos_ref, sin_ref, cq_ref, ckv_ref, kr_ref):
    y = _dot(x_ref[...], w_ref[...])
    cq = y[:, :MLA_Q_LORA]
    ckv = y[:, MLA_Q_LORA:MLA_Q_LORA + MLA_KV_LORA]
    kr = y[:, MLA_Q_LORA + MLA_KV_LORA:MLA_Q_LORA + MLA_KV_LORA + LANES]
    krs = y[:, MLA_Q_LORA + MLA_KV_LORA + LANES:]
    rms = lambda z, g: z * lax.rsqrt(jnp.mean(z * z, axis=-1, keepdims=True) + RMS_EPS) * g
    cq_ref[...] = rms(cq, gq_ref[...]).astype(cq_ref.dtype)
    ckv_ref[...] = rms(ckv, gkv_ref[...]).astype(ckv_ref.dtype)
    kr_ref[...] = (kr * cos_ref[...] + krs * sin_ref[...]).astype(kr_ref.dtype)


def _mla_a(hb, w, gq, gkv, cos, sin, lp, tm):
    n = hb.shape[0]
    tpb = lp // tm
    return pl.pallas_call(
        _mla_a_kernel,
        grid=(n // tm,),
        in_specs=[pl.BlockSpec((tm, D_MODEL), lambda i: (i, 0)),
                  pl.BlockSpec((D_MODEL, MLA_A_COLS), lambda i: (0, 0)),
                  pl.BlockSpec((1, MLA_Q_LORA), lambda i: (0, 0)),
                  pl.BlockSpec((1, MLA_KV_LORA), lambda i: (0, 0)),
                  pl.BlockSpec((tm, LANES), lambda i: (i % tpb, 0)),
                  pl.BlockSpec((tm, LANES), lambda i: (i % tpb, 0))],
        out_specs=[pl.BlockSpec((tm, MLA_Q_LORA), lambda i: (i, 0)),
                   pl.BlockSpec((tm, MLA_KV_LORA), lambda i: (i, 0)),
                   pl.BlockSpec((tm, LANES), lambda i: (i, 0))],
        out_shape=[jax.ShapeDtypeStruct((n, MLA_Q_LORA), BF16),
                   jax.ShapeDtypeStruct((n, MLA_KV_LORA), BF16),
                   jax.ShapeDtypeStruct((n, LANES), BF16)],
        compiler_params=_params("parallel"),
        name="mla_a",
    )(hb, w, gq, gkv, cos, sin)


def _mla_q_kernel(x_ref, w_ref, ws_ref, cos_ref, sin_ref, o_ref):
    x = x_ref[...]
    main = _dot(x, w_ref[...])
    swapped = _dot(x, ws_ref[...])
    o_ref[:, :LANES] = main[:, :LANES].astype(o_ref.dtype)
    o_ref[:, LANES:] = (main[:, LANES:] * cos_ref[...] + swapped * sin_ref[...]).astype(o_ref.dtype)


def _mla_q(cq, w, ws, cos, sin, lp, tm):
    n = cq.shape[0]
    tpb = lp // tm
    return pl.pallas_call(
        _mla_q_kernel,
        grid=(n // tm, PAIRS),
        in_specs=[pl.BlockSpec((tm, MLA_Q_LORA), lambda i, p: (i, 0)),
                  pl.BlockSpec((MLA_Q_LORA, 2 * LANES), lambda i, p: (0, p)),
                  pl.BlockSpec((MLA_Q_LORA, LANES), lambda i, p: (0, p)),
                  pl.BlockSpec((tm, LANES), lambda i, p: (i % tpb, 0)),
                  pl.BlockSpec((tm, LANES), lambda i, p: (i % tpb, 0))],
        out_specs=pl.BlockSpec((tm, 2 * LANES), lambda i, p: (i, p)),
        out_shape=jax.ShapeDtypeStruct((n, PAIRS * 2 * LANES), BF16),
        compiler_params=_params("parallel", "parallel"),
        name="mla_q",
    )(cq, w, ws, cos, sin)


def _oproj_ln_kernel(o_ref, w_ref, h_ref, g_ref, b_ref, hf_ref, hb_ref):
    x = DEEPNORM_ALPHA * h_ref[...] + _dot(o_ref[...], w_ref[...])
    y = _layer_norm(x, g_ref[...], b_ref[...])
    hf_ref[...] = y
    hb_ref[...] = y.astype(hb_ref.dtype)


def _oproj_ln(o, w, h, g, b, tm):
    n = o.shape[0]
    row = lambda i: (i, 0)
    fixed = lambda i: (0, 0)
    return pl.pallas_call(
        _oproj_ln_kernel,
        grid=(n // tm,),
        in_specs=[pl.BlockSpec((tm, D_MODEL), row), pl.BlockSpec((D_MODEL, D_MODEL), fixed),
                  pl.BlockSpec((tm, D_MODEL), row), pl.BlockSpec((1, D_MODEL), fixed),
                  pl.BlockSpec((1, D_MODEL), fixed)],
        out_specs=[pl.BlockSpec((tm, D_MODEL), row), pl.BlockSpec((tm, D_MODEL), row)],
        out_shape=[jax.ShapeDtypeStruct((n, D_MODEL), F32), jax.ShapeDtypeStruct((n, D_MODEL), BF16)],
        compiler_params=_params("parallel"),
        name="oproj_ln",
    )(o, w, h, g, b)


FFN_HALO = BF16_SUBLANES


def _ffn_kernel(x_ref, halo_ref, h_ref, wg_ref, wv_ref, cwg_ref, cwv_ref, cbg_ref, cbv_ref, wo_ref,
                g_ref, b_ref, hf_ref, hb_ref, xext_ref, ug_ref, uv_ref, acc_ref, *, tm, tiles_per_batch):
    i = pl.program_id(0)
    c = pl.program_id(1)
    nc = pl.num_programs(1)

    @pl.when(c == 0)
    def _():
        pos = (i % tiles_per_batch) * tm - FFN_HALO + lax.broadcasted_iota(jnp.int32, (tm + FFN_HALO, 1), 0)
        xe = jnp.concatenate([halo_ref[...], x_ref[...]], axis=0).astype(F32)
        xext_ref[...] = jnp.where(pos >= LEAD, xe, 0.0).astype(BF16)
        acc_ref[...] = jnp.zeros_like(acc_ref)

    xe = xext_ref[...]
    ug_ref[...] = _dot(xe, wg_ref[...])
    uv_ref[...] = _dot(xe, wv_ref[...])

    def conv(u_ref, cw_ref, cb_ref):
        y = cb_ref[...]
        for tap in range(CONV_W):
            y = y + u_ref[pl.ds(FFN_HALO - (CONV_W - 1) + tap, tm), :] * cw_ref[tap:tap + 1, :]
        return y

    yg = conv(ug_ref, cwg_ref, cbg_ref)
    yv = conv(uv_ref, cwv_ref, cbv_ref)
    act = (yg / (1.0 + jnp.exp(-yg))) * yv
    acc_ref[...] += _dot(act.astype(BF16), wo_ref[...])

    @pl.when(c == nc - 1)
    def _():
        y = _layer_norm(DEEPNORM_ALPHA * h_ref[...] + acc_ref[...], g_ref[...], b_ref[...])
        hf_ref[...] = y
        hb_ref[...] = y.astype(hb_ref.dtype)


def _ffn(hb, hf, w_in, conv_w, conv_b, w_out, g, b, lp, tm, fc):
    n = hb.shape[0]
    nc = D_FF // fc
    tpb = lp // tm
    halo_blocks = tm // FFN_HALO
    row = lambda i, c: (i, 0)
    fixed = lambda i, c: (0, 0)
    gate = lambda i, c: (0, c)
    val = lambda i, c: (0, nc + c)
    return pl.pallas_call(
        functools.partial(_ffn_kernel, tm=tm, tiles_per_batch=tpb),
        grid=(n // tm, nc),
        in_specs=[pl.BlockSpec((tm, D_MODEL), row),
                  pl.BlockSpec((FFN_HALO, D_MODEL), lambda i, c: (jnp.maximum(i * halo_blocks - 1, 0), 0)),
                  pl.BlockSpec((tm, D_MODEL), row),
                  pl.BlockSpec((D_MODEL, fc), gate), pl.BlockSpec((D_MODEL, fc), val),
                  pl.BlockSpec((CONV_W, fc), gate), pl.BlockSpec((CONV_W, fc), val),
                  pl.BlockSpec((1, fc), gate), pl.BlockSpec((1, fc), val),
                  pl.BlockSpec((fc, D_MODEL), lambda i, c: (c, 0)),
                  pl.BlockSpec((1, D_MODEL), fixed), pl.BlockSpec((1, D_MODEL), fixed)],
        out_specs=[pl.BlockSpec((tm, D_MODEL), row), pl.BlockSpec((tm, D_MODEL), row)],
        out_shape=[jax.ShapeDtypeStruct((n, D_MODEL), F32), jax.ShapeDtypeStruct((n, D_MODEL), BF16)],
        scratch_shapes=[pltpu.VMEM((tm + FFN_HALO, D_MODEL), BF16),
                        pltpu.VMEM((tm + FFN_HALO, fc), F32), pltpu.VMEM((tm + FFN_HALO, fc), F32),
                        pltpu.VMEM((tm, D_MODEL), F32)],
        compiler_params=_params("parallel", "arbitrary"),
        name="ffn",
    )(hb, hb, hf, w_in, w_in, conv_w, conv_w, conv_b, conv_b, w_out, g, b)


def _rope_tables(lp, dim, theta, lanes_per_group):
    pos = (jnp.arange(lp) - LEAD).astype(F32)
    inv = theta ** (-jnp.arange(0, dim, 2, dtype=F32) / dim)
    ang = pos[:, None] * inv[None, :]
    cos, sin = jnp.cos(ang), jnp.sin(ang)
    rest = lanes_per_group - dim
    cos_g = jnp.concatenate([cos, cos, jnp.ones((lp, rest), F32)], axis=1)
    sin_g = jnp.concatenate([-sin, sin, jnp.zeros((lp, rest), F32)], axis=1)
    reps = LANES // lanes_per_group
    return jnp.tile(cos_g, (1, reps)), jnp.tile(sin_g, (1, reps))


def _swap_halves(w, dim):
    return jnp.concatenate([w[..., dim // 2:dim], w[..., :dim // 2]], axis=-1)


def _mla_weights(w_a, w_uq, w_ukv):
    d = w_a.shape[0]
    w_kr = w_a[:, MLA_Q_LORA + MLA_KV_LORA:]
    zeros = jnp.zeros((d, LANES - 2 * MLA_ROPE), F32)
    w_a_cat = jnp.concatenate([w_a[:, :MLA_Q_LORA + MLA_KV_LORA], w_kr, w_kr, zeros,
                               _swap_halves(w_kr, MLA_ROPE), _swap_halves(w_kr, MLA_ROPE), zeros], axis=1)
    scale = (MLA_NOPE + MLA_ROPE) ** -0.5
    wq = (w_uq * scale).reshape(MLA_Q_LORA, PAIRS, 2, MLA_NOPE + MLA_ROPE)
    nope = wq[..., :MLA_NOPE].reshape(MLA_Q_LORA, PAIRS, 2 * MLA_NOPE)
    rope = wq[..., MLA_NOPE:]
    zq = jnp.zeros((MLA_Q_LORA, PAIRS, LANES - 2 * MLA_ROPE), F32)
    w_main = jnp.concatenate([nope, rope.reshape(MLA_Q_LORA, PAIRS, 2 * MLA_ROPE), zq], axis=2)
    w_swap = jnp.concatenate([_swap_halves(rope, MLA_ROPE).reshape(MLA_Q_LORA, PAIRS, 2 * MLA_ROPE), zq], axis=2)
    wkv = w_ukv.reshape(MLA_KV_LORA, HEADS, MLA_NOPE + HEAD_DIM)
    w_kv = jnp.concatenate([wkv[..., :MLA_NOPE].reshape(MLA_KV_LORA, HEADS * MLA_NOPE),
                            wkv[..., MLA_NOPE:].reshape(MLA_KV_LORA, HEADS * HEAD_DIM)], axis=1)
    return (w_a_cat.astype(BF16), w_main.reshape(MLA_Q_LORA, PAIRS * 2 * LANES).astype(BF16),
            w_swap.reshape(MLA_Q_LORA, PAIRS * LANES).astype(BF16), w_kv.astype(BF16))


def _swa_weights(w_in):
    qd = HEADS * HEAD_DIM
    kd = SWA_KV_HEADS * HEAD_DIM
    q = w_in[:, :qd] * (HEAD_DIM ** -0.5)
    dup = lambda w: jnp.concatenate([w[:, :HEAD_DIM], w[:, :HEAD_DIM], w[:, HEAD_DIM:], w[:, HEAD_DIM:]], axis=1)
    return jnp.concatenate([q, dup(w_in[:, qd:qd + kd]), dup(w_in[:, qd + kd:])], axis=1).astype(BF16)


def _fox_weights(w_in, b_f):
    hd = HEADS * HEAD_DIM
    d = w_in.shape[0]
    w_qkv = jnp.concatenate([w_in[:, :hd] * (HEAD_DIM ** -0.5), w_in[:, hd:3 * hd]], axis=1).astype(BF16)
    w_fg = jnp.concatenate([w_in[:, 3 * hd:], jnp.zeros((d, LANES - HEADS), F32)], axis=1).astype(BF16)
    b_fg = jnp.concatenate([b_f, jnp.zeros((LANES - HEADS,), F32)])[None, :]
    return w_qkv, w_fg, b_fg


def kernel(x, meta_tokens, ln1_g, ln1_b, ln2_g, ln2_b, fox_w_in, fox_b_f, fox_w_o, swa_w_in, swa_sinks, swa_w_o,
           mla_w_a, mla_g_q, mla_g_kv, mla_w_uq, mla_w_ukv, mla_w_o, ffn_w_in, ffn_conv_w, ffn_conv_b, ffn_w_out):
    batch, seq, d = x.shape
    assert d == D_MODEL and seq % 256 == 0
    lp = seq + FIRST_REAL
    n = batch * lp

    tm = _tile(lp, 768, 256)
    tq = _tile(lp, 768, 256)
    tk = 256
    tn = 512

    h0 = jnp.concatenate([jnp.zeros((batch, LEAD, d), x.dtype),
                          jnp.broadcast_to(meta_tokens.astype(x.dtype)[None], (batch, N_META, d)), x], axis=1)
    hf = h0.reshape(n, d)
    hb = hf.astype(BF16)

    cos_p, sin_p = _rope_tables(lp, ROPE_DIM, ROPE_THETA, HEAD_DIM)
    cos_m, sin_m = _rope_tables(lp, MLA_ROPE, MLA_ROPE_THETA, MLA_ROPE)
    mla_zero = (lax.broadcasted_iota(jnp.int32, (1, LANES), 1) < 2 * MLA_ROPE).astype(F32)
    cos_m, sin_m = cos_m * mla_zero, sin_m * mla_zero

    for i in range(DEPTH):
        kind, j = i % 3, i // 3
        if kind == 0:
            w_qkv, w_fg, b_fg = _fox_weights(fox_w_in[j], fox_b_f[j])
            qkv = _matmul(hb, w_qkv, BF16, tm, tn)
            ccol, crow = _fox_gate(hb, w_fg, b_fg, batch, lp, tm)
            o = _fox_attention(qkv.reshape(batch, lp, -1), ccol, crow, batch, lp, tq, tk)
            w_o = fox_w_o[j]
        elif kind == 1:
            qkv = _swa_proj(hb, _swa_weights(swa_w_in[j]), cos_p, sin_p, lp, tm, 2 * LANES, 5)
            o = _swa_attention(qkv.reshape(batch, lp, -1), swa_sinks[j].astype(F32), batch, lp)
            w_o = swa_w_o[j]
        else:
            w_a_cat, w_main, w_swap, w_kv = _mla_weights(mla_w_a[j], mla_w_uq[j], mla_w_ukv[j])
            cq, ckv, kr = _mla_a(hb, w_a_cat, mla_g_q[j][None, :], mla_g_kv[j][None, :], cos_m, sin_m, lp, tm)
            qcat = _mla_q(cq, w_main, w_swap, cos_m, sin_m, lp, tm)
            kv = _matmul(ckv, w_kv, BF16, tm, tn)
            o = _mla_attention(qcat.reshape(batch, lp, -1), kv.reshape(batch, lp, -1),
                               kr.reshape(batch, lp, -1), batch, lp, tq, tk)
            w_o = mla_w_o[j]
        hf, hb = _oproj_ln(o.reshape(n, d), w_o.astype(BF16), hf, ln1_g[i][None, :], ln1_b[i][None, :], tm)
        hf, hb = _ffn(hb, hf, ffn_w_in[i].astype(BF16), ffn_conv_w[i], ffn_conv_b[i][None, :],
                      ffn_w_out[i].astype(BF16), ln2_g[i][None, :], ln2_b[i][None, :], lp, tm, 256)
    return hf.reshape(batch, lp, d)[:, FIRST_REAL:]
```

```python
import functools
import math

import numpy as np
import jax
import jax.numpy as jnp
from jax import lax
from jax.experimental import pallas as pl
from jax.experimental.pallas import tpu as pltpu

F32 = jnp.float32
BF16 = jnp.bfloat16

D_MODEL = 1024
DEPTH = 4
N_META = 16
LEAD = 240
FIRST_REAL = LEAD + N_META
NEG = -1e30
DEEPNORM_ALPHA = (2.0 * DEPTH) ** 0.25
LN_EPS = 1e-5
RMS_EPS = 1e-6
HEADS = 16
HEAD_DIM = 64
PAIRS = HEADS // 2
SWA_KV_HEADS = 2
WINDOW = 128
ROPE_THETA = 500000.0
ROPE_DIM = 16
MLA_Q_LORA = 384
MLA_KV_LORA = 256
MLA_NOPE = 64
MLA_ROPE = 32
MLA_ROPE_THETA = 10000.0
D_FF = 2816
CONV_W = 3
LOG2E = math.log2(math.e)

LANES = 128
BF16_SUBLANES = 16
VMEM_LIMIT = 56 * 1024 * 1024


def _params(*sem):
    return pltpu.CompilerParams(dimension_semantics=sem, vmem_limit_bytes=VMEM_LIMIT)


def _tile(n, pref, mult):
    best = mult
    t = mult
    while t <= min(n, pref):
        if n % t == 0:
            best = t
        t += mult
    assert n % best == 0
    return best


def _dot(a, b):
    return jnp.dot(a, b, preferred_element_type=F32)


def _dot_nt(a, b):
    return lax.dot_general(a, b, (((1,), (1,)), ((), ())), preferred_element_type=F32)


def _layer_norm(x, g, b):
    mu = jnp.mean(x, axis=-1, keepdims=True)
    xc = x - mu
    var = jnp.mean(xc * xc, axis=-1, keepdims=True)
    return xc * lax.rsqrt(var + LN_EPS) * g + b


def _mm_kernel(x_ref, w_ref, o_ref):
    o_ref[...] = _dot(x_ref[...], w_ref[...]).astype(o_ref.dtype)


def _matmul(x, w, out_dtype, tm, tn):
    n, k = x.shape
    m = w.shape[1]
    return pl.pallas_call(
        _mm_kernel,
        grid=(n // tm, m // tn),
        in_specs=[pl.BlockSpec((tm, k), lambda i, j: (i, 0)),
                  pl.BlockSpec((k, tn), lambda i, j: (0, j))],
        out_specs=pl.BlockSpec((tm, tn), lambda i, j: (i, j)),
        out_shape=jax.ShapeDtypeStruct((n, m), out_dtype),
        compiler_params=_params("parallel", "parallel"),
        name="matmul",
    )(x, w)


def _mm_t_kernel(x_ref, w_ref, o_ref, *, tk):
    y = _dot(x_ref[...], w_ref[...])
    for c in range(o_ref.shape[0]):
        o_ref[c] = y[c * tk:(c + 1) * tk, :].T.astype(o_ref.dtype)


def _matmul_t(x, w, tm, tn, tk):
    n, k = x.shape
    m = w.shape[1]
    r = tm // tk
    return pl.pallas_call(
        functools.partial(_mm_t_kernel, tk=tk),
        grid=(n // tm, m // tn),
        in_specs=[pl.BlockSpec((tm, k), lambda i, j: (i, 0)),
                  pl.BlockSpec((k, tn), lambda i, j: (0, j))],
        out_specs=pl.BlockSpec((r, tn, tk), lambda i, j: (i, j, 0)),
        out_shape=jax.ShapeDtypeStruct((n // tk, m, tk), BF16),
        compiler_params=_params("parallel", "parallel"),
        name="matmul_t",
    )(x, w)


GATE_SLOTS = 3


def _gate_placement():
    sq = np.zeros((GATE_SLOTS * LANES, PAIRS * LANES), np.float32)
    sk = np.zeros_like(sq)
    oq = np.zeros((1, PAIRS * LANES), np.float32)
    ok = np.zeros_like(oq)
    for h in range(HEADS):
        base = (h // 2) * LANES + (h % 2) * HEAD_DIM
        for part in range(GATE_SLOTS):
            sq[part * LANES + h, base + part] = 1.0
            sk[part * LANES + h, base + GATE_SLOTS + part] = -1.0
            oq[0, base + GATE_SLOTS + part] = 1.0
            ok[0, base + part] = 1.0
    return sq, sk, oq, ok


def _split3(x):
    hi = x.astype(BF16)
    r1 = x - hi.astype(F32)
    mid = r1.astype(BF16)
    lo = (r1 - mid.astype(F32)).astype(BF16)
    return hi, mid, lo


def _fox_gate_kernel(x_ref, w_ref, b_ref, sq_ref, sk_ref, oq_ref, ok_ref, aq_ref, ak_ref, carry_ref,
                     *, tm, tiles_per_batch):
    i = pl.program_id(0)

    @pl.when(i % tiles_per_batch == 0)
    def _():
        carry_ref[...] = jnp.zeros_like(carry_ref)

    fg = _dot(x_ref[...], w_ref[...]) + b_ref[...]
    logf = jnp.minimum(fg, 0.0) - jnp.log(1.0 + jnp.exp(-jnp.abs(fg)))
    row = lax.broadcasted_iota(jnp.int32, (tm, tm), 0)
    col = lax.broadcasted_iota(jnp.int32, (tm, tm), 1)
    tri = jnp.where(col <= row, 1.0, 0.0).astype(BF16)
    hi, mid, lo = _split3(logf)
    cs = _dot(tri, hi) + _dot(tri, mid) + _dot(tri, lo) + carry_ref[...]
    carry_ref[...] = cs[tm - 1:tm, :]
    parts = jnp.concatenate(_split3(cs * LOG2E), axis=1)
    aq_ref[...] = (_dot(parts, sq_ref[...]) + oq_ref[...]).astype(aq_ref.dtype)
    ak_ref[...] = (_dot(parts, sk_ref[...]) + ok_ref[...]).astype(ak_ref.dtype)


def _fox_gate(hb, w_fg, b_fg, lp, tm):
    n = hb.shape[0]
    tpb = lp // tm
    sq, sk, oq, ok = _gate_placement()
    fixed = lambda i: (0, 0)
    wide = PAIRS * LANES
    return pl.pallas_call(
        functools.partial(_fox_gate_kernel, tm=tm, tiles_per_batch=tpb),
        grid=(n // tm,),
        in_specs=[pl.BlockSpec((tm, D_MODEL), lambda i: (i, 0)),
                  pl.BlockSpec((D_MODEL, LANES), fixed),
                  pl.BlockSpec((1, LANES), fixed),
                  pl.BlockSpec((GATE_SLOTS * LANES, wide), fixed),
                  pl.BlockSpec((GATE_SLOTS * LANES, wide), fixed),
                  pl.BlockSpec((1, wide), fixed),
                  pl.BlockSpec((1, wide), fixed)],
        out_specs=[pl.BlockSpec((tm, wide), lambda i: (i, 0)),
                   pl.BlockSpec((tm, wide), lambda i: (i, 0))],
        out_shape=[jax.ShapeDtypeStruct((n, wide), BF16), jax.ShapeDtypeStruct((n, wide), BF16)],
        scratch_shapes=[pltpu.VMEM((1, LANES), F32)],
        compiler_params=_params("arbitrary"),
        name="fox_gate",
    )(hb, w_fg, b_fg, jnp.asarray(sq, BF16), jnp.asarray(sk, BF16), jnp.asarray(oq), jnp.asarray(ok))


def _flash_kernel(q_ref, qx_ref, k_ref, kx_ref, vt_ref, o_ref, m_ref, l_ref, acc_ref, *, tq, tk):
    i = pl.program_id(2)
    lane = lax.broadcasted_iota(jnp.int32, (1, 2 * LANES), 1)
    in_a = (lane & HEAD_DIM) == 0
    qf = jnp.concatenate([q_ref[0], qx_ref[0]], axis=1).astype(F32)
    q_pair = (jnp.where(in_a, qf, 0.0).astype(BF16), jnp.where(in_a, 0.0, qf).astype(BF16))
    m_ref[...] = jnp.full_like(m_ref, NEG)
    l_ref[...] = jnp.zeros_like(l_ref)
    acc_ref[...] = jnp.zeros_like(acc_ref)

    def step(j, masked):
        off = pl.multiple_of(j * tk, tk)
        kc = jnp.concatenate([k_ref[0, pl.ds(off, tk), :], kx_ref[0, pl.ds(off, tk), :]], axis=1)
        vt = vt_ref[0, j]
        if masked:
            kpos = j * tk + lax.broadcasted_iota(jnp.int32, (tk, tq), 0)
            qpos = i * tq + lax.broadcasted_iota(jnp.int32, (tk, tq), 1)
            valid = (kpos <= qpos) & (kpos >= LEAD)
        for a in range(2):
            s = _dot_nt(kc, q_pair[a])
            if masked:
                s = jnp.where(valid, s, NEG)
            m_old = m_ref[a]
            m_new = jnp.maximum(m_old, jnp.max(s, axis=0, keepdims=True))
            alpha = jnp.exp2(m_old - m_new)
            p = jnp.exp2(s - m_new)
            l_ref[a] = alpha * l_ref[a] + jnp.sum(p, axis=0, keepdims=True)
            m_ref[a] = m_new
            pv = _dot(vt[a * HEAD_DIM:(a + 1) * HEAD_DIM, :], p.astype(BF16))
            acc_ref[a] = alpha * acc_ref[a] + pv

    jdiag = jnp.maximum((i * tq) // tk, 1)
    jend = ((i + 1) * tq + tk - 1) // tk

    def masked_body(t, c):
        step(jnp.where(t == 0, 0, jdiag + t - 1), True)
        return c

    def plain_body(j, c):
        step(j, False)
        return c

    lax.fori_loop(0, 1 + jend - jdiag, masked_body, 0)
    lax.fori_loop(1, jdiag, plain_body, 0)
    ot = jnp.concatenate([acc_ref[0] / l_ref[0], acc_ref[1] / l_ref[1]], axis=0)
    o_ref[0] = ot.T.astype(o_ref.dtype)


def _flash_attention(q, qx, k, kx, vt, cols, batch, lp, tq, tk):
    qc, qxc, kc, kxc = cols
    nk = lp // tk
    return pl.pallas_call(
        functools.partial(_flash_kernel, tq=tq, tk=tk),
        grid=(batch, PAIRS, lp // tq),
        in_specs=[pl.BlockSpec((1, tq, LANES), lambda b, p, i: (b, i, qc(p))),
                  pl.BlockSpec((1, tq, LANES), lambda b, p, i: (b, i, qxc(p))),
                  pl.BlockSpec((1, lp, LANES), lambda b, p, i: (b, 0, kc(p))),
                  pl.BlockSpec((1, lp, LANES), lambda b, p, i: (b, 0, kxc(p))),
                  pl.BlockSpec((1, nk, LANES, tk), lambda b, p, i: (b, 0, p, 0))],
        out_specs=pl.BlockSpec((1, tq, LANES), lambda b, p, i: (b, i, p)),
        out_shape=jax.ShapeDtypeStruct((batch, lp, HEADS * HEAD_DIM), BF16),
        scratch_shapes=[pltpu.VMEM((2, 1, tq), F32), pltpu.VMEM((2, 1, tq), F32),
                        pltpu.VMEM((2, HEAD_DIM, tq), F32)],
        compiler_params=_params("parallel", "parallel", "parallel"),
        name="flash_attention",
    )(q, qx, k, kx, vt)


SWA_TQ = 128


def _swa_attn_kernel(sink_ref, q_ref, km_ref, kp_ref, kc_ref, vm_ref, vp_ref, vc_ref, o_ref):
    i = pl.program_id(1)
    t = SWA_TQ
    lo = lax.broadcasted_iota(jnp.int32, (1, LANES), 1) < HEAD_DIM
    row = lax.broadcasted_iota(jnp.int32, (t, 3 * t), 0)
    col = lax.broadcasted_iota(jnp.int32, (t, 3 * t), 1)
    qpos = i * t + row
    kpos = jnp.where(col < t, t + col, (i - 2) * t + col)
    d = qpos - kpos
    valid = (d >= 0) & (((col < t) & (kpos >= LEAD)) |
                        ((col >= t) & (d < WINDOW) & (kpos >= FIRST_REAL)))
    for g in range(SWA_KV_HEADS):
        sl = slice(g * LANES, (g + 1) * LANES)
        kcat = jnp.concatenate([km_ref[0, :, sl], kp_ref[0, :, sl], kc_ref[0, :, sl]], axis=0)
        vf = jnp.concatenate([vm_ref[0, :, sl], vp_ref[0, :, sl], vc_ref[0, :, sl]], axis=0).astype(F32)
        v_lo = jnp.where(lo, vf, 0.0).astype(BF16)
        v_hi = jnp.where(lo, 0.0, vf).astype(BF16)
        for pp in range(PAIRS // SWA_KV_HEADS):
            p = g * (PAIRS // SWA_KV_HEADS) + pp
            qf = q_ref[0, :, p * LANES:(p + 1) * LANES].astype(F32)
            q_pair = (jnp.where(lo, qf, 0.0).astype(BF16), jnp.where(lo, 0.0, qf).astype(BF16))
            ps, inv = [], []
            for a in range(2):
                sink = sink_ref[2 * p + a]
                s = jnp.where(valid, _dot_nt(q_pair[a], kcat), NEG)
                m = jnp.maximum(jnp.max(s, axis=1, keepdims=True), sink)
                e = jnp.exp(s - m)
                den = jnp.sum(e, axis=1, keepdims=True) + jnp.exp(sink - m)
                ps.append(e.astype(BF16))
                inv.append(1.0 / den)
            o = (_dot(ps[0], v_lo) + _dot(ps[1], v_hi)) * jnp.where(lo, inv[0], inv[1])
            o_ref[0, :, p * LANES:(p + 1) * LANES] = o.astype(o_ref.dtype)


def _swa_attention(qkv, sinks, batch, lp):
    t = SWA_TQ
    kblk, vblk = 4, 5
    kv_spec = lambda col, row_of: pl.BlockSpec((1, t, 2 * LANES), lambda b, i: (b, row_of(i), col))
    meta = lambda i: 1
    prev = lambda i: jnp.maximum(i - 1, 0)
    cur = lambda i: i
    return pl.pallas_call(
        _swa_attn_kernel,
        grid=(batch, lp // t),
        in_specs=[pl.BlockSpec(memory_space=pltpu.SMEM),
                  pl.BlockSpec((1, t, HEADS * HEAD_DIM), lambda b, i: (b, i, 0)),
                  kv_spec(kblk, meta), kv_spec(kblk, prev), kv_spec(kblk, cur),
                  kv_spec(vblk, meta), kv_spec(vblk, prev), kv_spec(vblk, cur)],
        out_specs=pl.BlockSpec((1, t, HEADS * HEAD_DIM), lambda b, i: (b, i, 0)),
        out_shape=jax.ShapeDtypeStruct((batch, lp, HEADS * HEAD_DIM), BF16),
        compiler_params=_params("parallel", "parallel"),
        name="swa_attention",
    )(sinks, qkv, qkv, qkv, qkv, qkv, qkv, qkv)


def _swa_proj_kernel(x_ref, w_ref, cos_ref, sin_ref, o_ref, *, tn, n_rope_blocks):
    j = pl.program_id(1)
    y = _dot(x_ref[...], w_ref[...])

    @pl.when(j < n_rope_blocks)
    def _():
        reps = tn // LANES
        cos = jnp.concatenate([cos_ref[...]] * reps, axis=1)
        sin = jnp.concatenate([sin_ref[...]] * reps, axis=1)
        lane = lax.broadcasted_iota(jnp.int32, (1, tn), 1)
        half = ROPE_DIM // 2
        partner = jnp.where((lane & (HEAD_DIM - 1)) < half,
                            pltpu.roll(y, tn - half, 1),
                            pltpu.roll(y, half, 1))
        o_ref[...] = (y * cos + partner * sin).astype(o_ref.dtype)

    @pl.when(j >= n_rope_blocks)
    def _():
        o_ref[...] = y.astype(o_ref.dtype)


def _swa_proj(hb, w, cos, sin, lp, tm, tn, n_rope_blocks):
    n = hb.shape[0]
    m = w.shape[1]
    tpb = lp // tm
    return pl.pallas_call(
        functools.partial(_swa_proj_kernel, tn=tn, n_rope_blocks=n_rope_blocks),
        grid=(n // tm, m // tn),
        in_specs=[pl.BlockSpec((tm, D_MODEL), lambda i, j: (i, 0)),
                  pl.BlockSpec((D_MODEL, tn), lambda i, j: (0, j)),
                  pl.BlockSpec((tm, LANES), lambda i, j: (i % tpb, 0)),
                  pl.BlockSpec((tm, LANES), lambda i, j: (i % tpb, 0))],
        out_specs=pl.BlockSpec((tm, tn), lambda i, j: (i, j)),
        out_shape=jax.ShapeDtypeStruct((n, m), BF16),
        compiler_params=_params("parallel", "parallel"),
        name="swa_proj",
    )(hb, w, cos, sin)


MLA_A_COLS = MLA_Q_LORA + MLA_KV_LORA + 2 * LANES


def _mla_a_kernel(x_ref, w_ref, gq_ref, gkv_ref, cos_ref, sin_ref, cq_ref, ckv_ref, kr_ref):
    y = _dot(x_ref[...], w_ref[...])
    cq = y[:, :MLA_Q_LORA]
    ckv = y[:, MLA_Q_LORA:MLA_Q_LORA + MLA_KV_LORA]
    kr = y[:, MLA_Q_LORA + MLA_KV_LORA:MLA_Q_LORA + MLA_KV_LORA + LANES]
    krs = y[:, MLA_Q_LORA + MLA_KV_LORA + LANES:]
    rms = lambda z, g: z * lax.rsqrt(jnp.mean(z * z, axis=-1, keepdims=True) + RMS_EPS) * g
    cq_ref[...] = rms(cq, gq_ref[...]).astype(cq_ref.dtype)
    ckv_ref[...] = rms(ckv, gkv_ref[...]).astype(ckv_ref.dtype)
    kr_ref[...] = (kr * cos_ref[...] + krs * sin_ref[...]).astype(kr_ref.dtype)


def _mla_a(hb, w, gq, gkv, cos, sin, lp, tm):
    n = hb.shape[0]
    tpb = lp // tm
    return pl.pallas_call(
        _mla_a_kernel,
        grid=(n // tm,),
        in_specs=[pl.BlockSpec((tm, D_MODEL), lambda i: (i, 0)),
                  pl.BlockSpec((D_MODEL, MLA_A_COLS), lambda i: (0, 0)),
                  pl.BlockSpec((1, MLA_Q_LORA), lambda i: (0, 0)),
                  pl.BlockSpec((1, MLA_KV_LORA), lambda i: (0, 0)),
                  pl.BlockSpec((tm, LANES), lambda i: (i % tpb, 0)),
                  pl.BlockSpec((tm, LANES), lambda i: (i % tpb, 0))],
        out_specs=[pl.BlockSpec((tm, MLA_Q_LORA), lambda i: (i, 0)),
                   pl.BlockSpec((tm, MLA_KV_LORA), lambda i: (i, 0)),
                   pl.BlockSpec((tm, LANES), lambda i: (i, 0))],
        out_shape=[jax.ShapeDtypeStruct((n, MLA_Q_LORA), BF16),
                   jax.ShapeDtypeStruct((n, MLA_KV_LORA), BF16),
                   jax.ShapeDtypeStruct((n, LANES), BF16)],
        compiler_params=_params("parallel"),
        name="mla_a",
    )(hb, w, gq, gkv, cos, sin)


def _mla_q_kernel(x_ref, w_ref, ws_ref, cos_ref, sin_ref, o_ref):
    x = x_ref[...]
    main = _dot(x, w_ref[...])
    swapped = _dot(x, ws_ref[...])
    o_ref[:, :LANES] = main[:, :LANES].astype(o_ref.dtype)
    o_ref[:, LANES:] = (main[:, LANES:] * cos_ref[...] + swapped * sin_ref[...]).astype(o_ref.dtype)


def _mla_q(cq, w, ws, cos, sin, lp, tm):
    n = cq.shape[0]
    tpb = lp // tm
    return pl.pallas_call(
        _mla_q_kernel,
        grid=(n // tm, PAIRS),
        in_specs=[pl.BlockSpec((tm, MLA_Q_LORA), lambda i, p: (i, 0)),
                  pl.BlockSpec((MLA_Q_LORA, 2 * LANES), lambda i, p: (0, p)),
                  pl.BlockSpec((MLA_Q_LORA, LANES), lambda i, p: (0, p)),
                  pl.BlockSpec((tm, LANES), lambda i, p: (i % tpb, 0)),
                  pl.BlockSpec((tm, LANES), lambda i, p: (i % tpb, 0))],
        out_specs=pl.BlockSpec((tm, 2 * LANES), lambda i, p: (i, p)),
        out_shape=jax.ShapeDtypeStruct((n, PAIRS * 2 * LANES), BF16),
        compiler_params=_params("parallel", "parallel"),
        name="mla_q",
    )(cq, w, ws, cos, sin)


def _oproj_ln_kernel(o_ref, w_ref, h_ref, g_ref, b_ref, hf_ref, hb_ref):
    x = DEEPNORM_ALPHA * h_ref[...] + _dot(o_ref[...], w_ref[...])
    y = _layer_norm(x, g_ref[...], b_ref[...])
    hf_ref[...] = y
    hb_ref[...] = y.astype(hb_ref.dtype)


def _oproj_ln(o, w, h, g, b, tm):
    n = o.shape[0]
    row = lambda i: (i, 0)
    fixed = lambda i: (0, 0)
    return pl.pallas_call(
        _oproj_ln_kernel,
        grid=(n // tm,),
        in_specs=[pl.BlockSpec((tm, D_MODEL), row), pl.BlockSpec((D_MODEL, D_MODEL), fixed),
                  pl.BlockSpec((tm, D_MODEL), row), pl.BlockSpec((1, D_MODEL), fixed),
                  pl.BlockSpec((1, D_MODEL), fixed)],
        out_specs=[pl.BlockSpec((tm, D_MODEL), row), pl.BlockSpec((tm, D_MODEL), row)],
        out_shape=[jax.ShapeDtypeStruct((n, D_MODEL), F32), jax.ShapeDtypeStruct((n, D_MODEL), BF16)],
        compiler_params=_params("parallel"),
        name="oproj_ln",
    )(o, w, h, g, b)


FFN_HALO = BF16_SUBLANES


def _ffn_kernel(x_ref, halo_ref, h_ref, wg_ref, wv_ref, cwg_ref, cwv_ref, cbg_ref, cbv_ref, wo_ref,
                g_ref, b_ref, hf_ref, hb_ref, xext_ref, ug_ref, uv_ref, acc_ref, *, tm, tiles_per_batch):
    i = pl.program_id(0)
    c = pl.program_id(1)
    nc = pl.num_programs(1)

    @pl.when(c == 0)
    def _():
        pos = (i % tiles_per_batch) * tm - FFN_HALO + lax.broadcasted_iota(jnp.int32, (tm + FFN_HALO, 1), 0)
        xe = jnp.concatenate([halo_ref[...], x_ref[...]], axis=0).astype(F32)
        xext_ref[...] = jnp.where(pos >= LEAD, xe, 0.0).astype(BF16)
        acc_ref[...] = jnp.zeros_like(acc_ref)

    xe = xext_ref[...]
    ug_ref[...] = _dot(xe, wg_ref[...])
    uv_ref[...] = _dot(xe, wv_ref[...])

    def conv(u_ref, cw_ref, cb_ref):
        y = cb_ref[...]
        for tap in range(CONV_W):
            y = y + u_ref[pl.ds(FFN_HALO - (CONV_W - 1) + tap, tm), :] * cw_ref[tap:tap + 1, :]
        return y

    yg = conv(ug_ref, cwg_ref, cbg_ref)
    yv = conv(uv_ref, cwv_ref, cbv_ref)
    act = (yg / (1.0 + jnp.exp(-yg))) * yv
    acc_ref[...] += _dot(act.astype(BF16), wo_ref[...])

    @pl.when(c == nc - 1)
    def _():
        y = _layer_norm(DEEPNORM_ALPHA * h_ref[...] + acc_ref[...], g_ref[...], b_ref[...])
        hf_ref[...] = y
        hb_ref[...] = y.astype(hb_ref.dtype)


def _ffn(hb, hf, w_in, conv_w, conv_b, w_out, g, b, lp, tm, fc):
    n = hb.shape[0]
    nc = D_FF // fc
    tpb = lp // tm
    halo_blocks = tm // FFN_HALO
    row = lambda i, c: (i, 0)
    fixed = lambda i, c: (0, 0)
    gate = lambda i, c: (0, c)
    val = lambda i, c: (0, nc + c)
    return pl.pallas_call(
        functools.partial(_ffn_kernel, tm=tm, tiles_per_batch=tpb),
        grid=(n // tm, nc),
        in_specs=[pl.BlockSpec((tm, D_MODEL), row),
                  pl.BlockSpec((FFN_HALO, D_MODEL), lambda i, c: (jnp.maximum(i * halo_blocks - 1, 0), 0)),
                  pl.BlockSpec((tm, D_MODEL), row),
                  pl.BlockSpec((D_MODEL, fc), gate), pl.BlockSpec((D_MODEL, fc), val),
                  pl.BlockSpec((CONV_W, fc), gate), pl.BlockSpec((CONV_W, fc), val),
                  pl.BlockSpec((1, fc), gate), pl.BlockSpec((1, fc), val),
                  pl.BlockSpec((fc, D_MODEL), lambda i, c: (c, 0)),
                  pl.BlockSpec((1, D_MODEL), fixed), pl.BlockSpec((1, D_MODEL), fixed)],
        out_specs=[pl.BlockSpec((tm, D_MODEL), row), pl.BlockSpec((tm, D_MODEL), row)],
        out_shape=[jax.ShapeDtypeStruct((n, D_MODEL), F32), jax.ShapeDtypeStruct((n, D_MODEL), BF16)],
        scratch_shapes=[pltpu.VMEM((tm + FFN_HALO, D_MODEL), BF16),
                        pltpu.VMEM((tm + FFN_HALO, fc), F32), pltpu.VMEM((tm + FFN_HALO, fc), F32),
                        pltpu.VMEM((tm, D_MODEL), F32)],
        compiler_params=_params("parallel", "arbitrary"),
        name="ffn",
    )(hb, hb, hf, w_in, w_in, conv_w, conv_w, conv_b, conv_b, w_out, g, b)


def _rope_tables(lp, dim, theta, lanes_per_group):
    pos = (jnp.arange(lp) - LEAD).astype(F32)
    inv = theta ** (-jnp.arange(0, dim, 2, dtype=F32) / dim)
    ang = pos[:, None] * inv[None, :]
    cos, sin = jnp.cos(ang), jnp.sin(ang)
    rest = lanes_per_group - dim
    cos_g = jnp.concatenate([cos, cos, jnp.ones((lp, rest), F32)], axis=1)
    sin_g = jnp.concatenate([-sin, sin, jnp.zeros((lp, rest), F32)], axis=1)
    reps = LANES // lanes_per_group
    return jnp.tile(cos_g, (1, reps)), jnp.tile(sin_g, (1, reps))


def _swap_halves(w, dim):
    return jnp.concatenate([w[..., dim // 2:dim], w[..., :dim // 2]], axis=-1)


def _rope_block(w):
    pad = jnp.zeros(w.shape[:-2] + (2, HEAD_DIM - MLA_ROPE), F32)
    return jnp.concatenate([w, pad], axis=-1).reshape(w.shape[:-2] + (LANES,))


def _mla_weights(w_a, w_uq, w_ukv):
    d = w_a.shape[0]
    w_kr = w_a[:, MLA_Q_LORA + MLA_KV_LORA:]
    both = lambda w: _rope_block(jnp.stack([w, w], axis=1))
    w_a_cat = jnp.concatenate([w_a[:, :MLA_Q_LORA + MLA_KV_LORA], both(w_kr),
                               both(_swap_halves(w_kr, MLA_ROPE))], axis=1)
    scale = (MLA_NOPE + MLA_ROPE) ** -0.5 * LOG2E
    wq = (w_uq * scale).reshape(MLA_Q_LORA, PAIRS, 2, MLA_NOPE + MLA_ROPE)
    nope = wq[..., :MLA_NOPE].reshape(MLA_Q_LORA, PAIRS, 2 * MLA_NOPE)
    rope = wq[..., MLA_NOPE:]
    w_main = jnp.concatenate([nope, _rope_block(rope)], axis=2)
    w_swap = _rope_block(_swap_halves(rope, MLA_ROPE))
    wkv = w_ukv.reshape(MLA_KV_LORA, HEADS, MLA_NOPE + HEAD_DIM)
    w_kn = wkv[..., :MLA_NOPE].reshape(MLA_KV_LORA, HEADS * MLA_NOPE)
    w_v = wkv[..., MLA_NOPE:].reshape(MLA_KV_LORA, HEADS * HEAD_DIM)
    return (w_a_cat.astype(BF16), w_main.reshape(MLA_Q_LORA, PAIRS * 2 * LANES).astype(BF16),
            w_swap.reshape(MLA_Q_LORA, PAIRS * LANES).astype(BF16), w_kn.astype(BF16), w_v.astype(BF16))


def _swa_weights(w_in):
    qd = HEADS * HEAD_DIM
    kd = SWA_KV_HEADS * HEAD_DIM
    q = w_in[:, :qd] * (HEAD_DIM ** -0.5)
    dup = lambda w: jnp.concatenate([w[:, :HEAD_DIM], w[:, :HEAD_DIM], w[:, HEAD_DIM:], w[:, HEAD_DIM:]], axis=1)
    return jnp.concatenate([q, dup(w_in[:, qd:qd + kd]), dup(w_in[:, qd + kd:])], axis=1).astype(BF16)


def _fox_weights(w_in, b_f):
    hd = HEADS * HEAD_DIM
    d = w_in.shape[0]
    w_qk = jnp.concatenate([w_in[:, :hd] * (HEAD_DIM ** -0.5 * LOG2E), w_in[:, hd:2 * hd]], axis=1).astype(BF16)
    w_v = w_in[:, 2 * hd:3 * hd].astype(BF16)
    w_fg = jnp.concatenate([w_in[:, 3 * hd:], jnp.zeros((d, LANES - HEADS), F32)], axis=1).astype(BF16)
    b_fg = jnp.concatenate([b_f, jnp.zeros((LANES - HEADS,), F32)])[None, :]
    return w_qk, w_v, w_fg, b_fg


def kernel(x, meta_tokens, ln1_g, ln1_b, ln2_g, ln2_b, fox_w_in, fox_b_f, fox_w_o, swa_w_in, swa_sinks, swa_w_o,
           mla_w_a, mla_g_q, mla_g_kv, mla_w_uq, mla_w_ukv, mla_w_o, ffn_w_in, ffn_conv_w, ffn_conv_b, ffn_w_out):
    batch, seq, d = x.shape
    assert d == D_MODEL and seq % 256 == 0
    lp = seq + FIRST_REAL
    n = batch * lp

    tm = _tile(lp, 768, 256)
    tq = _tile(lp, 768, 256)
    tk = 256
    tn = 512
    nk = lp // tk

    h0 = jnp.concatenate([jnp.zeros((batch, LEAD, d), x.dtype),
                          jnp.broadcast_to(meta_tokens.astype(x.dtype)[None], (batch, N_META, d)), x], axis=1)
    hf = h0.reshape(n, d)
    hb = hf.astype(BF16)

    cos_p, sin_p = _rope_tables(lp, ROPE_DIM, ROPE_THETA, HEAD_DIM)
    cos_m, sin_m = _rope_tables(lp, MLA_ROPE, MLA_ROPE_THETA, HEAD_DIM)
    b3 = lambda a: a.reshape(batch, lp, -1)

    for i in range(DEPTH):
        kind, j = i % 3, i // 3
        if kind == 0:
            w_qk, w_v, w_fg, b_fg = _fox_weights(fox_w_in[j], fox_b_f[j])
            qk = b3(_matmul(hb, w_qk, BF16, tm, tn))
            vt = _matmul_t(hb, w_v, tm, tn, tk).reshape(batch, nk, HEADS * HEAD_DIM, tk)
            aq, ak = _fox_gate(hb, w_fg, b_fg, lp, tm)
            cols = (lambda p: p, lambda p: p, lambda p: PAIRS + p, lambda p: p)
            o = _flash_attention(qk, b3(aq), qk, b3(ak), vt, cols, batch, lp, tq, tk)
            w_o = fox_w_o[j]
        elif kind == 1:
            qkv = _swa_proj(hb, _swa_weights(swa_w_in[j]), cos_p, sin_p, lp, tm, 2 * LANES, 5)
            o = _swa_attention(b3(qkv), swa_sinks[j].astype(F32), batch, lp)
            w_o = swa_w_o[j]
        else:
            w_a_cat, w_main, w_swap, w_kn, w_v = _mla_weights(mla_w_a[j], mla_w_uq[j], mla_w_ukv[j])
            cq, ckv, kr = _mla_a(hb, w_a_cat, mla_g_q[j][None, :], mla_g_kv[j][None, :], cos_m, sin_m, lp, tm)
            qcat = b3(_mla_q(cq, w_main, w_swap, cos_m, sin_m, lp, tm))
            kn = b3(_matmul(ckv, w_kn, BF16, tm, tn))
            vt = _matmul_t(ckv, w_v, tm, tn, tk).reshape(batch, nk, HEADS * HEAD_DIM, tk)
            cols = (lambda p: 2 * p, lambda p: 2 * p + 1, lambda p: p, lambda p: 0)
            o = _flash_attention(qcat, qcat, kn, b3(kr), vt, cols, batch, lp, tq, tk)
            w_o = mla_w_o[j]
        hf, hb = _oproj_ln(o.reshape(n, d), w_o.astype(BF16), hf, ln1_g[i][None, :], ln1_b[i][None, :], tm)
        hf, hb = _ffn(hb, hf, ffn_w_in[i].astype(BF16), ffn_conv_w[i], ffn_conv_b[i][None, :],
                      ffn_w_out[i].astype(BF16), ln2_g[i][None, :], ln2_b[i][None, :], lp, tm, 256)
    return hf.reshape(batch, lp, d)[:, FIRST_REAL:]
```

```python
import functools
import math

import numpy as np
import jax
import jax.numpy as jnp
from jax import lax
from jax.experimental import pallas as pl
from jax.experimental.pallas import tpu as pltpu

F32 = jnp.float32
BF16 = jnp.bfloat16

D_MODEL = 1024
DEPTH = 4
N_META = 16
LEAD = 240
FIRST_REAL = LEAD + N_META
NEG = -1e30
DEEPNORM_ALPHA = (2.0 * DEPTH) ** 0.25
LN_EPS = 1e-5
RMS_EPS = 1e-6
HEADS = 16
HEAD_DIM = 64
PAIRS = HEADS // 2
SWA_KV_HEADS = 2
WINDOW = 128
ROPE_THETA = 500000.0
ROPE_DIM = 16
MLA_Q_LORA = 384
MLA_KV_LORA = 256
MLA_NOPE = 64
MLA_ROPE = 32
MLA_ROPE_THETA = 10000.0
D_FF = 2816
CONV_W = 3
LOG2E = math.log2(math.e)

LANES = 128
BF16_SUBLANES = 16
VMEM_LIMIT = 56 * 1024 * 1024

GATE_SLOTS = 3
FOX_DEAD_LANE = HEAD_DIM + 2 * GATE_SLOTS
MLA_DEAD_LANE = MLA_NOPE + MLA_ROPE
M_INIT = -3e38


def _params(*sem):
    return pltpu.CompilerParams(dimension_semantics=sem, vmem_limit_bytes=VMEM_LIMIT)


def _tile(n, pref, mult):
    best = mult
    t = mult
    while t <= min(n, pref):
        if n % t == 0:
            best = t
        t += mult
    assert n % best == 0
    return best


def _dot(a, b):
    return jnp.dot(a, b, preferred_element_type=F32)


def _dot_nt(a, b):
    return lax.dot_general(a, b, (((1,), (1,)), ((), ())), preferred_element_type=F32)


def _layer_norm(x, g, b):
    mu = jnp.mean(x, axis=-1, keepdims=True)
    xc = x - mu
    var = jnp.mean(xc * xc, axis=-1, keepdims=True)
    return xc * lax.rsqrt(var + LN_EPS) * g + b


def _one_hot_lanes(width, period, lane):
    idx = lax.broadcasted_iota(jnp.int32, (1, width), 1)
    return jnp.where((idx & (period - 1)) == lane, 1.0, 0.0)


def _dead_rows(tile_in_batch, tm):
    pos = tile_in_batch * tm + lax.broadcasted_iota(jnp.int32, (tm, 1), 0)
    return jnp.where(pos < LEAD, NEG, 0.0)


def _mm_kernel(x_ref, w_ref, o_ref):
    o_ref[...] = _dot(x_ref[...], w_ref[...]).astype(o_ref.dtype)


def _matmul(x, w, out_dtype, tm, tn):
    n, k = x.shape
    m = w.shape[1]
    return pl.pallas_call(
        _mm_kernel,
        grid=(n // tm, m // tn),
        in_specs=[pl.BlockSpec((tm, k), lambda i, j: (i, 0)),
                  pl.BlockSpec((k, tn), lambda i, j: (0, j))],
        out_specs=pl.BlockSpec((tm, tn), lambda i, j: (i, j)),
        out_shape=jax.ShapeDtypeStruct((n, m), out_dtype),
        compiler_params=_params("parallel", "parallel"),
        name="matmul",
    )(x, w)


def _mm_add_kernel(x_ref, w_ref, e_ref, o_ref):
    e = e_ref[...].astype(F32)
    reps = o_ref.shape[1] // e.shape[1]
    if reps > 1:
        e = jnp.concatenate([e] * reps, axis=1)
    o_ref[...] = (_dot(x_ref[...], w_ref[...]) + e).astype(o_ref.dtype)


def _matmul_add(x, w, e, tm, tn):
    n, k = x.shape
    m = w.shape[1]
    if e.shape[1] == m:
        e_spec = pl.BlockSpec((tm, tn), lambda i, j: (i, j))
    else:
        assert e.shape[1] == LANES
        e_spec = pl.BlockSpec((tm, LANES), lambda i, j: (i, 0))
    return pl.pallas_call(
        _mm_add_kernel,
        grid=(n // tm, m // tn),
        in_specs=[pl.BlockSpec((tm, k), lambda i, j: (i, 0)),
                  pl.BlockSpec((k, tn), lambda i, j: (0, j)),
                  e_spec],
        out_specs=pl.BlockSpec((tm, tn), lambda i, j: (i, j)),
        out_shape=jax.ShapeDtypeStruct((n, m), BF16),
        compiler_params=_params("parallel", "parallel"),
        name="matmul_add",
    )(x, w, e)


def _mm_t_kernel(x_ref, w_ref, o_ref, *, tk):
    y = _dot(x_ref[...], w_ref[...])
    for c in range(o_ref.shape[0]):
        o_ref[c] = y[c * tk:(c + 1) * tk, :].T.astype(o_ref.dtype)


def _matmul_t(x, w, tm, tn, tk):
    n, k = x.shape
    m = w.shape[1]
    r = tm // tk
    return pl.pallas_call(
        functools.partial(_mm_t_kernel, tk=tk),
        grid=(n // tm, m // tn),
        in_specs=[pl.BlockSpec((tm, k), lambda i, j: (i, 0)),
                  pl.BlockSpec((k, tn), lambda i, j: (0, j))],
        out_specs=pl.BlockSpec((r, tn, tk), lambda i, j: (i, j, 0)),
        out_shape=jax.ShapeDtypeStruct((n // tk, m, tk), BF16),
        compiler_params=_params("parallel", "parallel"),
        name="matmul_t",
    )(x, w)


def _gate_placement():
    wide = HEADS * LANES
    sq = np.zeros((GATE_SLOTS * LANES, wide), np.float32)
    sk = np.zeros_like(sq)
    oq = np.zeros((1, wide), np.float32)
    ok = np.zeros_like(oq)
    for h in range(HEADS):
        base = h * LANES + HEAD_DIM
        for part in range(GATE_SLOTS):
            sq[part * LANES + h, base + part] = 1.0
            sk[part * LANES + h, base + GATE_SLOTS + part] = -1.0
            oq[0, base + GATE_SLOTS + part] = 1.0
            ok[0, base + part] = 1.0
        oq[0, h * LANES + FOX_DEAD_LANE] = 1.0
    return sq, sk, oq, ok


def _split3(x):
    hi = x.astype(BF16)
    r1 = x - hi.astype(F32)
    mid = r1.astype(BF16)
    lo = (r1 - mid.astype(F32)).astype(BF16)
    return hi, mid, lo


def _fox_gate_kernel(x_ref, w_ref, b_ref, sq_ref, sk_ref, oq_ref, ok_ref, aq_ref, ak_ref, carry_ref,
                     *, tm, tiles_per_batch):
    i = pl.program_id(0)

    @pl.when(i % tiles_per_batch == 0)
    def _():
        carry_ref[...] = jnp.zeros_like(carry_ref)

    fg = _dot(x_ref[...], w_ref[...]) + b_ref[...]
    logf = jnp.minimum(fg, 0.0) - jnp.log(1.0 + jnp.exp(-jnp.abs(fg)))
    row = lax.broadcasted_iota(jnp.int32, (tm, tm), 0)
    col = lax.broadcasted_iota(jnp.int32, (tm, tm), 1)
    tri = jnp.where(col <= row, 1.0, 0.0).astype(BF16)
    hi, mid, lo = _split3(logf)
    cs = _dot(tri, hi) + _dot(tri, mid) + _dot(tri, lo) + carry_ref[...]
    carry_ref[...] = cs[tm - 1:tm, :]
    parts = jnp.concatenate(_split3(cs * LOG2E), axis=1)
    wide = aq_ref.shape[1]
    dead = _dead_rows(i % tiles_per_batch, tm) * _one_hot_lanes(wide, LANES, FOX_DEAD_LANE)
    aq_ref[...] = (_dot(parts, sq_ref[...]) + oq_ref[...]).astype(aq_ref.dtype)
    ak_ref[...] = (_dot(parts, sk_ref[...]) + ok_ref[...] + dead).astype(ak_ref.dtype)


def _fox_gate(hb, w_fg, b_fg, lp, tm):
    n = hb.shape[0]
    tpb = lp // tm
    sq, sk, oq, ok = _gate_placement()
    fixed = lambda i: (0, 0)
    wide = HEADS * LANES
    return pl.pallas_call(
        functools.partial(_fox_gate_kernel, tm=tm, tiles_per_batch=tpb),
        grid=(n // tm,),
        in_specs=[pl.BlockSpec((tm, D_MODEL), lambda i: (i, 0)),
                  pl.BlockSpec((D_MODEL, LANES), fixed),
                  pl.BlockSpec((1, LANES), fixed),
                  pl.BlockSpec((GATE_SLOTS * LANES, wide), fixed),
                  pl.BlockSpec((GATE_SLOTS * LANES, wide), fixed),
                  pl.BlockSpec((1, wide), fixed),
                  pl.BlockSpec((1, wide), fixed)],
        out_specs=[pl.BlockSpec((tm, wide), lambda i: (i, 0)),
                   pl.BlockSpec((tm, wide), lambda i: (i, 0))],
        out_shape=[jax.ShapeDtypeStruct((n, wide), BF16), jax.ShapeDtypeStruct((n, wide), BF16)],
        scratch_shapes=[pltpu.VMEM((1, LANES), F32)],
        compiler_params=_params("arbitrary"),
        name="fox_gate",
    )(hb, w_fg, b_fg, jnp.asarray(sq, BF16), jnp.asarray(sk, BF16), jnp.asarray(oq), jnp.asarray(ok))


def _flash_kernel(q_ref, k_ref, vt_ref, o_ref, sa_ref, sb_ref, xa_ref, xb_ref, m_ref, l_ref, acc_ref, *, tq, tk):
    i = pl.program_id(2)
    r = tq // tk
    m_ref[...] = jnp.full_like(m_ref, M_INIT)
    l_ref[...] = jnp.zeros_like(l_ref)
    acc_ref[...] = jnp.zeros_like(acc_ref)

    def diag_mask(s):
        keep = lax.broadcasted_iota(jnp.int32, s.shape, 0) <= lax.broadcasted_iota(jnp.int32, s.shape, 1)
        return jnp.where(keep, s, NEG)

    def scores(s_ref, x_ref, j, a, qs, diagonal):
        off = pl.multiple_of(j * tk, tk)
        s = _dot_nt(k_ref[0, pl.ds(off, tk), a * LANES:(a + 1) * LANES], q_ref[0, qs:, a * LANES:(a + 1) * LANES])
        if diagonal:
            s = diag_mask(s)
        s_ref[a, :, qs:] = s
        x_ref[a, :, qs:] = jnp.max(s, axis=0, keepdims=True)

    def consume(s_ref, x_ref, j, a, qs, mask_now):
        s = s_ref[a, :, qs:]
        if mask_now:
            s = diag_mask(s)
            smax = jnp.max(s, axis=0, keepdims=True)
        else:
            smax = x_ref[a, :, qs:]
        m_old = m_ref[a, :, qs:]
        m_new = jnp.maximum(m_old, smax)
        alpha = jnp.exp2(m_old - m_new)
        p = jnp.exp2(s - m_new)
        l_ref[a, :, qs:] = alpha * l_ref[a, :, qs:] + jnp.sum(p, axis=0, keepdims=True)
        m_ref[a, :, qs:] = m_new
        pv = _dot(vt_ref[0, j, a * HEAD_DIM:(a + 1) * HEAD_DIM, :], p.astype(BF16))
        acc_ref[a, :, qs:] = alpha * acc_ref[a, :, qs:] + pv

    buf_a, buf_b = (sa_ref, xa_ref), (sb_ref, xb_ref)
    jdiag = i * r
    odd = jdiag & 1
    for a in range(2):
        scores(*buf_a, 0, a, 0, False)

    @pl.when(odd == 1)
    def _():
        for a in range(2):
            scores(*buf_b, 1, a, 0, False)
            consume(*buf_a, 0, a, 0, False)
        sa_ref[...] = sb_ref[...]
        xa_ref[...] = xb_ref[...]

    def pair_body(t, c):
        j = odd + 2 * t
        for a in range(2):
            scores(*buf_b, j + 1, a, 0, False)
            consume(*buf_a, j, a, 0, False)
        for a in range(2):
            scores(*buf_a, j + 2, a, 0, False)
            consume(*buf_b, j + 1, a, 0, False)
        return c

    lax.fori_loop(0, (jdiag - odd) // 2, pair_body, 0)

    bufs = (buf_a, buf_b)
    for d in range(r):
        qs = d * tk
        for a in range(2):
            if d + 1 < r:
                scores(*bufs[(d + 1) & 1], jdiag + d + 1, a, qs + tk, True)
            consume(*bufs[d & 1], jdiag + d, a, qs, d == 0)

    ot = jnp.concatenate([acc_ref[0] / l_ref[0], acc_ref[1] / l_ref[1]], axis=0)
    o_ref[0] = ot.T.astype(o_ref.dtype)


def _flash_attention(qh, kh, vt, batch, lp, tq, tk):
    nk = lp // tk
    return pl.pallas_call(
        functools.partial(_flash_kernel, tq=tq, tk=tk),
        grid=(batch, PAIRS, lp // tq),
        in_specs=[pl.BlockSpec((1, tq, 2 * LANES), lambda b, p, i: (b, i, p)),
                  pl.BlockSpec((1, lp, 2 * LANES), lambda b, p, i: (b, 0, p)),
                  pl.BlockSpec((1, nk, 2 * HEAD_DIM, tk), lambda b, p, i: (b, 0, p, 0))],
        out_specs=pl.BlockSpec((1, tq, 2 * HEAD_DIM), lambda b, p, i: (b, i, p)),
        out_shape=jax.ShapeDtypeStruct((batch, lp, HEADS * HEAD_DIM), BF16),
        scratch_shapes=[pltpu.VMEM((2, tk, tq), F32), pltpu.VMEM((2, tk, tq), F32),
                        pltpu.VMEM((2, 1, tq), F32), pltpu.VMEM((2, 1, tq), F32),
                        pltpu.VMEM((2, 1, tq), F32), pltpu.VMEM((2, 1, tq), F32),
                        pltpu.VMEM((2, HEAD_DIM, tq), F32)],
        compiler_params=_params("parallel", "parallel", "parallel"),
        name="flash_attention",
    )(qh, kh, vt)


SWA_TQ = 128


def _swa_attn_kernel(sink_ref, q_ref, km_ref, kp_ref, kc_ref, vm_ref, vp_ref, vc_ref, o_ref):
    i = pl.program_id(1)
    t = SWA_TQ
    lo = lax.broadcasted_iota(jnp.int32, (1, LANES), 1) < HEAD_DIM
    row = lax.broadcasted_iota(jnp.int32, (t, 3 * t), 0)
    col = lax.broadcasted_iota(jnp.int32, (t, 3 * t), 1)
    qpos = i * t + row
    kpos = jnp.where(col < t, t + col, (i - 2) * t + col)
    d = qpos - kpos
    valid = (d >= 0) & (((col < t) & (kpos >= LEAD)) |
                        ((col >= t) & (d < WINDOW) & (kpos >= FIRST_REAL)))
    for g in range(SWA_KV_HEADS):
        sl = slice(g * LANES, (g + 1) * LANES)
        kcat = jnp.concatenate([km_ref[0, :, sl], kp_ref[0, :, sl], kc_ref[0, :, sl]], axis=0)
        vf = jnp.concatenate([vm_ref[0, :, sl], vp_ref[0, :, sl], vc_ref[0, :, sl]], axis=0).astype(F32)
        v_lo = jnp.where(lo, vf, 0.0).astype(BF16)
        v_hi = jnp.where(lo, 0.0, vf).astype(BF16)
        for pp in range(PAIRS // SWA_KV_HEADS):
            p = g * (PAIRS // SWA_KV_HEADS) + pp
            qf = q_ref[0, :, p * LANES:(p + 1) * LANES].astype(F32)
            q_pair = (jnp.where(lo, qf, 0.0).astype(BF16), jnp.where(lo, 0.0, qf).astype(BF16))
            ps, inv = [], []
            for a in range(2):
                sink = sink_ref[2 * p + a]
                s = jnp.where(valid, _dot_nt(q_pair[a], kcat), NEG)
                m = jnp.maximum(jnp.max(s, axis=1, keepdims=True), sink)
                e = jnp.exp(s - m)
                den = jnp.sum(e, axis=1, keepdims=True) + jnp.exp(sink - m)
                ps.append(e.astype(BF16))
                inv.append(1.0 / den)
            o = (_dot(ps[0], v_lo) + _dot(ps[1], v_hi)) * jnp.where(lo, inv[0], inv[1])
            o_ref[0, :, p * LANES:(p + 1) * LANES] = o.astype(o_ref.dtype)


def _swa_attention(qkv, sinks, batch, lp):
    t = SWA_TQ
    kblk, vblk = 4, 5
    kv_spec = lambda col, row_of: pl.BlockSpec((1, t, 2 * LANES), lambda b, i: (b, row_of(i), col))
    meta = lambda i: 1
    prev = lambda i: jnp.maximum(i - 1, 0)
    cur = lambda i: i
    return pl.pallas_call(
        _swa_attn_kernel,
        grid=(batch, lp // t),
        in_specs=[pl.BlockSpec(memory_space=pltpu.SMEM),
                  pl.BlockSpec((1, t, HEADS * HEAD_DIM), lambda b, i: (b, i, 0)),
                  kv_spec(kblk, meta), kv_spec(kblk, prev), kv_spec(kblk, cur),
                  kv_spec(vblk, meta), kv_spec(vblk, prev), kv_spec(vblk, cur)],
        out_specs=pl.BlockSpec((1, t, HEADS * HEAD_DIM), lambda b, i: (b, i, 0)),
        out_shape=jax.ShapeDtypeStruct((batch, lp, HEADS * HEAD_DIM), BF16),
        compiler_params=_params("parallel", "parallel"),
        name="swa_attention",
    )(sinks, qkv, qkv, qkv, qkv, qkv, qkv, qkv)


def _swa_proj_kernel(x_ref, w_ref, cos_ref, sin_ref, o_ref, *, tn, n_rope_blocks):
    j = pl.program_id(1)
    y = _dot(x_ref[...], w_ref[...])

    @pl.when(j < n_rope_blocks)
    def _():
        reps = tn // LANES
        cos = jnp.concatenate([cos_ref[...]] * reps, axis=1)
        sin = jnp.concatenate([sin_ref[...]] * reps, axis=1)
        lane = lax.broadcasted_iota(jnp.int32, (1, tn), 1)
        half = ROPE_DIM // 2
        partner = jnp.where((lane & (HEAD_DIM - 1)) < half,
                            pltpu.roll(y, tn - half, 1),
                            pltpu.roll(y, half, 1))
        o_ref[...] = (y * cos + partner * sin).astype(o_ref.dtype)

    @pl.when(j >= n_rope_blocks)
    def _():
        o_ref[...] = y.astype(o_ref.dtype)


def _swa_proj(hb, w, cos, sin, lp, tm, tn, n_rope_blocks):
    n = hb.shape[0]
    m = w.shape[1]
    tpb = lp // tm
    return pl.pallas_call(
        functools.partial(_swa_proj_kernel, tn=tn, n_rope_blocks=n_rope_blocks),
        grid=(n // tm, m // tn),
        in_specs=[pl.BlockSpec((tm, D_MODEL), lambda i, j: (i, 0)),
                  pl.BlockSpec((D_MODEL, tn), lambda i, j: (0, j)),
                  pl.BlockSpec((tm, LANES), lambda i, j: (i % tpb, 0)),
                  pl.BlockSpec((tm, LANES), lambda i, j: (i % tpb, 0))],
        out_specs=pl.BlockSpec((tm, tn), lambda i, j: (i, j)),
        out_shape=jax.ShapeDtypeStruct((n, m), BF16),
        compiler_params=_params("parallel", "parallel"),
        name="swa_proj",
    )(hb, w, cos, sin)


MLA_A_COLS = MLA_Q_LORA + MLA_KV_LORA + 2 * LANES


def _mla_a_kernel(x_ref, w_ref, gq_ref, gkv_ref, cos_ref, sin_ref, cq_ref, ckv_ref, kr_ref, *, tm, tiles_per_batch):
    y = _dot(x_ref[...], w_ref[...])
    cq = y[:, :MLA_Q_LORA]
    ckv = y[:, MLA_Q_LORA:MLA_Q_LORA + MLA_KV_LORA]
    kr = y[:, MLA_Q_LORA + MLA_KV_LORA:MLA_Q_LORA + MLA_KV_LORA + LANES]
    krs = y[:, MLA_Q_LORA + MLA_KV_LORA + LANES:]
    rms = lambda z, g: z * lax.rsqrt(jnp.mean(z * z, axis=-1, keepdims=True) + RMS_EPS) * g
    cq_ref[...] = rms(cq, gq_ref[...]).astype(cq_ref.dtype)
    ckv_ref[...] = rms(ckv, gkv_ref[...]).astype(ckv_ref.dtype)
    dead = _dead_rows(pl.program_id(0) % tiles_per_batch, tm) * _one_hot_lanes(LANES, LANES, MLA_DEAD_LANE)
    kr_ref[...] = (kr * cos_ref[...] + krs * sin_ref[...] + dead).astype(kr_ref.dtype)


def _mla_a(hb, w, gq, gkv, cos, sin, lp, tm):
    n = hb.shape[0]
    tpb = lp // tm
    return pl.pallas_call(
        functools.partial(_mla_a_kernel, tm=tm, tiles_per_batch=tpb),
        grid=(n // tm,),
        in_specs=[pl.BlockSpec((tm, D_MODEL), lambda i: (i, 0)),
                  pl.BlockSpec((D_MODEL, MLA_A_COLS), lambda i: (0, 0)),
                  pl.BlockSpec((1, MLA_Q_LORA), lambda i: (0, 0)),
                  pl.BlockSpec((1, MLA_KV_LORA), lambda i: (0, 0)),
                  pl.BlockSpec((tm, LANES), lambda i: (i % tpb, 0)),
                  pl.BlockSpec((tm, LANES), lambda i: (i % tpb, 0))],
        out_specs=[pl.BlockSpec((tm, MLA_Q_LORA), lambda i: (i, 0)),
                   pl.BlockSpec((tm, MLA_KV_LORA), lambda i: (i, 0)),
                   pl.BlockSpec((tm, LANES), lambda i: (i, 0))],
        out_shape=[jax.ShapeDtypeStruct((n, MLA_Q_LORA), BF16),
                   jax.ShapeDtypeStruct((n, MLA_KV_LORA), BF16),
                   jax.ShapeDtypeStruct((n, LANES), BF16)],
        compiler_params=_params("parallel"),
        name="mla_a",
    )(hb, w, gq, gkv, cos, sin)


def _mla_q_kernel(x_ref, w_ref, ws_ref, cos_ref, sin_ref, o_ref):
    x = x_ref[...]
    reps = o_ref.shape[1] // LANES
    cos = jnp.concatenate([cos_ref[...]] * reps, axis=1)
    sin = jnp.concatenate([sin_ref[...]] * reps, axis=1)
    y = _dot(x, w_ref[...]) * cos + _dot(x, ws_ref[...]) * sin
    o_ref[...] = (y + _one_hot_lanes(o_ref.shape[1], LANES, MLA_DEAD_LANE)).astype(o_ref.dtype)


def _mla_q(cq, w, ws, cos, sin, lp, tm, tn):
    n = cq.shape[0]
    m = w.shape[1]
    tpb = lp // tm
    return pl.pallas_call(
        _mla_q_kernel,
        grid=(n // tm, m // tn),
        in_specs=[pl.BlockSpec((tm, MLA_Q_LORA), lambda i, j: (i, 0)),
                  pl.BlockSpec((MLA_Q_LORA, tn), lambda i, j: (0, j)),
                  pl.BlockSpec((MLA_Q_LORA, tn), lambda i, j: (0, j)),
                  pl.BlockSpec((tm, LANES), lambda i, j: (i % tpb, 0)),
                  pl.BlockSpec((tm, LANES), lambda i, j: (i % tpb, 0))],
        out_specs=pl.BlockSpec((tm, tn), lambda i, j: (i, j)),
        out_shape=jax.ShapeDtypeStruct((n, m), BF16),
        compiler_params=_params("parallel", "parallel"),
        name="mla_q",
    )(cq, w, ws, cos, sin)


def _oproj_ln_kernel(o_ref, w_ref, h_ref, g_ref, b_ref, hf_ref, hb_ref):
    x = DEEPNORM_ALPHA * h_ref[...] + _dot(o_ref[...], w_ref[...])
    y = _layer_norm(x, g_ref[...], b_ref[...])
    hf_ref[...] = y
    hb_ref[...] = y.astype(hb_ref.dtype)


def _oproj_ln(o, w, h, g, b, tm):
    n = o.shape[0]
    row = lambda i: (i, 0)
    fixed = lambda i: (0, 0)
    return pl.pallas_call(
        _oproj_ln_kernel,
        grid=(n // tm,),
        in_specs=[pl.BlockSpec((tm, D_MODEL), row), pl.BlockSpec((D_MODEL, D_MODEL), fixed),
                  pl.BlockSpec((tm, D_MODEL), row), pl.BlockSpec((1, D_MODEL), fixed),
                  pl.BlockSpec((1, D_MODEL), fixed)],
        out_specs=[pl.BlockSpec((tm, D_MODEL), row), pl.BlockSpec((tm, D_MODEL), row)],
        out_shape=[jax.ShapeDtypeStruct((n, D_MODEL), F32), jax.ShapeDtypeStruct((n, D_MODEL), BF16)],
        compiler_params=_params("parallel"),
        name="oproj_ln",
    )(o, w, h, g, b)


FFN_HALO = BF16_SUBLANES


def _ffn_kernel(x_ref, halo_ref, h_ref, wg_ref, wv_ref, cwg_ref, cwv_ref, cbg_ref, cbv_ref, wo_ref,
                g_ref, b_ref, hf_ref, hb_ref, xext_ref, ug_ref, uv_ref, acc_ref, *, tm, tiles_per_batch):
    i = pl.program_id(0)
    c = pl.program_id(1)
    nc = pl.num_programs(1)

    @pl.when(c == 0)
    def _():
        pos = (i % tiles_per_batch) * tm - FFN_HALO + lax.broadcasted_iota(jnp.int32, (tm + FFN_HALO, 1), 0)
        xe = jnp.concatenate([halo_ref[...], x_ref[...]], axis=0).astype(F32)
        xext_ref[...] = jnp.where(pos >= LEAD, xe, 0.0).astype(BF16)
        acc_ref[...] = jnp.zeros_like(acc_ref)

    xe = xext_ref[...]
    ug_ref[...] = _dot(xe, wg_ref[...])
    uv_ref[...] = _dot(xe, wv_ref[...])

    def conv(u_ref, cw_ref, cb_ref):
        y = cb_ref[...]
        for tap in range(CONV_W):
            y = y + u_ref[pl.ds(FFN_HALO - (CONV_W - 1) + tap, tm), :] * cw_ref[tap:tap + 1, :]
        return y

    yg = conv(ug_ref, cwg_ref, cbg_ref)
    yv = conv(uv_ref, cwv_ref, cbv_ref)
    act = (yg / (1.0 + jnp.exp(-yg))) * yv
    acc_ref[...] += _dot(act.astype(BF16), wo_ref[...])

    @pl.when(c == nc - 1)
    def _():
        y = _layer_norm(DEEPNORM_ALPHA * h_ref[...] + acc_ref[...], g_ref[...], b_ref[...])
        hf_ref[...] = y
        hb_ref[...] = y.astype(hb_ref.dtype)


def _ffn(hb, hf, w_in, conv_w, conv_b, w_out, g, b, lp, tm, fc):
    n = hb.shape[0]
    nc = D_FF // fc
    tpb = lp // tm
    halo_blocks = tm // FFN_HALO
    row = lambda i, c: (i, 0)
    fixed = lambda i, c: (0, 0)
    gate = lambda i, c: (0, c)
    val = lambda i, c: (0, nc + c)
    return pl.pallas_call(
        functools.partial(_ffn_kernel, tm=tm, tiles_per_batch=tpb),
        grid=(n // tm, nc),
        in_specs=[pl.BlockSpec((tm, D_MODEL), row),
                  pl.BlockSpec((FFN_HALO, D_MODEL), lambda i, c: (jnp.maximum(i * halo_blocks - 1, 0), 0)),
                  pl.BlockSpec((tm, D_MODEL), row),
                  pl.BlockSpec((D_MODEL, fc), gate), pl.BlockSpec((D_MODEL, fc), val),
                  pl.BlockSpec((CONV_W, fc), gate), pl.BlockSpec((CONV_W, fc), val),
                  pl.BlockSpec((1, fc), gate), pl.BlockSpec((1, fc), val),
                  pl.BlockSpec((fc, D_MODEL), lambda i, c: (c, 0)),
                  pl.BlockSpec((1, D_MODEL), fixed), pl.BlockSpec((1, D_MODEL), fixed)],
        out_specs=[pl.BlockSpec((tm, D_MODEL), row), pl.BlockSpec((tm, D_MODEL), row)],
        out_shape=[jax.ShapeDtypeStruct((n, D_MODEL), F32), jax.ShapeDtypeStruct((n, D_MODEL), BF16)],
        scratch_shapes=[pltpu.VMEM((tm + FFN_HALO, D_MODEL), BF16),
                        pltpu.VMEM((tm + FFN_HALO, fc), F32), pltpu.VMEM((tm + FFN_HALO, fc), F32),
                        pltpu.VMEM((tm, D_MODEL), F32)],
        compiler_params=_params("parallel", "arbitrary"),
        name="ffn",
    )(hb, hb, hf, w_in, w_in, conv_w, conv_w, conv_b, conv_b, w_out, g, b)


def _rope_tables(lp, dim, theta, group, offset):
    pos = (jnp.arange(lp) - LEAD).astype(F32)
    inv = theta ** (-jnp.arange(0, dim, 2, dtype=F32) / dim)
    ang = pos[:, None] * inv[None, :]
    cos, sin = jnp.cos(ang), jnp.sin(ang)
    ones = lambda w: jnp.ones((lp, w), F32)
    zeros = lambda w: jnp.zeros((lp, w), F32)
    rest = group - offset - dim
    cos_g = jnp.concatenate([ones(offset), cos, cos, ones(rest)], axis=1)
    sin_g = jnp.concatenate([zeros(offset), -sin, sin, zeros(rest)], axis=1)
    reps = LANES // group
    return jnp.tile(cos_g, (1, reps)), jnp.tile(sin_g, (1, reps))


def _swap_halves(w, dim):
    return jnp.concatenate([w[..., dim // 2:dim], w[..., :dim // 2]], axis=-1)


def _head_blocks(main, extra):
    src = main if main is not None else extra
    rows, heads = src.shape[0], src.shape[1]
    m = main if main is not None else jnp.zeros((rows, heads, HEAD_DIM), F32)
    e = extra if extra is not None else jnp.zeros((rows, heads, 0), F32)
    pad = jnp.zeros((rows, heads, LANES - HEAD_DIM - e.shape[2]), F32)
    return jnp.concatenate([m, e, pad], axis=2).reshape(rows, heads * LANES)


def _mla_weights(w_a, w_uq, w_ukv):
    d = w_a.shape[0]
    w_kr = w_a[:, MLA_Q_LORA + MLA_KV_LORA:][:, None, :]
    w_a_cat = jnp.concatenate([w_a[:, :MLA_Q_LORA + MLA_KV_LORA], _head_blocks(None, w_kr),
                               _head_blocks(None, _swap_halves(w_kr, MLA_ROPE))], axis=1)
    scale = (MLA_NOPE + MLA_ROPE) ** -0.5 * LOG2E
    wq = (w_uq * scale).reshape(MLA_Q_LORA, HEADS, MLA_NOPE + MLA_ROPE)
    w_main = _head_blocks(wq[..., :MLA_NOPE], wq[..., MLA_NOPE:])
    w_swap = _head_blocks(None, _swap_halves(wq[..., MLA_NOPE:], MLA_ROPE))
    wkv = w_ukv.reshape(MLA_KV_LORA, HEADS, MLA_NOPE + HEAD_DIM)
    w_kn = _head_blocks(wkv[..., :MLA_NOPE], None)
    w_v = wkv[..., MLA_NOPE:].reshape(MLA_KV_LORA, HEADS * HEAD_DIM)
    return tuple(w.astype(BF16) for w in (w_a_cat, w_main, w_swap, w_kn, w_v))


def _swa_weights(w_in):
    qd = HEADS * HEAD_DIM
    kd = SWA_KV_HEADS * HEAD_DIM
    q = w_in[:, :qd] * (HEAD_DIM ** -0.5)
    dup = lambda w: jnp.concatenate([w[:, :HEAD_DIM], w[:, :HEAD_DIM], w[:, HEAD_DIM:], w[:, HEAD_DIM:]], axis=1)
    return jnp.concatenate([q, dup(w_in[:, qd:qd + kd]), dup(w_in[:, qd + kd:])], axis=1).astype(BF16)


def _fox_weights(w_in, b_f):
    hd = HEADS * HEAD_DIM
    d = w_in.shape[0]
    w_q = _head_blocks((w_in[:, :hd] * (HEAD_DIM ** -0.5 * LOG2E)).reshape(d, HEADS, HEAD_DIM), None).astype(BF16)
    w_k = _head_blocks(w_in[:, hd:2 * hd].reshape(d, HEADS, HEAD_DIM), None).astype(BF16)
    w_v = w_in[:, 2 * hd:3 * hd].astype(BF16)
    w_fg = jnp.concatenate([w_in[:, 3 * hd:], jnp.zeros((d, LANES - HEADS), F32)], axis=1).astype(BF16)
    b_fg = jnp.concatenate([b_f, jnp.zeros((LANES - HEADS,), F32)])[None, :]
    return w_q, w_k, w_v, w_fg, b_fg


def kernel(x, meta_tokens, ln1_g, ln1_b, ln2_g, ln2_b, fox_w_in, fox_b_f, fox_w_o, swa_w_in, swa_sinks, swa_w_o,
           mla_w_a, mla_g_q, mla_g_kv, mla_w_uq, mla_w_ukv, mla_w_o, ffn_w_in, ffn_conv_w, ffn_conv_b, ffn_w_out):
    batch, seq, d = x.shape
    assert d == D_MODEL and seq % 256 == 0
    lp = seq + FIRST_REAL
    n = batch * lp

    tm = _tile(lp, 768, 256)
    tq = _tile(lp, 768, 256)
    tk = 256
    tn = 512
    nk = lp // tk

    h0 = jnp.concatenate([jnp.zeros((batch, LEAD, d), x.dtype),
                          jnp.broadcast_to(meta_tokens.astype(x.dtype)[None], (batch, N_META, d)), x], axis=1)
    hf = h0.reshape(n, d)
    hb = hf.astype(BF16)

    cos_p, sin_p = _rope_tables(lp, ROPE_DIM, ROPE_THETA, HEAD_DIM, 0)
    cos_m, sin_m = _rope_tables(lp, MLA_ROPE, MLA_ROPE_THETA, LANES, MLA_NOPE)
    b3 = lambda a: a.reshape(batch, lp, -1)

    for i in range(DEPTH):
        kind, j = i % 3, i // 3
        if kind == 0:
            w_q, w_k, w_v, w_fg, b_fg = _fox_weights(fox_w_in[j], fox_b_f[j])
            aq, ak = _fox_gate(hb, w_fg, b_fg, lp, tm)
            qh = _matmul_add(hb, w_q, aq, tm, tn)
            kh = _matmul_add(hb, w_k, ak, tm, tn)
            vt = _matmul_t(hb, w_v, tm, tn, tk).reshape(batch, nk, HEADS * HEAD_DIM, tk)
            o = _flash_attention(b3(qh), b3(kh), vt, batch, lp, tq, tk)
            w_o = fox_w_o[j]
        elif kind == 1:
            qkv = _swa_proj(hb, _swa_weights(swa_w_in[j]), cos_p, sin_p, lp, tm, 2 * LANES, 5)
            o = _swa_attention(b3(qkv), swa_sinks[j].astype(F32), batch, lp)
            w_o = swa_w_o[j]
        else:
            w_a_cat, w_main, w_swap, w_kn, w_v = _mla_weights(mla_w_a[j], mla_w_uq[j], mla_w_ukv[j])
            cq, ckv, kr = _mla_a(hb, w_a_cat, mla_g_q[j][None, :], mla_g_kv[j][None, :], cos_m, sin_m, lp, tm)
            qh = _mla_q(cq, w_main, w_swap, cos_m, sin_m, lp, tm, tn)
            kh = _matmul_add(ckv, w_kn, kr, tm, tn)
            vt = _matmul_t(ckv, w_v, tm, tn, tk).reshape(batch, nk, HEADS * HEAD_DIM, tk)
            o = _flash_attention(b3(qh), b3(kh), vt, batch, lp, tq, tk)
            w_o = mla_w_o[j]
        hf, hb = _oproj_ln(o.reshape(n, d), w_o.astype(BF16), hf, ln1_g[i][None, :], ln1_b[i][None, :], tm)
        hf, hb = _ffn(hb, hf, ffn_w_in[i].astype(BF16), ffn_conv_w[i], ffn_conv_b[i][None, :],
                      ffn_w_out[i].astype(BF16), ln2_g[i][None, :], ln2_b[i][None, :], lp, tm, 256)
    return hf.reshape(batch, lp, d)[:, FIRST_REAL:]
```

```python
import functools
import math

import numpy as np
import jax
import jax.numpy as jnp
from jax import lax
from jax.experimental import pallas as pl
from jax.experimental.pallas import tpu as pltpu

F32 = jnp.float32
BF16 = jnp.bfloat16

D_MODEL = 1024
DEPTH = 4
N_META = 16
LEAD = 240
FIRST_REAL = LEAD + N_META
NEG = -1e30
DEEPNORM_ALPHA = (2.0 * DEPTH) ** 0.25
LN_EPS = 1e-5
RMS_EPS = 1e-6
HEADS = 16
HEAD_DIM = 64
PAIRS = HEADS // 2
SWA_KV_HEADS = 2
WINDOW = 128
ROPE_THETA = 500000.0
ROPE_DIM = 16
MLA_Q_LORA = 384
MLA_KV_LORA = 256
MLA_NOPE = 64
MLA_ROPE = 32
MLA_ROPE_THETA = 10000.0
D_FF = 2816
CONV_W = 3
LOG2E = math.log2(math.e)

LANES = 128
BF16_SUBLANES = 16
VMEM_LIMIT = 56 * 1024 * 1024

GATE_SLOTS = 3
FOX_DEAD_LANE = HEAD_DIM + 2 * GATE_SLOTS
MLA_DEAD_LANE = MLA_NOPE + MLA_ROPE
M_INIT = -3e38


def _params(*sem):
    return pltpu.CompilerParams(dimension_semantics=sem, vmem_limit_bytes=VMEM_LIMIT)


def _tile(n, pref, mult):
    best = mult
    t = mult
    while t <= min(n, pref):
        if n % t == 0:
            best = t
        t += mult
    assert n % best == 0
    return best


def _dot(a, b):
    return jnp.dot(a, b, preferred_element_type=F32)


def _dot_nt(a, b):
    return lax.dot_general(a, b, (((1,), (1,)), ((), ())), preferred_element_type=F32)


def _layer_norm(x, g, b):
    mu = jnp.mean(x, axis=-1, keepdims=True)
    xc = x - mu
    var = jnp.mean(xc * xc, axis=-1, keepdims=True)
    return xc * lax.rsqrt(var + LN_EPS) * g + b


def _one_hot_lanes(width, period, lane):
    idx = lax.broadcasted_iota(jnp.int32, (1, width), 1)
    return jnp.where((idx & (period - 1)) == lane, 1.0, 0.0)


def _dead_rows(tile_in_batch, tm):
    pos = tile_in_batch * tm + lax.broadcasted_iota(jnp.int32, (tm, 1), 0)
    return jnp.where(pos < LEAD, NEG, 0.0)


def _mm_kernel(x_ref, w_ref, o_ref):
    o_ref[...] = _dot(x_ref[...], w_ref[...]).astype(o_ref.dtype)


def _matmul(x, w, out_dtype, tm, tn):
    n, k = x.shape
    m = w.shape[1]
    return pl.pallas_call(
        _mm_kernel,
        grid=(n // tm, m // tn),
        in_specs=[pl.BlockSpec((tm, k), lambda i, j: (i, 0)),
                  pl.BlockSpec((k, tn), lambda i, j: (0, j))],
        out_specs=pl.BlockSpec((tm, tn), lambda i, j: (i, j)),
        out_shape=jax.ShapeDtypeStruct((n, m), out_dtype),
        compiler_params=_params("parallel", "parallel"),
        name="matmul",
    )(x, w)


def _head_proj_kernel(x_ref, w_ref, e_ref, o_ref):
    y = _dot(x_ref[...], w_ref[...])
    lo = lax.broadcasted_iota(jnp.int32, (1, LANES), 1) < HEAD_DIM
    shared = e_ref.shape[1] == LANES
    for pair in range(y.shape[1] // LANES):
        y_pair = y[:, pair * LANES:(pair + 1) * LANES]
        for a, feats in enumerate((y_pair, pltpu.roll(y_pair, HEAD_DIM, 1))):
            h = 2 * pair + a
            extra = e_ref[...] if shared else e_ref[:, h * LANES:(h + 1) * LANES]
            o_ref[:, h * LANES:(h + 1) * LANES] = jnp.where(lo, feats, extra.astype(F32)).astype(o_ref.dtype)


def _head_proj(x, w, e, tm, tn):
    n, k = x.shape
    m = w.shape[1]
    if e.shape[1] == LANES:
        e_spec = pl.BlockSpec((tm, LANES), lambda i, j: (i, 0))
    else:
        assert e.shape[1] == 2 * m
        e_spec = pl.BlockSpec((tm, 2 * tn), lambda i, j: (i, j))
    return pl.pallas_call(
        _head_proj_kernel,
        grid=(n // tm, m // tn),
        in_specs=[pl.BlockSpec((tm, k), lambda i, j: (i, 0)),
                  pl.BlockSpec((k, tn), lambda i, j: (0, j)),
                  e_spec],
        out_specs=pl.BlockSpec((tm, 2 * tn), lambda i, j: (i, j)),
        out_shape=jax.ShapeDtypeStruct((n, 2 * m), BF16),
        compiler_params=_params("parallel", "parallel"),
        name="head_proj",
    )(x, w, e)


def _mm_t_kernel(x_ref, w_ref, b_ref, o_ref, *, tk):
    y = _dot(x_ref[...], w_ref[...]) + b_ref[...]
    for c in range(o_ref.shape[0]):
        o_ref[c] = y[c * tk:(c + 1) * tk, :].T.astype(o_ref.dtype)


def _matmul_t(x, w, bias, tm, tn, tk):
    n, k = x.shape
    m = w.shape[1]
    r = tm // tk
    return pl.pallas_call(
        functools.partial(_mm_t_kernel, tk=tk),
        grid=(n // tm, m // tn),
        in_specs=[pl.BlockSpec((tm, k), lambda i, j: (i, 0)),
                  pl.BlockSpec((k, tn), lambda i, j: (0, j)),
                  pl.BlockSpec((1, tn), lambda i, j: (0, j))],
        out_specs=pl.BlockSpec((r, tn, tk), lambda i, j: (i, j, 0)),
        out_shape=jax.ShapeDtypeStruct((n // tk, m, tk), BF16),
        compiler_params=_params("parallel", "parallel"),
        name="matmul_t",
    )(x, w, bias)


V_ROWS = HEAD_DIM + BF16_SUBLANES
V_COLS_TILE = 8 * V_ROWS


def _value_weights(w_v):
    rows = w_v.shape[0]
    w = jnp.concatenate([w_v.reshape(rows, HEADS, HEAD_DIM),
                         jnp.zeros((rows, HEADS, V_ROWS - HEAD_DIM), w_v.dtype)], axis=2)
    bias = np.zeros((HEADS, V_ROWS), np.float32)
    bias[:, HEAD_DIM] = 1.0
    return w.reshape(rows, HEADS * V_ROWS).astype(BF16), jnp.asarray(bias.reshape(1, HEADS * V_ROWS))


def _gate_placement():
    wide = HEADS * LANES
    sq = np.zeros((GATE_SLOTS * LANES, wide), np.float32)
    sk = np.zeros_like(sq)
    oq = np.zeros((1, wide), np.float32)
    ok = np.zeros_like(oq)
    for h in range(HEADS):
        base = h * LANES + HEAD_DIM
        for part in range(GATE_SLOTS):
            sq[part * LANES + h, base + part] = 1.0
            sk[part * LANES + h, base + GATE_SLOTS + part] = -1.0
            oq[0, base + GATE_SLOTS + part] = 1.0
            ok[0, base + part] = 1.0
        oq[0, h * LANES + FOX_DEAD_LANE] = 1.0
    return sq, sk, oq, ok


def _split3(x):
    hi = x.astype(BF16)
    r1 = x - hi.astype(F32)
    mid = r1.astype(BF16)
    lo = (r1 - mid.astype(F32)).astype(BF16)
    return hi, mid, lo


def _fox_gate_kernel(x_ref, w_ref, b_ref, sq_ref, sk_ref, oq_ref, ok_ref, aq_ref, ak_ref, carry_ref,
                     *, tm, tiles_per_batch):
    i = pl.program_id(0)

    @pl.when(i % tiles_per_batch == 0)
    def _():
        carry_ref[...] = jnp.zeros_like(carry_ref)

    fg = _dot(x_ref[...], w_ref[...]) + b_ref[...]
    logf = jnp.minimum(fg, 0.0) - jnp.log(1.0 + jnp.exp(-jnp.abs(fg)))
    row = lax.broadcasted_iota(jnp.int32, (tm, tm), 0)
    col = lax.broadcasted_iota(jnp.int32, (tm, tm), 1)
    tri = jnp.where(col <= row, 1.0, 0.0).astype(BF16)
    hi, mid, lo = _split3(logf)
    cs = _dot(tri, hi) + _dot(tri, mid) + _dot(tri, lo) + carry_ref[...]
    carry_ref[...] = cs[tm - 1:tm, :]
    parts = jnp.concatenate(_split3(cs * LOG2E), axis=1)
    wide = aq_ref.shape[1]
    dead = _dead_rows(i % tiles_per_batch, tm) * _one_hot_lanes(wide, LANES, FOX_DEAD_LANE)
    aq_ref[...] = (_dot(parts, sq_ref[...]) + oq_ref[...]).astype(aq_ref.dtype)
    ak_ref[...] = (_dot(parts, sk_ref[...]) + ok_ref[...] + dead).astype(ak_ref.dtype)


def _fox_gate(hb, w_fg, b_fg, lp, tm):
    n = hb.shape[0]
    tpb = lp // tm
    sq, sk, oq, ok = _gate_placement()
    fixed = lambda i: (0, 0)
    wide = HEADS * LANES
    return pl.pallas_call(
        functools.partial(_fox_gate_kernel, tm=tm, tiles_per_batch=tpb),
        grid=(n // tm,),
        in_specs=[pl.BlockSpec((tm, D_MODEL), lambda i: (i, 0)),
                  pl.BlockSpec((D_MODEL, LANES), fixed),
                  pl.BlockSpec((1, LANES), fixed),
                  pl.BlockSpec((GATE_SLOTS * LANES, wide), fixed),
                  pl.BlockSpec((GATE_SLOTS * LANES, wide), fixed),
                  pl.BlockSpec((1, wide), fixed),
                  pl.BlockSpec((1, wide), fixed)],
        out_specs=[pl.BlockSpec((tm, wide), lambda i: (i, 0)),
                   pl.BlockSpec((tm, wide), lambda i: (i, 0))],
        out_shape=[jax.ShapeDtypeStruct((n, wide), BF16), jax.ShapeDtypeStruct((n, wide), BF16)],
        scratch_shapes=[pltpu.VMEM((1, LANES), F32)],
        compiler_params=_params("arbitrary"),
        name="fox_gate",
    )(hb, w_fg, b_fg, jnp.asarray(sq, BF16), jnp.asarray(sk, BF16), jnp.asarray(oq), jnp.asarray(ok))


def _flash_kernel(q_ref, k_ref, vt_ref, o_ref, sa_ref, sb_ref, xa_ref, xb_ref, m_ref, acc_ref, *, tq, tk):
    i = pl.program_id(2)
    r = tq // tk
    m_ref[...] = jnp.full_like(m_ref, M_INIT)
    acc_ref[...] = jnp.zeros_like(acc_ref)

    def diag_mask(s):
        keep = lax.broadcasted_iota(jnp.int32, s.shape, 0) <= lax.broadcasted_iota(jnp.int32, s.shape, 1)
        return jnp.where(keep, s, NEG)

    def scores(s_ref, x_ref, j, a, qs, diagonal):
        off = pl.multiple_of(j * tk, tk)
        s = _dot_nt(k_ref[0, pl.ds(off, tk), a * LANES:(a + 1) * LANES], q_ref[0, qs:, a * LANES:(a + 1) * LANES])
        if diagonal:
            s = diag_mask(s)
        s_ref[a, :, qs:] = s
        x_ref[a, :, qs:] = jnp.max(s, axis=0, keepdims=True)

    def consume(s_ref, x_ref, j, a, qs, mask_now):
        s = s_ref[a, :, qs:]
        if mask_now:
            s = diag_mask(s)
            smax = jnp.max(s, axis=0, keepdims=True)
        else:
            smax = x_ref[a, :, qs:]
        m_old = m_ref[a, :, qs:]
        m_new = jnp.maximum(m_old, smax)
        alpha = jnp.exp2(m_old - m_new)
        p = jnp.exp2(s - m_new)
        m_ref[a, :, qs:] = m_new
        pv = _dot(vt_ref[0, j, a * V_ROWS:(a + 1) * V_ROWS, :], p.astype(BF16))
        acc_ref[a, :, qs:] = alpha * acc_ref[a, :, qs:] + pv

    buf_a, buf_b = (sa_ref, xa_ref), (sb_ref, xb_ref)
    jdiag = i * r
    odd = jdiag & 1
    for a in range(2):
        scores(*buf_a, 0, a, 0, False)

    @pl.when(odd == 1)
    def _():
        for a in range(2):
            scores(*buf_b, 1, a, 0, False)
            consume(*buf_a, 0, a, 0, False)
        sa_ref[...] = sb_ref[...]
        xa_ref[...] = xb_ref[...]

    def pair_body(t, c):
        j = odd + 2 * t
        for a in range(2):
            scores(*buf_b, j + 1, a, 0, False)
            consume(*buf_a, j, a, 0, False)
        for a in range(2):
            scores(*buf_a, j + 2, a, 0, False)
            consume(*buf_b, j + 1, a, 0, False)
        return c

    lax.fori_loop(0, (jdiag - odd) // 2, pair_body, 0)

    bufs = (buf_a, buf_b)
    for d in range(r):
        qs = d * tk
        for a in range(2):
            if d + 1 < r:
                scores(*bufs[(d + 1) & 1], jdiag + d + 1, a, qs + tk, True)
            consume(*bufs[d & 1], jdiag + d, a, qs, d == 0)

    ot = jnp.concatenate([acc_ref[a, :HEAD_DIM, :] / acc_ref[a, HEAD_DIM:HEAD_DIM + 1, :] for a in range(2)],
                         axis=0)
    o_ref[0] = ot.T.astype(o_ref.dtype)


def _flash_attention(qh, kh, vt, batch, lp, tq, tk):
    nk = lp // tk
    return pl.pallas_call(
        functools.partial(_flash_kernel, tq=tq, tk=tk),
        grid=(batch, PAIRS, lp // tq),
        in_specs=[pl.BlockSpec((1, tq, 2 * LANES), lambda b, p, i: (b, i, p)),
                  pl.BlockSpec((1, lp, 2 * LANES), lambda b, p, i: (b, 0, p)),
                  pl.BlockSpec((1, nk, 2 * V_ROWS, tk), lambda b, p, i: (b, 0, p, 0))],
        out_specs=pl.BlockSpec((1, tq, 2 * HEAD_DIM), lambda b, p, i: (b, i, p)),
        out_shape=jax.ShapeDtypeStruct((batch, lp, HEADS * HEAD_DIM), BF16),
        scratch_shapes=[pltpu.VMEM((2, tk, tq), F32), pltpu.VMEM((2, tk, tq), F32),
                        pltpu.VMEM((2, 1, tq), F32), pltpu.VMEM((2, 1, tq), F32),
                        pltpu.VMEM((2, 1, tq), F32), pltpu.VMEM((2, V_ROWS, tq), F32)],
        compiler_params=_params("parallel", "parallel", "parallel"),
        name="flash_attention",
    )(qh, kh, vt)


SWA_TQ = 128


def _swa_attn_kernel(sink_ref, q_ref, km_ref, kp_ref, kc_ref, vm_ref, vp_ref, vc_ref, o_ref):
    i = pl.program_id(1)
    t = SWA_TQ
    lo = lax.broadcasted_iota(jnp.int32, (1, LANES), 1) < HEAD_DIM
    row = lax.broadcasted_iota(jnp.int32, (t, 3 * t), 0)
    col = lax.broadcasted_iota(jnp.int32, (t, 3 * t), 1)
    qpos = i * t + row
    kpos = jnp.where(col < t, t + col, (i - 2) * t + col)
    d = qpos - kpos
    valid = (d >= 0) & (((col < t) & (kpos >= LEAD)) |
                        ((col >= t) & (d < WINDOW) & (kpos >= FIRST_REAL)))
    for g in range(SWA_KV_HEADS):
        sl = slice(g * LANES, (g + 1) * LANES)
        kcat = jnp.concatenate([km_ref[0, :, sl], kp_ref[0, :, sl], kc_ref[0, :, sl]], axis=0)
        vf = jnp.concatenate([vm_ref[0, :, sl], vp_ref[0, :, sl], vc_ref[0, :, sl]], axis=0).astype(F32)
        v_lo = jnp.where(lo, vf, 0.0).astype(BF16)
        v_hi = jnp.where(lo, 0.0, vf).astype(BF16)
        for pp in range(PAIRS // SWA_KV_HEADS):
            p = g * (PAIRS // SWA_KV_HEADS) + pp
            qf = q_ref[0, :, p * LANES:(p + 1) * LANES].astype(F32)
            q_pair = (jnp.where(lo, qf, 0.0).astype(BF16), jnp.where(lo, 0.0, qf).astype(BF16))
            ps, inv = [], []
            for a in range(2):
                sink = sink_ref[2 * p + a]
                s = jnp.where(valid, _dot_nt(q_pair[a], kcat), NEG)
                m = jnp.maximum(jnp.max(s, axis=1, keepdims=True), sink)
                e = jnp.exp(s - m)
                den = jnp.sum(e, axis=1, keepdims=True) + jnp.exp(sink - m)
                ps.append(e.astype(BF16))
                inv.append(1.0 / den)
            o = (_dot(ps[0], v_lo) + _dot(ps[1], v_hi)) * jnp.where(lo, inv[0], inv[1])
            o_ref[0, :, p * LANES:(p + 1) * LANES] = o.astype(o_ref.dtype)


def _swa_attention(qkv, sinks, batch, lp):
    t = SWA_TQ
    kblk, vblk = 4, 5
    kv_spec = lambda col, row_of: pl.BlockSpec((1, t, 2 * LANES), lambda b, i: (b, row_of(i), col))
    meta = lambda i: 1
    prev = lambda i: jnp.maximum(i - 1, 0)
    cur = lambda i: i
    return pl.pallas_call(
        _swa_attn_kernel,
        grid=(batch, lp // t),
        in_specs=[pl.BlockSpec(memory_space=pltpu.SMEM),
                  pl.BlockSpec((1, t, HEADS * HEAD_DIM), lambda b, i: (b, i, 0)),
                  kv_spec(kblk, meta), kv_spec(kblk, prev), kv_spec(kblk, cur),
                  kv_spec(vblk, meta), kv_spec(vblk, prev), kv_spec(vblk, cur)],
        out_specs=pl.BlockSpec((1, t, HEADS * HEAD_DIM), lambda b, i: (b, i, 0)),
        out_shape=jax.ShapeDtypeStruct((batch, lp, HEADS * HEAD_DIM), BF16),
        compiler_params=_params("parallel", "parallel"),
        name="swa_attention",
    )(sinks, qkv, qkv, qkv, qkv, qkv, qkv, qkv)


def _swa_proj_kernel(x_ref, w_ref, cos_ref, sin_ref, o_ref, *, tn, n_rope_blocks):
    j = pl.program_id(1)
    y = _dot(x_ref[...], w_ref[...])

    @pl.when(j < n_rope_blocks)
    def _():
        reps = tn // LANES
        cos = jnp.concatenate([cos_ref[...]] * reps, axis=1)
        sin = jnp.concatenate([sin_ref[...]] * reps, axis=1)
        lane = lax.broadcasted_iota(jnp.int32, (1, tn), 1)
        half = ROPE_DIM // 2
        partner = jnp.where((lane & (HEAD_DIM - 1)) < half,
                            pltpu.roll(y, tn - half, 1),
                            pltpu.roll(y, half, 1))
        o_ref[...] = (y * cos + partner * sin).astype(o_ref.dtype)

    @pl.when(j >= n_rope_blocks)
    def _():
        o_ref[...] = y.astype(o_ref.dtype)


def _swa_proj(hb, w, cos, sin, lp, tm, tn, n_rope_blocks):
    n = hb.shape[0]
    m = w.shape[1]
    tpb = lp // tm
    return pl.pallas_call(
        functools.partial(_swa_proj_kernel, tn=tn, n_rope_blocks=n_rope_blocks),
        grid=(n // tm, m // tn),
        in_specs=[pl.BlockSpec((tm, D_MODEL), lambda i, j: (i, 0)),
                  pl.BlockSpec((D_MODEL, tn), lambda i, j: (0, j)),
                  pl.BlockSpec((tm, LANES), lambda i, j: (i % tpb, 0)),
                  pl.BlockSpec((tm, LANES), lambda i, j: (i % tpb, 0))],
        out_specs=pl.BlockSpec((tm, tn), lambda i, j: (i, j)),
        out_shape=jax.ShapeDtypeStruct((n, m), BF16),
        compiler_params=_params("parallel", "parallel"),
        name="swa_proj",
    )(hb, w, cos, sin)


MLA_A_COLS = MLA_Q_LORA + MLA_KV_LORA + 2 * LANES


def _mla_a_kernel(x_ref, w_ref, gq_ref, gkv_ref, cos_ref, sin_ref, cq_ref, ckv_ref, kr_ref, *, tm, tiles_per_batch):
    y = _dot(x_ref[...], w_ref[...])
    cq = y[:, :MLA_Q_LORA]
    ckv = y[:, MLA_Q_LORA:MLA_Q_LORA + MLA_KV_LORA]
    kr = y[:, MLA_Q_LORA + MLA_KV_LORA:MLA_Q_LORA + MLA_KV_LORA + LANES]
    krs = y[:, MLA_Q_LORA + MLA_KV_LORA + LANES:]
    rms = lambda z, g: z * lax.rsqrt(jnp.mean(z * z, axis=-1, keepdims=True) + RMS_EPS) * g
    cq_ref[...] = rms(cq, gq_ref[...]).astype(cq_ref.dtype)
    ckv_ref[...] = rms(ckv, gkv_ref[...]).astype(ckv_ref.dtype)
    dead = _dead_rows(pl.program_id(0) % tiles_per_batch, tm) * _one_hot_lanes(LANES, LANES, MLA_DEAD_LANE)
    kr_ref[...] = (kr * cos_ref[...] + krs * sin_ref[...] + dead).astype(kr_ref.dtype)


def _mla_a(hb, w, gq, gkv, cos, sin, lp, tm):
    n = hb.shape[0]
    tpb = lp // tm
    return pl.pallas_call(
        functools.partial(_mla_a_kernel, tm=tm, tiles_per_batch=tpb),
        grid=(n // tm,),
        in_specs=[pl.BlockSpec((tm, D_MODEL), lambda i: (i, 0)),
                  pl.BlockSpec((D_MODEL, MLA_A_COLS), lambda i: (0, 0)),
                  pl.BlockSpec((1, MLA_Q_LORA), lambda i: (0, 0)),
                  pl.BlockSpec((1, MLA_KV_LORA), lambda i: (0, 0)),
                  pl.BlockSpec((tm, LANES), lambda i: (i % tpb, 0)),
                  pl.BlockSpec((tm, LANES), lambda i: (i % tpb, 0))],
        out_specs=[pl.BlockSpec((tm, MLA_Q_LORA), lambda i: (i, 0)),
                   pl.BlockSpec((tm, MLA_KV_LORA), lambda i: (i, 0)),
                   pl.BlockSpec((tm, LANES), lambda i: (i, 0))],
        out_shape=[jax.ShapeDtypeStruct((n, MLA_Q_LORA), BF16),
                   jax.ShapeDtypeStruct((n, MLA_KV_LORA), BF16),
                   jax.ShapeDtypeStruct((n, LANES), BF16)],
        compiler_params=_params("parallel"),
        name="mla_a",
    )(hb, w, gq, gkv, cos, sin)


def _mla_q_kernel(x_ref, w_ref, ws_ref, cos_ref, sin_ref, o_ref):
    x = x_ref[...]
    reps = o_ref.shape[1] // LANES
    cos = jnp.concatenate([cos_ref[...]] * reps, axis=1)
    sin = jnp.concatenate([sin_ref[...]] * reps, axis=1)
    y = _dot(x, w_ref[...]) * cos + _dot(x, ws_ref[...]) * sin
    o_ref[...] = (y + _one_hot_lanes(o_ref.shape[1], LANES, MLA_DEAD_LANE)).astype(o_ref.dtype)


def _mla_q(cq, w, ws, cos, sin, lp, tm, tn):
    n = cq.shape[0]
    m = w.shape[1]
    tpb = lp // tm
    return pl.pallas_call(
        _mla_q_kernel,
        grid=(n // tm, m // tn),
        in_specs=[pl.BlockSpec((tm, MLA_Q_LORA), lambda i, j: (i, 0)),
                  pl.BlockSpec((MLA_Q_LORA, tn), lambda i, j: (0, j)),
                  pl.BlockSpec((MLA_Q_LORA, tn), lambda i, j: (0, j)),
                  pl.BlockSpec((tm, LANES), lambda i, j: (i % tpb, 0)),
                  pl.BlockSpec((tm, LANES), lambda i, j: (i % tpb, 0))],
        out_specs=pl.BlockSpec((tm, tn), lambda i, j: (i, j)),
        out_shape=jax.ShapeDtypeStruct((n, m), BF16),
        compiler_params=_params("parallel", "parallel"),
        name="mla_q",
    )(cq, w, ws, cos, sin)


def _oproj_ln_kernel(o_ref, w_ref, h_ref, g_ref, b_ref, hf_ref, hb_ref):
    x = DEEPNORM_ALPHA * h_ref[...] + _dot(o_ref[...], w_ref[...])
    y = _layer_norm(x, g_ref[...], b_ref[...])
    hf_ref[...] = y
    hb_ref[...] = y.astype(hb_ref.dtype)


def _oproj_ln(o, w, h, g, b, tm):
    n = o.shape[0]
    row = lambda i: (i, 0)
    fixed = lambda i: (0, 0)
    return pl.pallas_call(
        _oproj_ln_kernel,
        grid=(n // tm,),
        in_specs=[pl.BlockSpec((tm, D_MODEL), row), pl.BlockSpec((D_MODEL, D_MODEL), fixed),
                  pl.BlockSpec((tm, D_MODEL), row), pl.BlockSpec((1, D_MODEL), fixed),
                  pl.BlockSpec((1, D_MODEL), fixed)],
        out_specs=[pl.BlockSpec((tm, D_MODEL), row), pl.BlockSpec((tm, D_MODEL), row)],
        out_shape=[jax.ShapeDtypeStruct((n, D_MODEL), F32), jax.ShapeDtypeStruct((n, D_MODEL), BF16)],
        compiler_params=_params("parallel"),
        name="oproj_ln",
    )(o, w, h, g, b)


FFN_HALO = BF16_SUBLANES


FFN_CHUNK = 256
FFN_NC = D_FF // FFN_CHUNK
assert FFN_NC * FFN_CHUNK == D_FF and FFN_NC % 2 == 1


def _ffn_kernel(x_ref, halo_ref, h_ref, win_ref, cw_ref, cb_ref, wo_ref, g_ref, b_ref, hf_ref, hb_ref,
                xext_ref, ua_ref, ub_ref, act_ref, *, tm, tiles_per_batch):
    i = pl.program_id(0)
    nc = FFN_NC
    pos = (i % tiles_per_batch) * tm - FFN_HALO + lax.broadcasted_iota(jnp.int32, (tm + FFN_HALO, 1), 0)
    xe = jnp.concatenate([halo_ref[...], x_ref[...]], axis=0).astype(F32)
    xext_ref[...] = jnp.where(pos >= LEAD, xe, 0.0).astype(BF16)

    def up(u_ref, c):
        u_ref[0] = _dot(xext_ref[...], win_ref[c])
        u_ref[1] = _dot(xext_ref[...], win_ref[nc + c])

    def glu(u_ref, c):
        def conv(part, idx):
            y = cb_ref[idx]
            for tap in range(CONV_W):
                y = y + u_ref[part, pl.ds(FFN_HALO - (CONV_W - 1) + tap, tm), :] * cw_ref[idx, tap:tap + 1, :]
            return y

        yg = conv(0, c)
        yv = conv(1, nc + c)
        act_ref[c] = ((yg / (1.0 + jnp.exp(-yg))) * yv).astype(BF16)

    up(ua_ref, 0)

    def pair_body(t, carry):
        c = 2 * t
        up(ub_ref, c + 1)
        glu(ua_ref, c)
        up(ua_ref, c + 2)
        glu(ub_ref, c + 1)
        return carry

    lax.fori_loop(0, (nc - 1) // 2, pair_body, 0)
    glu(ua_ref, nc - 1)

    ffn = _dot(act_ref[0], wo_ref[0])
    for c in range(1, nc):
        ffn = ffn + _dot(act_ref[c], wo_ref[c])
    y = _layer_norm(DEEPNORM_ALPHA * h_ref[...] + ffn, g_ref[...], b_ref[...])
    hf_ref[...] = y
    hb_ref[...] = y.astype(hb_ref.dtype)


def _ffn(hb, hf, w_in, conv_w, conv_b, w_out, g, b, lp, tm):
    n = hb.shape[0]
    nc, fc = FFN_NC, FFN_CHUNK
    tpb = lp // tm
    halo_blocks = tm // FFN_HALO
    row = lambda i: (i, 0)
    fixed2 = lambda i: (0, 0)
    fixed3 = lambda i: (0, 0, 0)
    resident = dict(pipeline_mode=pl.Buffered(1))
    return pl.pallas_call(
        functools.partial(_ffn_kernel, tm=tm, tiles_per_batch=tpb),
        grid=(n // tm,),
        in_specs=[pl.BlockSpec((tm, D_MODEL), row),
                  pl.BlockSpec((FFN_HALO, D_MODEL), lambda i: (jnp.maximum(i * halo_blocks - 1, 0), 0)),
                  pl.BlockSpec((tm, D_MODEL), row),
                  pl.BlockSpec((2 * nc, D_MODEL, fc), fixed3, **resident),
                  pl.BlockSpec((2 * nc, CONV_W, fc), fixed3, **resident),
                  pl.BlockSpec((2 * nc, 1, fc), fixed3, **resident),
                  pl.BlockSpec((nc, fc, D_MODEL), fixed3, **resident),
                  pl.BlockSpec((1, D_MODEL), fixed2), pl.BlockSpec((1, D_MODEL), fixed2)],
        out_specs=[pl.BlockSpec((tm, D_MODEL), row), pl.BlockSpec((tm, D_MODEL), row)],
        out_shape=[jax.ShapeDtypeStruct((n, D_MODEL), F32), jax.ShapeDtypeStruct((n, D_MODEL), BF16)],
        scratch_shapes=[pltpu.VMEM((tm + FFN_HALO, D_MODEL), BF16),
                        pltpu.VMEM((2, tm + FFN_HALO, fc), F32), pltpu.VMEM((2, tm + FFN_HALO, fc), F32),
                        pltpu.VMEM((nc, tm, fc), BF16)],
        compiler_params=_params("parallel"),
        name="ffn",
    )(hb, hb, hf, w_in, conv_w, conv_b, w_out, g, b)


def _ffn_weights(w_in, conv_w, conv_b, w_out):
    d = w_in.shape[0]
    chunks = 2 * FFN_NC
    return (w_in.reshape(d, chunks, FFN_CHUNK).transpose(1, 0, 2).astype(BF16),
            conv_w.reshape(CONV_W, chunks, FFN_CHUNK).transpose(1, 0, 2),
            conv_b.reshape(chunks, 1, FFN_CHUNK),
            w_out.reshape(FFN_NC, FFN_CHUNK, d).astype(BF16))


def _rope_tables(lp, dim, theta, group, offset):
    pos = (jnp.arange(lp) - LEAD).astype(F32)
    inv = theta ** (-jnp.arange(0, dim, 2, dtype=F32) / dim)
    ang = pos[:, None] * inv[None, :]
    cos, sin = jnp.cos(ang), jnp.sin(ang)
    ones = lambda w: jnp.ones((lp, w), F32)
    zeros = lambda w: jnp.zeros((lp, w), F32)
    rest = group - offset - dim
    cos_g = jnp.concatenate([ones(offset), cos, cos, ones(rest)], axis=1)
    sin_g = jnp.concatenate([zeros(offset), -sin, sin, zeros(rest)], axis=1)
    reps = LANES // group
    return jnp.tile(cos_g, (1, reps)), jnp.tile(sin_g, (1, reps))


def _swap_halves(w, dim):
    return jnp.concatenate([w[..., dim // 2:dim], w[..., :dim // 2]], axis=-1)


def _head_blocks(main, extra):
    src = main if main is not None else extra
    rows, heads = src.shape[0], src.shape[1]
    m = main if main is not None else jnp.zeros((rows, heads, HEAD_DIM), F32)
    e = extra if extra is not None else jnp.zeros((rows, heads, 0), F32)
    pad = jnp.zeros((rows, heads, LANES - HEAD_DIM - e.shape[2]), F32)
    return jnp.concatenate([m, e, pad], axis=2).reshape(rows, heads * LANES)


def _mla_weights(w_a, w_uq, w_ukv):
    d = w_a.shape[0]
    w_kr = w_a[:, MLA_Q_LORA + MLA_KV_LORA:][:, None, :]
    w_a_cat = jnp.concatenate([w_a[:, :MLA_Q_LORA + MLA_KV_LORA], _head_blocks(None, w_kr),
                               _head_blocks(None, _swap_halves(w_kr, MLA_ROPE))], axis=1)
    scale = (MLA_NOPE + MLA_ROPE) ** -0.5 * LOG2E
    wq = (w_uq * scale).reshape(MLA_Q_LORA, HEADS, MLA_NOPE + MLA_ROPE)
    w_main = _head_blocks(wq[..., :MLA_NOPE], wq[..., MLA_NOPE:])
    w_swap = _head_blocks(None, _swap_halves(wq[..., MLA_NOPE:], MLA_ROPE))
    wkv = w_ukv.reshape(MLA_KV_LORA, HEADS, MLA_NOPE + HEAD_DIM)
    w_kn = wkv[..., :MLA_NOPE].reshape(MLA_KV_LORA, HEADS * MLA_NOPE)
    w_v = wkv[..., MLA_NOPE:].reshape(MLA_KV_LORA, HEADS * HEAD_DIM)
    return tuple(w.astype(BF16) for w in (w_a_cat, w_main, w_swap, w_kn)) + (w_v,)


def _swa_weights(w_in):
    qd = HEADS * HEAD_DIM
    kd = SWA_KV_HEADS * HEAD_DIM
    q = w_in[:, :qd] * (HEAD_DIM ** -0.5)
    dup = lambda w: jnp.concatenate([w[:, :HEAD_DIM], w[:, :HEAD_DIM], w[:, HEAD_DIM:], w[:, HEAD_DIM:]], axis=1)
    return jnp.concatenate([q, dup(w_in[:, qd:qd + kd]), dup(w_in[:, qd + kd:])], axis=1).astype(BF16)


def _fox_weights(w_in, b_f):
    hd = HEADS * HEAD_DIM
    d = w_in.shape[0]
    w_q = (w_in[:, :hd] * (HEAD_DIM ** -0.5 * LOG2E)).astype(BF16)
    w_k = w_in[:, hd:2 * hd].astype(BF16)
    w_v = w_in[:, 2 * hd:3 * hd]
    w_fg = jnp.concatenate([w_in[:, 3 * hd:], jnp.zeros((d, LANES - HEADS), F32)], axis=1).astype(BF16)
    b_fg = jnp.concatenate([b_f, jnp.zeros((LANES - HEADS,), F32)])[None, :]
    return w_q, w_k, w_v, w_fg, b_fg


def kernel(x, meta_tokens, ln1_g, ln1_b, ln2_g, ln2_b, fox_w_in, fox_b_f, fox_w_o, swa_w_in, swa_sinks, swa_w_o,
           mla_w_a, mla_g_q, mla_g_kv, mla_w_uq, mla_w_ukv, mla_w_o, ffn_w_in, ffn_conv_w, ffn_conv_b, ffn_w_out):
    batch, seq, d = x.shape
    assert d == D_MODEL and seq % 256 == 0
    lp = seq + FIRST_REAL
    n = batch * lp

    tm = _tile(lp, 768, 256)
    tq = _tile(lp, 768, 256)
    tk = 256
    tn = 512
    nk = lp // tk

    h0 = jnp.concatenate([jnp.zeros((batch, LEAD, d), x.dtype),
                          jnp.broadcast_to(meta_tokens.astype(x.dtype)[None], (batch, N_META, d)), x], axis=1)
    hf = h0.reshape(n, d)
    hb = hf.astype(BF16)

    cos_p, sin_p = _rope_tables(lp, ROPE_DIM, ROPE_THETA, HEAD_DIM, 0)
    cos_m, sin_m = _rope_tables(lp, MLA_ROPE, MLA_ROPE_THETA, LANES, MLA_NOPE)
    b3 = lambda a: a.reshape(batch, lp, -1)

    for i in range(DEPTH):
        kind, j = i % 3, i // 3
        if kind == 0:
            w_q, w_k, w_v, w_fg, b_fg = _fox_weights(fox_w_in[j], fox_b_f[j])
            aq, ak = _fox_gate(hb, w_fg, b_fg, lp, tm)
            qh = _head_proj(hb, w_q, aq, tm, tn)
            kh = _head_proj(hb, w_k, ak, tm, tn)
            vt = _matmul_t(hb, *_value_weights(w_v), tm, V_COLS_TILE, tk).reshape(batch, nk, HEADS * V_ROWS, tk)
            o = _flash_attention(b3(qh), b3(kh), vt, batch, lp, tq, tk)
            w_o = fox_w_o[j]
        elif kind == 1:
            qkv = _swa_proj(hb, _swa_weights(swa_w_in[j]), cos_p, sin_p, lp, tm, 2 * LANES, 5)
            o = _swa_attention(b3(qkv), swa_sinks[j].astype(F32), batch, lp)
            w_o = swa_w_o[j]
        else:
            w_a_cat, w_main, w_swap, w_kn, w_v = _mla_weights(mla_w_a[j], mla_w_uq[j], mla_w_ukv[j])
            cq, ckv, kr = _mla_a(hb, w_a_cat, mla_g_q[j][None, :], mla_g_kv[j][None, :], cos_m, sin_m, lp, tm)
            qh = _mla_q(cq, w_main, w_swap, cos_m, sin_m, lp, tm, tn)
            kh = _head_proj(ckv, w_kn, kr, tm, tn)
            vt = _matmul_t(ckv, *_value_weights(w_v), tm, V_COLS_TILE, tk).reshape(batch, nk, HEADS * V_ROWS, tk)
            o = _flash_attention(b3(qh), b3(kh), vt, batch, lp, tq, tk)
            w_o = mla_w_o[j]
        hf, hb = _oproj_ln(o.reshape(n, d), w_o.astype(BF16), hf, ln1_g[i][None, :], ln1_b[i][None, :], tm)
        hf, hb = _ffn(hb, hf, *_ffn_weights(ffn_w_in[i], ffn_conv_w[i], ffn_conv_b[i], ffn_w_out[i]),
                      ln2_g[i][None, :], ln2_b[i][None, :], lp, tm)
    return hf.reshape(batch, lp, d)[:, FIRST_REAL:]
```

```python
import functools
import math

import numpy as np
import jax
import jax.numpy as jnp
from jax import lax
from jax.experimental import pallas as pl
from jax.experimental.pallas import tpu as pltpu

F32 = jnp.float32
BF16 = jnp.bfloat16

D_MODEL = 1024
DEPTH = 4
N_META = 16
LEAD = 240
FIRST_REAL = LEAD + N_META
NEG = -1e30
DEEPNORM_ALPHA = (2.0 * DEPTH) ** 0.25
LN_EPS = 1e-5
RMS_EPS = 1e-6
HEADS = 16
HEAD_DIM = 64
PAIRS = HEADS // 2
SWA_KV_HEADS = 2
WINDOW = 128
ROPE_THETA = 500000.0
ROPE_DIM = 16
MLA_Q_LORA = 384
MLA_KV_LORA = 256
MLA_NOPE = 64
MLA_ROPE = 32
MLA_ROPE_THETA = 10000.0
D_FF = 2816
CONV_W = 3
LOG2E = math.log2(math.e)

LANES = 128
BF16_SUBLANES = 16
VMEM_LIMIT = 56 * 1024 * 1024

GATE_SLOTS = 3
FOX_DEAD_LANE = HEAD_DIM + 2 * GATE_SLOTS
MLA_DEAD_LANE = MLA_NOPE + MLA_ROPE
M_INIT = -3e38


def _params(*sem):
    return pltpu.CompilerParams(dimension_semantics=sem, vmem_limit_bytes=VMEM_LIMIT)


def _tile(n, pref, mult):
    best = mult
    t = mult
    while t <= min(n, pref):
        if n % t == 0:
            best = t
        t += mult
    assert n % best == 0
    return best


def _dot(a, b):
    return jnp.dot(a, b, preferred_element_type=F32)


def _dot_nt(a, b):
    return lax.dot_general(a, b, (((1,), (1,)), ((), ())), preferred_element_type=F32)


def _layer_norm(x, g, b):
    mu = jnp.mean(x, axis=-1, keepdims=True)
    xc = x - mu
    var = jnp.mean(xc * xc, axis=-1, keepdims=True)
    return xc * lax.rsqrt(var + LN_EPS) * g + b


def _one_hot_lanes(width, period, lane):
    idx = lax.broadcasted_iota(jnp.int32, (1, width), 1)
    return jnp.where((idx & (period - 1)) == lane, 1.0, 0.0)


def _dead_rows(tile_in_batch, tm):
    pos = tile_in_batch * tm + lax.broadcasted_iota(jnp.int32, (tm, 1), 0)
    return jnp.where(pos < LEAD, NEG, 0.0)


def _mm_kernel(x_ref, w_ref, o_ref):
    o_ref[...] = _dot(x_ref[...], w_ref[...]).astype(o_ref.dtype)


def _matmul(x, w, out_dtype, tm, tn):
    n, k = x.shape
    m = w.shape[1]
    return pl.pallas_call(
        _mm_kernel,
        grid=(n // tm, m // tn),
        in_specs=[pl.BlockSpec((tm, k), lambda i, j: (i, 0)),
                  pl.BlockSpec((k, tn), lambda i, j: (0, j))],
        out_specs=pl.BlockSpec((tm, tn), lambda i, j: (i, j)),
        out_shape=jax.ShapeDtypeStruct((n, m), out_dtype),
        compiler_params=_params("parallel", "parallel"),
        name="matmul",
    )(x, w)


def _head_proj_kernel(x_ref, w_ref, e_ref, o_ref):
    y = _dot(x_ref[...], w_ref[...])
    lo = lax.broadcasted_iota(jnp.int32, (1, LANES), 1) < HEAD_DIM
    shared = e_ref.shape[1] == LANES
    for pair in range(y.shape[1] // LANES):
        y_pair = y[:, pair * LANES:(pair + 1) * LANES]
        for a, feats in enumerate((y_pair, pltpu.roll(y_pair, HEAD_DIM, 1))):
            h = 2 * pair + a
            extra = e_ref[...] if shared else e_ref[:, h * LANES:(h + 1) * LANES]
            o_ref[:, h * LANES:(h + 1) * LANES] = jnp.where(lo, feats, extra.astype(F32)).astype(o_ref.dtype)


def _head_proj(x, w, e, tm, tn):
    n, k = x.shape
    m = w.shape[1]
    if e.shape[1] == LANES:
        e_spec = pl.BlockSpec((tm, LANES), lambda i, j: (i, 0))
    else:
        assert e.shape[1] == 2 * m
        e_spec = pl.BlockSpec((tm, 2 * tn), lambda i, j: (i, j))
    return pl.pallas_call(
        _head_proj_kernel,
        grid=(n // tm, m // tn),
        in_specs=[pl.BlockSpec((tm, k), lambda i, j: (i, 0)),
                  pl.BlockSpec((k, tn), lambda i, j: (0, j)),
                  e_spec],
        out_specs=pl.BlockSpec((tm, 2 * tn), lambda i, j: (i, j)),
        out_shape=jax.ShapeDtypeStruct((n, 2 * m), BF16),
        compiler_params=_params("parallel", "parallel"),
        name="head_proj",
    )(x, w, e)


def _mm_t_kernel(x_ref, w_ref, b_ref, o_ref, *, tk):
    y = _dot(x_ref[...], w_ref[...]) + b_ref[...]
    for c in range(o_ref.shape[0]):
        o_ref[c] = y[c * tk:(c + 1) * tk, :].T.astype(o_ref.dtype)


def _matmul_t(x, w, bias, tm, tn, tk):
    n, k = x.shape
    m = w.shape[1]
    r = tm // tk
    return pl.pallas_call(
        functools.partial(_mm_t_kernel, tk=tk),
        grid=(n // tm, m // tn),
        in_specs=[pl.BlockSpec((tm, k), lambda i, j: (i, 0)),
                  pl.BlockSpec((k, tn), lambda i, j: (0, j)),
                  pl.BlockSpec((1, tn), lambda i, j: (0, j))],
        out_specs=pl.BlockSpec((r, tn, tk), lambda i, j: (i, j, 0)),
        out_shape=jax.ShapeDtypeStruct((n // tk, m, tk), BF16),
        compiler_params=_params("parallel", "parallel"),
        name="matmul_t",
    )(x, w, bias)


V_ROWS = HEAD_DIM + BF16_SUBLANES
V_COLS_TILE = 8 * V_ROWS


def _value_weights(w_v):
    rows = w_v.shape[0]
    w = jnp.concatenate([w_v.reshape(rows, HEADS, HEAD_DIM),
                         jnp.zeros((rows, HEADS, V_ROWS - HEAD_DIM), w_v.dtype)], axis=2)
    bias = np.zeros((HEADS, V_ROWS), np.float32)
    bias[:, HEAD_DIM] = 1.0
    return w.reshape(rows, HEADS * V_ROWS).astype(BF16), jnp.asarray(bias.reshape(1, HEADS * V_ROWS))


def _gate_placement():
    wide = HEADS * LANES
    sq = np.zeros((LANES, wide), np.float32)
    sk = np.zeros_like(sq)
    oq = np.zeros((1, wide), np.float32)
    ok = np.zeros_like(oq)
    for h in range(HEADS):
        base = h * LANES + HEAD_DIM
        for part in range(GATE_SLOTS):
            sq[part * HEADS + h, base + part] = 1.0
            sk[part * HEADS + h, base + GATE_SLOTS + part] = -1.0
            oq[0, base + GATE_SLOTS + part] = 1.0
            ok[0, base + part] = 1.0
        oq[0, h * LANES + FOX_DEAD_LANE] = 1.0
    return sq, sk, oq, ok


def _split3(x):
    hi = x.astype(BF16)
    r1 = x - hi.astype(F32)
    mid = r1.astype(BF16)
    lo = (r1 - mid.astype(F32)).astype(BF16)
    return hi, mid, lo


def _fox_gate_kernel(x_ref, w_ref, b_ref, sq_ref, sk_ref, oq_ref, ok_ref, aq_ref, ak_ref, carry_ref,
                     *, tm, tiles_per_batch):
    i = pl.program_id(0)

    @pl.when(i % tiles_per_batch == 0)
    def _():
        carry_ref[...] = jnp.zeros_like(carry_ref)

    fg = _dot(x_ref[...], w_ref[...]) + b_ref[...]
    logf = jnp.minimum(fg, 0.0) - jnp.log(1.0 + jnp.exp(-jnp.abs(fg)))
    row = lax.broadcasted_iota(jnp.int32, (tm, tm), 0)
    col = lax.broadcasted_iota(jnp.int32, (tm, tm), 1)
    tri = jnp.where(col <= row, 1.0, 0.0).astype(BF16)
    hi, mid, lo = _split3(logf)
    cs = _dot(tri, hi) + _dot(tri, mid) + _dot(tri, lo) + carry_ref[...]
    carry_ref[...] = cs[tm - 1:tm, :]
    lane = lax.broadcasted_iota(jnp.int32, (1, LANES), 1)
    hi, mid, lo = _split3(cs * LOG2E)
    parts = jnp.where(lane < HEADS, hi.astype(F32),
                      jnp.where(lane < 2 * HEADS, mid.astype(F32), lo.astype(F32))).astype(BF16)
    wide = aq_ref.shape[1]
    dead = _dead_rows(i % tiles_per_batch, tm) * _one_hot_lanes(wide, LANES, FOX_DEAD_LANE)
    aq_ref[...] = (_dot(parts, sq_ref[...]) + oq_ref[...]).astype(aq_ref.dtype)
    ak_ref[...] = (_dot(parts, sk_ref[...]) + ok_ref[...] + dead).astype(ak_ref.dtype)


def _fox_gate(hb, w_fg, b_fg, lp, tm):
    n = hb.shape[0]
    tpb = lp // tm
    sq, sk, oq, ok = _gate_placement()
    fixed = lambda i: (0, 0)
    wide = HEADS * LANES
    return pl.pallas_call(
        functools.partial(_fox_gate_kernel, tm=tm, tiles_per_batch=tpb),
        grid=(n // tm,),
        in_specs=[pl.BlockSpec((tm, D_MODEL), lambda i: (i, 0)),
                  pl.BlockSpec((D_MODEL, LANES), fixed),
                  pl.BlockSpec((1, LANES), fixed),
                  pl.BlockSpec((LANES, wide), fixed),
                  pl.BlockSpec((LANES, wide), fixed),
                  pl.BlockSpec((1, wide), fixed),
                  pl.BlockSpec((1, wide), fixed)],
        out_specs=[pl.BlockSpec((tm, wide), lambda i: (i, 0)),
                   pl.BlockSpec((tm, wide), lambda i: (i, 0))],
        out_shape=[jax.ShapeDtypeStruct((n, wide), BF16), jax.ShapeDtypeStruct((n, wide), BF16)],
        scratch_shapes=[pltpu.VMEM((1, LANES), F32)],
        compiler_params=_params("arbitrary"),
        name="fox_gate",
    )(hb, w_fg, b_fg, jnp.asarray(sq, BF16), jnp.asarray(sk, BF16), jnp.asarray(oq), jnp.asarray(ok))


def _flash_kernel(q_ref, k_ref, vt_ref, o_ref, sa_ref, sb_ref, xa_ref, xb_ref, m_ref, acc_ref, *, tq, tk):
    i = pl.program_id(2)
    r = tq // tk
    m_ref[...] = jnp.full_like(m_ref, M_INIT)
    acc_ref[...] = jnp.zeros_like(acc_ref)

    def diag_mask(s):
        keep = lax.broadcasted_iota(jnp.int32, s.shape, 0) <= lax.broadcasted_iota(jnp.int32, s.shape, 1)
        return jnp.where(keep, s, NEG)

    def scores(s_ref, x_ref, j, a, qs, diagonal):
        off = pl.multiple_of(j * tk, tk)
        s = _dot_nt(k_ref[0, pl.ds(off, tk), a * LANES:(a + 1) * LANES], q_ref[0, qs:, a * LANES:(a + 1) * LANES])
        if diagonal:
            s = diag_mask(s)
        s_ref[a, :, qs:] = s
        x_ref[a, :, qs:] = jnp.max(s, axis=0, keepdims=True)

    def consume(s_ref, x_ref, j, a, qs, mask_now):
        s = s_ref[a, :, qs:]
        if mask_now:
            s = diag_mask(s)
            smax = jnp.max(s, axis=0, keepdims=True)
        else:
            smax = x_ref[a, :, qs:]
        m_old = m_ref[a, :, qs:]
        m_new = jnp.maximum(m_old, smax)
        alpha = jnp.exp2(m_old - m_new)
        p = jnp.exp2(s - m_new)
        m_ref[a, :, qs:] = m_new
        pv = _dot(vt_ref[0, j, a * V_ROWS:(a + 1) * V_ROWS, :], p.astype(BF16))
        acc_ref[a, :, qs:] = alpha * acc_ref[a, :, qs:] + pv

    buf_a, buf_b = (sa_ref, xa_ref), (sb_ref, xb_ref)
    jdiag = i * r
    odd = jdiag & 1
    for a in range(2):
        scores(*buf_a, 0, a, 0, False)

    @pl.when(odd == 1)
    def _():
        for a in range(2):
            scores(*buf_b, 1, a, 0, False)
            consume(*buf_a, 0, a, 0, False)
        sa_ref[...] = sb_ref[...]
        xa_ref[...] = xb_ref[...]

    def pair(j):
        for a in range(2):
            scores(*buf_b, j + 1, a, 0, False)
            consume(*buf_a, j, a, 0, False)
        for a in range(2):
            scores(*buf_a, j + 2, a, 0, False)
            consume(*buf_b, j + 1, a, 0, False)

    pairs = (jdiag - odd) // 2
    odd_pairs = pairs & 1

    @pl.when(odd_pairs == 1)
    def _():
        pair(odd)

    def quad_body(t, c):
        j = odd + 2 * odd_pairs + 4 * t
        pair(j)
        pair(j + 2)
        return c

    lax.fori_loop(0, pairs // 2, quad_body, 0)

    bufs = (buf_a, buf_b)
    for d in range(r):
        qs = d * tk
        for a in range(2):
            if d + 1 < r:
                scores(*bufs[(d + 1) & 1], jdiag + d + 1, a, qs + tk, True)
            consume(*bufs[d & 1], jdiag + d, a, qs, d == 0)

    ot = jnp.concatenate([acc_ref[a, :HEAD_DIM, :] / acc_ref[a, HEAD_DIM:HEAD_DIM + 1, :] for a in range(2)],
                         axis=0)
    o_ref[0] = ot.T.astype(o_ref.dtype)


def _flash_attention(qh, kh, vt, batch, lp, tq, tk):
    nk = lp // tk
    return pl.pallas_call(
        functools.partial(_flash_kernel, tq=tq, tk=tk),
        grid=(batch, PAIRS, lp // tq),
        in_specs=[pl.BlockSpec((1, tq, 2 * LANES), lambda b, p, i: (b, i, p)),
                  pl.BlockSpec((1, lp, 2 * LANES), lambda b, p, i: (b, 0, p)),
                  pl.BlockSpec((1, nk, 2 * V_ROWS, tk), lambda b, p, i: (b, 0, p, 0))],
        out_specs=pl.BlockSpec((1, tq, 2 * HEAD_DIM), lambda b, p, i: (b, i, p)),
        out_shape=jax.ShapeDtypeStruct((batch, lp, HEADS * HEAD_DIM), BF16),
        scratch_shapes=[pltpu.VMEM((2, tk, tq), F32), pltpu.VMEM((2, tk, tq), F32),
                        pltpu.VMEM((2, 1, tq), F32), pltpu.VMEM((2, 1, tq), F32),
                        pltpu.VMEM((2, 1, tq), F32), pltpu.VMEM((2, V_ROWS, tq), F32)],
        compiler_params=_params("parallel", "parallel", "parallel"),
        name="flash_attention",
    )(qh, kh, vt)


SWA_TQ = 128


def _swa_attn_kernel(sink_ref, q_ref, km_ref, kp_ref, kc_ref, vm_ref, vp_ref, vc_ref, o_ref):
    i = pl.program_id(1)
    t = SWA_TQ
    lo = lax.broadcasted_iota(jnp.int32, (1, LANES), 1) < HEAD_DIM
    row = lax.broadcasted_iota(jnp.int32, (t, 3 * t), 0)
    col = lax.broadcasted_iota(jnp.int32, (t, 3 * t), 1)
    qpos = i * t + row
    kpos = jnp.where(col < t, t + col, (i - 2) * t + col)
    d = qpos - kpos
    valid = (d >= 0) & (((col < t) & (kpos >= LEAD)) |
                        ((col >= t) & (d < WINDOW) & (kpos >= FIRST_REAL)))
    kcat, v_lo, v_hi = [], [], []
    for g in range(SWA_KV_HEADS):
        sl = slice(g * LANES, (g + 1) * LANES)
        kcat.append(jnp.concatenate([km_ref[0, :, sl], kp_ref[0, :, sl], kc_ref[0, :, sl]], axis=0))
        vf = jnp.concatenate([vm_ref[0, :, sl], vp_ref[0, :, sl], vc_ref[0, :, sl]], axis=0).astype(F32)
        v_lo.append(jnp.where(lo, vf, 0.0).astype(BF16))
        v_hi.append(jnp.where(lo, 0.0, vf).astype(BF16))
    pairs_per_group = PAIRS // SWA_KV_HEADS

    def logits(p):
        qf = q_ref[0, :, p * LANES:(p + 1) * LANES].astype(F32)
        q_pair = (jnp.where(lo, qf, 0.0).astype(BF16), jnp.where(lo, 0.0, qf).astype(BF16))
        return [_dot_nt(q_pair[a], kcat[p // pairs_per_group]) for a in range(2)]

    def finish(p, s_pair):
        g = p // pairs_per_group
        ps, inv = [], []
        for a in range(2):
            sink = sink_ref[2 * p + a]
            s = jnp.where(valid, s_pair[a], NEG)
            m = jnp.maximum(jnp.max(s, axis=1, keepdims=True), sink)
            e = jnp.exp(s - m)
            den = jnp.sum(e, axis=1, keepdims=True) + jnp.exp(sink - m)
            ps.append(e.astype(BF16))
            inv.append(1.0 / den)
        o = (_dot(ps[0], v_lo[g]) + _dot(ps[1], v_hi[g])) * jnp.where(lo, inv[0], inv[1])
        o_ref[0, :, p * LANES:(p + 1) * LANES] = o.astype(o_ref.dtype)

    s_next = logits(0)
    for p in range(PAIRS):
        s_cur = s_next
        if p + 1 < PAIRS:
            s_next = logits(p + 1)
        finish(p, s_cur)


def _swa_attention(qkv, sinks, batch, lp):
    t = SWA_TQ
    kblk, vblk = 4, 5
    kv_spec = lambda col, row_of: pl.BlockSpec((1, t, 2 * LANES), lambda b, i: (b, row_of(i), col))
    meta = lambda i: 1
    prev = lambda i: jnp.maximum(i - 1, 0)
    cur = lambda i: i
    return pl.pallas_call(
        _swa_attn_kernel,
        grid=(batch, lp // t),
        in_specs=[pl.BlockSpec(memory_space=pltpu.SMEM),
                  pl.BlockSpec((1, t, HEADS * HEAD_DIM), lambda b, i: (b, i, 0)),
                  kv_spec(kblk, meta), kv_spec(kblk, prev), kv_spec(kblk, cur),
                  kv_spec(vblk, meta), kv_spec(vblk, prev), kv_spec(vblk, cur)],
        out_specs=pl.BlockSpec((1, t, HEADS * HEAD_DIM), lambda b, i: (b, i, 0)),
        out_shape=jax.ShapeDtypeStruct((batch, lp, HEADS * HEAD_DIM), BF16),
        compiler_params=_params("parallel", "parallel"),
        name="swa_attention",
    )(sinks, qkv, qkv, qkv, qkv, qkv, qkv, qkv)


def _swa_proj_kernel(x_ref, w_ref, cos_ref, sin_ref, o_ref, *, tn, n_rope_blocks):
    j = pl.program_id(1)
    y = _dot(x_ref[...], w_ref[...])

    @pl.when(j < n_rope_blocks)
    def _():
        reps = tn // LANES
        cos = jnp.concatenate([cos_ref[...]] * reps, axis=1)
        sin = jnp.concatenate([sin_ref[...]] * reps, axis=1)
        lane = lax.broadcasted_iota(jnp.int32, (1, tn), 1)
        half = ROPE_DIM // 2
        partner = jnp.where((lane & (HEAD_DIM - 1)) < half,
                            pltpu.roll(y, tn - half, 1),
                            pltpu.roll(y, half, 1))
        o_ref[...] = (y * cos + partner * sin).astype(o_ref.dtype)

    @pl.when(j >= n_rope_blocks)
    def _():
        o_ref[...] = y.astype(o_ref.dtype)


def _swa_proj(hb, w, cos, sin, lp, tm, tn, n_rope_blocks):
    n = hb.shape[0]
    m = w.shape[1]
    tpb = lp // tm
    return pl.pallas_call(
        functools.partial(_swa_proj_kernel, tn=tn, n_rope_blocks=n_rope_blocks),
        grid=(n // tm, m // tn),
        in_specs=[pl.BlockSpec((tm, D_MODEL), lambda i, j: (i, 0)),
                  pl.BlockSpec((D_MODEL, tn), lambda i, j: (0, j)),
                  pl.BlockSpec((tm, LANES), lambda i, j: (i % tpb, 0)),
                  pl.BlockSpec((tm, LANES), lambda i, j: (i % tpb, 0))],
        out_specs=pl.BlockSpec((tm, tn), lambda i, j: (i, j)),
        out_shape=jax.ShapeDtypeStruct((n, m), BF16),
        compiler_params=_params("parallel", "parallel"),
        name="swa_proj",
    )(hb, w, cos, sin)


MLA_A_COLS = MLA_Q_LORA + MLA_KV_LORA + 2 * LANES


def _mla_a_kernel(x_ref, w_ref, gq_ref, gkv_ref, cos_ref, sin_ref, cq_ref, ckv_ref, kr_ref, *, tm, tiles_per_batch):
    y = _dot(x_ref[...], w_ref[...])
    cq = y[:, :MLA_Q_LORA]
    ckv = y[:, MLA_Q_LORA:MLA_Q_LORA + MLA_KV_LORA]
    kr = y[:, MLA_Q_LORA + MLA_KV_LORA:MLA_Q_LORA + MLA_KV_LORA + LANES]
    krs = y[:, MLA_Q_LORA + MLA_KV_LORA + LANES:]
    rms = lambda z, g: z * lax.rsqrt(jnp.mean(z * z, axis=-1, keepdims=True) + RMS_EPS) * g
    cq_ref[...] = rms(cq, gq_ref[...]).astype(cq_ref.dtype)
    ckv_ref[...] = rms(ckv, gkv_ref[...]).astype(ckv_ref.dtype)
    dead = _dead_rows(pl.program_id(0) % tiles_per_batch, tm) * _one_hot_lanes(LANES, LANES, MLA_DEAD_LANE)
    kr_ref[...] = (kr * cos_ref[...] + krs * sin_ref[...] + dead).astype(kr_ref.dtype)


def _mla_a(hb, w, gq, gkv, cos, sin, lp, tm):
    n = hb.shape[0]
    tpb = lp // tm
    return pl.pallas_call(
        functools.partial(_mla_a_kernel, tm=tm, tiles_per_batch=tpb),
        grid=(n // tm,),
        in_specs=[pl.BlockSpec((tm, D_MODEL), lambda i: (i, 0)),
                  pl.BlockSpec((D_MODEL, MLA_A_COLS), lambda i: (0, 0)),
                  pl.BlockSpec((1, MLA_Q_LORA), lambda i: (0, 0)),
                  pl.BlockSpec((1, MLA_KV_LORA), lambda i: (0, 0)),
                  pl.BlockSpec((tm, LANES), lambda i: (i % tpb, 0)),
                  pl.BlockSpec((tm, LANES), lambda i: (i % tpb, 0))],
        out_specs=[pl.BlockSpec((tm, MLA_Q_LORA), lambda i: (i, 0)),
                   pl.BlockSpec((tm, MLA_KV_LORA), lambda i: (i, 0)),
                   pl.BlockSpec((tm, LANES), lambda i: (i, 0))],
        out_shape=[jax.ShapeDtypeStruct((n, MLA_Q_LORA), BF16),
                   jax.ShapeDtypeStruct((n, MLA_KV_LORA), BF16),
                   jax.ShapeDtypeStruct((n, LANES), BF16)],
        compiler_params=_params("parallel"),
        name="mla_a",
    )(hb, w, gq, gkv, cos, sin)


def _mla_q_kernel(x_ref, w_ref, ws_ref, cos_ref, sin_ref, o_ref):
    x = x_ref[...]
    reps = o_ref.shape[1] // LANES
    cos = jnp.concatenate([cos_ref[...]] * reps, axis=1)
    sin = jnp.concatenate([sin_ref[...]] * reps, axis=1)
    y = _dot(x, w_ref[...]) * cos + _dot(x, ws_ref[...]) * sin
    o_ref[...] = (y + _one_hot_lanes(o_ref.shape[1], LANES, MLA_DEAD_LANE)).astype(o_ref.dtype)


def _mla_q(cq, w, ws, cos, sin, lp, tm, tn):
    n = cq.shape[0]
    m = w.shape[1]
    tpb = lp // tm
    return pl.pallas_call(
        _mla_q_kernel,
        grid=(n // tm, m // tn),
        in_specs=[pl.BlockSpec((tm, MLA_Q_LORA), lambda i, j: (i, 0)),
                  pl.BlockSpec((MLA_Q_LORA, tn), lambda i, j: (0, j)),
                  pl.BlockSpec((MLA_Q_LORA, tn), lambda i, j: (0, j)),
                  pl.BlockSpec((tm, LANES), lambda i, j: (i % tpb, 0)),
                  pl.BlockSpec((tm, LANES), lambda i, j: (i % tpb, 0))],
        out_specs=pl.BlockSpec((tm, tn), lambda i, j: (i, j)),
        out_shape=jax.ShapeDtypeStruct((n, m), BF16),
        compiler_params=_params("parallel", "parallel"),
        name="mla_q",
    )(cq, w, ws, cos, sin)


def _oproj_ln_kernel(o_ref, w_ref, h_ref, g_ref, b_ref, hf_ref, hb_ref):
    x = DEEPNORM_ALPHA * h_ref[...] + _dot(o_ref[...], w_ref[...])
    y = _layer_norm(x, g_ref[...], b_ref[...])
    hf_ref[...] = y
    hb_ref[...] = y.astype(hb_ref.dtype)


def _oproj_ln(o, w, h, g, b, tm):
    n = o.shape[0]
    row = lambda i: (i, 0)
    fixed = lambda i: (0, 0)
    return pl.pallas_call(
        _oproj_ln_kernel,
        grid=(n // tm,),
        in_specs=[pl.BlockSpec((tm, D_MODEL), row), pl.BlockSpec((D_MODEL, D_MODEL), fixed),
                  pl.BlockSpec((tm, D_MODEL), row), pl.BlockSpec((1, D_MODEL), fixed),
                  pl.BlockSpec((1, D_MODEL), fixed)],
        out_specs=[pl.BlockSpec((tm, D_MODEL), row), pl.BlockSpec((tm, D_MODEL), row)],
        out_shape=[jax.ShapeDtypeStruct((n, D_MODEL), F32), jax.ShapeDtypeStruct((n, D_MODEL), BF16)],
        compiler_params=_params("parallel"),
        name="oproj_ln",
    )(o, w, h, g, b)


FFN_HALO = BF16_SUBLANES


FFN_CHUNK = 256
FFN_NC = D_FF // FFN_CHUNK
assert FFN_NC * FFN_CHUNK == D_FF and FFN_NC % 2 == 1


def _ffn_kernel(x_ref, halo_ref, h_ref, win_ref, cw_ref, cb_ref, wo_ref, g_ref, b_ref, hf_ref, hb_ref,
                xext_ref, ua_ref, ub_ref, act_ref, *, tm, tiles_per_batch):
    i = pl.program_id(0)
    nc = FFN_NC
    pos = (i % tiles_per_batch) * tm - FFN_HALO + lax.broadcasted_iota(jnp.int32, (tm + FFN_HALO, 1), 0)
    xe = jnp.concatenate([halo_ref[...], x_ref[...]], axis=0).astype(F32)
    xext_ref[...] = jnp.where(pos >= LEAD, xe, 0.0).astype(BF16)

    def up(u_ref, c):
        u_ref[0] = _dot(xext_ref[...], win_ref[c])
        u_ref[1] = _dot(xext_ref[...], win_ref[nc + c])

    def glu(u_ref, c):
        def conv(part, idx):
            u = u_ref[part]
            delayed = u * cw_ref[idx, 0:1, :]
            for tap in range(1, CONV_W):
                delayed = u * cw_ref[idx, tap:tap + 1, :] + pltpu.roll(delayed, 1, 0)
            return cb_ref[idx] + delayed[FFN_HALO:, :]

        yg = conv(0, c)
        yv = conv(1, nc + c)
        act_ref[c] = ((yg / (1.0 + jnp.exp(-yg))) * yv).astype(BF16)

    up(ua_ref, 0)

    def pair_body(t, carry):
        c = 2 * t
        up(ub_ref, c + 1)
        glu(ua_ref, c)
        up(ua_ref, c + 2)
        glu(ub_ref, c + 1)
        return carry

    lax.fori_loop(0, (nc - 1) // 2, pair_body, 0)
    glu(ua_ref, nc - 1)

    ffn = _dot(act_ref[0], wo_ref[0])
    for c in range(1, nc):
        ffn = ffn + _dot(act_ref[c], wo_ref[c])
    y = _layer_norm(DEEPNORM_ALPHA * h_ref[...] + ffn, g_ref[...], b_ref[...])
    hf_ref[...] = y
    hb_ref[...] = y.astype(hb_ref.dtype)


def _ffn(hb, hf, w_in, conv_w, conv_b, w_out, g, b, lp, tm):
    n = hb.shape[0]
    nc, fc = FFN_NC, FFN_CHUNK
    tpb = lp // tm
    halo_blocks = tm // FFN_HALO
    row = lambda i: (i, 0)
    fixed2 = lambda i: (0, 0)
    fixed3 = lambda i: (0, 0, 0)
    resident = dict(pipeline_mode=pl.Buffered(1))
    return pl.pallas_call(
        functools.partial(_ffn_kernel, tm=tm, tiles_per_batch=tpb),
        grid=(n // tm,),
        in_specs=[pl.BlockSpec((tm, D_MODEL), row),
                  pl.BlockSpec((FFN_HALO, D_MODEL), lambda i: (jnp.maximum(i * halo_blocks - 1, 0), 0)),
                  pl.BlockSpec((tm, D_MODEL), row),
                  pl.BlockSpec((2 * nc, D_MODEL, fc), fixed3, **resident),
                  pl.BlockSpec((2 * nc, CONV_W, fc), fixed3, **resident),
                  pl.BlockSpec((2 * nc, 1, fc), fixed3, **resident),
                  pl.BlockSpec((nc, fc, D_MODEL), fixed3, **resident),
                  pl.BlockSpec((1, D_MODEL), fixed2), pl.BlockSpec((1, D_MODEL), fixed2)],
        out_specs=[pl.BlockSpec((tm, D_MODEL), row), pl.BlockSpec((tm, D_MODEL), row)],
        out_shape=[jax.ShapeDtypeStruct((n, D_MODEL), F32), jax.ShapeDtypeStruct((n, D_MODEL), BF16)],
        scratch_shapes=[pltpu.VMEM((tm + FFN_HALO, D_MODEL), BF16),
                        pltpu.VMEM((2, tm + FFN_HALO, fc), F32), pltpu.VMEM((2, tm + FFN_HALO, fc), F32),
                        pltpu.VMEM((nc, tm, fc), BF16)],
        compiler_params=_params("parallel"),
        name="ffn",
    )(hb, hb, hf, w_in, conv_w, conv_b, w_out, g, b)


def _ffn_weights(w_in, conv_w, conv_b, w_out):
    d = w_in.shape[0]
    chunks = 2 * FFN_NC
    return (w_in.reshape(d, chunks, FFN_CHUNK).transpose(1, 0, 2).astype(BF16),
            conv_w.reshape(CONV_W, chunks, FFN_CHUNK).transpose(1, 0, 2),
            conv_b.reshape(chunks, 1, FFN_CHUNK),
            w_out.reshape(FFN_NC, FFN_CHUNK, d).astype(BF16))


def _rope_tables(lp, dim, theta, group, offset):
    pos = (jnp.arange(lp) - LEAD).astype(F32)
    inv = theta ** (-jnp.arange(0, dim, 2, dtype=F32) / dim)
    ang = pos[:, None] * inv[None, :]
    cos, sin = jnp.cos(ang), jnp.sin(ang)
    ones = lambda w: jnp.ones((lp, w), F32)
    zeros = lambda w: jnp.zeros((lp, w), F32)
    rest = group - offset - dim
    cos_g = jnp.concatenate([ones(offset), cos, cos, ones(rest)], axis=1)
    sin_g = jnp.concatenate([zeros(offset), -sin, sin, zeros(rest)], axis=1)
    reps = LANES // group
    return jnp.tile(cos_g, (1, reps)), jnp.tile(sin_g, (1, reps))


def _swap_halves(w, dim):
    return jnp.concatenate([w[..., dim // 2:dim], w[..., :dim // 2]], axis=-1)


def _head_blocks(main, extra):
    src = main if main is not None else extra
    rows, heads = src.shape[0], src.shape[1]
    m = main if main is not None else jnp.zeros((rows, heads, HEAD_DIM), F32)
    e = extra if extra is not None else jnp.zeros((rows, heads, 0), F32)
    pad = jnp.zeros((rows, heads, LANES - HEAD_DIM - e.shape[2]), F32)
    return jnp.concatenate([m, e, pad], axis=2).reshape(rows, heads * LANES)


def _mla_weights(w_a, w_uq, w_ukv):
    d = w_a.shape[0]
    w_kr = w_a[:, MLA_Q_LORA + MLA_KV_LORA:][:, None, :]
    w_a_cat = jnp.concatenate([w_a[:, :MLA_Q_LORA + MLA_KV_LORA], _head_blocks(None, w_kr),
                               _head_blocks(None, _swap_halves(w_kr, MLA_ROPE))], axis=1)
    scale = (MLA_NOPE + MLA_ROPE) ** -0.5 * LOG2E
    wq = (w_uq * scale).reshape(MLA_Q_LORA, HEADS, MLA_NOPE + MLA_ROPE)
    w_main = _head_blocks(wq[..., :MLA_NOPE], wq[..., MLA_NOPE:])
    w_swap = _head_blocks(None, _swap_halves(wq[..., MLA_NOPE:], MLA_ROPE))
    wkv = w_ukv.reshape(MLA_KV_LORA, HEADS, MLA_NOPE + HEAD_DIM)
    w_kn = wkv[..., :MLA_NOPE].reshape(MLA_KV_LORA, HEADS * MLA_NOPE)
    w_v = wkv[..., MLA_NOPE:].reshape(MLA_KV_LORA, HEADS * HEAD_DIM)
    return tuple(w.astype(BF16) for w in (w_a_cat, w_main, w_swap, w_kn)) + (w_v,)


def _swa_weights(w_in):
    qd = HEADS * HEAD_DIM
    kd = SWA_KV_HEADS * HEAD_DIM
    q = w_in[:, :qd] * (HEAD_DIM ** -0.5)
    dup = lambda w: jnp.concatenate([w[:, :HEAD_DIM], w[:, :HEAD_DIM], w[:, HEAD_DIM:], w[:, HEAD_DIM:]], axis=1)
    return jnp.concatenate([q, dup(w_in[:, qd:qd + kd]), dup(w_in[:, qd + kd:])], axis=1).astype(BF16)


def _fox_weights(w_in, b_f):
    hd = HEADS * HEAD_DIM
    d = w_in.shape[0]
    w_q = (w_in[:, :hd] * (HEAD_DIM ** -0.5 * LOG2E)).astype(BF16)
    w_k = w_in[:, hd:2 * hd].astype(BF16)
    w_v = w_in[:, 2 * hd:3 * hd]
    w_gate = w_in[:, 3 * hd:]
    w_fg = jnp.concatenate([w_gate] * GATE_SLOTS + [jnp.zeros((d, LANES - GATE_SLOTS * HEADS), F32)],
                           axis=1).astype(BF16)
    b_fg = jnp.concatenate([b_f] * GATE_SLOTS + [jnp.zeros((LANES - GATE_SLOTS * HEADS,), F32)])[None, :]
    return w_q, w_k, w_v, w_fg, b_fg


def kernel(x, meta_tokens, ln1_g, ln1_b, ln2_g, ln2_b, fox_w_in, fox_b_f, fox_w_o, swa_w_in, swa_sinks, swa_w_o,
           mla_w_a, mla_g_q, mla_g_kv, mla_w_uq, mla_w_ukv, mla_w_o, ffn_w_in, ffn_conv_w, ffn_conv_b, ffn_w_out):
    batch, seq, d = x.shape
    assert d == D_MODEL and seq % 256 == 0
    lp = seq + FIRST_REAL
    n = batch * lp

    tm = _tile(lp, 768, 256)
    tq = _tile(lp, 768, 256)
    tk = 256
    tn = 512
    nk = lp // tk

    h0 = jnp.concatenate([jnp.zeros((batch, LEAD, d), x.dtype),
                          jnp.broadcast_to(meta_tokens.astype(x.dtype)[None], (batch, N_META, d)), x], axis=1)
    hf = h0.reshape(n, d)
    hb = hf.astype(BF16)

    cos_p, sin_p = _rope_tables(lp, ROPE_DIM, ROPE_THETA, HEAD_DIM, 0)
    cos_m, sin_m = _rope_tables(lp, MLA_ROPE, MLA_ROPE_THETA, LANES, MLA_NOPE)
    b3 = lambda a: a.reshape(batch, lp, -1)

    for i in range(DEPTH):
        kind, j = i % 3, i // 3
        if kind == 0:
            w_q, w_k, w_v, w_fg, b_fg = _fox_weights(fox_w_in[j], fox_b_f[j])
            aq, ak = _fox_gate(hb, w_fg, b_fg, lp, tm)
            qh = _head_proj(hb, w_q, aq, tm, tn)
            kh = _head_proj(hb, w_k, ak, tm, tn)
            vt = _matmul_t(hb, *_value_weights(w_v), tm, V_COLS_TILE, tk).reshape(batch, nk, HEADS * V_ROWS, tk)
            o = _flash_attention(b3(qh), b3(kh), vt, batch, lp, tq, tk)
            w_o = fox_w_o[j]
        elif kind == 1:
            qkv = _swa_proj(hb, _swa_weights(swa_w_in[j]), cos_p, sin_p, lp, tm, 2 * LANES, 5)
            o = _swa_attention(b3(qkv), swa_sinks[j].astype(F32), batch, lp)
            w_o = swa_w_o[j]
        else:
            w_a_cat, w_main, w_swap, w_kn, w_v = _mla_weights(mla_w_a[j], mla_w_uq[j], mla_w_ukv[j])
            cq, ckv, kr = _mla_a(hb, w_a_cat, mla_g_q[j][None, :], mla_g_kv[j][None, :], cos_m, sin_m, lp, tm)
            qh = _mla_q(cq, w_main, w_swap, cos_m, sin_m, lp, tm, tn)
            kh = _head_proj(ckv, w_kn, kr, tm, tn)
            vt = _matmul_t(ckv, *_value_weights(w_v), tm, V_COLS_TILE, tk).reshape(batch, nk, HEADS * V_ROWS, tk)
            o = _flash_attention(b3(qh), b3(kh), vt, batch, lp, tq, tk)
            w_o = mla_w_o[j]
        hf, hb = _oproj_ln(o.reshape(n, d), w_o.astype(BF16), hf, ln1_g[i][None, :], ln1_b[i][None, :], tm)
        hf, hb = _ffn(hb, hf, *_ffn_weights(ffn_w_in[i], ffn_conv_w[i], ffn_conv_b[i], ffn_w_out[i]),
                      ln2_g[i][None, :], ln2_b[i][None, :], lp, tm)
    return hf.reshape(batch, lp, d)[:, FIRST_REAL:]
```

```python
import functools
import math

import numpy as np
import jax
import jax.numpy as jnp
from jax import lax
from jax.experimental import pallas as pl
from jax.experimental.pallas import tpu as pltpu

F32 = jnp.float32
BF16 = jnp.bfloat16

D_MODEL = 1024
DEPTH = 4
N_META = 16
LEAD = 240
FIRST_REAL = LEAD + N_META
NEG = -1e30
DEEPNORM_ALPHA = (2.0 * DEPTH) ** 0.25
LN_EPS = 1e-5
RMS_EPS = 1e-6
HEADS = 16
HEAD_DIM = 64
PAIRS = HEADS // 2
SWA_KV_HEADS = 2
WINDOW = 128
ROPE_THETA = 500000.0
ROPE_DIM = 16
MLA_Q_LORA = 384
MLA_KV_LORA = 256
MLA_NOPE = 64
MLA_ROPE = 32
MLA_ROPE_THETA = 10000.0
D_FF = 2816
CONV_W = 3
LOG2E = math.log2(math.e)

LANES = 128
BF16_SUBLANES = 16
VMEM_LIMIT = 56 * 1024 * 1024

GATE_SLOTS = 3
FOX_DEAD_LANE = HEAD_DIM + 2 * GATE_SLOTS
MLA_DEAD_LANE = MLA_NOPE + MLA_ROPE
M_INIT = -3e38


def _params(*sem):
    return pltpu.CompilerParams(dimension_semantics=sem, vmem_limit_bytes=VMEM_LIMIT)


def _tile(n, pref, mult):
    best = mult
    t = mult
    while t <= min(n, pref):
        if n % t == 0:
            best = t
        t += mult
    assert n % best == 0
    return best


def _dot(a, b):
    return jnp.dot(a, b, preferred_element_type=F32)


def _dot_nt(a, b):
    return lax.dot_general(a, b, (((1,), (1,)), ((), ())), preferred_element_type=F32)


def _layer_norm(x, g, b):
    mu = jnp.mean(x, axis=-1, keepdims=True)
    xc = x - mu
    var = jnp.mean(xc * xc, axis=-1, keepdims=True)
    return xc * lax.rsqrt(var + LN_EPS) * g + b


def _one_hot_lanes(width, period, lane):
    idx = lax.broadcasted_iota(jnp.int32, (1, width), 1)
    return jnp.where((idx & (period - 1)) == lane, 1.0, 0.0)


def _dead_rows(tile_in_batch, tm):
    pos = tile_in_batch * tm + lax.broadcasted_iota(jnp.int32, (tm, 1), 0)
    return jnp.where(pos < LEAD, NEG, 0.0)


def _mm_kernel(x_ref, w_ref, o_ref):
    o_ref[...] = _dot(x_ref[...], w_ref[...]).astype(o_ref.dtype)


def _matmul(x, w, out_dtype, tm, tn):
    n, k = x.shape
    m = w.shape[1]
    return pl.pallas_call(
        _mm_kernel,
        grid=(n // tm, m // tn),
        in_specs=[pl.BlockSpec((tm, k), lambda i, j: (i, 0)),
                  pl.BlockSpec((k, tn), lambda i, j: (0, j))],
        out_specs=pl.BlockSpec((tm, tn), lambda i, j: (i, j)),
        out_shape=jax.ShapeDtypeStruct((n, m), out_dtype),
        compiler_params=_params("parallel", "parallel"),
        name="matmul",
    )(x, w)


def _head_proj_kernel(x_ref, w_ref, e_ref, o_ref):
    y = _dot(x_ref[...], w_ref[...])
    lo = lax.broadcasted_iota(jnp.int32, (1, LANES), 1) < HEAD_DIM
    shared = e_ref.shape[1] == LANES
    for pair in range(y.shape[1] // LANES):
        y_pair = y[:, pair * LANES:(pair + 1) * LANES]
        for a, feats in enumerate((y_pair, pltpu.roll(y_pair, HEAD_DIM, 1))):
            h = 2 * pair + a
            extra = e_ref[...] if shared else e_ref[:, h * LANES:(h + 1) * LANES]
            o_ref[:, h * LANES:(h + 1) * LANES] = jnp.where(lo, feats, extra.astype(F32)).astype(o_ref.dtype)


def _head_proj(x, w, e, tm, tn):
    n, k = x.shape
    m = w.shape[1]
    if e.shape[1] == LANES:
        e_spec = pl.BlockSpec((tm, LANES), lambda i, j: (i, 0))
    else:
        assert e.shape[1] == 2 * m
        e_spec = pl.BlockSpec((tm, 2 * tn), lambda i, j: (i, j))
    return pl.pallas_call(
        _head_proj_kernel,
        grid=(n // tm, m // tn),
        in_specs=[pl.BlockSpec((tm, k), lambda i, j: (i, 0)),
                  pl.BlockSpec((k, tn), lambda i, j: (0, j)),
                  e_spec],
        out_specs=pl.BlockSpec((tm, 2 * tn), lambda i, j: (i, j)),
        out_shape=jax.ShapeDtypeStruct((n, 2 * m), BF16),
        compiler_params=_params("parallel", "parallel"),
        name="head_proj",
    )(x, w, e)


def _mm_t_kernel(x_ref, w_ref, b_ref, o_ref, *, tk):
    y = _dot(x_ref[...], w_ref[...]) + b_ref[...]
    for c in range(o_ref.shape[0]):
        o_ref[c] = y[c * tk:(c + 1) * tk, :].T.astype(o_ref.dtype)


def _matmul_t(x, w, bias, tm, tn, tk):
    n, k = x.shape
    m = w.shape[1]
    r = tm // tk
    return pl.pallas_call(
        functools.partial(_mm_t_kernel, tk=tk),
        grid=(n // tm, m // tn),
        in_specs=[pl.BlockSpec((tm, k), lambda i, j: (i, 0)),
                  pl.BlockSpec((k, tn), lambda i, j: (0, j)),
                  pl.BlockSpec((1, tn), lambda i, j: (0, j))],
        out_specs=pl.BlockSpec((r, tn, tk), lambda i, j: (i, j, 0)),
        out_shape=jax.ShapeDtypeStruct((n // tk, m, tk), BF16),
        compiler_params=_params("parallel", "parallel"),
        name="matmul_t",
    )(x, w, bias)


V_ROWS = HEAD_DIM + BF16_SUBLANES
V_COLS_TILE = 8 * V_ROWS


def _value_weights(w_v):
    rows = w_v.shape[0]
    w = jnp.concatenate([w_v.reshape(rows, HEADS, HEAD_DIM),
                         jnp.zeros((rows, HEADS, V_ROWS - HEAD_DIM), w_v.dtype)], axis=2)
    bias = np.zeros((HEADS, V_ROWS), np.float32)
    bias[:, HEAD_DIM] = 1.0
    return w.reshape(rows, HEADS * V_ROWS).astype(BF16), jnp.asarray(bias.reshape(1, HEADS * V_ROWS))


def _gate_placement():
    wide = HEADS * LANES
    sq = np.zeros((LANES, wide), np.float32)
    sk = np.zeros_like(sq)
    oq = np.zeros((1, wide), np.float32)
    ok = np.zeros_like(oq)
    for h in range(HEADS):
        base = h * LANES + HEAD_DIM
        for part in range(GATE_SLOTS):
            sq[part * HEADS + h, base + part] = 1.0
            sk[part * HEADS + h, base + GATE_SLOTS + part] = -1.0
            oq[0, base + GATE_SLOTS + part] = 1.0
            ok[0, base + part] = 1.0
        oq[0, h * LANES + FOX_DEAD_LANE] = 1.0
    return sq, sk, oq, ok


def _split3(x):
    hi = x.astype(BF16)
    r1 = x - hi.astype(F32)
    mid = r1.astype(BF16)
    lo = (r1 - mid.astype(F32)).astype(BF16)
    return hi, mid, lo


def _fox_gate_kernel(x_ref, w_ref, b_ref, sq_ref, sk_ref, oq_ref, ok_ref, aq_ref, ak_ref, carry_ref,
                     *, tm, tiles_per_batch):
    i = pl.program_id(0)

    @pl.when(i % tiles_per_batch == 0)
    def _():
        carry_ref[...] = jnp.zeros_like(carry_ref)

    fg = _dot(x_ref[...], w_ref[...]) + b_ref[...]
    logf = jnp.minimum(fg, 0.0) - jnp.log(1.0 + jnp.exp(-jnp.abs(fg)))
    row = lax.broadcasted_iota(jnp.int32, (tm, tm), 0)
    col = lax.broadcasted_iota(jnp.int32, (tm, tm), 1)
    tri = jnp.where(col <= row, 1.0, 0.0).astype(BF16)
    hi, mid, lo = _split3(logf)
    cs = _dot(tri, hi) + _dot(tri, mid) + _dot(tri, lo) + carry_ref[...]
    carry_ref[...] = cs[tm - 1:tm, :]
    lane = lax.broadcasted_iota(jnp.int32, (1, LANES), 1)
    hi, mid, lo = _split3(cs * LOG2E)
    parts = jnp.where(lane < HEADS, hi.astype(F32),
                      jnp.where(lane < 2 * HEADS, mid.astype(F32), lo.astype(F32))).astype(BF16)
    wide = aq_ref.shape[1]
    dead = _dead_rows(i % tiles_per_batch, tm) * _one_hot_lanes(wide, LANES, FOX_DEAD_LANE)
    aq_ref[...] = (_dot(parts, sq_ref[...]) + oq_ref[...]).astype(aq_ref.dtype)
    ak_ref[...] = (_dot(parts, sk_ref[...]) + ok_ref[...] + dead).astype(ak_ref.dtype)


def _fox_gate(hb, w_fg, b_fg, lp, tm):
    n = hb.shape[0]
    tpb = lp // tm
    sq, sk, oq, ok = _gate_placement()
    fixed = lambda i: (0, 0)
    wide = HEADS * LANES
    return pl.pallas_call(
        functools.partial(_fox_gate_kernel, tm=tm, tiles_per_batch=tpb),
        grid=(n // tm,),
        in_specs=[pl.BlockSpec((tm, D_MODEL), lambda i: (i, 0)),
                  pl.BlockSpec((D_MODEL, LANES), fixed),
                  pl.BlockSpec((1, LANES), fixed),
                  pl.BlockSpec((LANES, wide), fixed),
                  pl.BlockSpec((LANES, wide), fixed),
                  pl.BlockSpec((1, wide), fixed),
                  pl.BlockSpec((1, wide), fixed)],
        out_specs=[pl.BlockSpec((tm, wide), lambda i: (i, 0)),
                   pl.BlockSpec((tm, wide), lambda i: (i, 0))],
        out_shape=[jax.ShapeDtypeStruct((n, wide), BF16), jax.ShapeDtypeStruct((n, wide), BF16)],
        scratch_shapes=[pltpu.VMEM((1, LANES), F32)],
        compiler_params=_params("arbitrary"),
        name="fox_gate",
    )(hb, w_fg, b_fg, jnp.asarray(sq, BF16), jnp.asarray(sk, BF16), jnp.asarray(oq), jnp.asarray(ok))


def _flash_kernel(q_ref, k_ref, vt_ref, o_ref, sa_ref, sb_ref, xa_ref, xb_ref, m_ref, acc_ref, *, tq, tk):
    i = pl.program_id(2)
    r = tq // tk
    m_ref[...] = jnp.full_like(m_ref, M_INIT)
    acc_ref[...] = jnp.zeros_like(acc_ref)

    def diag_mask(s):
        keep = lax.broadcasted_iota(jnp.int32, s.shape, 0) <= lax.broadcasted_iota(jnp.int32, s.shape, 1)
        return jnp.where(keep, s, NEG)

    def scores(s_ref, x_ref, j, a, qs, diagonal):
        off = pl.multiple_of(j * tk, tk)
        s = _dot_nt(k_ref[0, pl.ds(off, tk), a * LANES:(a + 1) * LANES], q_ref[0, qs:, a * LANES:(a + 1) * LANES])
        if diagonal:
            s = diag_mask(s)
        s_ref[a, :, qs:] = s
        x_ref[a, :, qs:] = jnp.max(s, axis=0, keepdims=True)

    def consume(s_ref, x_ref, j, a, qs, mask_now):
        s = s_ref[a, :, qs:]
        if mask_now:
            s = diag_mask(s)
            smax = jnp.max(s, axis=0, keepdims=True)
        else:
            smax = x_ref[a, :, qs:]
        m_old = m_ref[a, :, qs:]
        m_new = jnp.maximum(m_old, smax)
        alpha = jnp.exp2(m_old - m_new)
        p = jnp.exp2(s - m_new)
        m_ref[a, :, qs:] = m_new
        pv = _dot(vt_ref[0, j, a * V_ROWS:(a + 1) * V_ROWS, :], p.astype(BF16))
        acc_ref[a, :, qs:] = alpha * acc_ref[a, :, qs:] + pv

    buf_a, buf_b = (sa_ref, xa_ref), (sb_ref, xb_ref)
    jdiag = i * r
    odd = jdiag & 1
    for a in range(2):
        scores(*buf_a, 0, a, 0, False)

    @pl.when(odd == 1)
    def _():
        for a in range(2):
            scores(*buf_b, 1, a, 0, False)
            consume(*buf_a, 0, a, 0, False)
        sa_ref[...] = sb_ref[...]
        xa_ref[...] = xb_ref[...]

    def pair(j):
        for a in range(2):
            scores(*buf_b, j + 1, a, 0, False)
            consume(*buf_a, j, a, 0, False)
        for a in range(2):
            scores(*buf_a, j + 2, a, 0, False)
            consume(*buf_b, j + 1, a, 0, False)

    pairs = (jdiag - odd) // 2
    odd_pairs = pairs & 1

    @pl.when(odd_pairs == 1)
    def _():
        pair(odd)

    def quad_body(t, c):
        j = odd + 2 * odd_pairs + 4 * t
        pair(j)
        pair(j + 2)
        return c

    lax.fori_loop(0, pairs // 2, quad_body, 0)

    bufs = (buf_a, buf_b)
    for d in range(r):
        qs = d * tk
        for a in range(2):
            if d + 1 < r:
                scores(*bufs[(d + 1) & 1], jdiag + d + 1, a, qs + tk, True)
            consume(*bufs[d & 1], jdiag + d, a, qs, d == 0)

    ot = jnp.concatenate([acc_ref[a, :HEAD_DIM, :] / acc_ref[a, HEAD_DIM:HEAD_DIM + 1, :] for a in range(2)],
                         axis=0)
    o_ref[0] = ot.T.astype(o_ref.dtype)


def _flash_attention(qh, kh, vt, batch, lp, tq, tk):
    nk = lp // tk
    return pl.pallas_call(
        functools.partial(_flash_kernel, tq=tq, tk=tk),
        grid=(batch, PAIRS, lp // tq),
        in_specs=[pl.BlockSpec((1, tq, 2 * LANES), lambda b, p, i: (b, i, p)),
                  pl.BlockSpec((1, lp, 2 * LANES), lambda b, p, i: (b, 0, p)),
                  pl.BlockSpec((1, nk, 2 * V_ROWS, tk), lambda b, p, i: (b, 0, p, 0))],
        out_specs=pl.BlockSpec((1, tq, 2 * HEAD_DIM), lambda b, p, i: (b, i, p)),
        out_shape=jax.ShapeDtypeStruct((batch, lp, HEADS * HEAD_DIM), BF16),
        scratch_shapes=[pltpu.VMEM((2, tk, tq), F32), pltpu.VMEM((2, tk, tq), F32),
                        pltpu.VMEM((2, 1, tq), F32), pltpu.VMEM((2, 1, tq), F32),
                        pltpu.VMEM((2, 1, tq), F32), pltpu.VMEM((2, V_ROWS, tq), F32)],
        compiler_params=_params("parallel", "parallel", "parallel"),
        name="flash_attention",
    )(qh, kh, vt)


SWA_TQ = 128


def _swa_attn_kernel(sink_ref, q_ref, km_ref, kp_ref, kc_ref, vm_ref, vp_ref, vc_ref, o_ref):
    i = pl.program_id(1)
    t = SWA_TQ
    lo = lax.broadcasted_iota(jnp.int32, (1, LANES), 1) < HEAD_DIM
    row = lax.broadcasted_iota(jnp.int32, (t, 3 * t), 0)
    col = lax.broadcasted_iota(jnp.int32, (t, 3 * t), 1)
    qpos = i * t + row
    kpos = jnp.where(col < t, t + col, (i - 2) * t + col)
    d = qpos - kpos
    valid = (d >= 0) & (((col < t) & (kpos >= LEAD)) |
                        ((col >= t) & (d < WINDOW) & (kpos >= FIRST_REAL)))
    kcat, v_lo, v_hi = [], [], []
    for g in range(SWA_KV_HEADS):
        sl = slice(g * LANES, (g + 1) * LANES)
        kcat.append(jnp.concatenate([km_ref[0, :, sl], kp_ref[0, :, sl], kc_ref[0, :, sl]], axis=0))
        vf = jnp.concatenate([vm_ref[0, :, sl], vp_ref[0, :, sl], vc_ref[0, :, sl]], axis=0).astype(F32)
        v_lo.append(jnp.where(lo, vf, 0.0).astype(BF16))
        v_hi.append(jnp.where(lo, 0.0, vf).astype(BF16))
    pairs_per_group = PAIRS // SWA_KV_HEADS

    def logits(p):
        qf = q_ref[0, :, p * LANES:(p + 1) * LANES].astype(F32)
        q_pair = (jnp.where(lo, qf, 0.0).astype(BF16), jnp.where(lo, 0.0, qf).astype(BF16))
        return [_dot_nt(q_pair[a], kcat[p // pairs_per_group]) for a in range(2)]

    def finish(p, s_pair):
        g = p // pairs_per_group
        ps, inv = [], []
        for a in range(2):
            sink = sink_ref[2 * p + a]
            s = jnp.where(valid, s_pair[a], NEG)
            m = jnp.maximum(jnp.max(s, axis=1, keepdims=True), sink)
            e = jnp.exp(s - m)
            den = jnp.sum(e, axis=1, keepdims=True) + jnp.exp(sink - m)
            ps.append(e.astype(BF16))
            inv.append(1.0 / den)
        o = (_dot(ps[0], v_lo[g]) + _dot(ps[1], v_hi[g])) * jnp.where(lo, inv[0], inv[1])
        o_ref[0, :, p * LANES:(p + 1) * LANES] = o.astype(o_ref.dtype)

    s_next = logits(0)
    for p in range(PAIRS):
        s_cur = s_next
        if p + 1 < PAIRS:
            s_next = logits(p + 1)
        finish(p, s_cur)


def _swa_attention(qkv, sinks, batch, lp):
    t = SWA_TQ
    kblk, vblk = 4, 5
    kv_spec = lambda col, row_of: pl.BlockSpec((1, t, 2 * LANES), lambda b, i: (b, row_of(i), col))
    meta = lambda i: 1
    prev = lambda i: jnp.maximum(i - 1, 0)
    cur = lambda i: i
    return pl.pallas_call(
        _swa_attn_kernel,
        grid=(batch, lp // t),
        in_specs=[pl.BlockSpec(memory_space=pltpu.SMEM),
                  pl.BlockSpec((1, t, HEADS * HEAD_DIM), lambda b, i: (b, i, 0)),
                  kv_spec(kblk, meta), kv_spec(kblk, prev), kv_spec(kblk, cur),
                  kv_spec(vblk, meta), kv_spec(vblk, prev), kv_spec(vblk, cur)],
        out_specs=pl.BlockSpec((1, t, HEADS * HEAD_DIM), lambda b, i: (b, i, 0)),
        out_shape=jax.ShapeDtypeStruct((batch, lp, HEADS * HEAD_DIM), BF16),
        compiler_params=_params("parallel", "parallel"),
        name="swa_attention",
    )(sinks, qkv, qkv, qkv, qkv, qkv, qkv, qkv)


def _swa_proj_kernel(x_ref, w_ref, cos_ref, sin_ref, o_ref, *, tn, n_rope_blocks):
    j = pl.program_id(1)
    y = _dot(x_ref[...], w_ref[...])

    @pl.when(j < n_rope_blocks)
    def _():
        reps = tn // LANES
        cos = jnp.concatenate([cos_ref[...]] * reps, axis=1)
        sin = jnp.concatenate([sin_ref[...]] * reps, axis=1)
        lane = lax.broadcasted_iota(jnp.int32, (1, tn), 1)
        half = ROPE_DIM // 2
        partner = jnp.where((lane & (HEAD_DIM - 1)) < half,
                            pltpu.roll(y, tn - half, 1),
                            pltpu.roll(y, half, 1))
        o_ref[...] = (y * cos + partner * sin).astype(o_ref.dtype)

    @pl.when(j >= n_rope_blocks)
    def _():
        o_ref[...] = y.astype(o_ref.dtype)


def _swa_proj(hb, w, cos, sin, lp, tm, tn, n_rope_blocks):
    n = hb.shape[0]
    m = w.shape[1]
    tpb = lp // tm
    return pl.pallas_call(
        functools.partial(_swa_proj_kernel, tn=tn, n_rope_blocks=n_rope_blocks),
        grid=(n // tm, m // tn),
        in_specs=[pl.BlockSpec((tm, D_MODEL), lambda i, j: (i, 0)),
                  pl.BlockSpec((D_MODEL, tn), lambda i, j: (0, j)),
                  pl.BlockSpec((tm, LANES), lambda i, j: (i % tpb, 0)),
                  pl.BlockSpec((tm, LANES), lambda i, j: (i % tpb, 0))],
        out_specs=pl.BlockSpec((tm, tn), lambda i, j: (i, j)),
        out_shape=jax.ShapeDtypeStruct((n, m), BF16),
        compiler_params=_params("parallel", "parallel"),
        name="swa_proj",
    )(hb, w, cos, sin)


MLA_A_COLS = MLA_Q_LORA + MLA_KV_LORA + 2 * LANES


def _mla_a_kernel(x_ref, w_ref, gq_ref, gkv_ref, cos_ref, sin_ref, cq_ref, ckv_ref, kr_ref, *, tm, tiles_per_batch):
    y = _dot(x_ref[...], w_ref[...])
    cq = y[:, :MLA_Q_LORA]
    ckv = y[:, MLA_Q_LORA:MLA_Q_LORA + MLA_KV_LORA]
    kr = y[:, MLA_Q_LORA + MLA_KV_LORA:MLA_Q_LORA + MLA_KV_LORA + LANES]
    krs = y[:, MLA_Q_LORA + MLA_KV_LORA + LANES:]
    rms = lambda z, g: z * lax.rsqrt(jnp.mean(z * z, axis=-1, keepdims=True) + RMS_EPS) * g
    cq_ref[...] = rms(cq, gq_ref[...]).astype(cq_ref.dtype)
    ckv_ref[...] = rms(ckv, gkv_ref[...]).astype(ckv_ref.dtype)
    dead = _dead_rows(pl.program_id(0) % tiles_per_batch, tm) * _one_hot_lanes(LANES, LANES, MLA_DEAD_LANE)
    kr_ref[...] = (kr * cos_ref[...] + krs * sin_ref[...] + dead).astype(kr_ref.dtype)


def _mla_a(hb, w, gq, gkv, cos, sin, lp, tm):
    n = hb.shape[0]
    tpb = lp // tm
    return pl.pallas_call(
        functools.partial(_mla_a_kernel, tm=tm, tiles_per_batch=tpb),
        grid=(n // tm,),
        in_specs=[pl.BlockSpec((tm, D_MODEL), lambda i: (i, 0)),
                  pl.BlockSpec((D_MODEL, MLA_A_COLS), lambda i: (0, 0)),
                  pl.BlockSpec((1, MLA_Q_LORA), lambda i: (0, 0)),
                  pl.BlockSpec((1, MLA_KV_LORA), lambda i: (0, 0)),
                  pl.BlockSpec((tm, LANES), lambda i: (i % tpb, 0)),
                  pl.BlockSpec((tm, LANES), lambda i: (i % tpb, 0))],
        out_specs=[pl.BlockSpec((tm, MLA_Q_LORA), lambda i: (i, 0)),
                   pl.BlockSpec((tm, MLA_KV_LORA), lambda i: (i, 0)),
                   pl.BlockSpec((tm, LANES), lambda i: (i, 0))],
        out_shape=[jax.ShapeDtypeStruct((n, MLA_Q_LORA), BF16),
                   jax.ShapeDtypeStruct((n, MLA_KV_LORA), BF16),
                   jax.ShapeDtypeStruct((n, LANES), BF16)],
        compiler_params=_params("parallel"),
        name="mla_a",
    )(hb, w, gq, gkv, cos, sin)


def _mla_q_kernel(x_ref, w_ref, ws_ref, cos_ref, sin_ref, o_ref):
    x = x_ref[...]
    reps = o_ref.shape[1] // LANES
    cos = jnp.concatenate([cos_ref[...]] * reps, axis=1)
    sin = jnp.concatenate([sin_ref[...]] * reps, axis=1)
    y = _dot(x, w_ref[...]) * cos + _dot(x, ws_ref[...]) * sin
    o_ref[...] = (y + _one_hot_lanes(o_ref.shape[1], LANES, MLA_DEAD_LANE)).astype(o_ref.dtype)


def _mla_q(cq, w, ws, cos, sin, lp, tm, tn):
    n = cq.shape[0]
    m = w.shape[1]
    tpb = lp // tm
    return pl.pallas_call(
        _mla_q_kernel,
        grid=(n // tm, m // tn),
        in_specs=[pl.BlockSpec((tm, MLA_Q_LORA), lambda i, j: (i, 0)),
                  pl.BlockSpec((MLA_Q_LORA, tn), lambda i, j: (0, j)),
                  pl.BlockSpec((MLA_Q_LORA, tn), lambda i, j: (0, j)),
                  pl.BlockSpec((tm, LANES), lambda i, j: (i % tpb, 0)),
                  pl.BlockSpec((tm, LANES), lambda i, j: (i % tpb, 0))],
        out_specs=pl.BlockSpec((tm, tn), lambda i, j: (i, j)),
        out_shape=jax.ShapeDtypeStruct((n, m), BF16),
        compiler_params=_params("parallel", "parallel"),
        name="mla_q",
    )(cq, w, ws, cos, sin)


def _oproj_ln_kernel(o_ref, w_ref, h_ref, g_ref, b_ref, hf_ref, hb_ref):
    x = DEEPNORM_ALPHA * h_ref[...] + _dot(o_ref[...], w_ref[...])
    y = _layer_norm(x, g_ref[...], b_ref[...])
    hf_ref[...] = y
    hb_ref[...] = y.astype(hb_ref.dtype)


def _oproj_ln(o, w, h, g, b, tm):
    n = o.shape[0]
    row = lambda i: (i, 0)
    fixed = lambda i: (0, 0)
    return pl.pallas_call(
        _oproj_ln_kernel,
        grid=(n // tm,),
        in_specs=[pl.BlockSpec((tm, D_MODEL), row), pl.BlockSpec((D_MODEL, D_MODEL), fixed),
                  pl.BlockSpec((tm, D_MODEL), row), pl.BlockSpec((1, D_MODEL), fixed),
                  pl.BlockSpec((1, D_MODEL), fixed)],
        out_specs=[pl.BlockSpec((tm, D_MODEL), row), pl.BlockSpec((tm, D_MODEL), row)],
        out_shape=[jax.ShapeDtypeStruct((n, D_MODEL), F32), jax.ShapeDtypeStruct((n, D_MODEL), BF16)],
        compiler_params=_params("parallel"),
        name="oproj_ln",
    )(o, w, h, g, b)


FFN_HALO = BF16_SUBLANES


FFN_CHUNK = 256
FFN_NC = D_FF // FFN_CHUNK
assert FFN_NC * FFN_CHUNK == D_FF and FFN_NC % 2 == 1


def _ffn_kernel(x_ref, halo_ref, h_ref, win_ref, cw_ref, cb_ref, wo_ref, g_ref, b_ref, hf_ref, hb_ref,
                xext_ref, ua_ref, ub_ref, acc_ref, *, tm, tiles_per_batch):
    i = pl.program_id(0)
    nc = FFN_NC
    pos = (i % tiles_per_batch) * tm - FFN_HALO + lax.broadcasted_iota(jnp.int32, (tm + FFN_HALO, 1), 0)
    xe = jnp.concatenate([halo_ref[...], x_ref[...]], axis=0).astype(F32)
    xext_ref[...] = jnp.where(pos >= LEAD, xe, 0.0).astype(BF16)

    def up(u_ref, c):
        u_ref[0] = _dot(xext_ref[...], win_ref[c])
        u_ref[1] = _dot(xext_ref[...], win_ref[nc + c])

    def glu(u_ref, c):
        def conv(part, idx):
            u = u_ref[part]
            delayed = u * cw_ref[idx, 0:1, :]
            for tap in range(1, CONV_W):
                delayed = u * cw_ref[idx, tap:tap + 1, :] + pltpu.roll(delayed, 1, 0)
            return cb_ref[idx] + delayed[FFN_HALO:, :]

        yg = conv(0, c)
        yv = conv(1, nc + c)
        return ((yg / (1.0 + jnp.exp(-yg))) * yv).astype(BF16)

    acc_ref[...] = DEEPNORM_ALPHA * h_ref[...]
    up(ua_ref, 0)

    def pair_body(t, carry):
        c = 2 * t
        up(ub_ref, c + 1)
        act_a = glu(ua_ref, c)
        up(ua_ref, c + 2)
        act_b = glu(ub_ref, c + 1)
        acc_ref[...] += _dot(act_a, wo_ref[c]) + _dot(act_b, wo_ref[c + 1])
        return carry

    lax.fori_loop(0, (nc - 1) // 2, pair_body, 0)
    y = _layer_norm(acc_ref[...] + _dot(glu(ua_ref, nc - 1), wo_ref[nc - 1]), g_ref[...], b_ref[...])
    hf_ref[...] = y
    hb_ref[...] = y.astype(hb_ref.dtype)


def _ffn(hb, hf, w_in, conv_w, conv_b, w_out, g, b, lp, tm):
    n = hb.shape[0]
    nc, fc = FFN_NC, FFN_CHUNK
    tpb = lp // tm
    halo_blocks = tm // FFN_HALO
    row = lambda i: (i, 0)
    fixed2 = lambda i: (0, 0)
    fixed3 = lambda i: (0, 0, 0)
    resident = dict(pipeline_mode=pl.Buffered(1))
    return pl.pallas_call(
        functools.partial(_ffn_kernel, tm=tm, tiles_per_batch=tpb),
        grid=(n // tm,),
        in_specs=[pl.BlockSpec((tm, D_MODEL), row),
                  pl.BlockSpec((FFN_HALO, D_MODEL), lambda i: (jnp.maximum(i * halo_blocks - 1, 0), 0)),
                  pl.BlockSpec((tm, D_MODEL), row),
                  pl.BlockSpec((2 * nc, D_MODEL, fc), fixed3, **resident),
                  pl.BlockSpec((2 * nc, CONV_W, fc), fixed3, **resident),
                  pl.BlockSpec((2 * nc, 1, fc), fixed3, **resident),
                  pl.BlockSpec((nc, fc, D_MODEL), fixed3, **resident),
                  pl.BlockSpec((1, D_MODEL), fixed2), pl.BlockSpec((1, D_MODEL), fixed2)],
        out_specs=[pl.BlockSpec((tm, D_MODEL), row), pl.BlockSpec((tm, D_MODEL), row)],
        out_shape=[jax.ShapeDtypeStruct((n, D_MODEL), F32), jax.ShapeDtypeStruct((n, D_MODEL), BF16)],
        scratch_shapes=[pltpu.VMEM((tm + FFN_HALO, D_MODEL), BF16),
                        pltpu.VMEM((2, tm + FFN_HALO, fc), F32), pltpu.VMEM((2, tm + FFN_HALO, fc), F32),
                        pltpu.VMEM((tm, D_MODEL), F32)],
        compiler_params=_params("parallel"),
        name="ffn",
    )(hb, hb, hf, w_in, conv_w, conv_b, w_out, g, b)


def _ffn_weights(w_in, conv_w, conv_b, w_out):
    d = w_in.shape[0]
    chunks = 2 * FFN_NC
    return (w_in.reshape(d, chunks, FFN_CHUNK).transpose(1, 0, 2).astype(BF16),
            conv_w.reshape(CONV_W, chunks, FFN_CHUNK).transpose(1, 0, 2),
            conv_b.reshape(chunks, 1, FFN_CHUNK),
            w_out.reshape(FFN_NC, FFN_CHUNK, d).astype(BF16))


def _rope_tables(lp, dim, theta, group, offset):
    pos = (jnp.arange(lp) - LEAD).astype(F32)
    inv = theta ** (-jnp.arange(0, dim, 2, dtype=F32) / dim)
    ang = pos[:, None] * inv[None, :]
    cos, sin = jnp.cos(ang), jnp.sin(ang)
    ones = lambda w: jnp.ones((lp, w), F32)
    zeros = lambda w: jnp.zeros((lp, w), F32)
    rest = group - offset - dim
    cos_g = jnp.concatenate([ones(offset), cos, cos, ones(rest)], axis=1)
    sin_g = jnp.concatenate([zeros(offset), -sin, sin, zeros(rest)], axis=1)
    reps = LANES // group
    return jnp.tile(cos_g, (1, reps)), jnp.tile(sin_g, (1, reps))


def _swap_halves(w, dim):
    return jnp.concatenate([w[..., dim // 2:dim], w[..., :dim // 2]], axis=-1)


def _head_blocks(main, extra):
    src = main if main is not None else extra
    rows, heads = src.shape[0], src.shape[1]
    m = main if main is not None else jnp.zeros((rows, heads, HEAD_DIM), F32)
    e = extra if extra is not None else jnp.zeros((rows, heads, 0), F32)
    pad = jnp.zeros((rows, heads, LANES - HEAD_DIM - e.shape[2]), F32)
    return jnp.concatenate([m, e, pad], axis=2).reshape(rows, heads * LANES)


def _mla_weights(w_a, w_uq, w_ukv):
    d = w_a.shape[0]
    w_kr = w_a[:, MLA_Q_LORA + MLA_KV_LORA:][:, None, :]
    w_a_cat = jnp.concatenate([w_a[:, :MLA_Q_LORA + MLA_KV_LORA], _head_blocks(None, w_kr),
                               _head_blocks(None, _swap_halves(w_kr, MLA_ROPE))], axis=1)
    scale = (MLA_NOPE + MLA_ROPE) ** -0.5 * LOG2E
    wq = (w_uq * scale).reshape(MLA_Q_LORA, HEADS, MLA_NOPE + MLA_ROPE)
    w_main = _head_blocks(wq[..., :MLA_NOPE], wq[..., MLA_NOPE:])
    w_swap = _head_blocks(None, _swap_halves(wq[..., MLA_NOPE:], MLA_ROPE))
    wkv = w_ukv.reshape(MLA_KV_LORA, HEADS, MLA_NOPE + HEAD_DIM)
    w_kn = wkv[..., :MLA_NOPE].reshape(MLA_KV_LORA, HEADS * MLA_NOPE)
    w_v = wkv[..., MLA_NOPE:].reshape(MLA_KV_LORA, HEADS * HEAD_DIM)
    return tuple(w.astype(BF16) for w in (w_a_cat, w_main, w_swap, w_kn)) + (w_v,)


def _swa_weights(w_in):
    qd = HEADS * HEAD_DIM
    kd = SWA_KV_HEADS * HEAD_DIM
    q = w_in[:, :qd] * (HEAD_DIM ** -0.5)
    dup = lambda w: jnp.concatenate([w[:, :HEAD_DIM], w[:, :HEAD_DIM], w[:, HEAD_DIM:], w[:, HEAD_DIM:]], axis=1)
    return jnp.concatenate([q, dup(w_in[:, qd:qd + kd]), dup(w_in[:, qd + kd:])], axis=1).astype(BF16)


def _fox_weights(w_in, b_f):
    hd = HEADS * HEAD_DIM
    d = w_in.shape[0]
    w_q = (w_in[:, :hd] * (HEAD_DIM ** -0.5 * LOG2E)).astype(BF16)
    w_k = w_in[:, hd:2 * hd].astype(BF16)
    w_v = w_in[:, 2 * hd:3 * hd]
    w_gate = w_in[:, 3 * hd:]
    w_fg = jnp.concatenate([w_gate] * GATE_SLOTS + [jnp.zeros((d, LANES - GATE_SLOTS * HEADS), F32)],
                           axis=1).astype(BF16)
    b_fg = jnp.concatenate([b_f] * GATE_SLOTS + [jnp.zeros((LANES - GATE_SLOTS * HEADS,), F32)])[None, :]
    return w_q, w_k, w_v, w_fg, b_fg


def kernel(x, meta_tokens, ln1_g, ln1_b, ln2_g, ln2_b, fox_w_in, fox_b_f, fox_w_o, swa_w_in, swa_sinks, swa_w_o,
           mla_w_a, mla_g_q, mla_g_kv, mla_w_uq, mla_w_ukv, mla_w_o, ffn_w_in, ffn_conv_w, ffn_conv_b, ffn_w_out):
    batch, seq, d = x.shape
    assert d == D_MODEL and seq % 256 == 0
    lp = seq + FIRST_REAL
    n = batch * lp

    tm = _tile(lp, 768, 256)
    tq = _tile(lp, 768, 256)
    tk = 256
    tn = 512
    nk = lp // tk

    h0 = jnp.concatenate([jnp.zeros((batch, LEAD, d), x.dtype),
                          jnp.broadcast_to(meta_tokens.astype(x.dtype)[None], (batch, N_META, d)), x], axis=1)
    hf = h0.reshape(n, d)
    hb = hf.astype(BF16)

    cos_p, sin_p = _rope_tables(lp, ROPE_DIM, ROPE_THETA, HEAD_DIM, 0)
    cos_m, sin_m = _rope_tables(lp, MLA_ROPE, MLA_ROPE_THETA, LANES, MLA_NOPE)
    b3 = lambda a: a.reshape(batch, lp, -1)

    for i in range(DEPTH):
        kind, j = i % 3, i // 3
        if kind == 0:
            w_q, w_k, w_v, w_fg, b_fg = _fox_weights(fox_w_in[j], fox_b_f[j])
            aq, ak = _fox_gate(hb, w_fg, b_fg, lp, tm)
            qh = _head_proj(hb, w_q, aq, tm, tn)
            kh = _head_proj(hb, w_k, ak, tm, tn)
            vt = _matmul_t(hb, *_value_weights(w_v), tm, V_COLS_TILE, tk).reshape(batch, nk, HEADS * V_ROWS, tk)
            o = _flash_attention(b3(qh), b3(kh), vt, batch, lp, tq, tk)
            w_o = fox_w_o[j]
        elif kind == 1:
            qkv = _swa_proj(hb, _swa_weights(swa_w_in[j]), cos_p, sin_p, lp, tm, 2 * LANES, 5)
            o = _swa_attention(b3(qkv), swa_sinks[j].astype(F32), batch, lp)
            w_o = swa_w_o[j]
        else:
            w_a_cat, w_main, w_swap, w_kn, w_v = _mla_weights(mla_w_a[j], mla_w_uq[j], mla_w_ukv[j])
            cq, ckv, kr = _mla_a(hb, w_a_cat, mla_g_q[j][None, :], mla_g_kv[j][None, :], cos_m, sin_m, lp, tm)
            qh = _mla_q(cq, w_main, w_swap, cos_m, sin_m, lp, tm, tn)
            kh = _head_proj(ckv, w_kn, kr, tm, tn)
            vt = _matmul_t(ckv, *_value_weights(w_v), tm, V_COLS_TILE, tk).reshape(batch, nk, HEADS * V_ROWS, tk)
            o = _flash_attention(b3(qh), b3(kh), vt, batch, lp, tq, tk)
            w_o = mla_w_o[j]
        hf, hb = _oproj_ln(o.reshape(n, d), w_o.astype(BF16), hf, ln1_g[i][None, :], ln1_b[i][None, :], tm)
        hf, hb = _ffn(hb, hf, *_ffn_weights(ffn_w_in[i], ffn_conv_w[i], ffn_conv_b[i], ffn_w_out[i]),
                      ln2_g[i][None, :], ln2_b[i][None, :], lp, tm)
    return hf.reshape(batch, lp, d)[:, FIRST_REAL:]
```

```python
import functools
import math

import numpy as np
import jax
import jax.numpy as jnp
from jax import lax
from jax.experimental import pallas as pl
from jax.experimental.pallas import tpu as pltpu

F32 = jnp.float32
BF16 = jnp.bfloat16

D_MODEL = 1024
DEPTH = 4
N_META = 16
LEAD = 240
FIRST_REAL = LEAD + N_META
NEG = -1e30
DEEPNORM_ALPHA = (2.0 * DEPTH) ** 0.25
LN_EPS = 1e-5
RMS_EPS = 1e-6
HEADS = 16
HEAD_DIM = 64
PAIRS = HEADS // 2
SWA_KV_HEADS = 2
WINDOW = 128
ROPE_THETA = 500000.0
ROPE_DIM = 16
MLA_Q_LORA = 384
MLA_KV_LORA = 256
MLA_NOPE = 64
MLA_ROPE = 32
MLA_ROPE_THETA = 10000.0
D_FF = 2816
CONV_W = 3
LOG2E = math.log2(math.e)

LANES = 128
BF16_SUBLANES = 16
VMEM_LIMIT = 56 * 1024 * 1024

GATE_SLOTS = 3
FOX_DEAD_LANE = HEAD_DIM + 2 * GATE_SLOTS
MLA_DEAD_LANE = MLA_NOPE + MLA_ROPE
M_INIT = -3e38


def _params(*sem):
    return pltpu.CompilerParams(dimension_semantics=sem, vmem_limit_bytes=VMEM_LIMIT)


def _tile(n, pref, mult):
    best = mult
    t = mult
    while t <= min(n, pref):
        if n % t == 0:
            best = t
        t += mult
    assert n % best == 0
    return best


def _dot(a, b):
    return jnp.dot(a, b, preferred_element_type=F32)


def _dot_nt(a, b):
    return lax.dot_general(a, b, (((1,), (1,)), ((), ())), preferred_element_type=F32)


def _layer_norm(x, g, b):
    mu = jnp.mean(x, axis=-1, keepdims=True)
    xc = x - mu
    var = jnp.mean(xc * xc, axis=-1, keepdims=True)
    return xc * lax.rsqrt(var + LN_EPS) * g + b


def _one_hot_lanes(width, period, lane):
    idx = lax.broadcasted_iota(jnp.int32, (1, width), 1)
    return jnp.where((idx & (period - 1)) == lane, 1.0, 0.0)


def _dead_rows(tile_in_batch, tm):
    pos = tile_in_batch * tm + lax.broadcasted_iota(jnp.int32, (tm, 1), 0)
    return jnp.where(pos < LEAD, NEG, 0.0)


def _mm_kernel(x_ref, w_ref, o_ref):
    o_ref[...] = _dot(x_ref[...], w_ref[...]).astype(o_ref.dtype)


def _matmul(x, w, out_dtype, tm, tn):
    n, k = x.shape
    m = w.shape[1]
    return pl.pallas_call(
        _mm_kernel,
        grid=(n // tm, m // tn),
        in_specs=[pl.BlockSpec((tm, k), lambda i, j: (i, 0)),
                  pl.BlockSpec((k, tn), lambda i, j: (0, j))],
        out_specs=pl.BlockSpec((tm, tn), lambda i, j: (i, j)),
        out_shape=jax.ShapeDtypeStruct((n, m), out_dtype),
        compiler_params=_params("parallel", "parallel"),
        name="matmul",
    )(x, w)


def _head_proj_kernel(x_ref, w_ref, e_ref, o_ref):
    y = _dot(x_ref[...], w_ref[...])
    lo = lax.broadcasted_iota(jnp.int32, (1, LANES), 1) < HEAD_DIM
    shared = e_ref.shape[1] == LANES
    for pair in range(y.shape[1] // LANES):
        y_pair = y[:, pair * LANES:(pair + 1) * LANES]
        for a, feats in enumerate((y_pair, pltpu.roll(y_pair, HEAD_DIM, 1))):
            h = 2 * pair + a
            extra = e_ref[...] if shared else e_ref[:, h * LANES:(h + 1) * LANES]
            o_ref[:, h * LANES:(h + 1) * LANES] = jnp.where(lo, feats, extra.astype(F32)).astype(o_ref.dtype)


def _head_proj(x, w, e, tm, tn):
    n, k = x.shape
    m = w.shape[1]
    if e.shape[1] == LANES:
        e_spec = pl.BlockSpec((tm, LANES), lambda i, j: (i, 0))
    else:
        assert e.shape[1] == 2 * m
        e_spec = pl.BlockSpec((tm, 2 * tn), lambda i, j: (i, j))
    return pl.pallas_call(
        _head_proj_kernel,
        grid=(n // tm, m // tn),
        in_specs=[pl.BlockSpec((tm, k), lambda i, j: (i, 0)),
                  pl.BlockSpec((k, tn), lambda i, j: (0, j)),
                  e_spec],
        out_specs=pl.BlockSpec((tm, 2 * tn), lambda i, j: (i, j)),
        out_shape=jax.ShapeDtypeStruct((n, 2 * m), BF16),
        compiler_params=_params("parallel", "parallel"),
        name="head_proj",
    )(x, w, e)


def _mm_t_kernel(x_ref, w_ref, b_ref, o_ref, *, tk):
    y = _dot(x_ref[...], w_ref[...]) + b_ref[...]
    for c in range(o_ref.shape[0]):
        o_ref[c] = y[c * tk:(c + 1) * tk, :].T.astype(o_ref.dtype)


def _matmul_t(x, w, bias, tm, tn, tk):
    n, k = x.shape
    m = w.shape[1]
    r = tm // tk
    return pl.pallas_call(
        functools.partial(_mm_t_kernel, tk=tk),
        grid=(n // tm, m // tn),
        in_specs=[pl.BlockSpec((tm, k), lambda i, j: (i, 0)),
                  pl.BlockSpec((k, tn), lambda i, j: (0, j)),
                  pl.BlockSpec((1, tn), lambda i, j: (0, j))],
        out_specs=pl.BlockSpec((r, tn, tk), lambda i, j: (i, j, 0)),
        out_shape=jax.ShapeDtypeStruct((n // tk, m, tk), BF16),
        compiler_params=_params("parallel", "parallel"),
        name="matmul_t",
    )(x, w, bias)


V_ROWS = HEAD_DIM + BF16_SUBLANES
V_COLS_TILE = 8 * V_ROWS


def _value_weights(w_v):
    rows = w_v.shape[0]
    w = jnp.concatenate([w_v.reshape(rows, HEADS, HEAD_DIM),
                         jnp.zeros((rows, HEADS, V_ROWS - HEAD_DIM), w_v.dtype)], axis=2)
    bias = np.zeros((HEADS, V_ROWS), np.float32)
    bias[:, HEAD_DIM] = 1.0
    return w.reshape(rows, HEADS * V_ROWS).astype(BF16), jnp.asarray(bias.reshape(1, HEADS * V_ROWS))


def _gate_placement():
    wide = HEADS * LANES
    sq = np.zeros((LANES, wide), np.float32)
    sk = np.zeros_like(sq)
    oq = np.zeros((1, wide), np.float32)
    ok = np.zeros_like(oq)
    for h in range(HEADS):
        base = h * LANES + HEAD_DIM
        for part in range(GATE_SLOTS):
            sq[part * HEADS + h, base + part] = 1.0
            sk[part * HEADS + h, base + GATE_SLOTS + part] = -1.0
            oq[0, base + GATE_SLOTS + part] = 1.0
            ok[0, base + part] = 1.0
        oq[0, h * LANES + FOX_DEAD_LANE] = 1.0
    return sq, sk, oq, ok


def _split3(x):
    hi = x.astype(BF16)
    r1 = x - hi.astype(F32)
    mid = r1.astype(BF16)
    lo = (r1 - mid.astype(F32)).astype(BF16)
    return hi, mid, lo


def _fox_gate_kernel(x_ref, w_ref, b_ref, sq_ref, sk_ref, oq_ref, ok_ref, aq_ref, ak_ref, carry_ref,
                     *, tm, tiles_per_batch):
    i = pl.program_id(0)

    @pl.when(i % tiles_per_batch == 0)
    def _():
        carry_ref[...] = jnp.zeros_like(carry_ref)

    fg = _dot(x_ref[...], w_ref[...]) + b_ref[...]
    logf = jnp.minimum(fg, 0.0) - jnp.log(1.0 + jnp.exp(-jnp.abs(fg)))
    row = lax.broadcasted_iota(jnp.int32, (tm, tm), 0)
    col = lax.broadcasted_iota(jnp.int32, (tm, tm), 1)
    tri = jnp.where(col <= row, 1.0, 0.0).astype(BF16)
    hi, mid, lo = _split3(logf)
    cs = _dot(tri, hi) + _dot(tri, mid) + _dot(tri, lo) + carry_ref[...]
    carry_ref[...] = cs[tm - 1:tm, :]
    lane = lax.broadcasted_iota(jnp.int32, (1, LANES), 1)
    hi, mid, lo = _split3(cs * LOG2E)
    parts = jnp.where(lane < HEADS, hi.astype(F32),
                      jnp.where(lane < 2 * HEADS, mid.astype(F32), lo.astype(F32))).astype(BF16)
    wide = aq_ref.shape[1]
    dead = _dead_rows(i % tiles_per_batch, tm) * _one_hot_lanes(wide, LANES, FOX_DEAD_LANE)
    aq_ref[...] = (_dot(parts, sq_ref[...]) + oq_ref[...]).astype(aq_ref.dtype)
    ak_ref[...] = (_dot(parts, sk_ref[...]) + ok_ref[...] + dead).astype(ak_ref.dtype)


def _fox_gate(hb, w_fg, b_fg, lp, tm):
    n = hb.shape[0]
    tpb = lp // tm
    sq, sk, oq, ok = _gate_placement()
    fixed = lambda i: (0, 0)
    wide = HEADS * LANES
    return pl.pallas_call(
        functools.partial(_fox_gate_kernel, tm=tm, tiles_per_batch=tpb),
        grid=(n // tm,),
        in_specs=[pl.BlockSpec((tm, D_MODEL), lambda i: (i, 0)),
                  pl.BlockSpec((D_MODEL, LANES), fixed),
                  pl.BlockSpec((1, LANES), fixed),
                  pl.BlockSpec((LANES, wide), fixed),
                  pl.BlockSpec((LANES, wide), fixed),
                  pl.BlockSpec((1, wide), fixed),
                  pl.BlockSpec((1, wide), fixed)],
        out_specs=[pl.BlockSpec((tm, wide), lambda i: (i, 0)),
                   pl.BlockSpec((tm, wide), lambda i: (i, 0))],
        out_shape=[jax.ShapeDtypeStruct((n, wide), BF16), jax.ShapeDtypeStruct((n, wide), BF16)],
        scratch_shapes=[pltpu.VMEM((1, LANES), F32)],
        compiler_params=_params("arbitrary"),
        name="fox_gate",
    )(hb, w_fg, b_fg, jnp.asarray(sq, BF16), jnp.asarray(sk, BF16), jnp.asarray(oq), jnp.asarray(ok))


def _flash_kernel(q_ref, k_ref, vt_ref, o_ref, sa_ref, sb_ref, xa_ref, xb_ref, m_ref, acc_ref, *, tq, tk):
    i = pl.program_id(2)
    r = tq // tk
    m_ref[...] = jnp.full_like(m_ref, M_INIT)
    acc_ref[...] = jnp.zeros_like(acc_ref)

    def diag_mask(s):
        keep = lax.broadcasted_iota(jnp.int32, s.shape, 0) <= lax.broadcasted_iota(jnp.int32, s.shape, 1)
        return jnp.where(keep, s, NEG)

    def scores(s_ref, x_ref, j, a, qs, diagonal, blk=None):
        off = pl.multiple_of(j * tk, tk)
        q0 = pl.multiple_of((i if blk is None else blk) * tq + qs, tk)
        s = _dot_nt(k_ref[0, pl.ds(off, tk), a * LANES:(a + 1) * LANES],
                    q_ref[0, pl.ds(q0, tq - qs), a * LANES:(a + 1) * LANES])
        if diagonal:
            s = diag_mask(s)
        s_ref[a, :, qs:] = s
        x_ref[a, :, qs:] = jnp.max(s, axis=0, keepdims=True)

    def consume(s_ref, x_ref, j, a, qs, mask_now):
        s = s_ref[a, :, qs:]
        if mask_now:
            s = diag_mask(s)
            smax = jnp.max(s, axis=0, keepdims=True)
        else:
            smax = x_ref[a, :, qs:]
        m_old = m_ref[a, :, qs:]
        m_new = jnp.maximum(m_old, smax)
        alpha = jnp.exp2(m_old - m_new)
        p = jnp.exp2(s - m_new)
        m_ref[a, :, qs:] = m_new
        pv = _dot(vt_ref[0, j, a * V_ROWS:(a + 1) * V_ROWS, :], p.astype(BF16))
        acc_ref[a, :, qs:] = alpha * acc_ref[a, :, qs:] + pv

    buf_a, buf_b = (sa_ref, xa_ref), (sb_ref, xb_ref)
    jdiag = i * r
    odd = jdiag & 1

    @pl.when(i == 0)
    def _():
        for a in range(2):
            scores(*buf_a, 0, a, 0, False)

    @pl.when(odd == 1)
    def _():
        for a in range(2):
            scores(*buf_b, 1, a, 0, False)
            consume(*buf_a, 0, a, 0, False)
        sa_ref[...] = sb_ref[...]
        xa_ref[...] = xb_ref[...]

    def pair(j):
        for a in range(2):
            scores(*buf_b, j + 1, a, 0, False)
            consume(*buf_a, j, a, 0, False)
        for a in range(2):
            scores(*buf_a, j + 2, a, 0, False)
            consume(*buf_b, j + 1, a, 0, False)

    pairs = (jdiag - odd) // 2
    one = pairs & 1
    two = (pairs >> 1) & 1

    @pl.when(one == 1)
    def _():
        pair(odd)

    @pl.when(two == 1)
    def _():
        j = odd + 2 * one
        pair(j)
        pair(j + 2)

    def octo_body(t, c):
        j = odd + 2 * one + 4 * two + 8 * t
        for u in range(4):
            pair(j + 2 * u)
        return c

    lax.fori_loop(0, pairs >> 2, octo_body, 0)

    bufs = (buf_a, buf_b)
    for d in range(r):
        qs = d * tk
        for a in range(2):
            if d + 1 < r:
                scores(*bufs[(d + 1) & 1], jdiag + d + 1, a, qs + tk, True)
            consume(*bufs[d & 1], jdiag + d, a, qs, d == 0)

    for a in range(2):
        scores(*buf_a, 0, a, 0, False, blk=jnp.minimum(i + 1, pl.num_programs(2) - 1))

    ot = jnp.concatenate([acc_ref[a, :HEAD_DIM, :] / acc_ref[a, HEAD_DIM:HEAD_DIM + 1, :] for a in range(2)],
                         axis=0)
    o_ref[0] = ot.T.astype(o_ref.dtype)


def _flash_attention(qh, kh, vt, batch, lp, tq, tk):
    nk = lp // tk
    return pl.pallas_call(
        functools.partial(_flash_kernel, tq=tq, tk=tk),
        grid=(batch, PAIRS, lp // tq),
        in_specs=[pl.BlockSpec((1, lp, 2 * LANES), lambda b, p, i: (b, 0, p)),
                  pl.BlockSpec((1, lp, 2 * LANES), lambda b, p, i: (b, 0, p)),
                  pl.BlockSpec((1, nk, 2 * V_ROWS, tk), lambda b, p, i: (b, 0, p, 0))],
        out_specs=pl.BlockSpec((1, tq, 2 * HEAD_DIM), lambda b, p, i: (b, i, p)),
        out_shape=jax.ShapeDtypeStruct((batch, lp, HEADS * HEAD_DIM), BF16),
        scratch_shapes=[pltpu.VMEM((2, tk, tq), F32), pltpu.VMEM((2, tk, tq), F32),
                        pltpu.VMEM((2, 1, tq), F32), pltpu.VMEM((2, 1, tq), F32),
                        pltpu.VMEM((2, 1, tq), F32), pltpu.VMEM((2, V_ROWS, tq), F32)],
        compiler_params=_params("parallel", "parallel", "arbitrary"),
        name="flash_attention",
    )(qh, kh, vt)


SWA_TQ = 128


def _swa_attn_kernel(sink_ref, q_ref, km_ref, kp_ref, kc_ref, vm_ref, vp_ref, vc_ref, o_ref):
    i = pl.program_id(1)
    t = SWA_TQ
    lo = lax.broadcasted_iota(jnp.int32, (1, LANES), 1) < HEAD_DIM
    row = lax.broadcasted_iota(jnp.int32, (t, 3 * t), 0)
    col = lax.broadcasted_iota(jnp.int32, (t, 3 * t), 1)
    qpos = i * t + row
    kpos = jnp.where(col < t, t + col, (i - 2) * t + col)
    d = qpos - kpos
    valid = (d >= 0) & (((col < t) & (kpos >= LEAD)) |
                        ((col >= t) & (d < WINDOW) & (kpos >= FIRST_REAL)))
    kcat, v_lo, v_hi = [], [], []
    for g in range(SWA_KV_HEADS):
        sl = slice(g * LANES, (g + 1) * LANES)
        kcat.append(jnp.concatenate([km_ref[0, :, sl], kp_ref[0, :, sl], kc_ref[0, :, sl]], axis=0))
        vf = jnp.concatenate([vm_ref[0, :, sl], vp_ref[0, :, sl], vc_ref[0, :, sl]], axis=0).astype(F32)
        v_lo.append(jnp.where(lo, vf, 0.0).astype(BF16))
        v_hi.append(jnp.where(lo, 0.0, vf).astype(BF16))
    pairs_per_group = PAIRS // SWA_KV_HEADS

    def logits(p):
        qf = q_ref[0, :, p * LANES:(p + 1) * LANES].astype(F32)
        q_pair = (jnp.where(lo, qf, 0.0).astype(BF16), jnp.where(lo, 0.0, qf).astype(BF16))
        return [_dot_nt(q_pair[a], kcat[p // pairs_per_group]) for a in range(2)]

    def finish(p, s_pair):
        g = p // pairs_per_group
        ps, inv = [], []
        for a in range(2):
            sink = sink_ref[2 * p + a]
            s = jnp.where(valid, s_pair[a], NEG)
            m = jnp.maximum(jnp.max(s, axis=1, keepdims=True), sink)
            e = jnp.exp(s - m)
            den = jnp.sum(e, axis=1, keepdims=True) + jnp.exp(sink - m)
            ps.append(e.astype(BF16))
            inv.append(1.0 / den)
        o = (_dot(ps[0], v_lo[g]) + _dot(ps[1], v_hi[g])) * jnp.where(lo, inv[0], inv[1])
        o_ref[0, :, p * LANES:(p + 1) * LANES] = o.astype(o_ref.dtype)

    s_next = logits(0)
    for p in range(PAIRS):
        s_cur = s_next
        if p + 1 < PAIRS:
            s_next = logits(p + 1)
        finish(p, s_cur)


def _swa_attention(qkv, sinks, batch, lp):
    t = SWA_TQ
    kblk, vblk = 4, 5
    kv_spec = lambda col, row_of: pl.BlockSpec((1, t, 2 * LANES), lambda b, i: (b, row_of(i), col))
    meta = lambda i: 1
    prev = lambda i: jnp.maximum(i - 1, 0)
    cur = lambda i: i
    return pl.pallas_call(
        _swa_attn_kernel,
        grid=(batch, lp // t),
        in_specs=[pl.BlockSpec(memory_space=pltpu.SMEM),
                  pl.BlockSpec((1, t, HEADS * HEAD_DIM), lambda b, i: (b, i, 0)),
                  kv_spec(kblk, meta), kv_spec(kblk, prev), kv_spec(kblk, cur),
                  kv_spec(vblk, meta), kv_spec(vblk, prev), kv_spec(vblk, cur)],
        out_specs=pl.BlockSpec((1, t, HEADS * HEAD_DIM), lambda b, i: (b, i, 0)),
        out_shape=jax.ShapeDtypeStruct((batch, lp, HEADS * HEAD_DIM), BF16),
        compiler_params=_params("parallel", "parallel"),
        name="swa_attention",
    )(sinks, qkv, qkv, qkv, qkv, qkv, qkv, qkv)


def _swa_proj_kernel(x_ref, w_ref, cos_ref, sin_ref, o_ref, *, tn, n_rope_blocks):
    j = pl.program_id(1)
    y = _dot(x_ref[...], w_ref[...])

    @pl.when(j < n_rope_blocks)
    def _():
        reps = tn // LANES
        cos = jnp.concatenate([cos_ref[...]] * reps, axis=1)
        sin = jnp.concatenate([sin_ref[...]] * reps, axis=1)
        lane = lax.broadcasted_iota(jnp.int32, (1, tn), 1)
        half = ROPE_DIM // 2
        partner = jnp.where((lane & (HEAD_DIM - 1)) < half,
                            pltpu.roll(y, tn - half, 1),
                            pltpu.roll(y, half, 1))
        o_ref[...] = (y * cos + partner * sin).astype(o_ref.dtype)

    @pl.when(j >= n_rope_blocks)
    def _():
        o_ref[...] = y.astype(o_ref.dtype)


def _swa_proj(hb, w, cos, sin, lp, tm, tn, n_rope_blocks):
    n = hb.shape[0]
    m = w.shape[1]
    tpb = lp // tm
    return pl.pallas_call(
        functools.partial(_swa_proj_kernel, tn=tn, n_rope_blocks=n_rope_blocks),
        grid=(n // tm, m // tn),
        in_specs=[pl.BlockSpec((tm, D_MODEL), lambda i, j: (i, 0)),
                  pl.BlockSpec((D_MODEL, tn), lambda i, j: (0, j)),
                  pl.BlockSpec((tm, LANES), lambda i, j: (i % tpb, 0)),
                  pl.BlockSpec((tm, LANES), lambda i, j: (i % tpb, 0))],
        out_specs=pl.BlockSpec((tm, tn), lambda i, j: (i, j)),
        out_shape=jax.ShapeDtypeStruct((n, m), BF16),
        compiler_params=_params("parallel", "parallel"),
        name="swa_proj",
    )(hb, w, cos, sin)


MLA_A_COLS = MLA_Q_LORA + MLA_KV_LORA + 2 * LANES


def _mla_a_kernel(x_ref, w_ref, gq_ref, gkv_ref, cos_ref, sin_ref, cq_ref, ckv_ref, kr_ref, *, tm, tiles_per_batch):
    y = _dot(x_ref[...], w_ref[...])
    cq = y[:, :MLA_Q_LORA]
    ckv = y[:, MLA_Q_LORA:MLA_Q_LORA + MLA_KV_LORA]
    kr = y[:, MLA_Q_LORA + MLA_KV_LORA:MLA_Q_LORA + MLA_KV_LORA + LANES]
    krs = y[:, MLA_Q_LORA + MLA_KV_LORA + LANES:]
    rms = lambda z, g: z * lax.rsqrt(jnp.mean(z * z, axis=-1, keepdims=True) + RMS_EPS) * g
    cq_ref[...] = rms(cq, gq_ref[...]).astype(cq_ref.dtype)
    ckv_ref[...] = rms(ckv, gkv_ref[...]).astype(ckv_ref.dtype)
    dead = _dead_rows(pl.program_id(0) % tiles_per_batch, tm) * _one_hot_lanes(LANES, LANES, MLA_DEAD_LANE)
    kr_ref[...] = (kr * cos_ref[...] + krs * sin_ref[...] + dead).astype(kr_ref.dtype)


def _mla_a(hb, w, gq, gkv, cos, sin, lp, tm):
    n = hb.shape[0]
    tpb = lp // tm
    return pl.pallas_call(
        functools.partial(_mla_a_kernel, tm=tm, tiles_per_batch=tpb),
        grid=(n // tm,),
        in_specs=[pl.BlockSpec((tm, D_MODEL), lambda i: (i, 0)),
                  pl.BlockSpec((D_MODEL, MLA_A_COLS), lambda i: (0, 0)),
                  pl.BlockSpec((1, MLA_Q_LORA), lambda i: (0, 0)),
                  pl.BlockSpec((1, MLA_KV_LORA), lambda i: (0, 0)),
                  pl.BlockSpec((tm, LANES), lambda i: (i % tpb, 0)),
                  pl.BlockSpec((tm, LANES), lambda i: (i % tpb, 0))],
        out_specs=[pl.BlockSpec((tm, MLA_Q_LORA), lambda i: (i, 0)),
                   pl.BlockSpec((tm, MLA_KV_LORA), lambda i: (i, 0)),
                   pl.BlockSpec((tm, LANES), lambda i: (i, 0))],
        out_shape=[jax.ShapeDtypeStruct((n, MLA_Q_LORA), BF16),
                   jax.ShapeDtypeStruct((n, MLA_KV_LORA), BF16),
                   jax.ShapeDtypeStruct((n, LANES), BF16)],
        compiler_params=_params("parallel"),
        name="mla_a",
    )(hb, w, gq, gkv, cos, sin)


def _mla_q_kernel(x_ref, w_ref, ws_ref, cos_ref, sin_ref, o_ref):
    x = x_ref[...]
    reps = o_ref.shape[1] // LANES
    cos = jnp.concatenate([cos_ref[...]] * reps, axis=1)
    sin = jnp.concatenate([sin_ref[...]] * reps, axis=1)
    y = _dot(x, w_ref[...]) * cos + _dot(x, ws_ref[...]) * sin
    o_ref[...] = (y + _one_hot_lanes(o_ref.shape[1], LANES, MLA_DEAD_LANE)).astype(o_ref.dtype)


def _mla_q(cq, w, ws, cos, sin, lp, tm, tn):
    n = cq.shape[0]
    m = w.shape[1]
    tpb = lp // tm
    return pl.pallas_call(
        _mla_q_kernel,
        grid=(n // tm, m // tn),
        in_specs=[pl.BlockSpec((tm, MLA_Q_LORA), lambda i, j: (i, 0)),
                  pl.BlockSpec((MLA_Q_LORA, tn), lambda i, j: (0, j)),
                  pl.BlockSpec((MLA_Q_LORA, tn), lambda i, j: (0, j)),
                  pl.BlockSpec((tm, LANES), lambda i, j: (i % tpb, 0)),
                  pl.BlockSpec((tm, LANES), lambda i, j: (i % tpb, 0))],
        out_specs=pl.BlockSpec((tm, tn), lambda i, j: (i, j)),
        out_shape=jax.ShapeDtypeStruct((n, m), BF16),
        compiler_params=_params("parallel", "parallel"),
        name="mla_q",
    )(cq, w, ws, cos, sin)


def _oproj_ln_kernel(o_ref, w_ref, h_ref, g_ref, b_ref, hf_ref, hb_ref):
    x = DEEPNORM_ALPHA * h_ref[...] + _dot(o_ref[...], w_ref[...])
    y = _layer_norm(x, g_ref[...], b_ref[...])
    hf_ref[...] = y
    hb_ref[...] = y.astype(hb_ref.dtype)


def _oproj_ln(o, w, h, g, b, tm):
    n = o.shape[0]
    row = lambda i: (i, 0)
    fixed = lambda i: (0, 0)
    return pl.pallas_call(
        _oproj_ln_kernel,
        grid=(n // tm,),
        in_specs=[pl.BlockSpec((tm, D_MODEL), row), pl.BlockSpec((D_MODEL, D_MODEL), fixed),
                  pl.BlockSpec((tm, D_MODEL), row), pl.BlockSpec((1, D_MODEL), fixed),
                  pl.BlockSpec((1, D_MODEL), fixed)],
        out_specs=[pl.BlockSpec((tm, D_MODEL), row), pl.BlockSpec((tm, D_MODEL), row)],
        out_shape=[jax.ShapeDtypeStruct((n, D_MODEL), F32), jax.ShapeDtypeStruct((n, D_MODEL), BF16)],
        compiler_params=_params("parallel"),
        name="oproj_ln",
    )(o, w, h, g, b)


FFN_HALO = BF16_SUBLANES


FFN_CHUNK = 256
FFN_NC = D_FF // FFN_CHUNK
assert FFN_NC * FFN_CHUNK == D_FF and FFN_NC % 2 == 1


def _ffn_kernel(x_ref, halo_ref, h_ref, win_ref, cw_ref, cb_ref, wo_ref, g_ref, b_ref, hf_ref, hb_ref,
                xext_ref, ua_ref, ub_ref, acc_ref, *, tm, tiles_per_batch):
    i = pl.program_id(0)
    nc = FFN_NC
    pos = (i % tiles_per_batch) * tm - FFN_HALO + lax.broadcasted_iota(jnp.int32, (tm + FFN_HALO, 1), 0)
    xe = jnp.concatenate([halo_ref[...], x_ref[...]], axis=0).astype(F32)
    xext_ref[...] = jnp.where(pos >= LEAD, xe, 0.0).astype(BF16)

    def up(u_ref, c):
        u_ref[0] = _dot(xext_ref[...], win_ref[c])
        u_ref[1] = _dot(xext_ref[...], win_ref[nc + c])

    def glu(u_ref, c):
        def conv(part, idx):
            u = u_ref[part]
            delayed = u * cw_ref[idx, 0:1, :]
            for tap in range(1, CONV_W):
                delayed = u * cw_ref[idx, tap:tap + 1, :] + pltpu.roll(delayed, 1, 0)
            return cb_ref[idx] + delayed[FFN_HALO:, :]

        yg = conv(0, c)
        yv = conv(1, nc + c)
        return ((yg / (1.0 + jnp.exp(-yg))) * yv).astype(BF16)

    acc_ref[...] = DEEPNORM_ALPHA * h_ref[...]
    up(ua_ref, 0)

    def pair_body(t, carry):
        c = 2 * t
        up(ub_ref, c + 1)
        act_a = glu(ua_ref, c)
        up(ua_ref, c + 2)
        act_b = glu(ub_ref, c + 1)
        acc_ref[...] += _dot(act_a, wo_ref[c]) + _dot(act_b, wo_ref[c + 1])
        return carry

    lax.fori_loop(0, (nc - 1) // 2, pair_body, 0)
    y = _layer_norm(acc_ref[...] + _dot(glu(ua_ref, nc - 1), wo_ref[nc - 1]), g_ref[...], b_ref[...])
    hf_ref[...] = y
    hb_ref[...] = y.astype(hb_ref.dtype)


def _ffn(hb, hf, w_in, conv_w, conv_b, w_out, g, b, lp, tm):
    n = hb.shape[0]
    nc, fc = FFN_NC, FFN_CHUNK
    tpb = lp // tm
    halo_blocks = tm // FFN_HALO
    row = lambda i: (i, 0)
    fixed2 = lambda i: (0, 0)
    fixed3 = lambda i: (0, 0, 0)
    resident = dict(pipeline_mode=pl.Buffered(1))
    return pl.pallas_call(
        functools.partial(_ffn_kernel, tm=tm, tiles_per_batch=tpb),
        grid=(n // tm,),
        in_specs=[pl.BlockSpec((tm, D_MODEL), row),
                  pl.BlockSpec((FFN_HALO, D_MODEL), lambda i: (jnp.maximum(i * halo_blocks - 1, 0), 0)),
                  pl.BlockSpec((tm, D_MODEL), row),
                  pl.BlockSpec((2 * nc, D_MODEL, fc), fixed3, **resident),
                  pl.BlockSpec((2 * nc, CONV_W, fc), fixed3, **resident),
                  pl.BlockSpec((2 * nc, 1, fc), fixed3, **resident),
                  pl.BlockSpec((nc, fc, D_MODEL), fixed3, **resident),
                  pl.BlockSpec((1, D_MODEL), fixed2), pl.BlockSpec((1, D_MODEL), fixed2)],
        out_specs=[pl.BlockSpec((tm, D_MODEL), row), pl.BlockSpec((tm, D_MODEL), row)],
        out_shape=[jax.ShapeDtypeStruct((n, D_MODEL), F32), jax.ShapeDtypeStruct((n, D_MODEL), BF16)],
        scratch_shapes=[pltpu.VMEM((tm + FFN_HALO, D_MODEL), BF16),
                        pltpu.VMEM((2, tm + FFN_HALO, fc), F32), pltpu.VMEM((2, tm + FFN_HALO, fc), F32),
                        pltpu.VMEM((tm, D_MODEL), F32)],
        compiler_params=_params("parallel"),
        name="ffn",
    )(hb, hb, hf, w_in, conv_w, conv_b, w_out, g, b)


def _ffn_weights(w_in, conv_w, conv_b, w_out):
    d = w_in.shape[0]
    chunks = 2 * FFN_NC
    return (w_in.reshape(d, chunks, FFN_CHUNK).transpose(1, 0, 2).astype(BF16),
            conv_w.reshape(CONV_W, chunks, FFN_CHUNK).transpose(1, 0, 2),
            conv_b.reshape(chunks, 1, FFN_CHUNK),
            w_out.reshape(FFN_NC, FFN_CHUNK, d).astype(BF16))


def _rope_tables(lp, dim, theta, group, offset):
    pos = (jnp.arange(lp) - LEAD).astype(F32)
    inv = theta ** (-jnp.arange(0, dim, 2, dtype=F32) / dim)
    ang = pos[:, None] * inv[None, :]
    cos, sin = jnp.cos(ang), jnp.sin(ang)
    ones = lambda w: jnp.ones((lp, w), F32)
    zeros = lambda w: jnp.zeros((lp, w), F32)
    rest = group - offset - dim
    cos_g = jnp.concatenate([ones(offset), cos, cos, ones(rest)], axis=1)
    sin_g = jnp.concatenate([zeros(offset), -sin, sin, zeros(rest)], axis=1)
    reps = LANES // group
    return jnp.tile(cos_g, (1, reps)), jnp.tile(sin_g, (1, reps))


def _swap_halves(w, dim):
    return jnp.concatenate([w[..., dim // 2:dim], w[..., :dim // 2]], axis=-1)


def _head_blocks(main, extra):
    src = main if main is not None else extra
    rows, heads = src.shape[0], src.shape[1]
    m = main if main is not None else jnp.zeros((rows, heads, HEAD_DIM), F32)
    e = extra if extra is not None else jnp.zeros((rows, heads, 0), F32)
    pad = jnp.zeros((rows, heads, LANES - HEAD_DIM - e.shape[2]), F32)
    return jnp.concatenate([m, e, pad], axis=2).reshape(rows, heads * LANES)


def _mla_weights(w_a, w_uq, w_ukv):
    d = w_a.shape[0]
    w_kr = w_a[:, MLA_Q_LORA + MLA_KV_LORA:][:, None, :]
    w_a_cat = jnp.concatenate([w_a[:, :MLA_Q_LORA + MLA_KV_LORA], _head_blocks(None, w_kr),
                               _head_blocks(None, _swap_halves(w_kr, MLA_ROPE))], axis=1)
    scale = (MLA_NOPE + MLA_ROPE) ** -0.5 * LOG2E
    wq = (w_uq * scale).reshape(MLA_Q_LORA, HEADS, MLA_NOPE + MLA_ROPE)
    w_main = _head_blocks(wq[..., :MLA_NOPE], wq[..., MLA_NOPE:])
    w_swap = _head_blocks(None, _swap_halves(wq[..., MLA_NOPE:], MLA_ROPE))
    wkv = w_ukv.reshape(MLA_KV_LORA, HEADS, MLA_NOPE + HEAD_DIM)
    w_kn = wkv[..., :MLA_NOPE].reshape(MLA_KV_LORA, HEADS * MLA_NOPE)
    w_v = wkv[..., MLA_NOPE:].reshape(MLA_KV_LORA, HEADS * HEAD_DIM)
    return tuple(w.astype(BF16) for w in (w_a_cat, w_main, w_swap, w_kn)) + (w_v,)


def _swa_weights(w_in):
    qd = HEADS * HEAD_DIM
    kd = SWA_KV_HEADS * HEAD_DIM
    q = w_in[:, :qd] * (HEAD_DIM ** -0.5)
    dup = lambda w: jnp.concatenate([w[:, :HEAD_DIM], w[:, :HEAD_DIM], w[:, HEAD_DIM:], w[:, HEAD_DIM:]], axis=1)
    return jnp.concatenate([q, dup(w_in[:, qd:qd + kd]), dup(w_in[:, qd + kd:])], axis=1).astype(BF16)


def _fox_weights(w_in, b_f):
    hd = HEADS * HEAD_DIM
    d = w_in.shape[0]
    w_q = (w_in[:, :hd] * (HEAD_DIM ** -0.5 * LOG2E)).astype(BF16)
    w_k = w_in[:, hd:2 * hd].astype(BF16)
    w_v = w_in[:, 2 * hd:3 * hd]
    w_gate = w_in[:, 3 * hd:]
    w_fg = jnp.concatenate([w_gate] * GATE_SLOTS + [jnp.zeros((d, LANES - GATE_SLOTS * HEADS), F32)],
                           axis=1).astype(BF16)
    b_fg = jnp.concatenate([b_f] * GATE_SLOTS + [jnp.zeros((LANES - GATE_SLOTS * HEADS,), F32)])[None, :]
    return w_q, w_k, w_v, w_fg, b_fg


def kernel(x, meta_tokens, ln1_g, ln1_b, ln2_g, ln2_b, fox_w_in, fox_b_f, fox_w_o, swa_w_in, swa_sinks, swa_w_o,
           mla_w_a, mla_g_q, mla_g_kv, mla_w_uq, mla_w_ukv, mla_w_o, ffn_w_in, ffn_conv_w, ffn_conv_b, ffn_w_out):
    batch, seq, d = x.shape
    assert d == D_MODEL and seq % 256 == 0
    lp = seq + FIRST_REAL
    n = batch * lp

    tm = _tile(lp, 768, 256)
    tq = _tile(lp, 768, 256)
    tk = 256
    tn = 512
    nk = lp // tk

    h0 = jnp.concatenate([jnp.zeros((batch, LEAD, d), x.dtype),
                          jnp.broadcast_to(meta_tokens.astype(x.dtype)[None], (batch, N_META, d)), x], axis=1)
    hf = h0.reshape(n, d)
    hb = hf.astype(BF16)

    cos_p, sin_p = _rope_tables(lp, ROPE_DIM, ROPE_THETA, HEAD_DIM, 0)
    cos_m, sin_m = _rope_tables(lp, MLA_ROPE, MLA_ROPE_THETA, LANES, MLA_NOPE)
    b3 = lambda a: a.reshape(batch, lp, -1)

    for i in range(DEPTH):
        kind, j = i % 3, i // 3
        if kind == 0:
            w_q, w_k, w_v, w_fg, b_fg = _fox_weights(fox_w_in[j], fox_b_f[j])
            aq, ak = _fox_gate(hb, w_fg, b_fg, lp, tm)
            qh = _head_proj(hb, w_q, aq, tm, tn)
            kh = _head_proj(hb, w_k, ak, tm, tn)
            vt = _matmul_t(hb, *_value_weights(w_v), tm, V_COLS_TILE, tk).reshape(batch, nk, HEADS * V_ROWS, tk)
            o = _flash_attention(b3(qh), b3(kh), vt, batch, lp, tq, tk)
            w_o = fox_w_o[j]
        elif kind == 1:
            qkv = _swa_proj(hb, _swa_weights(swa_w_in[j]), cos_p, sin_p, lp, tm, 2 * LANES, 5)
            o = _swa_attention(b3(qkv), swa_sinks[j].astype(F32), batch, lp)
            w_o = swa_w_o[j]
        else:
            w_a_cat, w_main, w_swap, w_kn, w_v = _mla_weights(mla_w_a[j], mla_w_uq[j], mla_w_ukv[j])
            cq, ckv, kr = _mla_a(hb, w_a_cat, mla_g_q[j][None, :], mla_g_kv[j][None, :], cos_m, sin_m, lp, tm)
            qh = _mla_q(cq, w_main, w_swap, cos_m, sin_m, lp, tm, tn)
            kh = _head_proj(ckv, w_kn, kr, tm, tn)
            vt = _matmul_t(ckv, *_value_weights(w_v), tm, V_COLS_TILE, tk).reshape(batch, nk, HEADS * V_ROWS, tk)
            o = _flash_attention(b3(qh), b3(kh), vt, batch, lp, tq, tk)
            w_o = mla_w_o[j]
        hf, hb = _oproj_ln(o.reshape(n, d), w_o.astype(BF16), hf, ln1_g[i][None, :], ln1_b[i][None, :], tm)
        hf, hb = _ffn(hb, hf, *_ffn_weights(ffn_w_in[i], ffn_conv_w[i], ffn_conv_b[i], ffn_w_out[i]),
                      ln2_g[i][None, :], ln2_b[i][None, :], lp, tm)
    return hf.reshape(batch, lp, d)[:, FIRST_REAL:]
```

```python
import functools
import math

import numpy as np
import jax
import jax.numpy as jnp
from jax import lax
from jax.experimental import pallas as pl
from jax.experimental.pallas import tpu as pltpu

F32 = jnp.float32
BF16 = jnp.bfloat16

D_MODEL = 1024
DEPTH = 4
N_META = 16
LEAD = 240
FIRST_REAL = LEAD + N_META
NEG = -1e30
DEEPNORM_ALPHA = (2.0 * DEPTH) ** 0.25
LN_EPS = 1e-5
RMS_EPS = 1e-6
HEADS = 16
HEAD_DIM = 64
PAIRS = HEADS // 2
SWA_KV_HEADS = 2
WINDOW = 128
ROPE_THETA = 500000.0
ROPE_DIM = 16
MLA_Q_LORA = 384
MLA_KV_LORA = 256
MLA_NOPE = 64
MLA_ROPE = 32
MLA_ROPE_THETA = 10000.0
D_FF = 2816
CONV_W = 3
LOG2E = math.log2(math.e)

LANES = 128
BF16_SUBLANES = 16
VMEM_LIMIT = 56 * 1024 * 1024

GATE_SLOTS = 3
FOX_DEAD_LANE = HEAD_DIM + 2 * GATE_SLOTS
MLA_DEAD_LANE = MLA_NOPE + MLA_ROPE
M_INIT = -3e38


def _params(*sem):
    return pltpu.CompilerParams(dimension_semantics=sem, vmem_limit_bytes=VMEM_LIMIT)


def _tile(n, pref, mult):
    best = mult
    t = mult
    while t <= min(n, pref):
        if n % t == 0:
            best = t
        t += mult
    assert n % best == 0
    return best


def _dot(a, b):
    return jnp.dot(a, b, preferred_element_type=F32)


def _dot_nt(a, b):
    return lax.dot_general(a, b, (((1,), (1,)), ((), ())), preferred_element_type=F32)


def _layer_norm(x, g, b):
    mu = jnp.mean(x, axis=-1, keepdims=True)
    xc = x - mu
    var = jnp.mean(xc * xc, axis=-1, keepdims=True)
    return xc * lax.rsqrt(var + LN_EPS) * g + b


def _one_hot_lanes(width, period, lane):
    idx = lax.broadcasted_iota(jnp.int32, (1, width), 1)
    return jnp.where((idx & (period - 1)) == lane, 1.0, 0.0)


def _dead_rows(tile_in_batch, tm):
    pos = tile_in_batch * tm + lax.broadcasted_iota(jnp.int32, (tm, 1), 0)
    return jnp.where(pos < LEAD, NEG, 0.0)


def _mm_kernel(x_ref, w_ref, o_ref):
    o_ref[...] = _dot(x_ref[...], w_ref[...]).astype(o_ref.dtype)


def _matmul(x, w, out_dtype, tm, tn):
    n, k = x.shape
    m = w.shape[1]
    return pl.pallas_call(
        _mm_kernel,
        grid=(n // tm, m // tn),
        in_specs=[pl.BlockSpec((tm, k), lambda i, j: (i, 0)),
                  pl.BlockSpec((k, tn), lambda i, j: (0, j))],
        out_specs=pl.BlockSpec((tm, tn), lambda i, j: (i, j)),
        out_shape=jax.ShapeDtypeStruct((n, m), out_dtype),
        compiler_params=_params("parallel", "parallel"),
        name="matmul",
    )(x, w)


def _head_proj_kernel(x_ref, w_ref, e_ref, o_ref):
    y = _dot(x_ref[...], w_ref[...])
    lo = lax.broadcasted_iota(jnp.int32, (1, LANES), 1) < HEAD_DIM
    shared = e_ref.shape[1] == LANES
    for pair in range(y.shape[1] // LANES):
        y_pair = y[:, pair * LANES:(pair + 1) * LANES]
        for a, feats in enumerate((y_pair, pltpu.roll(y_pair, HEAD_DIM, 1))):
            h = 2 * pair + a
            extra = e_ref[...] if shared else e_ref[:, h * LANES:(h + 1) * LANES]
            o_ref[:, h * LANES:(h + 1) * LANES] = jnp.where(lo, feats, extra.astype(F32)).astype(o_ref.dtype)


def _head_proj(x, w, e, tm, tn):
    n, k = x.shape
    m = w.shape[1]
    if e.shape[1] == LANES:
        e_spec = pl.BlockSpec((tm, LANES), lambda i, j: (i, 0))
    else:
        assert e.shape[1] == 2 * m
        e_spec = pl.BlockSpec((tm, 2 * tn), lambda i, j: (i, j))
    return pl.pallas_call(
        _head_proj_kernel,
        grid=(n // tm, m // tn),
        in_specs=[pl.BlockSpec((tm, k), lambda i, j: (i, 0)),
                  pl.BlockSpec((k, tn), lambda i, j: (0, j)),
                  e_spec],
        out_specs=pl.BlockSpec((tm, 2 * tn), lambda i, j: (i, j)),
        out_shape=jax.ShapeDtypeStruct((n, 2 * m), BF16),
        compiler_params=_params("parallel", "parallel"),
        name="head_proj",
    )(x, w, e)


def _mm_t_kernel(x_ref, w_ref, b_ref, o_ref, *, tk):
    y = _dot(x_ref[...], w_ref[...]) + b_ref[...]
    for c in range(o_ref.shape[0]):
        o_ref[c] = y[c * tk:(c + 1) * tk, :].T.astype(o_ref.dtype)


def _matmul_t(x, w, bias, tm, tn, tk):
    n, k = x.shape
    m = w.shape[1]
    r = tm // tk
    return pl.pallas_call(
        functools.partial(_mm_t_kernel, tk=tk),
        grid=(n // tm, m // tn),
        in_specs=[pl.BlockSpec((tm, k), lambda i, j: (i, 0)),
                  pl.BlockSpec((k, tn), lambda i, j: (0, j)),
                  pl.BlockSpec((1, tn), lambda i, j: (0, j))],
        out_specs=pl.BlockSpec((r, tn, tk), lambda i, j: (i, j, 0)),
        out_shape=jax.ShapeDtypeStruct((n // tk, m, tk), BF16),
        compiler_params=_params("parallel", "parallel"),
        name="matmul_t",
    )(x, w, bias)


V_ROWS = HEAD_DIM + BF16_SUBLANES
V_COLS_TILE = 8 * V_ROWS


def _value_weights(w_v):
    rows = w_v.shape[0]
    w = jnp.concatenate([w_v.reshape(rows, HEADS, HEAD_DIM),
                         jnp.zeros((rows, HEADS, V_ROWS - HEAD_DIM), w_v.dtype)], axis=2)
    bias = np.zeros((HEADS, V_ROWS), np.float32)
    bias[:, HEAD_DIM] = 1.0
    return w.reshape(rows, HEADS * V_ROWS).astype(BF16), jnp.asarray(bias.reshape(1, HEADS * V_ROWS))


def _gate_placement():
    wide = HEADS * LANES
    sq = np.zeros((LANES, wide), np.float32)
    sk = np.zeros_like(sq)
    oq = np.zeros((1, wide), np.float32)
    ok = np.zeros_like(oq)
    for h in range(HEADS):
        base = h * LANES + HEAD_DIM
        for part in range(GATE_SLOTS):
            sq[part * HEADS + h, base + part] = 1.0
            sk[part * HEADS + h, base + GATE_SLOTS + part] = -1.0
            oq[0, base + GATE_SLOTS + part] = 1.0
            ok[0, base + part] = 1.0
        oq[0, h * LANES + FOX_DEAD_LANE] = 1.0
    return sq, sk, oq, ok


def _split3(x):
    hi = x.astype(BF16)
    r1 = x - hi.astype(F32)
    mid = r1.astype(BF16)
    lo = (r1 - mid.astype(F32)).astype(BF16)
    return hi, mid, lo


def _fox_gate_kernel(x_ref, w_ref, b_ref, sq_ref, sk_ref, oq_ref, ok_ref, aq_ref, ak_ref, carry_ref,
                     *, tm, tiles_per_batch):
    i = pl.program_id(0)

    @pl.when(i % tiles_per_batch == 0)
    def _():
        carry_ref[...] = jnp.zeros_like(carry_ref)

    fg = _dot(x_ref[...], w_ref[...]) + b_ref[...]
    logf = jnp.minimum(fg, 0.0) - jnp.log(1.0 + jnp.exp(-jnp.abs(fg)))
    row = lax.broadcasted_iota(jnp.int32, (tm, tm), 0)
    col = lax.broadcasted_iota(jnp.int32, (tm, tm), 1)
    tri = jnp.where(col <= row, 1.0, 0.0).astype(BF16)
    hi, mid, lo = _split3(logf)
    cs = _dot(tri, hi) + _dot(tri, mid) + _dot(tri, lo) + carry_ref[...]
    carry_ref[...] = cs[tm - 1:tm, :]
    lane = lax.broadcasted_iota(jnp.int32, (1, LANES), 1)
    hi, mid, lo = _split3(cs * LOG2E)
    parts = jnp.where(lane < HEADS, hi.astype(F32),
                      jnp.where(lane < 2 * HEADS, mid.astype(F32), lo.astype(F32))).astype(BF16)
    wide = aq_ref.shape[1]
    dead = _dead_rows(i % tiles_per_batch, tm) * _one_hot_lanes(wide, LANES, FOX_DEAD_LANE)
    aq_ref[...] = (_dot(parts, sq_ref[...]) + oq_ref[...]).astype(aq_ref.dtype)
    ak_ref[...] = (_dot(parts, sk_ref[...]) + ok_ref[...] + dead).astype(ak_ref.dtype)


def _fox_gate(hb, w_fg, b_fg, lp, tm):
    n = hb.shape[0]
    tpb = lp // tm
    sq, sk, oq, ok = _gate_placement()
    fixed = lambda i: (0, 0)
    wide = HEADS * LANES
    return pl.pallas_call(
        functools.partial(_fox_gate_kernel, tm=tm, tiles_per_batch=tpb),
        grid=(n // tm,),
        in_specs=[pl.BlockSpec((tm, D_MODEL), lambda i: (i, 0)),
                  pl.BlockSpec((D_MODEL, LANES), fixed),
                  pl.BlockSpec((1, LANES), fixed),
                  pl.BlockSpec((LANES, wide), fixed),
                  pl.BlockSpec((LANES, wide), fixed),
                  pl.BlockSpec((1, wide), fixed),
                  pl.BlockSpec((1, wide), fixed)],
        out_specs=[pl.BlockSpec((tm, wide), lambda i: (i, 0)),
                   pl.BlockSpec((tm, wide), lambda i: (i, 0))],
        out_shape=[jax.ShapeDtypeStruct((n, wide), BF16), jax.ShapeDtypeStruct((n, wide), BF16)],
        scratch_shapes=[pltpu.VMEM((1, LANES), F32)],
        compiler_params=_params("arbitrary"),
        name="fox_gate",
    )(hb, w_fg, b_fg, jnp.asarray(sq, BF16), jnp.asarray(sk, BF16), jnp.asarray(oq), jnp.asarray(ok))


def _flash_kernel(q_ref, k_ref, vt_ref, o_ref, sa_ref, sb_ref, xa_ref, xb_ref, m_ref, acc_ref, *, tq, tk):
    i = pl.program_id(2)
    r = tq // tk
    m_ref[...] = jnp.full_like(m_ref, M_INIT)
    acc_ref[...] = jnp.zeros_like(acc_ref)

    def diag_mask(s):
        keep = lax.broadcasted_iota(jnp.int32, s.shape, 0) <= lax.broadcasted_iota(jnp.int32, s.shape, 1)
        return jnp.where(keep, s, NEG)

    def scores(s_ref, x_ref, j, a, qs, diagonal, blk=None):
        off = pl.multiple_of(j * tk, tk)
        q0 = pl.multiple_of((i if blk is None else blk) * tq + qs, tk)
        s = _dot_nt(k_ref[0, pl.ds(off, tk), a * LANES:(a + 1) * LANES],
                    q_ref[0, pl.ds(q0, tq - qs), a * LANES:(a + 1) * LANES])
        if diagonal:
            s = diag_mask(s)
        s_ref[a, :, qs:] = s
        x_ref[a, :, qs:] = jnp.max(s, axis=0, keepdims=True)

    def consume(s_ref, x_ref, j, a, qs, mask_now):
        s = s_ref[a, :, qs:]
        if mask_now:
            s = diag_mask(s)
            smax = jnp.max(s, axis=0, keepdims=True)
        else:
            smax = x_ref[a, :, qs:]
        m_old = m_ref[a, :, qs:]
        m_new = jnp.maximum(m_old, smax)
        alpha = jnp.exp2(m_old - m_new)
        p = jnp.exp2(s - m_new)
        m_ref[a, :, qs:] = m_new
        pv = _dot(vt_ref[0, j, a * V_ROWS:(a + 1) * V_ROWS, :], p.astype(BF16))
        acc_ref[a, :, qs:] = alpha * acc_ref[a, :, qs:] + pv

    buf_a, buf_b = (sa_ref, xa_ref), (sb_ref, xb_ref)
    jdiag = i * r
    odd = jdiag & 1

    @pl.when(i == 0)
    def _():
        for a in range(2):
            scores(*buf_a, 0, a, 0, False)

    @pl.when(odd == 1)
    def _():
        for a in range(2):
            scores(*buf_b, 1, a, 0, False)
            consume(*buf_a, 0, a, 0, False)
        sa_ref[...] = sb_ref[...]
        xa_ref[...] = xb_ref[...]

    def pair(j):
        for a in range(2):
            scores(*buf_b, j + 1, a, 0, False)
            consume(*buf_a, j, a, 0, False)
        for a in range(2):
            scores(*buf_a, j + 2, a, 0, False)
            consume(*buf_b, j + 1, a, 0, False)

    pairs = (jdiag - odd) // 2
    one = pairs & 1
    two = (pairs >> 1) & 1

    @pl.when(one == 1)
    def _():
        pair(odd)

    @pl.when(two == 1)
    def _():
        j = odd + 2 * one
        pair(j)
        pair(j + 2)

    def octo_body(t, c):
        j = odd + 2 * one + 4 * two + 8 * t
        for u in range(4):
            pair(j + 2 * u)
        return c

    lax.fori_loop(0, pairs >> 2, octo_body, 0)

    bufs = (buf_a, buf_b)
    for d in range(r):
        qs = d * tk
        for a in range(2):
            if d + 1 < r:
                scores(*bufs[(d + 1) & 1], jdiag + d + 1, a, qs + tk, True)
            consume(*bufs[d & 1], jdiag + d, a, qs, d == 0)

    for a in range(2):
        scores(*buf_a, 0, a, 0, False, blk=jnp.minimum(i + 1, pl.num_programs(2) - 1))

    ot = jnp.concatenate([acc_ref[a, :HEAD_DIM, :] * (1.0 / acc_ref[a, HEAD_DIM:HEAD_DIM + 1, :]) for a in range(2)],
                         axis=0)
    o_ref[0] = ot.T.astype(o_ref.dtype)


def _flash_attention(qh, kh, vt, batch, lp, tq, tk):
    nk = lp // tk
    return pl.pallas_call(
        functools.partial(_flash_kernel, tq=tq, tk=tk),
        grid=(batch, PAIRS, lp // tq),
        in_specs=[pl.BlockSpec((1, lp, 2 * LANES), lambda b, p, i: (b, 0, p)),
                  pl.BlockSpec((1, lp, 2 * LANES), lambda b, p, i: (b, 0, p)),
                  pl.BlockSpec((1, nk, 2 * V_ROWS, tk), lambda b, p, i: (b, 0, p, 0))],
        out_specs=pl.BlockSpec((1, tq, 2 * HEAD_DIM), lambda b, p, i: (b, i, p)),
        out_shape=jax.ShapeDtypeStruct((batch, lp, HEADS * HEAD_DIM), BF16),
        scratch_shapes=[pltpu.VMEM((2, tk, tq), F32), pltpu.VMEM((2, tk, tq), F32),
                        pltpu.VMEM((2, 1, tq), F32), pltpu.VMEM((2, 1, tq), F32),
                        pltpu.VMEM((2, 1, tq), F32), pltpu.VMEM((2, V_ROWS, tq), F32)],
        compiler_params=_params("parallel", "parallel", "arbitrary"),
        name="flash_attention",
    )(qh, kh, vt)


SWA_TQ = 128


def _swa_attn_kernel(sink_ref, q_ref, km_ref, kp_ref, kc_ref, vm_ref, vp_ref, vc_ref, o_ref):
    i = pl.program_id(1)
    t = SWA_TQ
    lo = lax.broadcasted_iota(jnp.int32, (1, LANES), 1) < HEAD_DIM
    row = lax.broadcasted_iota(jnp.int32, (t, 3 * t), 0)
    col = lax.broadcasted_iota(jnp.int32, (t, 3 * t), 1)
    qpos = i * t + row
    kpos = jnp.where(col < t, t + col, (i - 2) * t + col)
    d = qpos - kpos
    valid = (d >= 0) & (((col < t) & (kpos >= LEAD)) |
                        ((col >= t) & (d < WINDOW) & (kpos >= FIRST_REAL)))
    kcat, v_lo, v_hi = [], [], []
    for g in range(SWA_KV_HEADS):
        sl = slice(g * LANES, (g + 1) * LANES)
        kcat.append(jnp.concatenate([km_ref[0, :, sl], kp_ref[0, :, sl], kc_ref[0, :, sl]], axis=0))
        vf = jnp.concatenate([vm_ref[0, :, sl], vp_ref[0, :, sl], vc_ref[0, :, sl]], axis=0).astype(F32)
        v_lo.append(jnp.where(lo, vf, 0.0).astype(BF16))
        v_hi.append(jnp.where(lo, 0.0, vf).astype(BF16))
    pairs_per_group = PAIRS // SWA_KV_HEADS

    def logits(p):
        qf = q_ref[0, :, p * LANES:(p + 1) * LANES].astype(F32)
        q_pair = (jnp.where(lo, qf, 0.0).astype(BF16), jnp.where(lo, 0.0, qf).astype(BF16))
        return [_dot_nt(q_pair[a], kcat[p // pairs_per_group]) for a in range(2)]

    def finish(p, s_pair):
        g = p // pairs_per_group
        ps, inv = [], []
        for a in range(2):
            sink = sink_ref[2 * p + a]
            s = jnp.where(valid, s_pair[a], NEG)
            m = jnp.maximum(jnp.max(s, axis=1, keepdims=True), sink)
            e = jnp.exp(s - m)
            den = jnp.sum(e, axis=1, keepdims=True) + jnp.exp(sink - m)
            ps.append(e.astype(BF16))
            inv.append(1.0 / den)
        o = (_dot(ps[0], v_lo[g]) + _dot(ps[1], v_hi[g])) * jnp.where(lo, inv[0], inv[1])
        o_ref[0, :, p * LANES:(p + 1) * LANES] = o.astype(o_ref.dtype)

    s_next = logits(0)
    for p in range(PAIRS):
        s_cur = s_next
        if p + 1 < PAIRS:
            s_next = logits(p + 1)
        finish(p, s_cur)


def _swa_attention(qkv, sinks, batch, lp):
    t = SWA_TQ
    kblk, vblk = 4, 5
    kv_spec = lambda col, row_of: pl.BlockSpec((1, t, 2 * LANES), lambda b, i: (b, row_of(i), col))
    meta = lambda i: 1
    prev = lambda i: jnp.maximum(i - 1, 0)
    cur = lambda i: i
    return pl.pallas_call(
        _swa_attn_kernel,
        grid=(batch, lp // t),
        in_specs=[pl.BlockSpec(memory_space=pltpu.SMEM),
                  pl.BlockSpec((1, t, HEADS * HEAD_DIM), lambda b, i: (b, i, 0)),
                  kv_spec(kblk, meta), kv_spec(kblk, prev), kv_spec(kblk, cur),
                  kv_spec(vblk, meta), kv_spec(vblk, prev), kv_spec(vblk, cur)],
        out_specs=pl.BlockSpec((1, t, HEADS * HEAD_DIM), lambda b, i: (b, i, 0)),
        out_shape=jax.ShapeDtypeStruct((batch, lp, HEADS * HEAD_DIM), BF16),
        compiler_params=_params("parallel", "parallel"),
        name="swa_attention",
    )(sinks, qkv, qkv, qkv, qkv, qkv, qkv, qkv)


def _swa_proj_kernel(x_ref, w_ref, cos_ref, sin_ref, perm_ref, o_ref, *, tn, n_rope_blocks):
    j = pl.program_id(1)
    y = _dot(x_ref[...], w_ref[...])

    @pl.when(j < n_rope_blocks)
    def _():
        reps = tn // LANES
        cos = jnp.concatenate([cos_ref[...]] * reps, axis=1)
        sin = jnp.concatenate([sin_ref[...]] * reps, axis=1)
        partner = _dot(y.astype(BF16), perm_ref[...])
        o_ref[...] = (y * cos + partner * sin).astype(o_ref.dtype)

    @pl.when(j >= n_rope_blocks)
    def _():
        o_ref[...] = y.astype(o_ref.dtype)


def _swa_proj(hb, w, cos, sin, lp, tm, tn, n_rope_blocks):
    n = hb.shape[0]
    m = w.shape[1]
    tpb = lp // tm
    return pl.pallas_call(
        functools.partial(_swa_proj_kernel, tn=tn, n_rope_blocks=n_rope_blocks),
        grid=(n // tm, m // tn),
        in_specs=[pl.BlockSpec((tm, D_MODEL), lambda i, j: (i, 0)),
                  pl.BlockSpec((D_MODEL, tn), lambda i, j: (0, j)),
                  pl.BlockSpec((tm, LANES), lambda i, j: (i % tpb, 0)),
                  pl.BlockSpec((tm, LANES), lambda i, j: (i % tpb, 0)),
                  pl.BlockSpec((tn, tn), lambda i, j: (0, 0))],
        out_specs=pl.BlockSpec((tm, tn), lambda i, j: (i, j)),
        out_shape=jax.ShapeDtypeStruct((n, m), BF16),
        compiler_params=_params("parallel", "parallel"),
        name="swa_proj",
    )(hb, w, cos, sin, jnp.asarray(_rope_partner_matrix(tn), BF16))


def _rope_partner_matrix(width):
    p = np.zeros((width, width), np.float32)
    half = ROPE_DIM // 2
    for base in range(0, width, HEAD_DIM):
        for l in range(half):
            p[base + l + half, base + l] = 1.0
            p[base + l, base + l + half] = 1.0
    return p


MLA_A_COLS = MLA_Q_LORA + MLA_KV_LORA + 2 * LANES


def _mla_a_kernel(x_ref, w_ref, gq_ref, gkv_ref, cos_ref, sin_ref, cq_ref, ckv_ref, kr_ref, *, tm, tiles_per_batch):
    y = _dot(x_ref[...], w_ref[...])
    cq = y[:, :MLA_Q_LORA]
    ckv = y[:, MLA_Q_LORA:MLA_Q_LORA + MLA_KV_LORA]
    kr = y[:, MLA_Q_LORA + MLA_KV_LORA:MLA_Q_LORA + MLA_KV_LORA + LANES]
    krs = y[:, MLA_Q_LORA + MLA_KV_LORA + LANES:]
    rms = lambda z, g: z * lax.rsqrt(jnp.mean(z * z, axis=-1, keepdims=True) + RMS_EPS) * g
    cq_ref[...] = rms(cq, gq_ref[...]).astype(cq_ref.dtype)
    ckv_ref[...] = rms(ckv, gkv_ref[...]).astype(ckv_ref.dtype)
    dead = _dead_rows(pl.program_id(0) % tiles_per_batch, tm) * _one_hot_lanes(LANES, LANES, MLA_DEAD_LANE)
    kr_ref[...] = (kr * cos_ref[...] + krs * sin_ref[...] + dead).astype(kr_ref.dtype)


def _mla_a(hb, w, gq, gkv, cos, sin, lp, tm):
    n = hb.shape[0]
    tpb = lp // tm
    return pl.pallas_call(
        functools.partial(_mla_a_kernel, tm=tm, tiles_per_batch=tpb),
        grid=(n // tm,),
        in_specs=[pl.BlockSpec((tm, D_MODEL), lambda i: (i, 0)),
                  pl.BlockSpec((D_MODEL, MLA_A_COLS), lambda i: (0, 0)),
                  pl.BlockSpec((1, MLA_Q_LORA), lambda i: (0, 0)),
                  pl.BlockSpec((1, MLA_KV_LORA), lambda i: (0, 0)),
                  pl.BlockSpec((tm, LANES), lambda i: (i % tpb, 0)),
                  pl.BlockSpec((tm, LANES), lambda i: (i % tpb, 0))],
        out_specs=[pl.BlockSpec((tm, MLA_Q_LORA), lambda i: (i, 0)),
                   pl.BlockSpec((tm, MLA_KV_LORA), lambda i: (i, 0)),
                   pl.BlockSpec((tm, LANES), lambda i: (i, 0))],
        out_shape=[jax.ShapeDtypeStruct((n, MLA_Q_LORA), BF16),
                   jax.ShapeDtypeStruct((n, MLA_KV_LORA), BF16),
                   jax.ShapeDtypeStruct((n, LANES), BF16)],
        compiler_params=_params("parallel"),
        name="mla_a",
    )(hb, w, gq, gkv, cos, sin)


def _mla_q_kernel(x_ref, w_ref, ws_ref, cos_ref, sin_ref, o_ref):
    x = x_ref[...]
    reps = o_ref.shape[1] // LANES
    cos = jnp.concatenate([cos_ref[...]] * reps, axis=1)
    sin = jnp.concatenate([sin_ref[...]] * reps, axis=1)
    y = _dot(x, w_ref[...]) * cos + _dot(x, ws_ref[...]) * sin
    o_ref[...] = (y + _one_hot_lanes(o_ref.shape[1], LANES, MLA_DEAD_LANE)).astype(o_ref.dtype)


def _mla_q(cq, w, ws, cos, sin, lp, tm, tn):
    n = cq.shape[0]
    m = w.shape[1]
    tpb = lp // tm
    return pl.pallas_call(
        _mla_q_kernel,
        grid=(n // tm, m // tn),
        in_specs=[pl.BlockSpec((tm, MLA_Q_LORA), lambda i, j: (i, 0)),
                  pl.BlockSpec((MLA_Q_LORA, tn), lambda i, j: (0, j)),
                  pl.BlockSpec((MLA_Q_LORA, tn), lambda i, j: (0, j)),
                  pl.BlockSpec((tm, LANES), lambda i, j: (i % tpb, 0)),
                  pl.BlockSpec((tm, LANES), lambda i, j: (i % tpb, 0))],
        out_specs=pl.BlockSpec((tm, tn), lambda i, j: (i, j)),
        out_shape=jax.ShapeDtypeStruct((n, m), BF16),
        compiler_params=_params("parallel", "parallel"),
        name="mla_q",
    )(cq, w, ws, cos, sin)


def _oproj_ln_kernel(o_ref, w_ref, h_ref, g_ref, b_ref, hf_ref, hb_ref):
    x = DEEPNORM_ALPHA * h_ref[...] + _dot(o_ref[...], w_ref[...])
    y = _layer_norm(x, g_ref[...], b_ref[...])
    hf_ref[...] = y
    hb_ref[...] = y.astype(hb_ref.dtype)


def _oproj_ln(o, w, h, g, b, tm):
    n = o.shape[0]
    row = lambda i: (i, 0)
    fixed = lambda i: (0, 0)
    return pl.pallas_call(
        _oproj_ln_kernel,
        grid=(n // tm,),
        in_specs=[pl.BlockSpec((tm, D_MODEL), row), pl.BlockSpec((D_MODEL, D_MODEL), fixed),
                  pl.BlockSpec((tm, D_MODEL), row), pl.BlockSpec((1, D_MODEL), fixed),
                  pl.BlockSpec((1, D_MODEL), fixed)],
        out_specs=[pl.BlockSpec((tm, D_MODEL), row), pl.BlockSpec((tm, D_MODEL), row)],
        out_shape=[jax.ShapeDtypeStruct((n, D_MODEL), F32), jax.ShapeDtypeStruct((n, D_MODEL), BF16)],
        compiler_params=_params("parallel"),
        name="oproj_ln",
    )(o, w, h, g, b)


FFN_HALO = BF16_SUBLANES


FFN_CHUNK = 256
FFN_NC = D_FF // FFN_CHUNK
assert FFN_NC * FFN_CHUNK == D_FF and FFN_NC % 2 == 1


def _ffn_kernel(x_ref, halo_ref, h_ref, win_ref, cw_ref, cb_ref, wo_ref, g_ref, b_ref, hf_ref, hb_ref,
                xext_ref, ua_ref, ub_ref, acc_ref, *, tm, tiles_per_batch):
    i = pl.program_id(0)
    nc = FFN_NC
    pos = (i % tiles_per_batch) * tm - FFN_HALO + lax.broadcasted_iota(jnp.int32, (tm + FFN_HALO, 1), 0)
    xe = jnp.concatenate([halo_ref[...], x_ref[...]], axis=0).astype(F32)
    xext_ref[...] = jnp.where(pos >= LEAD, xe, 0.0).astype(BF16)

    def up(u_ref, c):
        u_ref[0] = _dot(xext_ref[...], win_ref[c])
        u_ref[1] = _dot(xext_ref[...], win_ref[nc + c])

    def glu(u_ref, c):
        def conv(part, idx):
            u = u_ref[part]
            delayed = u * cw_ref[idx, 0:1, :]
            for tap in range(1, CONV_W):
                delayed = u * cw_ref[idx, tap:tap + 1, :] + pltpu.roll(delayed, 1, 0)
            return cb_ref[idx] + delayed[FFN_HALO:, :]

        yg = conv(0, c)
        yv = conv(1, nc + c)
        return ((yg / (1.0 + jnp.exp(-yg))) * yv).astype(BF16)

    acc_ref[...] = DEEPNORM_ALPHA * h_ref[...]
    up(ua_ref, 0)

    def pair_body(t, carry):
        c = 2 * t
        up(ub_ref, c + 1)
        act_a = glu(ua_ref, c)
        up(ua_ref, c + 2)
        act_b = glu(ub_ref, c + 1)
        acc_ref[...] += _dot(act_a, wo_ref[c]) + _dot(act_b, wo_ref[c + 1])
        return carry

    lax.fori_loop(0, (nc - 1) // 2, pair_body, 0)
    y = _layer_norm(acc_ref[...] + _dot(glu(ua_ref, nc - 1), wo_ref[nc - 1]), g_ref[...], b_ref[...])
    hf_ref[...] = y
    hb_ref[...] = y.astype(hb_ref.dtype)


def _ffn(hb, hf, w_in, conv_w, conv_b, w_out, g, b, lp, tm):
    n = hb.shape[0]
    nc, fc = FFN_NC, FFN_CHUNK
    tpb = lp // tm
    halo_blocks = tm // FFN_HALO
    row = lambda i: (i, 0)
    fixed2 = lambda i: (0, 0)
    fixed3 = lambda i: (0, 0, 0)
    resident = dict(pipeline_mode=pl.Buffered(1))
    return pl.pallas_call(
        functools.partial(_ffn_kernel, tm=tm, tiles_per_batch=tpb),
        grid=(n // tm,),
        in_specs=[pl.BlockSpec((tm, D_MODEL), row),
                  pl.BlockSpec((FFN_HALO, D_MODEL), lambda i: (jnp.maximum(i * halo_blocks - 1, 0), 0)),
                  pl.BlockSpec((tm, D_MODEL), row),
                  pl.BlockSpec((2 * nc, D_MODEL, fc), fixed3, **resident),
                  pl.BlockSpec((2 * nc, CONV_W, fc), fixed3, **resident),
                  pl.BlockSpec((2 * nc, 1, fc), fixed3, **resident),
                  pl.BlockSpec((nc, fc, D_MODEL), fixed3, **resident),
                  pl.BlockSpec((1, D_MODEL), fixed2), pl.BlockSpec((1, D_MODEL), fixed2)],
        out_specs=[pl.BlockSpec((tm, D_MODEL), row), pl.BlockSpec((tm, D_MODEL), row)],
        out_shape=[jax.ShapeDtypeStruct((n, D_MODEL), F32), jax.ShapeDtypeStruct((n, D_MODEL), BF16)],
        scratch_shapes=[pltpu.VMEM((tm + FFN_HALO, D_MODEL), BF16),
                        pltpu.VMEM((2, tm + FFN_HALO, fc), F32), pltpu.VMEM((2, tm + FFN_HALO, fc), F32),
                        pltpu.VMEM((tm, D_MODEL), F32)],
        compiler_params=_params("parallel"),
        name="ffn",
    )(hb, hb, hf, w_in, conv_w, conv_b, w_out, g, b)


def _ffn_weights(w_in, conv_w, conv_b, w_out):
    d = w_in.shape[0]
    chunks = 2 * FFN_NC
    return (w_in.reshape(d, chunks, FFN_CHUNK).transpose(1, 0, 2).astype(BF16),
            conv_w.reshape(CONV_W, chunks, FFN_CHUNK).transpose(1, 0, 2),
            conv_b.reshape(chunks, 1, FFN_CHUNK),
            w_out.reshape(FFN_NC, FFN_CHUNK, d).astype(BF16))


def _rope_tables(lp, dim, theta, group, offset):
    pos = (jnp.arange(lp) - LEAD).astype(F32)
    inv = theta ** (-jnp.arange(0, dim, 2, dtype=F32) / dim)
    ang = pos[:, None] * inv[None, :]
    cos, sin = jnp.cos(ang), jnp.sin(ang)
    ones = lambda w: jnp.ones((lp, w), F32)
    zeros = lambda w: jnp.zeros((lp, w), F32)
    rest = group - offset - dim
    cos_g = jnp.concatenate([ones(offset), cos, cos, ones(rest)], axis=1)
    sin_g = jnp.concatenate([zeros(offset), -sin, sin, zeros(rest)], axis=1)
    reps = LANES // group
    return jnp.tile(cos_g, (1, reps)), jnp.tile(sin_g, (1, reps))


def _swap_halves(w, dim):
    return jnp.concatenate([w[..., dim // 2:dim], w[..., :dim // 2]], axis=-1)


def _head_blocks(main, extra):
    src = main if main is not None else extra
    rows, heads = src.shape[0], src.shape[1]
    m = main if main is not None else jnp.zeros((rows, heads, HEAD_DIM), F32)
    e = extra if extra is not None else jnp.zeros((rows, heads, 0), F32)
    pad = jnp.zeros((rows, heads, LANES - HEAD_DIM - e.shape[2]), F32)
    return jnp.concatenate([m, e, pad], axis=2).reshape(rows, heads * LANES)


def _mla_weights(w_a, w_uq, w_ukv):
    d = w_a.shape[0]
    w_kr = w_a[:, MLA_Q_LORA + MLA_KV_LORA:][:, None, :]
    w_a_cat = jnp.concatenate([w_a[:, :MLA_Q_LORA + MLA_KV_LORA], _head_blocks(None, w_kr),
                               _head_blocks(None, _swap_halves(w_kr, MLA_ROPE))], axis=1)
    scale = (MLA_NOPE + MLA_ROPE) ** -0.5 * LOG2E
    wq = (w_uq * scale).reshape(MLA_Q_LORA, HEADS, MLA_NOPE + MLA_ROPE)
    w_main = _head_blocks(wq[..., :MLA_NOPE], wq[..., MLA_NOPE:])
    w_swap = _head_blocks(None, _swap_halves(wq[..., MLA_NOPE:], MLA_ROPE))
    wkv = w_ukv.reshape(MLA_KV_LORA, HEADS, MLA_NOPE + HEAD_DIM)
    w_kn = wkv[..., :MLA_NOPE].reshape(MLA_KV_LORA, HEADS * MLA_NOPE)
    w_v = wkv[..., MLA_NOPE:].reshape(MLA_KV_LORA, HEADS * HEAD_DIM)
    return tuple(w.astype(BF16) for w in (w_a_cat, w_main, w_swap, w_kn)) + (w_v,)


def _swa_weights(w_in):
    qd = HEADS * HEAD_DIM
    kd = SWA_KV_HEADS * HEAD_DIM
    q = w_in[:, :qd] * (HEAD_DIM ** -0.5)
    dup = lambda w: jnp.concatenate([w[:, :HEAD_DIM], w[:, :HEAD_DIM], w[:, HEAD_DIM:], w[:, HEAD_DIM:]], axis=1)
    return jnp.concatenate([q, dup(w_in[:, qd:qd + kd]), dup(w_in[:, qd + kd:])], axis=1).astype(BF16)


def _fox_weights(w_in, b_f):
    hd = HEADS * HEAD_DIM
    d = w_in.shape[0]
    w_q = (w_in[:, :hd] * (HEAD_DIM ** -0.5 * LOG2E)).astype(BF16)
    w_k = w_in[:, hd:2 * hd].astype(BF16)
    w_v = w_in[:, 2 * hd:3 * hd]
    w_gate = w_in[:, 3 * hd:]
    w_fg = jnp.concatenate([w_gate] * GATE_SLOTS + [jnp.zeros((d, LANES - GATE_SLOTS * HEADS), F32)],
                           axis=1).astype(BF16)
    b_fg = jnp.concatenate([b_f] * GATE_SLOTS + [jnp.zeros((LANES - GATE_SLOTS * HEADS,), F32)])[None, :]
    return w_q, w_k, w_v, w_fg, b_fg


def kernel(x, meta_tokens, ln1_g, ln1_b, ln2_g, ln2_b, fox_w_in, fox_b_f, fox_w_o, swa_w_in, swa_sinks, swa_w_o,
           mla_w_a, mla_g_q, mla_g_kv, mla_w_uq, mla_w_ukv, mla_w_o, ffn_w_in, ffn_conv_w, ffn_conv_b, ffn_w_out):
    batch, seq, d = x.shape
    assert d == D_MODEL and seq % 256 == 0
    lp = seq + FIRST_REAL
    n = batch * lp

    tm = _tile(lp, 768, 256)
    tq = _tile(lp, 768, 256)
    tk = 256
    tn = 512
    nk = lp // tk

    h0 = jnp.concatenate([jnp.zeros((batch, LEAD, d), x.dtype),
                          jnp.broadcast_to(meta_tokens.astype(x.dtype)[None], (batch, N_META, d)), x], axis=1)
    hf = h0.reshape(n, d)
    hb = hf.astype(BF16)

    cos_p, sin_p = _rope_tables(lp, ROPE_DIM, ROPE_THETA, HEAD_DIM, 0)
    cos_m, sin_m = _rope_tables(lp, MLA_ROPE, MLA_ROPE_THETA, LANES, MLA_NOPE)
    b3 = lambda a: a.reshape(batch, lp, -1)

    for i in range(DEPTH):
        kind, j = i % 3, i // 3
        if kind == 0:
            w_q, w_k, w_v, w_fg, b_fg = _fox_weights(fox_w_in[j], fox_b_f[j])
            aq, ak = _fox_gate(hb, w_fg, b_fg, lp, tm)
            qh = _head_proj(hb, w_q, aq, tm, 2 * tn)
            kh = _head_proj(hb, w_k, ak, tm, 2 * tn)
            vt = _matmul_t(hb, *_value_weights(w_v), tm, V_COLS_TILE, tk).reshape(batch, nk, HEADS * V_ROWS, tk)
            o = _flash_attention(b3(qh), b3(kh), vt, batch, lp, tq, tk)
            w_o = fox_w_o[j]
        elif kind == 1:
            qkv = _swa_proj(hb, _swa_weights(swa_w_in[j]), cos_p, sin_p, lp, tm, 2 * LANES, 5)
            o = _swa_attention(b3(qkv), swa_sinks[j].astype(F32), batch, lp)
            w_o = swa_w_o[j]
        else:
            w_a_cat, w_main, w_swap, w_kn, w_v = _mla_weights(mla_w_a[j], mla_w_uq[j], mla_w_ukv[j])
            cq, ckv, kr = _mla_a(hb, w_a_cat, mla_g_q[j][None, :], mla_g_kv[j][None, :], cos_m, sin_m, lp, tm)
            qh = _mla_q(cq, w_main, w_swap, cos_m, sin_m, lp, tm, 2 * tn)
            kh = _head_proj(ckv, w_kn, kr, tm, 2 * tn)
            vt = _matmul_t(ckv, *_value_weights(w_v), tm, V_COLS_TILE, tk).reshape(batch, nk, HEADS * V_ROWS, tk)
            o = _flash_attention(b3(qh), b3(kh), vt, batch, lp, tq, tk)
            w_o = mla_w_o[j]
        hf, hb = _oproj_ln(o.reshape(n, d), w_o.astype(BF16), hf, ln1_g[i][None, :], ln1_b[i][None, :], tm)
        hf, hb = _ffn(hb, hf, *_ffn_weights(ffn_w_in[i], ffn_conv_w[i], ffn_conv_b[i], ffn_w_out[i]),
                      ln2_g[i][None, :], ln2_b[i][None, :], lp, tm)
    return hf.reshape(batch, lp, d)[:, FIRST_REAL:]
```

```python
import functools
import math

import numpy as np
import jax
import jax.numpy as jnp
from jax import lax
from jax.experimental import pallas as pl
from jax.experimental.pallas import tpu as pltpu

F32 = jnp.float32
BF16 = jnp.bfloat16

D_MODEL = 1024
DEPTH = 4
N_META = 16
LEAD = 240
FIRST_REAL = LEAD + N_META
NEG = -1e30
DEEPNORM_ALPHA = (2.0 * DEPTH) ** 0.25
LN_EPS = 1e-5
RMS_EPS = 1e-6
HEADS = 16
HEAD_DIM = 64
PAIRS = HEADS // 2
SWA_KV_HEADS = 2
WINDOW = 128
ROPE_THETA = 500000.0
ROPE_DIM = 16
MLA_Q_LORA = 384
MLA_KV_LORA = 256
MLA_NOPE = 64
MLA_ROPE = 32
MLA_ROPE_THETA = 10000.0
D_FF = 2816
CONV_W = 3
LOG2E = math.log2(math.e)

LANES = 128
BF16_SUBLANES = 16
VMEM_LIMIT = 56 * 1024 * 1024

GATE_SLOTS = 3
FOX_DEAD_LANE = HEAD_DIM + 2 * GATE_SLOTS
MLA_DEAD_LANE = MLA_NOPE + MLA_ROPE
M_INIT = -3e38


def _params(*sem):
    return pltpu.CompilerParams(dimension_semantics=sem, vmem_limit_bytes=VMEM_LIMIT)


def _tile(n, pref, mult):
    best = mult
    t = mult
    while t <= min(n, pref):
        if n % t == 0:
            best = t
        t += mult
    assert n % best == 0
    return best


def _dot(a, b):
    return jnp.dot(a, b, preferred_element_type=F32)


def _dot_nt(a, b):
    return lax.dot_general(a, b, (((1,), (1,)), ((), ())), preferred_element_type=F32)


def _layer_norm(x, g, b):
    mu = jnp.mean(x, axis=-1, keepdims=True)
    xc = x - mu
    var = jnp.mean(xc * xc, axis=-1, keepdims=True)
    return xc * lax.rsqrt(var + LN_EPS) * g + b


def _one_hot_lanes(width, period, lane):
    idx = lax.broadcasted_iota(jnp.int32, (1, width), 1)
    return jnp.where((idx & (period - 1)) == lane, 1.0, 0.0)


def _dead_rows(tile_in_batch, tm):
    pos = tile_in_batch * tm + lax.broadcasted_iota(jnp.int32, (tm, 1), 0)
    return jnp.where(pos < LEAD, NEG, 0.0)


def _embed_kernel(x_ref, meta_ref, hf_ref, hb_ref):
    t = pl.program_id(1)

    @pl.when(t == 0)
    def _():
        lead = jnp.concatenate([jnp.zeros((LEAD, D_MODEL), F32), meta_ref[...]], axis=0)
        hf_ref[...] = lead
        hb_ref[...] = lead.astype(hb_ref.dtype)

    @pl.when(t > 0)
    def _():
        hf_ref[...] = x_ref[0]
        hb_ref[...] = x_ref[0].astype(hb_ref.dtype)


def _embed(x, meta):
    batch, seq, d = x.shape
    blocks = seq // FIRST_REAL + 1
    return pl.pallas_call(
        _embed_kernel,
        grid=(batch, blocks),
        in_specs=[pl.BlockSpec((1, FIRST_REAL, d), lambda b, t: (b, jnp.maximum(t - 1, 0), 0)),
                  pl.BlockSpec((N_META, d), lambda b, t: (0, 0))],
        out_specs=[pl.BlockSpec((FIRST_REAL, d), lambda b, t: (b * blocks + t, 0)),
                   pl.BlockSpec((FIRST_REAL, d), lambda b, t: (b * blocks + t, 0))],
        out_shape=[jax.ShapeDtypeStruct((batch * blocks * FIRST_REAL, d), F32),
                   jax.ShapeDtypeStruct((batch * blocks * FIRST_REAL, d), BF16)],
        compiler_params=_params("parallel", "parallel"),
        name="embed",
    )(x, meta)


def _mm_kernel(x_ref, w_ref, o_ref):
    o_ref[...] = _dot(x_ref[...], w_ref[...]).astype(o_ref.dtype)


def _matmul(x, w, out_dtype, tm, tn):
    n, k = x.shape
    m = w.shape[1]
    return pl.pallas_call(
        _mm_kernel,
        grid=(n // tm, m // tn),
        in_specs=[pl.BlockSpec((tm, k), lambda i, j: (i, 0)),
                  pl.BlockSpec((k, tn), lambda i, j: (0, j))],
        out_specs=pl.BlockSpec((tm, tn), lambda i, j: (i, j)),
        out_shape=jax.ShapeDtypeStruct((n, m), out_dtype),
        compiler_params=_params("parallel", "parallel"),
        name="matmul",
    )(x, w)


def _head_proj_kernel(x_ref, w_ref, e_ref, o_ref):
    y = _dot(x_ref[...], w_ref[...])
    lo = lax.broadcasted_iota(jnp.int32, (1, LANES), 1) < HEAD_DIM
    shared = e_ref.shape[1] == LANES
    for pair in range(y.shape[1] // LANES):
        y_pair = y[:, pair * LANES:(pair + 1) * LANES]
        for a, feats in enumerate((y_pair, pltpu.roll(y_pair, HEAD_DIM, 1))):
            h = 2 * pair + a
            extra = e_ref[...] if shared else e_ref[:, h * LANES:(h + 1) * LANES]
            o_ref[:, h * LANES:(h + 1) * LANES] = jnp.where(lo, feats, extra.astype(F32)).astype(o_ref.dtype)


def _head_proj(x, w, e, tm, tn):
    n, k = x.shape
    m = w.shape[1]
    if e.shape[1] == LANES:
        e_spec = pl.BlockSpec((tm, LANES), lambda i, j: (i, 0))
    else:
        assert e.shape[1] == 2 * m
        e_spec = pl.BlockSpec((tm, 2 * tn), lambda i, j: (i, j))
    return pl.pallas_call(
        _head_proj_kernel,
        grid=(n // tm, m // tn),
        in_specs=[pl.BlockSpec((tm, k), lambda i, j: (i, 0)),
                  pl.BlockSpec((k, tn), lambda i, j: (0, j)),
                  e_spec],
        out_specs=pl.BlockSpec((tm, 2 * tn), lambda i, j: (i, j)),
        out_shape=jax.ShapeDtypeStruct((n, 2 * m), BF16),
        compiler_params=_params("parallel", "parallel"),
        name="head_proj",
    )(x, w, e)


def _mm_t_kernel(x_ref, w_ref, b_ref, o_ref, *, tk):
    y = _dot(x_ref[...], w_ref[...]) + b_ref[...]
    for c in range(o_ref.shape[0]):
        o_ref[c] = y[c * tk:(c + 1) * tk, :].T.astype(o_ref.dtype)


def _matmul_t(x, w, bias, tm, tn, tk):
    n, k = x.shape
    m = w.shape[1]
    r = tm // tk
    return pl.pallas_call(
        functools.partial(_mm_t_kernel, tk=tk),
        grid=(n // tm, m // tn),
        in_specs=[pl.BlockSpec((tm, k), lambda i, j: (i, 0)),
                  pl.BlockSpec((k, tn), lambda i, j: (0, j)),
                  pl.BlockSpec((1, tn), lambda i, j: (0, j))],
        out_specs=pl.BlockSpec((r, tn, tk), lambda i, j: (i, j, 0)),
        out_shape=jax.ShapeDtypeStruct((n // tk, m, tk), BF16),
        compiler_params=_params("parallel", "parallel"),
        name="matmul_t",
    )(x, w, bias)


V_ROWS = HEAD_DIM + BF16_SUBLANES
V_COLS_TILE = 8 * V_ROWS


def _value_weights(w_v):
    rows = w_v.shape[0]
    w = jnp.concatenate([w_v.reshape(rows, HEADS, HEAD_DIM),
                         jnp.zeros((rows, HEADS, V_ROWS - HEAD_DIM), w_v.dtype)], axis=2)
    bias = np.zeros((HEADS, V_ROWS), np.float32)
    bias[:, HEAD_DIM] = 1.0
    return w.reshape(rows, HEADS * V_ROWS).astype(BF16), jnp.asarray(bias.reshape(1, HEADS * V_ROWS))


def _gate_placement():
    wide = HEADS * LANES
    sq = np.zeros((LANES, wide), np.float32)
    sk = np.zeros_like(sq)
    oq = np.zeros((1, wide), np.float32)
    ok = np.zeros_like(oq)
    for h in range(HEADS):
        base = h * LANES + HEAD_DIM
        for part in range(GATE_SLOTS):
            sq[part * HEADS + h, base + part] = 1.0
            sk[part * HEADS + h, base + GATE_SLOTS + part] = -1.0
            oq[0, base + GATE_SLOTS + part] = 1.0
            ok[0, base + part] = 1.0
        oq[0, h * LANES + FOX_DEAD_LANE] = 1.0
    return sq, sk, oq, ok


def _split3(x):
    hi = x.astype(BF16)
    r1 = x - hi.astype(F32)
    mid = r1.astype(BF16)
    lo = (r1 - mid.astype(F32)).astype(BF16)
    return hi, mid, lo


def _fox_gate_kernel(x_ref, w_ref, b_ref, sq_ref, sk_ref, oq_ref, ok_ref, aq_ref, ak_ref, carry_ref,
                     *, tm, tiles_per_batch):
    i = pl.program_id(0)

    @pl.when(i % tiles_per_batch == 0)
    def _():
        carry_ref[...] = jnp.zeros_like(carry_ref)

    fg = _dot(x_ref[...], w_ref[...]) + b_ref[...]
    logf = jnp.minimum(fg, 0.0) - jnp.log(1.0 + jnp.exp(-jnp.abs(fg)))
    row = lax.broadcasted_iota(jnp.int32, (tm, tm), 0)
    col = lax.broadcasted_iota(jnp.int32, (tm, tm), 1)
    tri = jnp.where(col <= row, 1.0, 0.0).astype(BF16)
    hi, mid, lo = _split3(logf)
    cs = _dot(tri, hi) + _dot(tri, mid) + _dot(tri, lo) + carry_ref[...]
    carry_ref[...] = cs[tm - 1:tm, :]
    lane = lax.broadcasted_iota(jnp.int32, (1, LANES), 1)
    hi, mid, lo = _split3(cs * LOG2E)
    parts = jnp.where(lane < HEADS, hi.astype(F32),
                      jnp.where(lane < 2 * HEADS, mid.astype(F32), lo.astype(F32))).astype(BF16)
    wide = aq_ref.shape[1]
    dead = _dead_rows(i % tiles_per_batch, tm) * _one_hot_lanes(wide, LANES, FOX_DEAD_LANE)
    aq_ref[...] = (_dot(parts, sq_ref[...]) + oq_ref[...]).astype(aq_ref.dtype)
    ak_ref[...] = (_dot(parts, sk_ref[...]) + ok_ref[...] + dead).astype(ak_ref.dtype)


def _fox_gate(hb, w_fg, b_fg, lp, tm):
    n = hb.shape[0]
    tpb = lp // tm
    sq, sk, oq, ok = _gate_placement()
    fixed = lambda i: (0, 0)
    wide = HEADS * LANES
    return pl.pallas_call(
        functools.partial(_fox_gate_kernel, tm=tm, tiles_per_batch=tpb),
        grid=(n // tm,),
        in_specs=[pl.BlockSpec((tm, D_MODEL), lambda i: (i, 0)),
                  pl.BlockSpec((D_MODEL, LANES), fixed),
                  pl.BlockSpec((1, LANES), fixed),
                  pl.BlockSpec((LANES, wide), fixed),
                  pl.BlockSpec((LANES, wide), fixed),
                  pl.BlockSpec((1, wide), fixed),
                  pl.BlockSpec((1, wide), fixed)],
        out_specs=[pl.BlockSpec((tm, wide), lambda i: (i, 0)),
                   pl.BlockSpec((tm, wide), lambda i: (i, 0))],
        out_shape=[jax.ShapeDtypeStruct((n, wide), BF16), jax.ShapeDtypeStruct((n, wide), BF16)],
        scratch_shapes=[pltpu.VMEM((1, LANES), F32)],
        compiler_params=_params("arbitrary"),
        name="fox_gate",
    )(hb, w_fg, b_fg, jnp.asarray(sq, BF16), jnp.asarray(sk, BF16), jnp.asarray(oq), jnp.asarray(ok))


def _flash_kernel(q_ref, k_ref, vt_ref, o_ref, sa_ref, sb_ref, xa_ref, xb_ref, m_ref, acc_ref, *, tq, tk):
    i = pl.program_id(2)
    r = tq // tk
    m_ref[...] = jnp.full_like(m_ref, M_INIT)
    acc_ref[...] = jnp.zeros_like(acc_ref)

    def diag_mask(s):
        keep = lax.broadcasted_iota(jnp.int32, s.shape, 0) <= lax.broadcasted_iota(jnp.int32, s.shape, 1)
        return jnp.where(keep, s, NEG)

    def scores(s_ref, x_ref, j, a, qs, diagonal, blk=None):
        off = pl.multiple_of(j * tk, tk)
        q0 = pl.multiple_of((i if blk is None else blk) * tq + qs, tk)
        s = _dot_nt(k_ref[0, pl.ds(off, tk), a * LANES:(a + 1) * LANES],
                    q_ref[0, pl.ds(q0, tq - qs), a * LANES:(a + 1) * LANES])
        if diagonal:
            s = diag_mask(s)
        s_ref[a, :, qs:] = s
        x_ref[a, :, qs:] = jnp.max(s, axis=0, keepdims=True)

    def consume(s_ref, x_ref, j, a, qs, mask_now):
        s = s_ref[a, :, qs:]
        if mask_now:
            s = diag_mask(s)
            smax = jnp.max(s, axis=0, keepdims=True)
        else:
            smax = x_ref[a, :, qs:]
        m_old = m_ref[a, :, qs:]
        m_new = jnp.maximum(m_old, smax)
        alpha = jnp.exp2(m_old - m_new)
        p = jnp.exp2(s - m_new)
        m_ref[a, :, qs:] = m_new
        pv = _dot(vt_ref[0, j, a * V_ROWS:(a + 1) * V_ROWS, :], p.astype(BF16))
        acc_ref[a, :, qs:] = alpha * acc_ref[a, :, qs:] + pv

    buf_a, buf_b = (sa_ref, xa_ref), (sb_ref, xb_ref)
    jdiag = i * r
    odd = jdiag & 1

    @pl.when(i == 0)
    def _():
        for a in range(2):
            scores(*buf_a, 0, a, 0, False)

    @pl.when(odd == 1)
    def _():
        for a in range(2):
            scores(*buf_b, 1, a, 0, False)
            consume(*buf_a, 0, a, 0, False)
        sa_ref[...] = sb_ref[...]
        xa_ref[...] = xb_ref[...]

    def pair(j):
        for a in range(2):
            scores(*buf_b, j + 1, a, 0, False)
            consume(*buf_a, j, a, 0, False)
        for a in range(2):
            scores(*buf_a, j + 2, a, 0, False)
            consume(*buf_b, j + 1, a, 0, False)

    pairs = (jdiag - odd) // 2
    one = pairs & 1
    two = (pairs >> 1) & 1

    @pl.when(one == 1)
    def _():
        pair(odd)

    @pl.when(two == 1)
    def _():
        j = odd + 2 * one
        pair(j)
        pair(j + 2)

    def octo_body(t, c):
        j = odd + 2 * one + 4 * two + 8 * t
        for u in range(4):
            pair(j + 2 * u)
        return c

    lax.fori_loop(0, pairs >> 2, octo_body, 0)

    bufs = (buf_a, buf_b)
    for d in range(r):
        qs = d * tk
        for a in range(2):
            if d + 1 < r:
                scores(*bufs[(d + 1) & 1], jdiag + d + 1, a, qs + tk, True)
            consume(*bufs[d & 1], jdiag + d, a, qs, d == 0)

    for a in range(2):
        scores(*buf_a, 0, a, 0, False, blk=jnp.minimum(i + 1, pl.num_programs(2) - 1))

    ot = jnp.concatenate([acc_ref[a, :HEAD_DIM, :] * (1.0 / acc_ref[a, HEAD_DIM:HEAD_DIM + 1, :]) for a in range(2)],
                         axis=0)
    o_ref[0] = ot.T.astype(o_ref.dtype)


def _flash_attention(qh, kh, vt, batch, lp, tq, tk):
    nk = lp // tk
    return pl.pallas_call(
        functools.partial(_flash_kernel, tq=tq, tk=tk),
        grid=(batch, PAIRS, lp // tq),
        in_specs=[pl.BlockSpec((1, lp, 2 * LANES), lambda b, p, i: (b, 0, p)),
                  pl.BlockSpec((1, lp, 2 * LANES), lambda b, p, i: (b, 0, p)),
                  pl.BlockSpec((1, nk, 2 * V_ROWS, tk), lambda b, p, i: (b, 0, p, 0))],
        out_specs=pl.BlockSpec((1, tq, 2 * HEAD_DIM), lambda b, p, i: (b, i, p)),
        out_shape=jax.ShapeDtypeStruct((batch, lp, HEADS * HEAD_DIM), BF16),
        scratch_shapes=[pltpu.VMEM((2, tk, tq), F32), pltpu.VMEM((2, tk, tq), F32),
                        pltpu.VMEM((2, 1, tq), F32), pltpu.VMEM((2, 1, tq), F32),
                        pltpu.VMEM((2, 1, tq), F32), pltpu.VMEM((2, V_ROWS, tq), F32)],
        compiler_params=_params("parallel", "parallel", "arbitrary"),
        name="flash_attention",
    )(qh, kh, vt)


SWA_TQ = 128


def _swa_attn_kernel(sink_ref, q_ref, km_ref, kp_ref, kc_ref, vm_ref, vp_ref, vc_ref, o_ref):
    i = pl.program_id(1)
    t = SWA_TQ
    lo = lax.broadcasted_iota(jnp.int32, (1, LANES), 1) < HEAD_DIM
    row = lax.broadcasted_iota(jnp.int32, (t, 3 * t), 0)
    col = lax.broadcasted_iota(jnp.int32, (t, 3 * t), 1)
    qpos = i * t + row
    kpos = jnp.where(col < t, t + col, (i - 2) * t + col)
    d = qpos - kpos
    valid = (d >= 0) & (((col < t) & (kpos >= LEAD)) |
                        ((col >= t) & (d < WINDOW) & (kpos >= FIRST_REAL)))
    kcat, v_lo, v_hi = [], [], []
    for g in range(SWA_KV_HEADS):
        sl = slice(g * LANES, (g + 1) * LANES)
        kcat.append(jnp.concatenate([km_ref[0, :, sl], kp_ref[0, :, sl], kc_ref[0, :, sl]], axis=0))
        vf = jnp.concatenate([vm_ref[0, :, sl], vp_ref[0, :, sl], vc_ref[0, :, sl]], axis=0).astype(F32)
        v_lo.append(jnp.where(lo, vf, 0.0).astype(BF16))
        v_hi.append(jnp.where(lo, 0.0, vf).astype(BF16))
    pairs_per_group = PAIRS // SWA_KV_HEADS

    def logits(p):
        qf = q_ref[0, :, p * LANES:(p + 1) * LANES].astype(F32)
        q_pair = (jnp.where(lo, qf, 0.0).astype(BF16), jnp.where(lo, 0.0, qf).astype(BF16))
        return [_dot_nt(q_pair[a], kcat[p // pairs_per_group]) for a in range(2)]

    def finish(p, s_pair):
        g = p // pairs_per_group
        ps, inv = [], []
        for a in range(2):
            sink = sink_ref[2 * p + a]
            s = jnp.where(valid, s_pair[a], NEG)
            m = jnp.maximum(jnp.max(s, axis=1, keepdims=True), sink)
            e = jnp.exp(s - m)
            den = jnp.sum(e, axis=1, keepdims=True) + jnp.exp(sink - m)
            ps.append(e.astype(BF16))
            inv.append(1.0 / den)
        o = (_dot(ps[0], v_lo[g]) + _dot(ps[1], v_hi[g])) * jnp.where(lo, inv[0], inv[1])
        o_ref[0, :, p * LANES:(p + 1) * LANES] = o.astype(o_ref.dtype)

    s_next = logits(0)
    for p in range(PAIRS):
        s_cur = s_next
        if p + 1 < PAIRS:
            s_next = logits(p + 1)
        finish(p, s_cur)


def _swa_attention(qkv, sinks, batch, lp):
    t = SWA_TQ
    kblk, vblk = 4, 5
    kv_spec = lambda col, row_of: pl.BlockSpec((1, t, 2 * LANES), lambda b, i: (b, row_of(i), col))
    meta = lambda i: 1
    prev = lambda i: jnp.maximum(i - 1, 0)
    cur = lambda i: i
    return pl.pallas_call(
        _swa_attn_kernel,
        grid=(batch, lp // t),
        in_specs=[pl.BlockSpec(memory_space=pltpu.SMEM),
                  pl.BlockSpec((1, t, HEADS * HEAD_DIM), lambda b, i: (b, i, 0)),
                  kv_spec(kblk, meta), kv_spec(kblk, prev), kv_spec(kblk, cur),
                  kv_spec(vblk, meta), kv_spec(vblk, prev), kv_spec(vblk, cur)],
        out_specs=pl.BlockSpec((1, t, HEADS * HEAD_DIM), lambda b, i: (b, i, 0)),
        out_shape=jax.ShapeDtypeStruct((batch, lp, HEADS * HEAD_DIM), BF16),
        compiler_params=_params("parallel", "parallel"),
        name="swa_attention",
    )(sinks, qkv, qkv, qkv, qkv, qkv, qkv, qkv)


def _swa_proj_kernel(x_ref, w_ref, cos_ref, sin_ref, perm_ref, o_ref, *, tn, n_rope_blocks):
    j = pl.program_id(1)
    y = _dot(x_ref[...], w_ref[...])

    @pl.when(j < n_rope_blocks)
    def _():
        reps = tn // LANES
        cos = jnp.concatenate([cos_ref[...]] * reps, axis=1)
        sin = jnp.concatenate([sin_ref[...]] * reps, axis=1)
        partner = _dot(y.astype(BF16), perm_ref[...])
        o_ref[...] = (y * cos + partner * sin).astype(o_ref.dtype)

    @pl.when(j >= n_rope_blocks)
    def _():
        o_ref[...] = y.astype(o_ref.dtype)


def _swa_proj(hb, w, cos, sin, lp, tm, tn, n_rope_blocks):
    n = hb.shape[0]
    m = w.shape[1]
    tpb = lp // tm
    return pl.pallas_call(
        functools.partial(_swa_proj_kernel, tn=tn, n_rope_blocks=n_rope_blocks),
        grid=(n // tm, m // tn),
        in_specs=[pl.BlockSpec((tm, D_MODEL), lambda i, j: (i, 0)),
                  pl.BlockSpec((D_MODEL, tn), lambda i, j: (0, j)),
                  pl.BlockSpec((tm, LANES), lambda i, j: (i % tpb, 0)),
                  pl.BlockSpec((tm, LANES), lambda i, j: (i % tpb, 0)),
                  pl.BlockSpec((tn, tn), lambda i, j: (0, 0))],
        out_specs=pl.BlockSpec((tm, tn), lambda i, j: (i, j)),
        out_shape=jax.ShapeDtypeStruct((n, m), BF16),
        compiler_params=_params("parallel", "parallel"),
        name="swa_proj",
    )(hb, w, cos, sin, jnp.asarray(_rope_partner_matrix(tn), BF16))


def _rope_partner_matrix(width):
    p = np.zeros((width, width), np.float32)
    half = ROPE_DIM // 2
    for base in range(0, width, HEAD_DIM):
        for l in range(half):
            p[base + l + half, base + l] = 1.0
            p[base + l, base + l + half] = 1.0
    return p


MLA_A_COLS = MLA_Q_LORA + MLA_KV_LORA + 2 * LANES


def _mla_a_kernel(x_ref, w_ref, gq_ref, gkv_ref, cos_ref, sin_ref, cq_ref, ckv_ref, kr_ref, *, tm, tiles_per_batch):
    y = _dot(x_ref[...], w_ref[...])
    cq = y[:, :MLA_Q_LORA]
    ckv = y[:, MLA_Q_LORA:MLA_Q_LORA + MLA_KV_LORA]
    kr = y[:, MLA_Q_LORA + MLA_KV_LORA:MLA_Q_LORA + MLA_KV_LORA + LANES]
    krs = y[:, MLA_Q_LORA + MLA_KV_LORA + LANES:]
    rms = lambda z, g: z * lax.rsqrt(jnp.mean(z * z, axis=-1, keepdims=True) + RMS_EPS) * g
    cq_ref[...] = rms(cq, gq_ref[...]).astype(cq_ref.dtype)
    ckv_ref[...] = rms(ckv, gkv_ref[...]).astype(ckv_ref.dtype)
    dead = _dead_rows(pl.program_id(0) % tiles_per_batch, tm) * _one_hot_lanes(LANES, LANES, MLA_DEAD_LANE)
    kr_ref[...] = (kr * cos_ref[...] + krs * sin_ref[...] + dead).astype(kr_ref.dtype)


def _mla_a(hb, w, gq, gkv, cos, sin, lp, tm):
    n = hb.shape[0]
    tpb = lp // tm
    return pl.pallas_call(
        functools.partial(_mla_a_kernel, tm=tm, tiles_per_batch=tpb),
        grid=(n // tm,),
        in_specs=[pl.BlockSpec((tm, D_MODEL), lambda i: (i, 0)),
                  pl.BlockSpec((D_MODEL, MLA_A_COLS), lambda i: (0, 0)),
                  pl.BlockSpec((1, MLA_Q_LORA), lambda i: (0, 0)),
                  pl.BlockSpec((1, MLA_KV_LORA), lambda i: (0, 0)),
                  pl.BlockSpec((tm, LANES), lambda i: (i % tpb, 0)),
                  pl.BlockSpec((tm, LANES), lambda i: (i % tpb, 0))],
        out_specs=[pl.BlockSpec((tm, MLA_Q_LORA), lambda i: (i, 0)),
                   pl.BlockSpec((tm, MLA_KV_LORA), lambda i: (i, 0)),
                   pl.BlockSpec((tm, LANES), lambda i: (i, 0))],
        out_shape=[jax.ShapeDtypeStruct((n, MLA_Q_LORA), BF16),
                   jax.ShapeDtypeStruct((n, MLA_KV_LORA), BF16),
                   jax.ShapeDtypeStruct((n, LANES), BF16)],
        compiler_params=_params("parallel"),
        name="mla_a",
    )(hb, w, gq, gkv, cos, sin)


def _mla_q_kernel(x_ref, w_ref, ws_ref, cos_ref, sin_ref, o_ref):
    x = x_ref[...]
    reps = o_ref.shape[1] // LANES
    cos = jnp.concatenate([cos_ref[...]] * reps, axis=1)
    sin = jnp.concatenate([sin_ref[...]] * reps, axis=1)
    y = _dot(x, w_ref[...]) * cos + _dot(x, ws_ref[...]) * sin
    o_ref[...] = (y + _one_hot_lanes(o_ref.shape[1], LANES, MLA_DEAD_LANE)).astype(o_ref.dtype)


def _mla_q(cq, w, ws, cos, sin, lp, tm, tn):
    n = cq.shape[0]
    m = w.shape[1]
    tpb = lp // tm
    return pl.pallas_call(
        _mla_q_kernel,
        grid=(n // tm, m // tn),
        in_specs=[pl.BlockSpec((tm, MLA_Q_LORA), lambda i, j: (i, 0)),
                  pl.BlockSpec((MLA_Q_LORA, tn), lambda i, j: (0, j)),
                  pl.BlockSpec((MLA_Q_LORA, tn), lambda i, j: (0, j)),
                  pl.BlockSpec((tm, LANES), lambda i, j: (i % tpb, 0)),
                  pl.BlockSpec((tm, LANES), lambda i, j: (i % tpb, 0))],
        out_specs=pl.BlockSpec((tm, tn), lambda i, j: (i, j)),
        out_shape=jax.ShapeDtypeStruct((n, m), BF16),
        compiler_params=_params("parallel", "parallel"),
        name="mla_q",
    )(cq, w, ws, cos, sin)


def _oproj_ln_kernel(o_ref, w_ref, h_ref, g_ref, b_ref, hf_ref, hb_ref):
    x = DEEPNORM_ALPHA * h_ref[...] + _dot(o_ref[...], w_ref[...])
    y = _layer_norm(x, g_ref[...], b_ref[...])
    hf_ref[...] = y
    hb_ref[...] = y.astype(hb_ref.dtype)


def _oproj_ln(o, w, h, g, b, tm):
    n = o.shape[0]
    row = lambda i: (i, 0)
    fixed = lambda i: (0, 0)
    return pl.pallas_call(
        _oproj_ln_kernel,
        grid=(n // tm,),
        in_specs=[pl.BlockSpec((tm, D_MODEL), row), pl.BlockSpec((D_MODEL, D_MODEL), fixed),
                  pl.BlockSpec((tm, D_MODEL), row), pl.BlockSpec((1, D_MODEL), fixed),
                  pl.BlockSpec((1, D_MODEL), fixed)],
        out_specs=[pl.BlockSpec((tm, D_MODEL), row), pl.BlockSpec((tm, D_MODEL), row)],
        out_shape=[jax.ShapeDtypeStruct((n, D_MODEL), F32), jax.ShapeDtypeStruct((n, D_MODEL), BF16)],
        compiler_params=_params("parallel"),
        name="oproj_ln",
    )(o, w, h, g, b)


FFN_HALO = BF16_SUBLANES


FFN_CHUNK = 256
FFN_NC = D_FF // FFN_CHUNK
assert FFN_NC * FFN_CHUNK == D_FF and FFN_NC % 2 == 1


def _ffn_kernel(x_ref, halo_ref, h_ref, win_ref, cw_ref, cb_ref, wo_ref, g_ref, b_ref, hf_ref, hb_ref,
                xext_ref, ua_ref, ub_ref, acc_ref, *, tm, tiles_per_batch):
    i = pl.program_id(0)
    nc = FFN_NC
    pos = (i % tiles_per_batch) * tm - FFN_HALO + lax.broadcasted_iota(jnp.int32, (tm + FFN_HALO, 1), 0)
    xe = jnp.concatenate([halo_ref[...], x_ref[...]], axis=0).astype(F32)
    xext_ref[...] = jnp.where(pos >= LEAD, xe, 0.0).astype(BF16)

    def up(u_ref, c):
        u_ref[0] = _dot(xext_ref[...], win_ref[c])
        u_ref[1] = _dot(xext_ref[...], win_ref[nc + c])

    def glu(u_ref, c):
        def conv(part, idx):
            u = u_ref[part]
            delayed = u * cw_ref[idx, 0:1, :]
            for tap in range(1, CONV_W):
                delayed = u * cw_ref[idx, tap:tap + 1, :] + pltpu.roll(delayed, 1, 0)
            return cb_ref[idx] + delayed[FFN_HALO:, :]

        yg = conv(0, c)
        yv = conv(1, nc + c)
        return ((yg / (1.0 + jnp.exp(-yg))) * yv).astype(BF16)

    acc_ref[...] = DEEPNORM_ALPHA * h_ref[...]
    up(ua_ref, 0)

    def pair_body(t, carry):
        c = 2 * t
        up(ub_ref, c + 1)
        act_a = glu(ua_ref, c)
        up(ua_ref, c + 2)
        act_b = glu(ub_ref, c + 1)
        acc_ref[...] += _dot(act_a, wo_ref[c]) + _dot(act_b, wo_ref[c + 1])
        return carry

    lax.fori_loop(0, (nc - 1) // 2, pair_body, 0)
    y = _layer_norm(acc_ref[...] + _dot(glu(ua_ref, nc - 1), wo_ref[nc - 1]), g_ref[...], b_ref[...])
    hf_ref[...] = y
    hb_ref[...] = y.astype(hb_ref.dtype)


def _ffn(hb, hf, w_in, conv_w, conv_b, w_out, g, b, lp, tm):
    n = hb.shape[0]
    nc, fc = FFN_NC, FFN_CHUNK
    tpb = lp // tm
    halo_blocks = tm // FFN_HALO
    row = lambda i: (i, 0)
    fixed2 = lambda i: (0, 0)
    fixed3 = lambda i: (0, 0, 0)
    resident = dict(pipeline_mode=pl.Buffered(1))
    return pl.pallas_call(
        functools.partial(_ffn_kernel, tm=tm, tiles_per_batch=tpb),
        grid=(n // tm,),
        in_specs=[pl.BlockSpec((tm, D_MODEL), row),
                  pl.BlockSpec((FFN_HALO, D_MODEL), lambda i: (jnp.maximum(i * halo_blocks - 1, 0), 0)),
                  pl.BlockSpec((tm, D_MODEL), row),
                  pl.BlockSpec((2 * nc, D_MODEL, fc), fixed3, **resident),
                  pl.BlockSpec((2 * nc, CONV_W, fc), fixed3, **resident),
                  pl.BlockSpec((2 * nc, 1, fc), fixed3, **resident),
                  pl.BlockSpec((nc, fc, D_MODEL), fixed3, **resident),
                  pl.BlockSpec((1, D_MODEL), fixed2), pl.BlockSpec((1, D_MODEL), fixed2)],
        out_specs=[pl.BlockSpec((tm, D_MODEL), row), pl.BlockSpec((tm, D_MODEL), row)],
        out_shape=[jax.ShapeDtypeStruct((n, D_MODEL), F32), jax.ShapeDtypeStruct((n, D_MODEL), BF16)],
        scratch_shapes=[pltpu.VMEM((tm + FFN_HALO, D_MODEL), BF16),
                        pltpu.VMEM((2, tm + FFN_HALO, fc), F32), pltpu.VMEM((2, tm + FFN_HALO, fc), F32),
                        pltpu.VMEM((tm, D_MODEL), F32)],
        compiler_params=_params("parallel"),
        name="ffn",
    )(hb, hb, hf, w_in, conv_w, conv_b, w_out, g, b)


def _ffn_weights(w_in, conv_w, conv_b, w_out):
    d = w_in.shape[0]
    chunks = 2 * FFN_NC
    return (w_in.reshape(d, chunks, FFN_CHUNK).transpose(1, 0, 2).astype(BF16),
            conv_w.reshape(CONV_W, chunks, FFN_CHUNK).transpose(1, 0, 2),
            conv_b.reshape(chunks, 1, FFN_CHUNK),
            w_out.reshape(FFN_NC, FFN_CHUNK, d).astype(BF16))


def _rope_tables(lp, dim, theta, group, offset):
    pos = (jnp.arange(lp) - LEAD).astype(F32)
    inv = theta ** (-jnp.arange(0, dim, 2, dtype=F32) / dim)
    ang = pos[:, None] * inv[None, :]
    cos, sin = jnp.cos(ang), jnp.sin(ang)
    ones = lambda w: jnp.ones((lp, w), F32)
    zeros = lambda w: jnp.zeros((lp, w), F32)
    rest = group - offset - dim
    cos_g = jnp.concatenate([ones(offset), cos, cos, ones(rest)], axis=1)
    sin_g = jnp.concatenate([zeros(offset), -sin, sin, zeros(rest)], axis=1)
    reps = LANES // group
    return jnp.tile(cos_g, (1, reps)), jnp.tile(sin_g, (1, reps))


def _swap_halves(w, dim):
    return jnp.concatenate([w[..., dim // 2:dim], w[..., :dim // 2]], axis=-1)


def _head_blocks(main, extra):
    src = main if main is not None else extra
    rows, heads = src.shape[0], src.shape[1]
    m = main if main is not None else jnp.zeros((rows, heads, HEAD_DIM), F32)
    e = extra if extra is not None else jnp.zeros((rows, heads, 0), F32)
    pad = jnp.zeros((rows, heads, LANES - HEAD_DIM - e.shape[2]), F32)
    return jnp.concatenate([m, e, pad], axis=2).reshape(rows, heads * LANES)


def _mla_weights(w_a, w_uq, w_ukv):
    d = w_a.shape[0]
    w_kr = w_a[:, MLA_Q_LORA + MLA_KV_LORA:][:, None, :]
    w_a_cat = jnp.concatenate([w_a[:, :MLA_Q_LORA + MLA_KV_LORA], _head_blocks(None, w_kr),
                               _head_blocks(None, _swap_halves(w_kr, MLA_ROPE))], axis=1)
    scale = (MLA_NOPE + MLA_ROPE) ** -0.5 * LOG2E
    wq = (w_uq * scale).reshape(MLA_Q_LORA, HEADS, MLA_NOPE + MLA_ROPE)
    w_main = _head_blocks(wq[..., :MLA_NOPE], wq[..., MLA_NOPE:])
    w_swap = _head_blocks(None, _swap_halves(wq[..., MLA_NOPE:], MLA_ROPE))
    wkv = w_ukv.reshape(MLA_KV_LORA, HEADS, MLA_NOPE + HEAD_DIM)
    w_kn = wkv[..., :MLA_NOPE].reshape(MLA_KV_LORA, HEADS * MLA_NOPE)
    w_v = wkv[..., MLA_NOPE:].reshape(MLA_KV_LORA, HEADS * HEAD_DIM)
    return tuple(w.astype(BF16) for w in (w_a_cat, w_main, w_swap, w_kn)) + (w_v,)


def _swa_weights(w_in):
    qd = HEADS * HEAD_DIM
    kd = SWA_KV_HEADS * HEAD_DIM
    q = w_in[:, :qd] * (HEAD_DIM ** -0.5)
    dup = lambda w: jnp.concatenate([w[:, :HEAD_DIM], w[:, :HEAD_DIM], w[:, HEAD_DIM:], w[:, HEAD_DIM:]], axis=1)
    return jnp.concatenate([q, dup(w_in[:, qd:qd + kd]), dup(w_in[:, qd + kd:])], axis=1).astype(BF16)


def _fox_weights(w_in, b_f):
    hd = HEADS * HEAD_DIM
    d = w_in.shape[0]
    w_q = (w_in[:, :hd] * (HEAD_DIM ** -0.5 * LOG2E)).astype(BF16)
    w_k = w_in[:, hd:2 * hd].astype(BF16)
    w_v = w_in[:, 2 * hd:3 * hd]
    w_gate = w_in[:, 3 * hd:]
    w_fg = jnp.concatenate([w_gate] * GATE_SLOTS + [jnp.zeros((d, LANES - GATE_SLOTS * HEADS), F32)],
                           axis=1).astype(BF16)
    b_fg = jnp.concatenate([b_f] * GATE_SLOTS + [jnp.zeros((LANES - GATE_SLOTS * HEADS,), F32)])[None, :]
    return w_q, w_k, w_v, w_fg, b_fg


def kernel(x, meta_tokens, ln1_g, ln1_b, ln2_g, ln2_b, fox_w_in, fox_b_f, fox_w_o, swa_w_in, swa_sinks, swa_w_o,
           mla_w_a, mla_g_q, mla_g_kv, mla_w_uq, mla_w_ukv, mla_w_o, ffn_w_in, ffn_conv_w, ffn_conv_b, ffn_w_out):
    batch, seq, d = x.shape
    assert d == D_MODEL and seq % 256 == 0
    lp = seq + FIRST_REAL
    n = batch * lp

    tm = _tile(lp, 768, 256)
    tq = _tile(lp, 768, 256)
    tk = 256
    tn = 512
    nk = lp // tk

    hf, hb = _embed(x, meta_tokens.astype(x.dtype))

    cos_p, sin_p = _rope_tables(lp, ROPE_DIM, ROPE_THETA, HEAD_DIM, 0)
    cos_m, sin_m = _rope_tables(lp, MLA_ROPE, MLA_ROPE_THETA, LANES, MLA_NOPE)
    b3 = lambda a: a.reshape(batch, lp, -1)

    for i in range(DEPTH):
        kind, j = i % 3, i // 3
        if kind == 0:
            w_q, w_k, w_v, w_fg, b_fg = _fox_weights(fox_w_in[j], fox_b_f[j])
            aq, ak = _fox_gate(hb, w_fg, b_fg, lp, tm)
            qh = _head_proj(hb, w_q, aq, tm, 2 * tn)
            kh = _head_proj(hb, w_k, ak, tm, 2 * tn)
            vt = _matmul_t(hb, *_value_weights(w_v), tm, V_COLS_TILE, tk).reshape(batch, nk, HEADS * V_ROWS, tk)
            o = _flash_attention(b3(qh), b3(kh), vt, batch, lp, tq, tk)
            w_o = fox_w_o[j]
        elif kind == 1:
            qkv = _swa_proj(hb, _swa_weights(swa_w_in[j]), cos_p, sin_p, lp, tm, 2 * LANES, 5)
            o = _swa_attention(b3(qkv), swa_sinks[j].astype(F32), batch, lp)
            w_o = swa_w_o[j]
        else:
            w_a_cat, w_main, w_swap, w_kn, w_v = _mla_weights(mla_w_a[j], mla_w_uq[j], mla_w_ukv[j])
            cq, ckv, kr = _mla_a(hb, w_a_cat, mla_g_q[j][None, :], mla_g_kv[j][None, :], cos_m, sin_m, lp, tm)
            qh = _mla_q(cq, w_main, w_swap, cos_m, sin_m, lp, tm, 2 * tn)
            kh = _head_proj(ckv, w_kn, kr, tm, 2 * tn)
            vt = _matmul_t(ckv, *_value_weights(w_v), tm, V_COLS_TILE, tk).reshape(batch, nk, HEADS * V_ROWS, tk)
            o = _flash_attention(b3(qh), b3(kh), vt, batch, lp, tq, tk)
            w_o = mla_w_o[j]
        hf, hb = _oproj_ln(o.reshape(n, d), w_o.astype(BF16), hf, ln1_g[i][None, :], ln1_b[i][None, :], tm)
        hf, hb = _ffn(hb, hf, *_ffn_weights(ffn_w_in[i], ffn_conv_w[i], ffn_conv_b[i], ffn_w_out[i]),
                      ln2_g[i][None, :], ln2_b[i][None, :], lp, tm)
    return hf.reshape(batch, lp, d)[:, FIRST_REAL:]
```

```python
import functools
import math

import numpy as np
import jax
import jax.numpy as jnp
from jax import lax
from jax.experimental import pallas as pl
from jax.experimental.pallas import tpu as pltpu

F32 = jnp.float32
BF16 = jnp.bfloat16

D_MODEL = 1024
DEPTH = 4
N_META = 16
LEAD = 240
FIRST_REAL = LEAD + N_META
NEG = -1e30
DEEPNORM_ALPHA = (2.0 * DEPTH) ** 0.25
LN_EPS = 1e-5
RMS_EPS = 1e-6
HEADS = 16
HEAD_DIM = 64
PAIRS = HEADS // 2
SWA_KV_HEADS = 2
WINDOW = 128
ROPE_THETA = 500000.0
ROPE_DIM = 16
MLA_Q_LORA = 384
MLA_KV_LORA = 256
MLA_NOPE = 64
MLA_ROPE = 32
MLA_ROPE_THETA = 10000.0
D_FF = 2816
CONV_W = 3
LOG2E = math.log2(math.e)

LANES = 128
BF16_SUBLANES = 16
VMEM_LIMIT = 56 * 1024 * 1024

GATE_SLOTS = 3
FOX_DEAD_LANE = HEAD_DIM + 2 * GATE_SLOTS
MLA_DEAD_LANE = MLA_NOPE + MLA_ROPE
M_INIT = -3e38


def _params(*sem):
    return pltpu.CompilerParams(dimension_semantics=sem, vmem_limit_bytes=VMEM_LIMIT)


def _tile(n, pref, mult):
    best = mult
    t = mult
    while t <= min(n, pref):
        if n % t == 0:
            best = t
        t += mult
    assert n % best == 0
    return best


def _dot(a, b):
    return jnp.dot(a, b, preferred_element_type=F32)


def _dot_nt(a, b):
    return lax.dot_general(a, b, (((1,), (1,)), ((), ())), preferred_element_type=F32)


def _layer_norm(x, g, b):
    mu = jnp.mean(x, axis=-1, keepdims=True)
    xc = x - mu
    var = jnp.mean(xc * xc, axis=-1, keepdims=True)
    return xc * lax.rsqrt(var + LN_EPS) * g + b


def _one_hot_lanes(width, period, lane):
    idx = lax.broadcasted_iota(jnp.int32, (1, width), 1)
    return jnp.where((idx & (period - 1)) == lane, 1.0, 0.0)


def _dead_rows(tile_in_batch, tm):
    pos = tile_in_batch * tm + lax.broadcasted_iota(jnp.int32, (tm, 1), 0)
    return jnp.where(pos < LEAD, NEG, 0.0)


def _embed_kernel(x_ref, meta_ref, hf_ref, hb_ref):
    t = pl.program_id(1)

    @pl.when(t == 0)
    def _():
        lead = jnp.concatenate([jnp.zeros((LEAD, D_MODEL), F32), meta_ref[...]], axis=0)
        hf_ref[...] = lead
        hb_ref[...] = lead.astype(hb_ref.dtype)

    @pl.when(t > 0)
    def _():
        hf_ref[...] = x_ref[0]
        hb_ref[...] = x_ref[0].astype(hb_ref.dtype)


def _embed(x, meta):
    batch, seq, d = x.shape
    blocks = seq // FIRST_REAL + 1
    return pl.pallas_call(
        _embed_kernel,
        grid=(batch, blocks),
        in_specs=[pl.BlockSpec((1, FIRST_REAL, d), lambda b, t: (b, jnp.maximum(t - 1, 0), 0)),
                  pl.BlockSpec((N_META, d), lambda b, t: (0, 0))],
        out_specs=[pl.BlockSpec((FIRST_REAL, d), lambda b, t: (b * blocks + t, 0)),
                   pl.BlockSpec((FIRST_REAL, d), lambda b, t: (b * blocks + t, 0))],
        out_shape=[jax.ShapeDtypeStruct((batch * blocks * FIRST_REAL, d), F32),
                   jax.ShapeDtypeStruct((batch * blocks * FIRST_REAL, d), BF16)],
        compiler_params=_params("parallel", "parallel"),
        name="embed",
    )(x, meta)


def _mm_kernel(x_ref, w_ref, o_ref):
    o_ref[...] = _dot(x_ref[...], w_ref[...]).astype(o_ref.dtype)


def _matmul(x, w, out_dtype, tm, tn):
    n, k = x.shape
    m = w.shape[1]
    return pl.pallas_call(
        _mm_kernel,
        grid=(n // tm, m // tn),
        in_specs=[pl.BlockSpec((tm, k), lambda i, j: (i, 0)),
                  pl.BlockSpec((k, tn), lambda i, j: (0, j))],
        out_specs=pl.BlockSpec((tm, tn), lambda i, j: (i, j)),
        out_shape=jax.ShapeDtypeStruct((n, m), out_dtype),
        compiler_params=_params("parallel", "parallel"),
        name="matmul",
    )(x, w)


def _head_proj_kernel(x_ref, w_ref, e_ref, o_ref):
    y = _dot(x_ref[...], w_ref[...])
    lo = lax.broadcasted_iota(jnp.int32, (1, LANES), 1) < HEAD_DIM
    shared = e_ref.shape[1] == LANES
    for pair in range(y.shape[1] // LANES):
        y_pair = y[:, pair * LANES:(pair + 1) * LANES]
        for a, feats in enumerate((y_pair, pltpu.roll(y_pair, HEAD_DIM, 1))):
            h = 2 * pair + a
            extra = e_ref[...] if shared else e_ref[:, h * LANES:(h + 1) * LANES]
            o_ref[:, h * LANES:(h + 1) * LANES] = jnp.where(lo, feats, extra.astype(F32)).astype(o_ref.dtype)


def _head_proj(x, w, e, tm, tn):
    n, k = x.shape
    m = w.shape[1]
    if e.shape[1] == LANES:
        e_spec = pl.BlockSpec((tm, LANES), lambda i, j: (i, 0))
    else:
        assert e.shape[1] == 2 * m
        e_spec = pl.BlockSpec((tm, 2 * tn), lambda i, j: (i, j))
    return pl.pallas_call(
        _head_proj_kernel,
        grid=(n // tm, m // tn),
        in_specs=[pl.BlockSpec((tm, k), lambda i, j: (i, 0)),
                  pl.BlockSpec((k, tn), lambda i, j: (0, j)),
                  e_spec],
        out_specs=pl.BlockSpec((tm, 2 * tn), lambda i, j: (i, j)),
        out_shape=jax.ShapeDtypeStruct((n, 2 * m), BF16),
        compiler_params=_params("parallel", "parallel"),
        name="head_proj",
    )(x, w, e)


def _mm_t_kernel(x_ref, w_ref, b_ref, o_ref, *, tk):
    y = _dot(x_ref[...], w_ref[...]) + b_ref[...]
    for c in range(o_ref.shape[0]):
        o_ref[c] = y[c * tk:(c + 1) * tk, :].T.astype(o_ref.dtype)


def _matmul_t(x, w, bias, tm, tn, tk):
    n, k = x.shape
    m = w.shape[1]
    r = tm // tk
    return pl.pallas_call(
        functools.partial(_mm_t_kernel, tk=tk),
        grid=(n // tm, m // tn),
        in_specs=[pl.BlockSpec((tm, k), lambda i, j: (i, 0)),
                  pl.BlockSpec((k, tn), lambda i, j: (0, j)),
                  pl.BlockSpec((1, tn), lambda i, j: (0, j))],
        out_specs=pl.BlockSpec((r, tn, tk), lambda i, j: (i, j, 0)),
        out_shape=jax.ShapeDtypeStruct((n // tk, m, tk), BF16),
        compiler_params=_params("parallel", "parallel"),
        name="matmul_t",
    )(x, w, bias)


V_ROWS = HEAD_DIM + BF16_SUBLANES
V_COLS_TILE = 8 * V_ROWS


def _value_weights(w_v):
    rows = w_v.shape[0]
    w = jnp.concatenate([w_v.reshape(rows, HEADS, HEAD_DIM),
                         jnp.zeros((rows, HEADS, V_ROWS - HEAD_DIM), w_v.dtype)], axis=2)
    bias = np.zeros((HEADS, V_ROWS), np.float32)
    bias[:, HEAD_DIM] = 1.0
    return w.reshape(rows, HEADS * V_ROWS).astype(BF16), jnp.asarray(bias.reshape(1, HEADS * V_ROWS))


def _gate_placement():
    wide = HEADS * LANES
    sq = np.zeros((LANES, wide), np.float32)
    sk = np.zeros_like(sq)
    oq = np.zeros((1, wide), np.float32)
    ok = np.zeros_like(oq)
    for h in range(HEADS):
        base = h * LANES + HEAD_DIM
        for part in range(GATE_SLOTS):
            sq[part * HEADS + h, base + part] = 1.0
            sk[part * HEADS + h, base + GATE_SLOTS + part] = -1.0
            oq[0, base + GATE_SLOTS + part] = 1.0
            ok[0, base + part] = 1.0
        oq[0, h * LANES + FOX_DEAD_LANE] = 1.0
    return sq, sk, oq, ok


def _split3(x):
    hi = x.astype(BF16)
    r1 = x - hi.astype(F32)
    mid = r1.astype(BF16)
    lo = (r1 - mid.astype(F32)).astype(BF16)
    return hi, mid, lo


def _fox_gate_kernel(x_ref, w_ref, b_ref, sq_ref, sk_ref, oq_ref, ok_ref, aq_ref, ak_ref, carry_ref,
                     *, tm, tiles_per_batch):
    i = pl.program_id(0)

    @pl.when(i % tiles_per_batch == 0)
    def _():
        carry_ref[...] = jnp.zeros_like(carry_ref)

    fg = _dot(x_ref[...], w_ref[...]) + b_ref[...]
    logf = jnp.minimum(fg, 0.0) - jnp.log(1.0 + jnp.exp(-jnp.abs(fg)))
    row = lax.broadcasted_iota(jnp.int32, (tm, tm), 0)
    col = lax.broadcasted_iota(jnp.int32, (tm, tm), 1)
    tri = jnp.where(col <= row, 1.0, 0.0).astype(BF16)
    hi, mid, lo = _split3(logf)
    cs = _dot(tri, hi) + _dot(tri, mid) + _dot(tri, lo) + carry_ref[...]
    carry_ref[...] = cs[tm - 1:tm, :]
    lane = lax.broadcasted_iota(jnp.int32, (1, LANES), 1)
    hi, mid, lo = _split3(cs * LOG2E)
    parts = jnp.where(lane < HEADS, hi.astype(F32),
                      jnp.where(lane < 2 * HEADS, mid.astype(F32), lo.astype(F32))).astype(BF16)
    wide = aq_ref.shape[1]
    dead = _dead_rows(i % tiles_per_batch, tm) * _one_hot_lanes(wide, LANES, FOX_DEAD_LANE)
    aq_ref[...] = (_dot(parts, sq_ref[...]) + oq_ref[...]).astype(aq_ref.dtype)
    ak_ref[...] = (_dot(parts, sk_ref[...]) + ok_ref[...] + dead).astype(ak_ref.dtype)


def _fox_gate(hb, w_fg, b_fg, lp, tm):
    n = hb.shape[0]
    tpb = lp // tm
    sq, sk, oq, ok = _gate_placement()
    fixed = lambda i: (0, 0)
    wide = HEADS * LANES
    return pl.pallas_call(
        functools.partial(_fox_gate_kernel, tm=tm, tiles_per_batch=tpb),
        grid=(n // tm,),
        in_specs=[pl.BlockSpec((tm, D_MODEL), lambda i: (i, 0)),
                  pl.BlockSpec((D_MODEL, LANES), fixed),
                  pl.BlockSpec((1, LANES), fixed),
                  pl.BlockSpec((LANES, wide), fixed),
                  pl.BlockSpec((LANES, wide), fixed),
                  pl.BlockSpec((1, wide), fixed),
                  pl.BlockSpec((1, wide), fixed)],
        out_specs=[pl.BlockSpec((tm, wide), lambda i: (i, 0)),
                   pl.BlockSpec((tm, wide), lambda i: (i, 0))],
        out_shape=[jax.ShapeDtypeStruct((n, wide), BF16), jax.ShapeDtypeStruct((n, wide), BF16)],
        scratch_shapes=[pltpu.VMEM((1, LANES), F32)],
        compiler_params=_params("arbitrary"),
        name="fox_gate",
    )(hb, w_fg, b_fg, jnp.asarray(sq, BF16), jnp.asarray(sk, BF16), jnp.asarray(oq), jnp.asarray(ok))


def _flash_kernel(q_ref, k_ref, vt_ref, o_ref, sa_ref, sb_ref, xa_ref, xb_ref, m_ref, acc_ref, *, tq, tk):
    i = pl.program_id(2)
    r = tq // tk
    m_ref[...] = jnp.full_like(m_ref, M_INIT)
    acc_ref[...] = jnp.zeros_like(acc_ref)

    def diag_mask(s):
        keep = lax.broadcasted_iota(jnp.int32, s.shape, 0) <= lax.broadcasted_iota(jnp.int32, s.shape, 1)
        return jnp.where(keep, s, NEG)

    def scores(s_ref, x_ref, j, a, qs, diagonal, blk=None):
        off = pl.multiple_of(j * tk, tk)
        q0 = pl.multiple_of((i if blk is None else blk) * tq + qs, tk)
        s = _dot_nt(k_ref[0, pl.ds(off, tk), a * LANES:(a + 1) * LANES],
                    q_ref[0, pl.ds(q0, tq - qs), a * LANES:(a + 1) * LANES])
        if diagonal:
            s = diag_mask(s)
        s_ref[a, :, qs:] = s
        x_ref[a, :, qs:] = jnp.max(s, axis=0, keepdims=True)

    def consume(s_ref, x_ref, j, a, qs, mask_now):
        s = s_ref[a, :, qs:]
        if mask_now:
            s = diag_mask(s)
            smax = jnp.max(s, axis=0, keepdims=True)
        else:
            smax = x_ref[a, :, qs:]
        m_old = m_ref[a, :, qs:]
        m_new = jnp.maximum(m_old, smax)
        alpha = jnp.exp2(m_old - m_new)
        p = jnp.exp2(s - m_new)
        m_ref[a, :, qs:] = m_new
        pv = _dot(vt_ref[0, j, a * V_ROWS:(a + 1) * V_ROWS, :], p.astype(BF16))
        acc_ref[a, :, qs:] = alpha * acc_ref[a, :, qs:] + pv

    buf_a, buf_b = (sa_ref, xa_ref), (sb_ref, xb_ref)
    jdiag = i * r
    odd = jdiag & 1

    @pl.when(i == 0)
    def _():
        for a in range(2):
            scores(*buf_a, 0, a, 0, False)

    @pl.when(odd == 1)
    def _():
        for a in range(2):
            scores(*buf_b, 1, a, 0, False)
            consume(*buf_a, 0, a, 0, False)
        sa_ref[...] = sb_ref[...]
        xa_ref[...] = xb_ref[...]

    def pair(j):
        for a in range(2):
            scores(*buf_b, j + 1, a, 0, False)
            consume(*buf_a, j, a, 0, False)
        for a in range(2):
            scores(*buf_a, j + 2, a, 0, False)
            consume(*buf_b, j + 1, a, 0, False)

    pairs = (jdiag - odd) // 2
    one = pairs & 1
    two = (pairs >> 1) & 1

    @pl.when(one == 1)
    def _():
        pair(odd)

    @pl.when(two == 1)
    def _():
        j = odd + 2 * one
        pair(j)
        pair(j + 2)

    def octo_body(t, c):
        j = odd + 2 * one + 4 * two + 8 * t
        for u in range(4):
            pair(j + 2 * u)
        return c

    lax.fori_loop(0, pairs >> 2, octo_body, 0)

    bufs = (buf_a, buf_b)
    for d in range(r):
        qs = d * tk
        for a in range(2):
            if d + 1 < r:
                scores(*bufs[(d + 1) & 1], jdiag + d + 1, a, qs + tk, True)
            consume(*bufs[d & 1], jdiag + d, a, qs, d == 0)

    for a in range(2):
        scores(*buf_a, 0, a, 0, False, blk=jnp.minimum(i + 1, pl.num_programs(2) - 1))

    ot = jnp.concatenate([acc_ref[a, :HEAD_DIM, :] * (1.0 / acc_ref[a, HEAD_DIM:HEAD_DIM + 1, :]) for a in range(2)],
                         axis=0)
    o_ref[0] = ot.T.astype(o_ref.dtype)


def _flash_attention(qh, kh, vt, batch, lp, tq, tk):
    nk = lp // tk
    return pl.pallas_call(
        functools.partial(_flash_kernel, tq=tq, tk=tk),
        grid=(batch, PAIRS, lp // tq),
        in_specs=[pl.BlockSpec((1, lp, 2 * LANES), lambda b, p, i: (b, 0, p)),
                  pl.BlockSpec((1, lp, 2 * LANES), lambda b, p, i: (b, 0, p)),
                  pl.BlockSpec((1, nk, 2 * V_ROWS, tk), lambda b, p, i: (b, 0, p, 0))],
        out_specs=pl.BlockSpec((1, tq, 2 * HEAD_DIM), lambda b, p, i: (b, i, p)),
        out_shape=jax.ShapeDtypeStruct((batch, lp, HEADS * HEAD_DIM), BF16),
        scratch_shapes=[pltpu.VMEM((2, tk, tq), F32), pltpu.VMEM((2, tk, tq), F32),
                        pltpu.VMEM((2, 1, tq), F32), pltpu.VMEM((2, 1, tq), F32),
                        pltpu.VMEM((2, 1, tq), F32), pltpu.VMEM((2, V_ROWS, tq), F32)],
        compiler_params=_params("parallel", "parallel", "arbitrary"),
        name="flash_attention",
    )(qh, kh, vt)


SWA_TQ = 128


def _swa_attn_kernel(sink_ref, q_ref, km_ref, kp_ref, kc_ref, vm_ref, vp_ref, vc_ref, o_ref):
    i = pl.program_id(1)
    t = SWA_TQ
    lo = lax.broadcasted_iota(jnp.int32, (1, LANES), 1) < HEAD_DIM
    row = lax.broadcasted_iota(jnp.int32, (t, 3 * t), 0)
    col = lax.broadcasted_iota(jnp.int32, (t, 3 * t), 1)
    qpos = i * t + row
    kpos = jnp.where(col < t, t + col, (i - 2) * t + col)
    d = qpos - kpos
    valid = (d >= 0) & (((col < t) & (kpos >= LEAD)) |
                        ((col >= t) & (d < WINDOW) & (kpos >= FIRST_REAL)))
    kcat, v_lo, v_hi = [], [], []
    for g in range(SWA_KV_HEADS):
        sl = slice(g * LANES, (g + 1) * LANES)
        kcat.append(jnp.concatenate([km_ref[0, :, sl], kp_ref[0, :, sl], kc_ref[0, :, sl]], axis=0))
        vf = jnp.concatenate([vm_ref[0, :, sl], vp_ref[0, :, sl], vc_ref[0, :, sl]], axis=0).astype(F32)
        v_lo.append(jnp.where(lo, vf, 0.0).astype(BF16))
        v_hi.append(jnp.where(lo, 0.0, vf).astype(BF16))
    pairs_per_group = PAIRS // SWA_KV_HEADS

    def logits(p):
        qf = q_ref[0, :, p * LANES:(p + 1) * LANES].astype(F32)
        q_pair = (jnp.where(lo, qf, 0.0).astype(BF16), jnp.where(lo, 0.0, qf).astype(BF16))
        return [_dot_nt(q_pair[a], kcat[p // pairs_per_group]) for a in range(2)]

    def finish(p, s_pair):
        g = p // pairs_per_group
        ps, inv = [], []
        for a in range(2):
            sink = sink_ref[2 * p + a]
            s = jnp.where(valid, s_pair[a], NEG)
            m = jnp.maximum(jnp.max(s, axis=1, keepdims=True), sink)
            e = jnp.exp(s - m)
            den = jnp.sum(e, axis=1, keepdims=True) + jnp.exp(sink - m)
            ps.append(e.astype(BF16))
            inv.append(1.0 / den)
        o = (_dot(ps[0], v_lo[g]) + _dot(ps[1], v_hi[g])) * jnp.where(lo, inv[0], inv[1])
        o_ref[0, :, p * LANES:(p + 1) * LANES] = o.astype(o_ref.dtype)

    s_next = logits(0)
    for p in range(PAIRS):
        s_cur = s_next
        if p + 1 < PAIRS:
            s_next = logits(p + 1)
        finish(p, s_cur)


def _swa_attention(qkv, sinks, batch, lp):
    t = SWA_TQ
    kblk, vblk = 4, 5
    kv_spec = lambda col, row_of: pl.BlockSpec((1, t, 2 * LANES), lambda b, i: (b, row_of(i), col))
    meta = lambda i: 1
    prev = lambda i: jnp.maximum(i - 1, 0)
    cur = lambda i: i
    return pl.pallas_call(
        _swa_attn_kernel,
        grid=(batch, lp // t),
        in_specs=[pl.BlockSpec(memory_space=pltpu.SMEM),
                  pl.BlockSpec((1, t, HEADS * HEAD_DIM), lambda b, i: (b, i, 0)),
                  kv_spec(kblk, meta), kv_spec(kblk, prev), kv_spec(kblk, cur),
                  kv_spec(vblk, meta), kv_spec(vblk, prev), kv_spec(vblk, cur)],
        out_specs=pl.BlockSpec((1, t, HEADS * HEAD_DIM), lambda b, i: (b, i, 0)),
        out_shape=jax.ShapeDtypeStruct((batch, lp, HEADS * HEAD_DIM), BF16),
        compiler_params=_params("parallel", "parallel"),
        name="swa_attention",
    )(sinks, qkv, qkv, qkv, qkv, qkv, qkv, qkv)


def _swa_proj_kernel(x_ref, w_ref, cos_ref, sin_ref, perm_ref, o_ref, *, tn, n_rope_blocks):
    j = pl.program_id(1)
    y = _dot(x_ref[...], w_ref[...])

    @pl.when(j < n_rope_blocks)
    def _():
        reps = tn // LANES
        cos = jnp.concatenate([cos_ref[...]] * reps, axis=1)
        sin = jnp.concatenate([sin_ref[...]] * reps, axis=1)
        partner = _dot(y.astype(BF16), perm_ref[...])
        o_ref[...] = (y * cos + partner * sin).astype(o_ref.dtype)

    @pl.when(j >= n_rope_blocks)
    def _():
        o_ref[...] = y.astype(o_ref.dtype)


def _swa_proj(hb, w, cos, sin, lp, tm, tn, n_rope_blocks):
    n = hb.shape[0]
    m = w.shape[1]
    tpb = lp // tm
    return pl.pallas_call(
        functools.partial(_swa_proj_kernel, tn=tn, n_rope_blocks=n_rope_blocks),
        grid=(n // tm, m // tn),
        in_specs=[pl.BlockSpec((tm, D_MODEL), lambda i, j: (i, 0)),
                  pl.BlockSpec((D_MODEL, tn), lambda i, j: (0, j)),
                  pl.BlockSpec((tm, LANES), lambda i, j: (i % tpb, 0)),
                  pl.BlockSpec((tm, LANES), lambda i, j: (i % tpb, 0)),
                  pl.BlockSpec((tn, tn), lambda i, j: (0, 0))],
        out_specs=pl.BlockSpec((tm, tn), lambda i, j: (i, j)),
        out_shape=jax.ShapeDtypeStruct((n, m), BF16),
        compiler_params=_params("parallel", "parallel"),
        name="swa_proj",
    )(hb, w, cos, sin, jnp.asarray(_rope_partner_matrix(tn), BF16))


def _rope_partner_matrix(width):
    p = np.zeros((width, width), np.float32)
    half = ROPE_DIM // 2
    for base in range(0, width, HEAD_DIM):
        for l in range(half):
            p[base + l + half, base + l] = 1.0
            p[base + l, base + l + half] = 1.0
    return p


MLA_A_COLS = MLA_Q_LORA + MLA_KV_LORA + 2 * LANES


def _mla_a_kernel(x_ref, w_ref, gq_ref, gkv_ref, cos_ref, sin_ref, cq_ref, ckv_ref, kr_ref, *, tm, tiles_per_batch):
    y = _dot(x_ref[...], w_ref[...])
    cq = y[:, :MLA_Q_LORA]
    ckv = y[:, MLA_Q_LORA:MLA_Q_LORA + MLA_KV_LORA]
    kr = y[:, MLA_Q_LORA + MLA_KV_LORA:MLA_Q_LORA + MLA_KV_LORA + LANES]
    krs = y[:, MLA_Q_LORA + MLA_KV_LORA + LANES:]
    rms = lambda z, g: z * lax.rsqrt(jnp.mean(z * z, axis=-1, keepdims=True) + RMS_EPS) * g
    cq_ref[...] = rms(cq, gq_ref[...]).astype(cq_ref.dtype)
    ckv_ref[...] = rms(ckv, gkv_ref[...]).astype(ckv_ref.dtype)
    dead = _dead_rows(pl.program_id(0) % tiles_per_batch, tm) * _one_hot_lanes(LANES, LANES, MLA_DEAD_LANE)
    kr_ref[...] = (kr * cos_ref[...] + krs * sin_ref[...] + dead).astype(kr_ref.dtype)


def _mla_a(hb, w, gq, gkv, cos, sin, lp, tm):
    n = hb.shape[0]
    tpb = lp // tm
    return pl.pallas_call(
        functools.partial(_mla_a_kernel, tm=tm, tiles_per_batch=tpb),
        grid=(n // tm,),
        in_specs=[pl.BlockSpec((tm, D_MODEL), lambda i: (i, 0)),
                  pl.BlockSpec((D_MODEL, MLA_A_COLS), lambda i: (0, 0)),
                  pl.BlockSpec((1, MLA_Q_LORA), lambda i: (0, 0)),
                  pl.BlockSpec((1, MLA_KV_LORA), lambda i: (0, 0)),
                  pl.BlockSpec((tm, LANES), lambda i: (i % tpb, 0)),
                  pl.BlockSpec((tm, LANES), lambda i: (i % tpb, 0))],
        out_specs=[pl.BlockSpec((tm, MLA_Q_LORA), lambda i: (i, 0)),
                   pl.BlockSpec((tm, MLA_KV_LORA), lambda i: (i, 0)),
                   pl.BlockSpec((tm, LANES), lambda i: (i, 0))],
        out_shape=[jax.ShapeDtypeStruct((n, MLA_Q_LORA), BF16),
                   jax.ShapeDtypeStruct((n, MLA_KV_LORA), BF16),
                   jax.ShapeDtypeStruct((n, LANES), BF16)],
        compiler_params=_params("parallel"),
        name="mla_a",
    )(hb, w, gq, gkv, cos, sin)


def _mla_q_kernel(x_ref, w_ref, ws_ref, cos_ref, sin_ref, o_ref):
    x = x_ref[...]
    reps = o_ref.shape[1] // LANES
    cos = jnp.concatenate([cos_ref[...]] * reps, axis=1)
    sin = jnp.concatenate([sin_ref[...]] * reps, axis=1)
    y = _dot(x, w_ref[...]) * cos + _dot(x, ws_ref[...]) * sin
    o_ref[...] = (y + _one_hot_lanes(o_ref.shape[1], LANES, MLA_DEAD_LANE)).astype(o_ref.dtype)


def _mla_q(cq, w, ws, cos, sin, lp, tm, tn):
    n = cq.shape[0]
    m = w.shape[1]
    tpb = lp // tm
    return pl.pallas_call(
        _mla_q_kernel,
        grid=(n // tm, m // tn),
        in_specs=[pl.BlockSpec((tm, MLA_Q_LORA), lambda i, j: (i, 0)),
                  pl.BlockSpec((MLA_Q_LORA, tn), lambda i, j: (0, j)),
                  pl.BlockSpec((MLA_Q_LORA, tn), lambda i, j: (0, j)),
                  pl.BlockSpec((tm, LANES), lambda i, j: (i % tpb, 0)),
                  pl.BlockSpec((tm, LANES), lambda i, j: (i % tpb, 0))],
        out_specs=pl.BlockSpec((tm, tn), lambda i, j: (i, j)),
        out_shape=jax.ShapeDtypeStruct((n, m), BF16),
        compiler_params=_params("parallel", "parallel"),
        name="mla_q",
    )(cq, w, ws, cos, sin)


def _oproj_ln_kernel(o_ref, w_ref, h_ref, g_ref, b_ref, hf_ref, hb_ref):
    x = DEEPNORM_ALPHA * h_ref[...] + _dot(o_ref[...], w_ref[...])
    y = _layer_norm(x, g_ref[...], b_ref[...])
    hf_ref[...] = y
    hb_ref[...] = y.astype(hb_ref.dtype)


def _oproj_ln(o, w, h, g, b, tm):
    n = o.shape[0]
    row = lambda i: (i, 0)
    fixed = lambda i: (0, 0)
    return pl.pallas_call(
        _oproj_ln_kernel,
        grid=(n // tm,),
        in_specs=[pl.BlockSpec((tm, D_MODEL), row), pl.BlockSpec((D_MODEL, D_MODEL), fixed),
                  pl.BlockSpec((tm, D_MODEL), row), pl.BlockSpec((1, D_MODEL), fixed),
                  pl.BlockSpec((1, D_MODEL), fixed)],
        out_specs=[pl.BlockSpec((tm, D_MODEL), row), pl.BlockSpec((tm, D_MODEL), row)],
        out_shape=[jax.ShapeDtypeStruct((n, D_MODEL), F32), jax.ShapeDtypeStruct((n, D_MODEL), BF16)],
        compiler_params=_params("parallel"),
        name="oproj_ln",
    )(o, w, h, g, b)


FFN_HALO = BF16_SUBLANES


FFN_CHUNK = 256
FFN_NC = D_FF // FFN_CHUNK
assert FFN_NC * FFN_CHUNK == D_FF and FFN_NC % 2 == 1


def _ffn_kernel(x_ref, halo_ref, h_ref, win_ref, cw_ref, cb_ref, wo_ref, g_ref, b_ref, hf_ref, hb_ref,
                xext_ref, ua_ref, ub_ref, acc_ref, *, tm, tiles_per_batch):
    i = pl.program_id(0)
    nc = FFN_NC
    pos = (i % tiles_per_batch) * tm - FFN_HALO + lax.broadcasted_iota(jnp.int32, (tm + FFN_HALO, 1), 0)
    xe = jnp.concatenate([halo_ref[...], x_ref[...]], axis=0).astype(F32)
    xext_ref[...] = jnp.where(pos >= LEAD, xe, 0.0).astype(BF16)

    cols = lambda idx: slice(idx * FFN_CHUNK, (idx + 1) * FFN_CHUNK)

    def up(u_ref, c):
        u_ref[0] = _dot(xext_ref[...], win_ref[:, cols(c)])
        u_ref[1] = _dot(xext_ref[...], win_ref[:, cols(nc + c)])

    def glu(u_ref, c):
        def conv(part, idx):
            u = u_ref[part]
            delayed = u * cw_ref[0:1, cols(idx)]
            for tap in range(1, CONV_W):
                delayed = u * cw_ref[tap:tap + 1, cols(idx)] + pltpu.roll(delayed, 1, 0)
            return cb_ref[:, cols(idx)] + delayed[FFN_HALO:, :]

        yg = conv(0, c)
        yv = conv(1, nc + c)
        return ((yg / (1.0 + jnp.exp(-yg))) * yv).astype(BF16)

    acc_ref[...] = DEEPNORM_ALPHA * h_ref[...]
    up(ua_ref, 0)

    for c in range(0, nc - 1, 2):
        up(ub_ref, c + 1)
        act_a = glu(ua_ref, c)
        up(ua_ref, c + 2)
        act_b = glu(ub_ref, c + 1)
        acc_ref[...] += _dot(act_a, wo_ref[cols(c), :]) + _dot(act_b, wo_ref[cols(c + 1), :])
    y = _layer_norm(acc_ref[...] + _dot(glu(ua_ref, nc - 1), wo_ref[cols(nc - 1), :]), g_ref[...], b_ref[...])
    hf_ref[...] = y
    hb_ref[...] = y.astype(hb_ref.dtype)


def _ffn(hb, hf, w_in, conv_w, conv_b, w_out, g, b, lp, tm):
    n = hb.shape[0]
    nc, fc = FFN_NC, FFN_CHUNK
    tpb = lp // tm
    halo_blocks = tm // FFN_HALO
    row = lambda i: (i, 0)
    fixed2 = lambda i: (0, 0)
    resident = dict(pipeline_mode=pl.Buffered(1))
    return pl.pallas_call(
        functools.partial(_ffn_kernel, tm=tm, tiles_per_batch=tpb),
        grid=(n // tm,),
        in_specs=[pl.BlockSpec((tm, D_MODEL), row),
                  pl.BlockSpec((FFN_HALO, D_MODEL), lambda i: (jnp.maximum(i * halo_blocks - 1, 0), 0)),
                  pl.BlockSpec((tm, D_MODEL), row),
                  pl.BlockSpec((D_MODEL, 2 * D_FF), fixed2, **resident),
                  pl.BlockSpec((CONV_W, 2 * D_FF), fixed2, **resident),
                  pl.BlockSpec((1, 2 * D_FF), fixed2, **resident),
                  pl.BlockSpec((D_FF, D_MODEL), fixed2, **resident),
                  pl.BlockSpec((1, D_MODEL), fixed2), pl.BlockSpec((1, D_MODEL), fixed2)],
        out_specs=[pl.BlockSpec((tm, D_MODEL), row), pl.BlockSpec((tm, D_MODEL), row)],
        out_shape=[jax.ShapeDtypeStruct((n, D_MODEL), F32), jax.ShapeDtypeStruct((n, D_MODEL), BF16)],
        scratch_shapes=[pltpu.VMEM((tm + FFN_HALO, D_MODEL), BF16),
                        pltpu.VMEM((2, tm + FFN_HALO, fc), F32), pltpu.VMEM((2, tm + FFN_HALO, fc), F32),
                        pltpu.VMEM((tm, D_MODEL), F32)],
        compiler_params=_params("parallel"),
        name="ffn",
    )(hb, hb, hf, w_in, conv_w, conv_b, w_out, g, b)


def _ffn_weights(w_in, conv_w, conv_b, w_out):
    return w_in.astype(BF16), conv_w, conv_b[None, :], w_out.astype(BF16)


def _rope_tables(lp, dim, theta, group, offset):
    pos = (jnp.arange(lp) - LEAD).astype(F32)
    inv = theta ** (-jnp.arange(0, dim, 2, dtype=F32) / dim)
    ang = pos[:, None] * inv[None, :]
    cos, sin = jnp.cos(ang), jnp.sin(ang)
    ones = lambda w: jnp.ones((lp, w), F32)
    zeros = lambda w: jnp.zeros((lp, w), F32)
    rest = group - offset - dim
    cos_g = jnp.concatenate([ones(offset), cos, cos, ones(rest)], axis=1)
    sin_g = jnp.concatenate([zeros(offset), -sin, sin, zeros(rest)], axis=1)
    reps = LANES // group
    return jnp.tile(cos_g, (1, reps)), jnp.tile(sin_g, (1, reps))


def _swap_halves(w, dim):
    return jnp.concatenate([w[..., dim // 2:dim], w[..., :dim // 2]], axis=-1)


def _head_blocks(main, extra):
    src = main if main is not None else extra
    rows, heads = src.shape[0], src.shape[1]
    m = main if main is not None else jnp.zeros((rows, heads, HEAD_DIM), F32)
    e = extra if extra is not None else jnp.zeros((rows, heads, 0), F32)
    pad = jnp.zeros((rows, heads, LANES - HEAD_DIM - e.shape[2]), F32)
    return jnp.concatenate([m, e, pad], axis=2).reshape(rows, heads * LANES)


def _mla_weights(w_a, w_uq, w_ukv):
    d = w_a.shape[0]
    w_kr = w_a[:, MLA_Q_LORA + MLA_KV_LORA:][:, None, :]
    w_a_cat = jnp.concatenate([w_a[:, :MLA_Q_LORA + MLA_KV_LORA], _head_blocks(None, w_kr),
                               _head_blocks(None, _swap_halves(w_kr, MLA_ROPE))], axis=1)
    scale = (MLA_NOPE + MLA_ROPE) ** -0.5 * LOG2E
    wq = (w_uq * scale).reshape(MLA_Q_LORA, HEADS, MLA_NOPE + MLA_ROPE)
    w_main = _head_blocks(wq[..., :MLA_NOPE], wq[..., MLA_NOPE:])
    w_swap = _head_blocks(None, _swap_halves(wq[..., MLA_NOPE:], MLA_ROPE))
    wkv = w_ukv.reshape(MLA_KV_LORA, HEADS, MLA_NOPE + HEAD_DIM)
    w_kn = wkv[..., :MLA_NOPE].reshape(MLA_KV_LORA, HEADS * MLA_NOPE)
    w_v = wkv[..., MLA_NOPE:].reshape(MLA_KV_LORA, HEADS * HEAD_DIM)
    return tuple(w.astype(BF16) for w in (w_a_cat, w_main, w_swap, w_kn)) + (w_v,)


def _swa_weights(w_in):
    qd = HEADS * HEAD_DIM
    kd = SWA_KV_HEADS * HEAD_DIM
    q = w_in[:, :qd] * (HEAD_DIM ** -0.5)
    dup = lambda w: jnp.concatenate([w[:, :HEAD_DIM], w[:, :HEAD_DIM], w[:, HEAD_DIM:], w[:, HEAD_DIM:]], axis=1)
    return jnp.concatenate([q, dup(w_in[:, qd:qd + kd]), dup(w_in[:, qd + kd:])], axis=1).astype(BF16)


def _fox_weights(w_in, b_f):
    hd = HEADS * HEAD_DIM
    d = w_in.shape[0]
    w_q = (w_in[:, :hd] * (HEAD_DIM ** -0.5 * LOG2E)).astype(BF16)
    w_k = w_in[:, hd:2 * hd].astype(BF16)
    w_v = w_in[:, 2 * hd:3 * hd]
    w_gate = w_in[:, 3 * hd:]
    w_fg = jnp.concatenate([w_gate] * GATE_SLOTS + [jnp.zeros((d, LANES - GATE_SLOTS * HEADS), F32)],
                           axis=1).astype(BF16)
    b_fg = jnp.concatenate([b_f] * GATE_SLOTS + [jnp.zeros((LANES - GATE_SLOTS * HEADS,), F32)])[None, :]
    return w_q, w_k, w_v, w_fg, b_fg


def kernel(x, meta_tokens, ln1_g, ln1_b, ln2_g, ln2_b, fox_w_in, fox_b_f, fox_w_o, swa_w_in, swa_sinks, swa_w_o,
           mla_w_a, mla_g_q, mla_g_kv, mla_w_uq, mla_w_ukv, mla_w_o, ffn_w_in, ffn_conv_w, ffn_conv_b, ffn_w_out):
    batch, seq, d = x.shape
    assert d == D_MODEL and seq % 256 == 0
    lp = seq + FIRST_REAL
    n = batch * lp

    tm = _tile(lp, 768, 256)
    tq = _tile(lp, 768, 256)
    tk = 256
    tn = 512
    nk = lp // tk

    hf, hb = _embed(x, meta_tokens.astype(x.dtype))

    cos_p, sin_p = _rope_tables(lp, ROPE_DIM, ROPE_THETA, HEAD_DIM, 0)
    cos_m, sin_m = _rope_tables(lp, MLA_ROPE, MLA_ROPE_THETA, LANES, MLA_NOPE)
    b3 = lambda a: a.reshape(batch, lp, -1)

    for i in range(DEPTH):
        kind, j = i % 3, i // 3
        if kind == 0:
            w_q, w_k, w_v, w_fg, b_fg = _fox_weights(fox_w_in[j], fox_b_f[j])
            aq, ak = _fox_gate(hb, w_fg, b_fg, lp, tm)
            qh = _head_proj(hb, w_q, aq, tm, 2 * tn)
            kh = _head_proj(hb, w_k, ak, tm, 2 * tn)
            vt = _matmul_t(hb, *_value_weights(w_v), tm, V_COLS_TILE, tk).reshape(batch, nk, HEADS * V_ROWS, tk)
            o = _flash_attention(b3(qh), b3(kh), vt, batch, lp, tq, tk)
            w_o = fox_w_o[j]
        elif kind == 1:
            qkv = _swa_proj(hb, _swa_weights(swa_w_in[j]), cos_p, sin_p, lp, tm, 2 * LANES, 5)
            o = _swa_attention(b3(qkv), swa_sinks[j].astype(F32), batch, lp)
            w_o = swa_w_o[j]
        else:
            w_a_cat, w_main, w_swap, w_kn, w_v = _mla_weights(mla_w_a[j], mla_w_uq[j], mla_w_ukv[j])
            cq, ckv, kr = _mla_a(hb, w_a_cat, mla_g_q[j][None, :], mla_g_kv[j][None, :], cos_m, sin_m, lp, tm)
            qh = _mla_q(cq, w_main, w_swap, cos_m, sin_m, lp, tm, 2 * tn)
            kh = _head_proj(ckv, w_kn, kr, tm, 2 * tn)
            vt = _matmul_t(ckv, *_value_weights(w_v), tm, V_COLS_TILE, tk).reshape(batch, nk, HEADS * V_ROWS, tk)
            o = _flash_attention(b3(qh), b3(kh), vt, batch, lp, tq, tk)
            w_o = mla_w_o[j]
        hf, hb = _oproj_ln(o.reshape(n, d), w_o.astype(BF16), hf, ln1_g[i][None, :], ln1_b[i][None, :], tm)
        hf, hb = _ffn(hb, hf, *_ffn_weights(ffn_w_in[i], ffn_conv_w[i], ffn_conv_b[i], ffn_w_out[i]),
                      ln2_g[i][None, :], ln2_b[i][None, :], lp, tm)
    return hf.reshape(batch, lp, d)[:, FIRST_REAL:]
```

```python
import functools
import math

import numpy as np
import jax
import jax.numpy as jnp
from jax import lax
from jax.experimental import pallas as pl
from jax.experimental.pallas import tpu as pltpu

F32 = jnp.float32
BF16 = jnp.bfloat16

D_MODEL = 1024
DEPTH = 4
N_META = 16
LEAD = 240
FIRST_REAL = LEAD + N_META
NEG = -1e30
DEEPNORM_ALPHA = (2.0 * DEPTH) ** 0.25
LN_EPS = 1e-5
RMS_EPS = 1e-6
HEADS = 16
HEAD_DIM = 64
PAIRS = HEADS // 2
SWA_KV_HEADS = 2
WINDOW = 128
ROPE_THETA = 500000.0
ROPE_DIM = 16
MLA_Q_LORA = 384
MLA_KV_LORA = 256
MLA_NOPE = 64
MLA_ROPE = 32
MLA_ROPE_THETA = 10000.0
D_FF = 2816
CONV_W = 3
LOG2E = math.log2(math.e)

LANES = 128
BF16_SUBLANES = 16
VMEM_LIMIT = 56 * 1024 * 1024

GATE_SLOTS = 3
FOX_DEAD_LANE = HEAD_DIM + 2 * GATE_SLOTS
MLA_DEAD_LANE = MLA_NOPE + MLA_ROPE
M_INIT = -3e38


def _params(*sem):
    return pltpu.CompilerParams(dimension_semantics=sem, vmem_limit_bytes=VMEM_LIMIT)


def _tile(n, pref, mult):
    best = mult
    t = mult
    while t <= min(n, pref):
        if n % t == 0:
            best = t
        t += mult
    assert n % best == 0
    return best


def _dot(a, b):
    return jnp.dot(a, b, preferred_element_type=F32)


def _dot_nt(a, b):
    return lax.dot_general(a, b, (((1,), (1,)), ((), ())), preferred_element_type=F32)


def _layer_norm(x, g, b):
    mu = jnp.mean(x, axis=-1, keepdims=True)
    xc = x - mu
    var = jnp.mean(xc * xc, axis=-1, keepdims=True)
    return xc * lax.rsqrt(var + LN_EPS) * g + b


def _one_hot_lanes(width, period, lane):
    idx = lax.broadcasted_iota(jnp.int32, (1, width), 1)
    return jnp.where((idx & (period - 1)) == lane, 1.0, 0.0)


def _dead_rows(tile_in_batch, tm):
    pos = tile_in_batch * tm + lax.broadcasted_iota(jnp.int32, (tm, 1), 0)
    return jnp.where(pos < LEAD, NEG, 0.0)


def _embed_kernel(x_ref, meta_ref, hf_ref, hb_ref):
    t = pl.program_id(1)

    @pl.when(t == 0)
    def _():
        lead = jnp.concatenate([jnp.zeros((LEAD, D_MODEL), F32), meta_ref[...]], axis=0)
        hf_ref[...] = lead
        hb_ref[...] = lead.astype(hb_ref.dtype)

    @pl.when(t > 0)
    def _():
        hf_ref[...] = x_ref[0]
        hb_ref[...] = x_ref[0].astype(hb_ref.dtype)


def _embed(x, meta):
    batch, seq, d = x.shape
    blocks = seq // FIRST_REAL + 1
    return pl.pallas_call(
        _embed_kernel,
        grid=(batch, blocks),
        in_specs=[pl.BlockSpec((1, FIRST_REAL, d), lambda b, t: (b, jnp.maximum(t - 1, 0), 0)),
                  pl.BlockSpec((N_META, d), lambda b, t: (0, 0))],
        out_specs=[pl.BlockSpec((FIRST_REAL, d), lambda b, t: (b * blocks + t, 0)),
                   pl.BlockSpec((FIRST_REAL, d), lambda b, t: (b * blocks + t, 0))],
        out_shape=[jax.ShapeDtypeStruct((batch * blocks * FIRST_REAL, d), F32),
                   jax.ShapeDtypeStruct((batch * blocks * FIRST_REAL, d), BF16)],
        compiler_params=_params("parallel", "parallel"),
        name="embed",
    )(x, meta)


def _mm_kernel(x_ref, w_ref, o_ref):
    o_ref[...] = _dot(x_ref[...], w_ref[...]).astype(o_ref.dtype)


def _matmul(x, w, out_dtype, tm, tn):
    n, k = x.shape
    m = w.shape[1]
    return pl.pallas_call(
        _mm_kernel,
        grid=(n // tm, m // tn),
        in_specs=[pl.BlockSpec((tm, k), lambda i, j: (i, 0)),
                  pl.BlockSpec((k, tn), lambda i, j: (0, j))],
        out_specs=pl.BlockSpec((tm, tn), lambda i, j: (i, j)),
        out_shape=jax.ShapeDtypeStruct((n, m), out_dtype),
        compiler_params=_params("parallel", "parallel"),
        name="matmul",
    )(x, w)


def _head_proj_kernel(x_ref, w_ref, e_ref, o_ref):
    y = _dot(x_ref[...], w_ref[...])
    lo = lax.broadcasted_iota(jnp.int32, (1, LANES), 1) < HEAD_DIM
    shared = e_ref.shape[1] == LANES
    for pair in range(y.shape[1] // LANES):
        y_pair = y[:, pair * LANES:(pair + 1) * LANES]
        for a, feats in enumerate((y_pair, pltpu.roll(y_pair, HEAD_DIM, 1))):
            h = 2 * pair + a
            extra = e_ref[...] if shared else e_ref[:, h * LANES:(h + 1) * LANES]
            o_ref[:, h * LANES:(h + 1) * LANES] = jnp.where(lo, feats, extra.astype(F32)).astype(o_ref.dtype)


def _head_proj(x, w, e, tm, tn):
    n, k = x.shape
    m = w.shape[1]
    if e.shape[1] == LANES:
        e_spec = pl.BlockSpec((tm, LANES), lambda i, j: (i, 0))
    else:
        assert e.shape[1] == 2 * m
        e_spec = pl.BlockSpec((tm, 2 * tn), lambda i, j: (i, j))
    return pl.pallas_call(
        _head_proj_kernel,
        grid=(n // tm, m // tn),
        in_specs=[pl.BlockSpec((tm, k), lambda i, j: (i, 0)),
                  pl.BlockSpec((k, tn), lambda i, j: (0, j)),
                  e_spec],
        out_specs=pl.BlockSpec((tm, 2 * tn), lambda i, j: (i, j)),
        out_shape=jax.ShapeDtypeStruct((n, 2 * m), BF16),
        compiler_params=_params("parallel", "parallel"),
        name="head_proj",
    )(x, w, e)


def _mm_t_kernel(x_ref, w_ref, b_ref, o_ref, *, tk):
    y = _dot(x_ref[...], w_ref[...]) + b_ref[...]
    for c in range(o_ref.shape[0]):
        o_ref[c] = y[c * tk:(c + 1) * tk, :].T.astype(o_ref.dtype)


def _matmul_t(x, w, bias, tm, tn, tk):
    n, k = x.shape
    m = w.shape[1]
    r = tm // tk
    return pl.pallas_call(
        functools.partial(_mm_t_kernel, tk=tk),
        grid=(n // tm, m // tn),
        in_specs=[pl.BlockSpec((tm, k), lambda i, j: (i, 0)),
                  pl.BlockSpec((k, tn), lambda i, j: (0, j)),
                  pl.BlockSpec((1, tn), lambda i, j: (0, j))],
        out_specs=pl.BlockSpec((r, tn, tk), lambda i, j: (i, j, 0)),
        out_shape=jax.ShapeDtypeStruct((n // tk, m, tk), BF16),
        compiler_params=_params("parallel", "parallel"),
        name="matmul_t",
    )(x, w, bias)


V_ROWS = HEAD_DIM + BF16_SUBLANES
V_COLS_TILE = 8 * V_ROWS


def _value_weights(w_v):
    rows = w_v.shape[0]
    w = jnp.concatenate([w_v.reshape(rows, HEADS, HEAD_DIM),
                         jnp.zeros((rows, HEADS, V_ROWS - HEAD_DIM), w_v.dtype)], axis=2)
    bias = np.zeros((HEADS, V_ROWS), np.float32)
    bias[:, HEAD_DIM] = 1.0
    return w.reshape(rows, HEADS * V_ROWS).astype(BF16), jnp.asarray(bias.reshape(1, HEADS * V_ROWS))


def _gate_placement():
    wide = HEADS * LANES
    sq = np.zeros((LANES, wide), np.float32)
    sk = np.zeros_like(sq)
    oq = np.zeros((1, wide), np.float32)
    ok = np.zeros_like(oq)
    for h in range(HEADS):
        base = h * LANES + HEAD_DIM
        for part in range(GATE_SLOTS):
            sq[part * HEADS + h, base + part] = 1.0
            sk[part * HEADS + h, base + GATE_SLOTS + part] = -1.0
            oq[0, base + GATE_SLOTS + part] = 1.0
            ok[0, base + part] = 1.0
        oq[0, h * LANES + FOX_DEAD_LANE] = 1.0
    return sq, sk, oq, ok


def _split3(x):
    hi = x.astype(BF16)
    r1 = x - hi.astype(F32)
    mid = r1.astype(BF16)
    lo = (r1 - mid.astype(F32)).astype(BF16)
    return hi, mid, lo


def _fox_gate_kernel(x_ref, w_ref, b_ref, sq_ref, sk_ref, oq_ref, ok_ref, aq_ref, ak_ref, carry_ref,
                     *, tm, tiles_per_batch):
    i = pl.program_id(0)

    @pl.when(i % tiles_per_batch == 0)
    def _():
        carry_ref[...] = jnp.zeros_like(carry_ref)

    fg = _dot(x_ref[...], w_ref[...]) + b_ref[...]
    logf = jnp.minimum(fg, 0.0) - jnp.log(1.0 + jnp.exp(-jnp.abs(fg)))
    sub = FIRST_REAL
    row = lax.broadcasted_iota(jnp.int32, (sub, sub), 0)
    col = lax.broadcasted_iota(jnp.int32, (sub, sub), 1)
    tri = jnp.where(col <= row, 1.0, 0.0).astype(BF16)
    hi, mid, lo = _split3(logf)
    carry = carry_ref[...]
    pieces = []
    for r0 in range(0, tm, sub):
        rows = slice(r0, r0 + sub)
        piece = _dot(tri, hi[rows]) + _dot(tri, mid[rows]) + _dot(tri, lo[rows]) + carry
        carry = piece[sub - 1:sub, :]
        pieces.append(piece)
    cs = jnp.concatenate(pieces, axis=0)
    carry_ref[...] = carry
    lane = lax.broadcasted_iota(jnp.int32, (1, LANES), 1)
    hi, mid, lo = _split3(cs * LOG2E)
    parts = jnp.where(lane < HEADS, hi.astype(F32),
                      jnp.where(lane < 2 * HEADS, mid.astype(F32), lo.astype(F32))).astype(BF16)
    wide = aq_ref.shape[1]
    dead = _dead_rows(i % tiles_per_batch, tm) * _one_hot_lanes(wide, LANES, FOX_DEAD_LANE)
    aq_ref[...] = (_dot(parts, sq_ref[...]) + oq_ref[...]).astype(aq_ref.dtype)
    ak_ref[...] = (_dot(parts, sk_ref[...]) + ok_ref[...] + dead).astype(ak_ref.dtype)


def _fox_gate(hb, w_fg, b_fg, lp, tm):
    n = hb.shape[0]
    tpb = lp // tm
    sq, sk, oq, ok = _gate_placement()
    fixed = lambda i: (0, 0)
    wide = HEADS * LANES
    return pl.pallas_call(
        functools.partial(_fox_gate_kernel, tm=tm, tiles_per_batch=tpb),
        grid=(n // tm,),
        in_specs=[pl.BlockSpec((tm, D_MODEL), lambda i: (i, 0)),
                  pl.BlockSpec((D_MODEL, LANES), fixed),
                  pl.BlockSpec((1, LANES), fixed),
                  pl.BlockSpec((LANES, wide), fixed),
                  pl.BlockSpec((LANES, wide), fixed),
                  pl.BlockSpec((1, wide), fixed),
                  pl.BlockSpec((1, wide), fixed)],
        out_specs=[pl.BlockSpec((tm, wide), lambda i: (i, 0)),
                   pl.BlockSpec((tm, wide), lambda i: (i, 0))],
        out_shape=[jax.ShapeDtypeStruct((n, wide), BF16), jax.ShapeDtypeStruct((n, wide), BF16)],
        scratch_shapes=[pltpu.VMEM((1, LANES), F32)],
        compiler_params=_params("arbitrary"),
        name="fox_gate",
    )(hb, w_fg, b_fg, jnp.asarray(sq, BF16), jnp.asarray(sk, BF16), jnp.asarray(oq), jnp.asarray(ok))


def _flash_kernel(q_ref, k_ref, vt_ref, o_ref, sa_ref, sb_ref, xa_ref, xb_ref, m_ref, acc_ref, *, tq, tk):
    i = pl.program_id(2)
    r = tq // tk
    m_ref[...] = jnp.full_like(m_ref, M_INIT)
    acc_ref[...] = jnp.zeros_like(acc_ref)

    def diag_mask(s):
        keep = lax.broadcasted_iota(jnp.int32, s.shape, 0) <= lax.broadcasted_iota(jnp.int32, s.shape, 1)
        return jnp.where(keep, s, NEG)

    def scores(s_ref, x_ref, j, a, qs, diagonal, blk=None):
        off = pl.multiple_of(j * tk, tk)
        q0 = pl.multiple_of((i if blk is None else blk) * tq + qs, tk)
        s = _dot_nt(k_ref[0, pl.ds(off, tk), a * LANES:(a + 1) * LANES],
                    q_ref[0, pl.ds(q0, tq - qs), a * LANES:(a + 1) * LANES])
        if diagonal:
            s = diag_mask(s)
        s_ref[a, :, qs:] = s
        x_ref[a, :, qs:] = jnp.max(s, axis=0, keepdims=True)

    def consume(s_ref, x_ref, j, a, qs, mask_now):
        s = s_ref[a, :, qs:]
        if mask_now:
            s = diag_mask(s)
            smax = jnp.max(s, axis=0, keepdims=True)
        else:
            smax = x_ref[a, :, qs:]
        m_old = m_ref[a, :, qs:]
        m_new = jnp.maximum(m_old, smax)
        alpha = jnp.exp2(m_old - m_new)
        p = jnp.exp2(s - m_new)
        m_ref[a, :, qs:] = m_new
        pv = _dot(vt_ref[0, j, a * V_ROWS:(a + 1) * V_ROWS, :], p.astype(BF16))
        acc_ref[a, :, qs:] = alpha * acc_ref[a, :, qs:] + pv

    buf_a, buf_b = (sa_ref, xa_ref), (sb_ref, xb_ref)
    jdiag = i * r
    odd = jdiag & 1

    @pl.when(i == 0)
    def _():
        for a in range(2):
            scores(*buf_a, 0, a, 0, False)

    @pl.when(odd == 1)
    def _():
        for a in range(2):
            scores(*buf_b, 1, a, 0, False)
            consume(*buf_a, 0, a, 0, False)
        sa_ref[...] = sb_ref[...]
        xa_ref[...] = xb_ref[...]

    def pair(j):
        for a in range(2):
            scores(*buf_b, j + 1, a, 0, False)
            consume(*buf_a, j, a, 0, False)
        for a in range(2):
            scores(*buf_a, j + 2, a, 0, False)
            consume(*buf_b, j + 1, a, 0, False)

    pairs = (jdiag - odd) // 2
    one = pairs & 1
    two = (pairs >> 1) & 1

    @pl.when(one == 1)
    def _():
        pair(odd)

    @pl.when(two == 1)
    def _():
        j = odd + 2 * one
        pair(j)
        pair(j + 2)

    def octo_body(t, c):
        j = odd + 2 * one + 4 * two + 8 * t
        for u in range(4):
            pair(j + 2 * u)
        return c

    lax.fori_loop(0, pairs >> 2, octo_body, 0)

    bufs = (buf_a, buf_b)
    for d in range(r):
        qs = d * tk
        for a in range(2):
            if d + 1 < r:
                scores(*bufs[(d + 1) & 1], jdiag + d + 1, a, qs + tk, True)
            consume(*bufs[d & 1], jdiag + d, a, qs, d == 0)

    for a in range(2):
        scores(*buf_a, 0, a, 0, False, blk=jnp.minimum(i + 1, pl.num_programs(2) - 1))

    ot = jnp.concatenate([acc_ref[a, :HEAD_DIM, :] * (1.0 / acc_ref[a, HEAD_DIM:HEAD_DIM + 1, :]) for a in range(2)],
                         axis=0)
    o_ref[0] = ot.T.astype(o_ref.dtype)


def _flash_attention(qh, kh, vt, batch, lp, tq, tk):
    nk = lp // tk
    return pl.pallas_call(
        functools.partial(_flash_kernel, tq=tq, tk=tk),
        grid=(batch, PAIRS, lp // tq),
        in_specs=[pl.BlockSpec((1, lp, 2 * LANES), lambda b, p, i: (b, 0, p)),
                  pl.BlockSpec((1, lp, 2 * LANES), lambda b, p, i: (b, 0, p)),
                  pl.BlockSpec((1, nk, 2 * V_ROWS, tk), lambda b, p, i: (b, 0, p, 0))],
        out_specs=pl.BlockSpec((1, tq, 2 * HEAD_DIM), lambda b, p, i: (b, i, p)),
        out_shape=jax.ShapeDtypeStruct((batch, lp, HEADS * HEAD_DIM), BF16),
        scratch_shapes=[pltpu.VMEM((2, tk, tq), F32), pltpu.VMEM((2, tk, tq), F32),
                        pltpu.VMEM((2, 1, tq), F32), pltpu.VMEM((2, 1, tq), F32),
                        pltpu.VMEM((2, 1, tq), F32), pltpu.VMEM((2, V_ROWS, tq), F32)],
        compiler_params=_params("parallel", "parallel", "arbitrary"),
        name="flash_attention",
    )(qh, kh, vt)


SWA_TQ = 128


def _swa_attn_kernel(sink_ref, q_ref, km_ref, kp_ref, kc_ref, vm_ref, vp_ref, vc_ref, o_ref):
    i = pl.program_id(1)
    t = SWA_TQ
    lo = lax.broadcasted_iota(jnp.int32, (1, LANES), 1) < HEAD_DIM
    row = lax.broadcasted_iota(jnp.int32, (t, 3 * t), 0)
    col = lax.broadcasted_iota(jnp.int32, (t, 3 * t), 1)
    qpos = i * t + row
    kpos = jnp.where(col < t, t + col, (i - 2) * t + col)
    d = qpos - kpos
    valid = (d >= 0) & (((col < t) & (kpos >= LEAD)) |
                        ((col >= t) & (d < WINDOW) & (kpos >= FIRST_REAL)))
    kcat, v_lo, v_hi = [], [], []
    for g in range(SWA_KV_HEADS):
        sl = slice(g * LANES, (g + 1) * LANES)
        kcat.append(jnp.concatenate([km_ref[0, :, sl], kp_ref[0, :, sl], kc_ref[0, :, sl]], axis=0))
        vf = jnp.concatenate([vm_ref[0, :, sl], vp_ref[0, :, sl], vc_ref[0, :, sl]], axis=0).astype(F32)
        v_lo.append(jnp.where(lo, vf, 0.0).astype(BF16))
        v_hi.append(jnp.where(lo, 0.0, vf).astype(BF16))
    pairs_per_group = PAIRS // SWA_KV_HEADS

    def logits(p):
        qf = q_ref[0, :, p * LANES:(p + 1) * LANES].astype(F32)
        q_pair = (jnp.where(lo, qf, 0.0).astype(BF16), jnp.where(lo, 0.0, qf).astype(BF16))
        return [_dot_nt(q_pair[a], kcat[p // pairs_per_group]) for a in range(2)]

    def finish(p, s_pair):
        g = p // pairs_per_group
        ps, inv = [], []
        for a in range(2):
            sink = sink_ref[2 * p + a]
            s = jnp.where(valid, s_pair[a], NEG)
            m = jnp.maximum(jnp.max(s, axis=1, keepdims=True), sink)
            e = jnp.exp(s - m)
            den = jnp.sum(e, axis=1, keepdims=True) + jnp.exp(sink - m)
            ps.append(e.astype(BF16))
            inv.append(1.0 / den)
        o = (_dot(ps[0], v_lo[g]) + _dot(ps[1], v_hi[g])) * jnp.where(lo, inv[0], inv[1])
        o_ref[0, :, p * LANES:(p + 1) * LANES] = o.astype(o_ref.dtype)

    s_next = logits(0)
    for p in range(PAIRS):
        s_cur = s_next
        if p + 1 < PAIRS:
            s_next = logits(p + 1)
        finish(p, s_cur)


def _swa_attention(qkv, sinks, batch, lp):
    t = SWA_TQ
    kblk, vblk = 4, 5
    kv_spec = lambda col, row_of: pl.BlockSpec((1, t, 2 * LANES), lambda b, i: (b, row_of(i), col))
    meta = lambda i: 1
    prev = lambda i: jnp.maximum(i - 1, 0)
    cur = lambda i: i
    return pl.pallas_call(
        _swa_attn_kernel,
        grid=(batch, lp // t),
        in_specs=[pl.BlockSpec(memory_space=pltpu.SMEM),
                  pl.BlockSpec((1, t, HEADS * HEAD_DIM), lambda b, i: (b, i, 0)),
                  kv_spec(kblk, meta), kv_spec(kblk, prev), kv_spec(kblk, cur),
                  kv_spec(vblk, meta), kv_spec(vblk, prev), kv_spec(vblk, cur)],
        out_specs=pl.BlockSpec((1, t, HEADS * HEAD_DIM), lambda b, i: (b, i, 0)),
        out_shape=jax.ShapeDtypeStruct((batch, lp, HEADS * HEAD_DIM), BF16),
        compiler_params=_params("parallel", "parallel"),
        name="swa_attention",
    )(sinks, qkv, qkv, qkv, qkv, qkv, qkv, qkv)


def _swa_proj_kernel(x_ref, w_ref, cos_ref, sin_ref, perm_ref, o_ref, *, tn, n_rope_blocks):
    j = pl.program_id(1)
    y = _dot(x_ref[...], w_ref[...])

    @pl.when(j < n_rope_blocks)
    def _():
        reps = tn // LANES
        cos = jnp.concatenate([cos_ref[...]] * reps, axis=1)
        sin = jnp.concatenate([sin_ref[...]] * reps, axis=1)
        partner = _dot(y.astype(BF16), perm_ref[...])
        o_ref[...] = (y * cos + partner * sin).astype(o_ref.dtype)

    @pl.when(j >= n_rope_blocks)
    def _():
        o_ref[...] = y.astype(o_ref.dtype)


def _swa_proj(hb, w, cos, sin, lp, tm, tn, n_rope_blocks):
    n = hb.shape[0]
    m = w.shape[1]
    tpb = lp // tm
    return pl.pallas_call(
        functools.partial(_swa_proj_kernel, tn=tn, n_rope_blocks=n_rope_blocks),
        grid=(n // tm, m // tn),
        in_specs=[pl.BlockSpec((tm, D_MODEL), lambda i, j: (i, 0)),
                  pl.BlockSpec((D_MODEL, tn), lambda i, j: (0, j)),
                  pl.BlockSpec((tm, LANES), lambda i, j: (i % tpb, 0)),
                  pl.BlockSpec((tm, LANES), lambda i, j: (i % tpb, 0)),
                  pl.BlockSpec((tn, tn), lambda i, j: (0, 0))],
        out_specs=pl.BlockSpec((tm, tn), lambda i, j: (i, j)),
        out_shape=jax.ShapeDtypeStruct((n, m), BF16),
        compiler_params=_params("parallel", "parallel"),
        name="swa_proj",
    )(hb, w, cos, sin, jnp.asarray(_rope_partner_matrix(tn), BF16))


def _rope_partner_matrix(width):
    p = np.zeros((width, width), np.float32)
    half = ROPE_DIM // 2
    for base in range(0, width, HEAD_DIM):
        for l in range(half):
            p[base + l + half, base + l] = 1.0
            p[base + l, base + l + half] = 1.0
    return p


MLA_A_COLS = MLA_Q_LORA + MLA_KV_LORA + 2 * LANES


def _mla_a_kernel(x_ref, w_ref, gq_ref, gkv_ref, cos_ref, sin_ref, cq_ref, ckv_ref, kr_ref, *, tm, tiles_per_batch):
    y = _dot(x_ref[...], w_ref[...])
    cq = y[:, :MLA_Q_LORA]
    ckv = y[:, MLA_Q_LORA:MLA_Q_LORA + MLA_KV_LORA]
    kr = y[:, MLA_Q_LORA + MLA_KV_LORA:MLA_Q_LORA + MLA_KV_LORA + LANES]
    krs = y[:, MLA_Q_LORA + MLA_KV_LORA + LANES:]
    rms = lambda z, g: z * lax.rsqrt(jnp.mean(z * z, axis=-1, keepdims=True) + RMS_EPS) * g
    cq_ref[...] = rms(cq, gq_ref[...]).astype(cq_ref.dtype)
    ckv_ref[...] = rms(ckv, gkv_ref[...]).astype(ckv_ref.dtype)
    dead = _dead_rows(pl.program_id(0) % tiles_per_batch, tm) * _one_hot_lanes(LANES, LANES, MLA_DEAD_LANE)
    kr_ref[...] = (kr * cos_ref[...] + krs * sin_ref[...] + dead).astype(kr_ref.dtype)


def _mla_a(hb, w, gq, gkv, cos, sin, lp, tm):
    n = hb.shape[0]
    tpb = lp // tm
    return pl.pallas_call(
        functools.partial(_mla_a_kernel, tm=tm, tiles_per_batch=tpb),
        grid=(n // tm,),
        in_specs=[pl.BlockSpec((tm, D_MODEL), lambda i: (i, 0)),
                  pl.BlockSpec((D_MODEL, MLA_A_COLS), lambda i: (0, 0)),
                  pl.BlockSpec((1, MLA_Q_LORA), lambda i: (0, 0)),
                  pl.BlockSpec((1, MLA_KV_LORA), lambda i: (0, 0)),
                  pl.BlockSpec((tm, LANES), lambda i: (i % tpb, 0)),
                  pl.BlockSpec((tm, LANES), lambda i: (i % tpb, 0))],
        out_specs=[pl.BlockSpec((tm, MLA_Q_LORA), lambda i: (i, 0)),
                   pl.BlockSpec((tm, MLA_KV_LORA), lambda i: (i, 0)),
                   pl.BlockSpec((tm, LANES), lambda i: (i, 0))],
        out_shape=[jax.ShapeDtypeStruct((n, MLA_Q_LORA), BF16),
                   jax.ShapeDtypeStruct((n, MLA_KV_LORA), BF16),
                   jax.ShapeDtypeStruct((n, LANES), BF16)],
        compiler_params=_params("parallel"),
        name="mla_a",
    )(hb, w, gq, gkv, cos, sin)


def _mla_q_kernel(x_ref, w_ref, ws_ref, cos_ref, sin_ref, o_ref):
    x = x_ref[...]
    reps = o_ref.shape[1] // LANES
    cos = jnp.concatenate([cos_ref[...]] * reps, axis=1)
    sin = jnp.concatenate([sin_ref[...]] * reps, axis=1)
    y = _dot(x, w_ref[...]) * cos + _dot(x, ws_ref[...]) * sin
    o_ref[...] = (y + _one_hot_lanes(o_ref.shape[1], LANES, MLA_DEAD_LANE)).astype(o_ref.dtype)


def _mla_q(cq, w, ws, cos, sin, lp, tm, tn):
    n = cq.shape[0]
    m = w.shape[1]
    tpb = lp // tm
    return pl.pallas_call(
        _mla_q_kernel,
        grid=(n // tm, m // tn),
        in_specs=[pl.BlockSpec((tm, MLA_Q_LORA), lambda i, j: (i, 0)),
                  pl.BlockSpec((MLA_Q_LORA, tn), lambda i, j: (0, j)),
                  pl.BlockSpec((MLA_Q_LORA, tn), lambda i, j: (0, j)),
                  pl.BlockSpec((tm, LANES), lambda i, j: (i % tpb, 0)),
                  pl.BlockSpec((tm, LANES), lambda i, j: (i % tpb, 0))],
        out_specs=pl.BlockSpec((tm, tn), lambda i, j: (i, j)),
        out_shape=jax.ShapeDtypeStruct((n, m), BF16),
        compiler_params=_params("parallel", "parallel"),
        name="mla_q",
    )(cq, w, ws, cos, sin)


def _oproj_ln_kernel(o_ref, w_ref, h_ref, g_ref, b_ref, hf_ref, hb_ref):
    half = o_ref.shape[0] // 2
    mix = [_dot(o_ref[r:r + half, :], w_ref[...]) for r in (0, half)]
    for r, m in zip((0, half), mix):
        y = _layer_norm(DEEPNORM_ALPHA * h_ref[r:r + half, :] + m, g_ref[...], b_ref[...])
        hf_ref[r:r + half, :] = y
        hb_ref[r:r + half, :] = y.astype(hb_ref.dtype)


def _oproj_ln(o, w, h, g, b, tm):
    n = o.shape[0]
    row = lambda i: (i, 0)
    fixed = lambda i: (0, 0)
    return pl.pallas_call(
        _oproj_ln_kernel,
        grid=(n // tm,),
        in_specs=[pl.BlockSpec((tm, D_MODEL), row), pl.BlockSpec((D_MODEL, D_MODEL), fixed),
                  pl.BlockSpec((tm, D_MODEL), row), pl.BlockSpec((1, D_MODEL), fixed),
                  pl.BlockSpec((1, D_MODEL), fixed)],
        out_specs=[pl.BlockSpec((tm, D_MODEL), row), pl.BlockSpec((tm, D_MODEL), row)],
        out_shape=[jax.ShapeDtypeStruct((n, D_MODEL), F32), jax.ShapeDtypeStruct((n, D_MODEL), BF16)],
        compiler_params=_params("parallel"),
        name="oproj_ln",
    )(o, w, h, g, b)


FFN_HALO = BF16_SUBLANES


FFN_CHUNK = 256
FFN_NC = D_FF // FFN_CHUNK
assert FFN_NC * FFN_CHUNK == D_FF and FFN_NC % 2 == 1


def _ffn_kernel(x_ref, halo_ref, h_ref, win_ref, cw_ref, cb_ref, wo_ref, g_ref, b_ref, hf_ref, hb_ref,
                xext_ref, ua_ref, ub_ref, acc_ref, *, tm, tiles_per_batch):
    i = pl.program_id(0)
    nc = FFN_NC
    pos = (i % tiles_per_batch) * tm - FFN_HALO + lax.broadcasted_iota(jnp.int32, (tm + FFN_HALO, 1), 0)
    xe = jnp.concatenate([halo_ref[...], x_ref[...]], axis=0).astype(F32)
    xext_ref[...] = jnp.where(pos >= LEAD, xe, 0.0).astype(BF16)

    cols = lambda idx: slice(idx * FFN_CHUNK, (idx + 1) * FFN_CHUNK)

    def up(u_ref, c):
        u_ref[0] = _dot(xext_ref[...], win_ref[:, cols(c)])
        u_ref[1] = _dot(xext_ref[...], win_ref[:, cols(nc + c)])

    def glu(u_ref, c):
        def conv(part, idx):
            u = u_ref[part]
            delayed = u * cw_ref[0:1, cols(idx)]
            for tap in range(1, CONV_W):
                delayed = u * cw_ref[tap:tap + 1, cols(idx)] + pltpu.roll(delayed, 1, 0)
            return cb_ref[:, cols(idx)] + delayed[FFN_HALO:, :]

        yg = conv(0, c)
        yv = conv(1, nc + c)
        return ((yg / (1.0 + jnp.exp(-yg))) * yv).astype(BF16)

    acc_ref[...] = DEEPNORM_ALPHA * h_ref[...]
    up(ua_ref, 0)

    for c in range(0, nc - 1, 2):
        up(ub_ref, c + 1)
        act_a = glu(ua_ref, c)
        up(ua_ref, c + 2)
        act_b = glu(ub_ref, c + 1)
        acc_ref[...] += _dot(act_a, wo_ref[cols(c), :]) + _dot(act_b, wo_ref[cols(c + 1), :])
    y = _layer_norm(acc_ref[...] + _dot(glu(ua_ref, nc - 1), wo_ref[cols(nc - 1), :]), g_ref[...], b_ref[...])
    hf_ref[...] = y
    hb_ref[...] = y.astype(hb_ref.dtype)


def _ffn(hb, hf, w_in, conv_w, conv_b, w_out, g, b, lp, tm):
    n = hb.shape[0]
    nc, fc = FFN_NC, FFN_CHUNK
    tpb = lp // tm
    halo_blocks = tm // FFN_HALO
    row = lambda i: (i, 0)
    fixed2 = lambda i: (0, 0)
    resident = dict(pipeline_mode=pl.Buffered(1))
    return pl.pallas_call(
        functools.partial(_ffn_kernel, tm=tm, tiles_per_batch=tpb),
        grid=(n // tm,),
        in_specs=[pl.BlockSpec((tm, D_MODEL), row),
                  pl.BlockSpec((FFN_HALO, D_MODEL), lambda i: (jnp.maximum(i * halo_blocks - 1, 0), 0)),
                  pl.BlockSpec((tm, D_MODEL), row),
                  pl.BlockSpec((D_MODEL, 2 * D_FF), fixed2, **resident),
                  pl.BlockSpec((CONV_W, 2 * D_FF), fixed2, **resident),
                  pl.BlockSpec((1, 2 * D_FF), fixed2, **resident),
                  pl.BlockSpec((D_FF, D_MODEL), fixed2, **resident),
                  pl.BlockSpec((1, D_MODEL), fixed2), pl.BlockSpec((1, D_MODEL), fixed2)],
        out_specs=[pl.BlockSpec((tm, D_MODEL), row), pl.BlockSpec((tm, D_MODEL), row)],
        out_shape=[jax.ShapeDtypeStruct((n, D_MODEL), F32), jax.ShapeDtypeStruct((n, D_MODEL), BF16)],
        scratch_shapes=[pltpu.VMEM((tm + FFN_HALO, D_MODEL), BF16),
                        pltpu.VMEM((2, tm + FFN_HALO, fc), F32), pltpu.VMEM((2, tm + FFN_HALO, fc), F32),
                        pltpu.VMEM((tm, D_MODEL), F32)],
        compiler_params=_params("parallel"),
        name="ffn",
    )(hb, hb, hf, w_in, conv_w, conv_b, w_out, g, b)


def _ffn_weights(w_in, conv_w, conv_b, w_out):
    return w_in.astype(BF16), conv_w, conv_b[None, :], w_out.astype(BF16)


def _rope_tables(lp, dim, theta, group, offset):
    pos = (jnp.arange(lp) - LEAD).astype(F32)
    inv = theta ** (-jnp.arange(0, dim, 2, dtype=F32) / dim)
    ang = pos[:, None] * inv[None, :]
    cos, sin = jnp.cos(ang), jnp.sin(ang)
    ones = lambda w: jnp.ones((lp, w), F32)
    zeros = lambda w: jnp.zeros((lp, w), F32)
    rest = group - offset - dim
    cos_g = jnp.concatenate([ones(offset), cos, cos, ones(rest)], axis=1)
    sin_g = jnp.concatenate([zeros(offset), -sin, sin, zeros(rest)], axis=1)
    reps = LANES // group
    return jnp.tile(cos_g, (1, reps)), jnp.tile(sin_g, (1, reps))


def _swap_halves(w, dim):
    return jnp.concatenate([w[..., dim // 2:dim], w[..., :dim // 2]], axis=-1)


def _head_blocks(main, extra):
    src = main if main is not None else extra
    rows, heads = src.shape[0], src.shape[1]
    m = main if main is not None else jnp.zeros((rows, heads, HEAD_DIM), F32)
    e = extra if extra is not None else jnp.zeros((rows, heads, 0), F32)
    pad = jnp.zeros((rows, heads, LANES - HEAD_DIM - e.shape[2]), F32)
    return jnp.concatenate([m, e, pad], axis=2).reshape(rows, heads * LANES)


def _mla_weights(w_a, w_uq, w_ukv):
    d = w_a.shape[0]
    w_kr = w_a[:, MLA_Q_LORA + MLA_KV_LORA:][:, None, :]
    w_a_cat = jnp.concatenate([w_a[:, :MLA_Q_LORA + MLA_KV_LORA], _head_blocks(None, w_kr),
                               _head_blocks(None, _swap_halves(w_kr, MLA_ROPE))], axis=1)
    scale = (MLA_NOPE + MLA_ROPE) ** -0.5 * LOG2E
    wq = (w_uq * scale).reshape(MLA_Q_LORA, HEADS, MLA_NOPE + MLA_ROPE)
    w_main = _head_blocks(wq[..., :MLA_NOPE], wq[..., MLA_NOPE:])
    w_swap = _head_blocks(None, _swap_halves(wq[..., MLA_NOPE:], MLA_ROPE))
    wkv = w_ukv.reshape(MLA_KV_LORA, HEADS, MLA_NOPE + HEAD_DIM)
    w_kn = wkv[..., :MLA_NOPE].reshape(MLA_KV_LORA, HEADS * MLA_NOPE)
    w_v = wkv[..., MLA_NOPE:].reshape(MLA_KV_LORA, HEADS * HEAD_DIM)
    return tuple(w.astype(BF16) for w in (w_a_cat, w_main, w_swap, w_kn)) + (w_v,)


def _swa_weights(w_in):
    qd = HEADS * HEAD_DIM
    kd = SWA_KV_HEADS * HEAD_DIM
    q = w_in[:, :qd] * (HEAD_DIM ** -0.5)
    dup = lambda w: jnp.concatenate([w[:, :HEAD_DIM], w[:, :HEAD_DIM], w[:, HEAD_DIM:], w[:, HEAD_DIM:]], axis=1)
    return jnp.concatenate([q, dup(w_in[:, qd:qd + kd]), dup(w_in[:, qd + kd:])], axis=1).astype(BF16)


def _fox_weights(w_in, b_f):
    hd = HEADS * HEAD_DIM
    d = w_in.shape[0]
    w_q = (w_in[:, :hd] * (HEAD_DIM ** -0.5 * LOG2E)).astype(BF16)
    w_k = w_in[:, hd:2 * hd].astype(BF16)
    w_v = w_in[:, 2 * hd:3 * hd]
    w_gate = w_in[:, 3 * hd:]
    w_fg = jnp.concatenate([w_gate] * GATE_SLOTS + [jnp.zeros((d, LANES - GATE_SLOTS * HEADS), F32)],
                           axis=1).astype(BF16)
    b_fg = jnp.concatenate([b_f] * GATE_SLOTS + [jnp.zeros((LANES - GATE_SLOTS * HEADS,), F32)])[None, :]
    return w_q, w_k, w_v, w_fg, b_fg


def kernel(x, meta_tokens, ln1_g, ln1_b, ln2_g, ln2_b, fox_w_in, fox_b_f, fox_w_o, swa_w_in, swa_sinks, swa_w_o,
           mla_w_a, mla_g_q, mla_g_kv, mla_w_uq, mla_w_ukv, mla_w_o, ffn_w_in, ffn_conv_w, ffn_conv_b, ffn_w_out):
    batch, seq, d = x.shape
    assert d == D_MODEL and seq % 256 == 0
    lp = seq + FIRST_REAL
    n = batch * lp

    tm = _tile(lp, 768, 256)
    tq = _tile(lp, 768, 256)
    tk = 256
    tn = 512
    nk = lp // tk

    hf, hb = _embed(x, meta_tokens.astype(x.dtype))

    cos_p, sin_p = _rope_tables(lp, ROPE_DIM, ROPE_THETA, HEAD_DIM, 0)
    cos_m, sin_m = _rope_tables(lp, MLA_ROPE, MLA_ROPE_THETA, LANES, MLA_NOPE)
    b3 = lambda a: a.reshape(batch, lp, -1)

    for i in range(DEPTH):
        kind, j = i % 3, i // 3
        if kind == 0:
            w_q, w_k, w_v, w_fg, b_fg = _fox_weights(fox_w_in[j], fox_b_f[j])
            aq, ak = _fox_gate(hb, w_fg, b_fg, lp, tm)
            qh = _head_proj(hb, w_q, aq, tm, 2 * tn)
            kh = _head_proj(hb, w_k, ak, tm, 2 * tn)
            vt = _matmul_t(hb, *_value_weights(w_v), tm, V_COLS_TILE, tk).reshape(batch, nk, HEADS * V_ROWS, tk)
            o = _flash_attention(b3(qh), b3(kh), vt, batch, lp, tq, tk)
            w_o = fox_w_o[j]
        elif kind == 1:
            qkv = _swa_proj(hb, _swa_weights(swa_w_in[j]), cos_p, sin_p, lp, tm, 2 * LANES, 5)
            o = _swa_attention(b3(qkv), swa_sinks[j].astype(F32), batch, lp)
            w_o = swa_w_o[j]
        else:
            w_a_cat, w_main, w_swap, w_kn, w_v = _mla_weights(mla_w_a[j], mla_w_uq[j], mla_w_ukv[j])
            cq, ckv, kr = _mla_a(hb, w_a_cat, mla_g_q[j][None, :], mla_g_kv[j][None, :], cos_m, sin_m, lp, tm)
            qh = _mla_q(cq, w_main, w_swap, cos_m, sin_m, lp, tm, 2 * tn)
            kh = _head_proj(ckv, w_kn, kr, tm, 2 * tn)
            vt = _matmul_t(ckv, *_value_weights(w_v), tm, V_COLS_TILE, tk).reshape(batch, nk, HEADS * V_ROWS, tk)
            o = _flash_attention(b3(qh), b3(kh), vt, batch, lp, tq, tk)
            w_o = mla_w_o[j]
        hf, hb = _oproj_ln(o.reshape(n, d), w_o.astype(BF16), hf, ln1_g[i][None, :], ln1_b[i][None, :], tm)
        hf, hb = _ffn(hb, hf, *_ffn_weights(ffn_w_in[i], ffn_conv_w[i], ffn_conv_b[i], ffn_w_out[i]),
                      ln2_g[i][None, :], ln2_b[i][None, :], lp, tm)
    return hf.reshape(batch, lp, d)[:, FIRST_REAL:]
```

```python
import functools
import math

import numpy as np
import jax
import jax.numpy as jnp
from jax import lax
from jax.experimental import pallas as pl
from jax.experimental.pallas import tpu as pltpu

F32 = jnp.float32
BF16 = jnp.bfloat16

D_MODEL = 1024
DEPTH = 4
N_META = 16
LEAD = 240
FIRST_REAL = LEAD + N_META
NEG = -1e30
DEEPNORM_ALPHA = (2.0 * DEPTH) ** 0.25
LN_EPS = 1e-5
RMS_EPS = 1e-6
HEADS = 16
HEAD_DIM = 64
PAIRS = HEADS // 2
SWA_KV_HEADS = 2
WINDOW = 128
ROPE_THETA = 500000.0
ROPE_DIM = 16
MLA_Q_LORA = 384
MLA_KV_LORA = 256
MLA_NOPE = 64
MLA_ROPE = 32
MLA_ROPE_THETA = 10000.0
D_FF = 2816
CONV_W = 3
LOG2E = math.log2(math.e)

LANES = 128
BF16_SUBLANES = 16
VMEM_LIMIT = 56 * 1024 * 1024

GATE_SLOTS = 3
FOX_DEAD_LANE = HEAD_DIM + 2 * GATE_SLOTS
MLA_DEAD_LANE = MLA_NOPE + MLA_ROPE
M_INIT = -3e38


def _params(*sem):
    return pltpu.CompilerParams(dimension_semantics=sem, vmem_limit_bytes=VMEM_LIMIT)


def _tile(n, pref, mult):
    best = mult
    t = mult
    while t <= min(n, pref):
        if n % t == 0:
            best = t
        t += mult
    assert n % best == 0
    return best


def _dot(a, b):
    return jnp.dot(a, b, preferred_element_type=F32)


def _dot_nt(a, b):
    return lax.dot_general(a, b, (((1,), (1,)), ((), ())), preferred_element_type=F32)


def _layer_norm(x, g, b):
    mu = jnp.mean(x, axis=-1, keepdims=True)
    xc = x - mu
    var = jnp.mean(xc * xc, axis=-1, keepdims=True)
    return xc * lax.rsqrt(var + LN_EPS) * g + b


def _one_hot_lanes(width, period, lane):
    idx = lax.broadcasted_iota(jnp.int32, (1, width), 1)
    return jnp.where((idx & (period - 1)) == lane, 1.0, 0.0)


def _dead_rows(tile_in_batch, tm):
    pos = tile_in_batch * tm + lax.broadcasted_iota(jnp.int32, (tm, 1), 0)
    return jnp.where(pos < LEAD, NEG, 0.0)


def _embed_kernel(x_ref, meta_ref, hf_ref, hb_ref):
    t = pl.program_id(1)

    @pl.when(t == 0)
    def _():
        lead = jnp.concatenate([jnp.zeros((LEAD, D_MODEL), F32), meta_ref[...]], axis=0)
        hf_ref[...] = lead
        hb_ref[...] = lead.astype(hb_ref.dtype)

    @pl.when(t > 0)
    def _():
        hf_ref[...] = x_ref[0]
        hb_ref[...] = x_ref[0].astype(hb_ref.dtype)


def _embed(x, meta):
    batch, seq, d = x.shape
    blocks = seq // FIRST_REAL + 1
    return pl.pallas_call(
        _embed_kernel,
        grid=(batch, blocks),
        in_specs=[pl.BlockSpec((1, FIRST_REAL, d), lambda b, t: (b, jnp.maximum(t - 1, 0), 0)),
                  pl.BlockSpec((N_META, d), lambda b, t: (0, 0))],
        out_specs=[pl.BlockSpec((FIRST_REAL, d), lambda b, t: (b * blocks + t, 0)),
                   pl.BlockSpec((FIRST_REAL, d), lambda b, t: (b * blocks + t, 0))],
        out_shape=[jax.ShapeDtypeStruct((batch * blocks * FIRST_REAL, d), F32),
                   jax.ShapeDtypeStruct((batch * blocks * FIRST_REAL, d), BF16)],
        compiler_params=_params("parallel", "parallel"),
        name="embed",
    )(x, meta)


def _head_proj_kernel(x_ref, w_ref, e_ref, o_ref):
    y = _dot(x_ref[...], w_ref[...])
    lo = lax.broadcasted_iota(jnp.int32, (1, LANES), 1) < HEAD_DIM
    shared = e_ref.shape[1] == LANES
    for pair in range(y.shape[1] // LANES):
        y_pair = y[:, pair * LANES:(pair + 1) * LANES]
        for a, feats in enumerate((y_pair, pltpu.roll(y_pair, HEAD_DIM, 1))):
            h = 2 * pair + a
            extra = e_ref[...] if shared else e_ref[:, h * LANES:(h + 1) * LANES]
            o_ref[:, h * LANES:(h + 1) * LANES] = jnp.where(lo, feats, extra.astype(F32)).astype(o_ref.dtype)


def _head_proj(x, w, e, tm, tn):
    n, k = x.shape
    m = w.shape[1]
    if e.shape[1] == LANES:
        e_spec = pl.BlockSpec((tm, LANES), lambda i, j: (i, 0))
    else:
        assert e.shape[1] == 2 * m
        e_spec = pl.BlockSpec((tm, 2 * tn), lambda i, j: (i, j))
    return pl.pallas_call(
        _head_proj_kernel,
        grid=(n // tm, m // tn),
        in_specs=[pl.BlockSpec((tm, k), lambda i, j: (i, 0)),
                  pl.BlockSpec((k, tn), lambda i, j: (0, j)),
                  e_spec],
        out_specs=pl.BlockSpec((tm, 2 * tn), lambda i, j: (i, j)),
        out_shape=jax.ShapeDtypeStruct((n, 2 * m), BF16),
        compiler_params=_params("parallel", "parallel"),
        name="head_proj",
    )(x, w, e)


def _mm_t_kernel(x_ref, w_ref, b_ref, o_ref, *, tk):
    y = _dot(x_ref[...], w_ref[...]) + b_ref[...]
    for c in range(o_ref.shape[0]):
        o_ref[c] = y[c * tk:(c + 1) * tk, :].T.astype(o_ref.dtype)


def _matmul_t(x, w, bias, tm, tn, tk):
    n, k = x.shape
    m = w.shape[1]
    r = tm // tk
    return pl.pallas_call(
        functools.partial(_mm_t_kernel, tk=tk),
        grid=(n // tm, m // tn),
        in_specs=[pl.BlockSpec((tm, k), lambda i, j: (i, 0)),
                  pl.BlockSpec((k, tn), lambda i, j: (0, j)),
                  pl.BlockSpec((1, tn), lambda i, j: (0, j))],
        out_specs=pl.BlockSpec((r, tn, tk), lambda i, j: (i, j, 0)),
        out_shape=jax.ShapeDtypeStruct((n // tk, m, tk), BF16),
        compiler_params=_params("parallel", "parallel"),
        name="matmul_t",
    )(x, w, bias)


V_ROWS = HEAD_DIM + BF16_SUBLANES
V_COLS_TILE = 8 * V_ROWS


def _value_weights(w_v):
    rows = w_v.shape[0]
    w = jnp.concatenate([w_v.reshape(rows, HEADS, HEAD_DIM),
                         jnp.zeros((rows, HEADS, V_ROWS - HEAD_DIM), w_v.dtype)], axis=2)
    bias = np.zeros((HEADS, V_ROWS), np.float32)
    bias[:, HEAD_DIM] = 1.0
    return w.reshape(rows, HEADS * V_ROWS).astype(BF16), jnp.asarray(bias.reshape(1, HEADS * V_ROWS))


def _gate_placement():
    wide = HEADS * LANES
    sq = np.zeros((LANES, wide), np.float32)
    sk = np.zeros_like(sq)
    oq = np.zeros((1, wide), np.float32)
    ok = np.zeros_like(oq)
    for h in range(HEADS):
        base = h * LANES + HEAD_DIM
        for part in range(GATE_SLOTS):
            sq[part * HEADS + h, base + part] = 1.0
            sk[part * HEADS + h, base + GATE_SLOTS + part] = -1.0
            oq[0, base + GATE_SLOTS + part] = 1.0
            ok[0, base + part] = 1.0
        oq[0, h * LANES + FOX_DEAD_LANE] = 1.0
    return sq, sk, oq, ok


def _split3(x):
    hi = x.astype(BF16)
    r1 = x - hi.astype(F32)
    mid = r1.astype(BF16)
    lo = (r1 - mid.astype(F32)).astype(BF16)
    return hi, mid, lo


def _fox_gate_kernel(x_ref, w_ref, b_ref, sq_ref, sk_ref, oq_ref, ok_ref, aq_ref, ak_ref, carry_ref,
                     *, tm, tiles_per_batch):
    i = pl.program_id(0)

    @pl.when(i % tiles_per_batch == 0)
    def _():
        carry_ref[...] = jnp.zeros_like(carry_ref)

    fg = _dot(x_ref[...], w_ref[...]) + b_ref[...]
    logf = jnp.minimum(fg, 0.0) - jnp.log(1.0 + jnp.exp(-jnp.abs(fg)))
    sub = FIRST_REAL
    row = lax.broadcasted_iota(jnp.int32, (sub, sub), 0)
    col = lax.broadcasted_iota(jnp.int32, (sub, sub), 1)
    tri = jnp.where(col <= row, 1.0, 0.0).astype(BF16)
    hi, mid, lo = _split3(logf)
    carry = carry_ref[...]
    pieces = []
    for r0 in range(0, tm, sub):
        rows = slice(r0, r0 + sub)
        piece = _dot(tri, hi[rows]) + _dot(tri, mid[rows]) + _dot(tri, lo[rows]) + carry
        carry = piece[sub - 1:sub, :]
        pieces.append(piece)
    cs = jnp.concatenate(pieces, axis=0)
    carry_ref[...] = carry
    lane = lax.broadcasted_iota(jnp.int32, (1, LANES), 1)
    hi, mid, lo = _split3(cs * LOG2E)
    parts = jnp.where(lane < HEADS, hi.astype(F32),
                      jnp.where(lane < 2 * HEADS, mid.astype(F32), lo.astype(F32))).astype(BF16)
    wide = aq_ref.shape[1]
    dead = _dead_rows(i % tiles_per_batch, tm) * _one_hot_lanes(wide, LANES, FOX_DEAD_LANE)
    aq_ref[...] = (_dot(parts, sq_ref[...]) + oq_ref[...]).astype(aq_ref.dtype)
    ak_ref[...] = (_dot(parts, sk_ref[...]) + ok_ref[...] + dead).astype(ak_ref.dtype)


def _fox_gate(hb, w_fg, b_fg, lp, tm):
    n = hb.shape[0]
    tpb = lp // tm
    sq, sk, oq, ok = _gate_placement()
    fixed = lambda i: (0, 0)
    wide = HEADS * LANES
    return pl.pallas_call(
        functools.partial(_fox_gate_kernel, tm=tm, tiles_per_batch=tpb),
        grid=(n // tm,),
        in_specs=[pl.BlockSpec((tm, D_MODEL), lambda i: (i, 0)),
                  pl.BlockSpec((D_MODEL, LANES), fixed),
                  pl.BlockSpec((1, LANES), fixed),
                  pl.BlockSpec((LANES, wide), fixed),
                  pl.BlockSpec((LANES, wide), fixed),
                  pl.BlockSpec((1, wide), fixed),
                  pl.BlockSpec((1, wide), fixed)],
        out_specs=[pl.BlockSpec((tm, wide), lambda i: (i, 0)),
                   pl.BlockSpec((tm, wide), lambda i: (i, 0))],
        out_shape=[jax.ShapeDtypeStruct((n, wide), BF16), jax.ShapeDtypeStruct((n, wide), BF16)],
        scratch_shapes=[pltpu.VMEM((1, LANES), F32)],
        compiler_params=_params("arbitrary"),
        name="fox_gate",
    )(hb, w_fg, b_fg, jnp.asarray(sq, BF16), jnp.asarray(sk, BF16), jnp.asarray(oq), jnp.asarray(ok))


def _flash_kernel(q_ref, k_ref, vt_ref, o_ref, sa_ref, sb_ref, xa_ref, xb_ref, m_ref, acc_ref, *, tq, tk):
    i = pl.program_id(2)
    r = tq // tk
    m_ref[...] = jnp.full_like(m_ref, M_INIT)
    acc_ref[...] = jnp.zeros_like(acc_ref)

    def diag_mask(s):
        keep = lax.broadcasted_iota(jnp.int32, s.shape, 0) <= lax.broadcasted_iota(jnp.int32, s.shape, 1)
        return jnp.where(keep, s, NEG)

    def scores(s_ref, x_ref, j, a, qs, diagonal, blk=None):
        off = pl.multiple_of(j * tk, tk)
        q0 = pl.multiple_of((i if blk is None else blk) * tq + qs, tk)
        s = _dot_nt(k_ref[0, pl.ds(off, tk), a * LANES:(a + 1) * LANES],
                    q_ref[0, pl.ds(q0, tq - qs), a * LANES:(a + 1) * LANES])
        if diagonal:
            s = diag_mask(s)
        s_ref[a, :, qs:] = s
        x_ref[a, :, qs:] = jnp.max(s, axis=0, keepdims=True)

    def consume(s_ref, x_ref, j, a, qs, mask_now):
        s = s_ref[a, :, qs:]
        if mask_now:
            s = diag_mask(s)
            smax = jnp.max(s, axis=0, keepdims=True)
        else:
            smax = x_ref[a, :, qs:]
        m_old = m_ref[a, :, qs:]
        m_new = jnp.maximum(m_old, smax)
        alpha = jnp.exp2(m_old - m_new)
        p = jnp.exp2(s - m_new)
        m_ref[a, :, qs:] = m_new
        pv = _dot(vt_ref[0, j, a * V_ROWS:(a + 1) * V_ROWS, :], p.astype(BF16))
        acc_ref[a, :, qs:] = alpha * acc_ref[a, :, qs:] + pv

    buf_a, buf_b = (sa_ref, xa_ref), (sb_ref, xb_ref)
    jdiag = i * r
    odd = jdiag & 1

    @pl.when(i == 0)
    def _():
        for a in range(2):
            scores(*buf_a, 0, a, 0, False)

    @pl.when(odd == 1)
    def _():
        for a in range(2):
            scores(*buf_b, 1, a, 0, False)
            consume(*buf_a, 0, a, 0, False)
        sa_ref[...] = sb_ref[...]
        xa_ref[...] = xb_ref[...]

    def pair(j):
        for a in range(2):
            scores(*buf_b, j + 1, a, 0, False)
            consume(*buf_a, j, a, 0, False)
        for a in range(2):
            scores(*buf_a, j + 2, a, 0, False)
            consume(*buf_b, j + 1, a, 0, False)

    pairs = (jdiag - odd) // 2
    one = pairs & 1
    two = (pairs >> 1) & 1

    @pl.when(one == 1)
    def _():
        pair(odd)

    @pl.when(two == 1)
    def _():
        j = odd + 2 * one
        pair(j)
        pair(j + 2)

    def octo_body(t, c):
        j = odd + 2 * one + 4 * two + 8 * t
        for u in range(4):
            pair(j + 2 * u)
        return c

    lax.fori_loop(0, pairs >> 2, octo_body, 0)

    bufs = (buf_a, buf_b)
    for d in range(r):
        qs = d * tk
        for a in range(2):
            if d + 1 < r:
                scores(*bufs[(d + 1) & 1], jdiag + d + 1, a, qs + tk, True)
            consume(*bufs[d & 1], jdiag + d, a, qs, d == 0)

    for a in range(2):
        scores(*buf_a, 0, a, 0, False, blk=jnp.minimum(i + 1, pl.num_programs(2) - 1))

    ot = jnp.concatenate([acc_ref[a, :HEAD_DIM, :] * (1.0 / acc_ref[a, HEAD_DIM:HEAD_DIM + 1, :]) for a in range(2)],
                         axis=0)
    o_ref[0] = ot.T.astype(o_ref.dtype)


def _flash_attention(qh, kh, vt, batch, lp, tq, tk):
    nk = lp // tk
    return pl.pallas_call(
        functools.partial(_flash_kernel, tq=tq, tk=tk),
        grid=(batch, PAIRS, lp // tq),
        in_specs=[pl.BlockSpec((1, lp, 2 * LANES), lambda b, p, i: (b, 0, p)),
                  pl.BlockSpec((1, lp, 2 * LANES), lambda b, p, i: (b, 0, p)),
                  pl.BlockSpec((1, nk, 2 * V_ROWS, tk), lambda b, p, i: (b, 0, p, 0))],
        out_specs=pl.BlockSpec((1, tq, 2 * HEAD_DIM), lambda b, p, i: (b, i, p)),
        out_shape=jax.ShapeDtypeStruct((batch, lp, HEADS * HEAD_DIM), BF16),
        scratch_shapes=[pltpu.VMEM((2, tk, tq), F32), pltpu.VMEM((2, tk, tq), F32),
                        pltpu.VMEM((2, 1, tq), F32), pltpu.VMEM((2, 1, tq), F32),
                        pltpu.VMEM((2, 1, tq), F32), pltpu.VMEM((2, V_ROWS, tq), F32)],
        compiler_params=_params("parallel", "parallel", "arbitrary"),
        name="flash_attention",
    )(qh, kh, vt)


SWA_TQ = 128


def _swa_attn_kernel(sink_ref, q_ref, km_ref, kp_ref, kc_ref, vm_ref, vp_ref, vc_ref, o_ref):
    i = pl.program_id(1)
    t = SWA_TQ
    lo = lax.broadcasted_iota(jnp.int32, (1, LANES), 1) < HEAD_DIM
    row = lax.broadcasted_iota(jnp.int32, (t, 3 * t), 0)
    col = lax.broadcasted_iota(jnp.int32, (t, 3 * t), 1)
    qpos = i * t + row
    kpos = jnp.where(col < t, t + col, (i - 2) * t + col)
    d = qpos - kpos
    valid = (d >= 0) & (((col < t) & (kpos >= LEAD)) |
                        ((col >= t) & (d < WINDOW) & (kpos >= FIRST_REAL)))
    kcat, v_lo, v_hi = [], [], []
    for g in range(SWA_KV_HEADS):
        sl = slice(g * LANES, (g + 1) * LANES)
        kcat.append(jnp.concatenate([km_ref[0, :, sl], kp_ref[0, :, sl], kc_ref[0, :, sl]], axis=0))
        vf = jnp.concatenate([vm_ref[0, :, sl], vp_ref[0, :, sl], vc_ref[0, :, sl]], axis=0).astype(F32)
        v_lo.append(jnp.where(lo, vf, 0.0).astype(BF16))
        v_hi.append(jnp.where(lo, 0.0, vf).astype(BF16))
    pairs_per_group = PAIRS // SWA_KV_HEADS

    def logits(p):
        qf = q_ref[0, :, p * LANES:(p + 1) * LANES].astype(F32)
        q_pair = (jnp.where(lo, qf, 0.0).astype(BF16), jnp.where(lo, 0.0, qf).astype(BF16))
        return [_dot_nt(q_pair[a], kcat[p // pairs_per_group]) for a in range(2)]

    def finish(p, s_pair):
        g = p // pairs_per_group
        ps, inv = [], []
        for a in range(2):
            sink = sink_ref[2 * p + a]
            s = jnp.where(valid, s_pair[a], NEG)
            m = jnp.maximum(jnp.max(s, axis=1, keepdims=True), sink)
            e = jnp.exp(s - m)
            den = jnp.sum(e, axis=1, keepdims=True) + jnp.exp(sink - m)
            ps.append(e.astype(BF16))
            inv.append(1.0 / den)
        o = (_dot(ps[0], v_lo[g]) + _dot(ps[1], v_hi[g])) * jnp.where(lo, inv[0], inv[1])
        o_ref[0, :, p * LANES:(p + 1) * LANES] = o.astype(o_ref.dtype)

    s_next = logits(0)
    for p in range(PAIRS):
        s_cur = s_next
        if p + 1 < PAIRS:
            s_next = logits(p + 1)
        finish(p, s_cur)


def _swa_attention(qkv, sinks, batch, lp):
    t = SWA_TQ
    kblk, vblk = 4, 5
    kv_spec = lambda col, row_of: pl.BlockSpec((1, t, 2 * LANES), lambda b, i: (b, row_of(i), col))
    meta = lambda i: 1
    prev = lambda i: jnp.maximum(i - 1, 0)
    cur = lambda i: i
    return pl.pallas_call(
        _swa_attn_kernel,
        grid=(batch, lp // t),
        in_specs=[pl.BlockSpec(memory_space=pltpu.SMEM),
                  pl.BlockSpec((1, t, HEADS * HEAD_DIM), lambda b, i: (b, i, 0)),
                  kv_spec(kblk, meta), kv_spec(kblk, prev), kv_spec(kblk, cur),
                  kv_spec(vblk, meta), kv_spec(vblk, prev), kv_spec(vblk, cur)],
        out_specs=pl.BlockSpec((1, t, HEADS * HEAD_DIM), lambda b, i: (b, i, 0)),
        out_shape=jax.ShapeDtypeStruct((batch, lp, HEADS * HEAD_DIM), BF16),
        compiler_params=_params("parallel", "parallel"),
        name="swa_attention",
    )(sinks, qkv, qkv, qkv, qkv, qkv, qkv, qkv)


def _swa_proj_kernel(x_ref, w_ref, cos_ref, sin_ref, perm_ref, o_ref, *, tn, n_rope_blocks):
    j = pl.program_id(1)
    y = _dot(x_ref[...], w_ref[...])

    @pl.when(j < n_rope_blocks)
    def _():
        reps = tn // LANES
        cos = jnp.concatenate([cos_ref[...]] * reps, axis=1)
        sin = jnp.concatenate([sin_ref[...]] * reps, axis=1)
        partner = _dot(y.astype(BF16), perm_ref[...])
        o_ref[...] = (y * cos + partner * sin).astype(o_ref.dtype)

    @pl.when(j >= n_rope_blocks)
    def _():
        o_ref[...] = y.astype(o_ref.dtype)


def _swa_proj(hb, w, cos, sin, lp, tm, tn, n_rope_blocks):
    n = hb.shape[0]
    m = w.shape[1]
    tpb = lp // tm
    return pl.pallas_call(
        functools.partial(_swa_proj_kernel, tn=tn, n_rope_blocks=n_rope_blocks),
        grid=(n // tm, m // tn),
        in_specs=[pl.BlockSpec((tm, D_MODEL), lambda i, j: (i, 0)),
                  pl.BlockSpec((D_MODEL, tn), lambda i, j: (0, j)),
                  pl.BlockSpec((tm, LANES), lambda i, j: (i % tpb, 0)),
                  pl.BlockSpec((tm, LANES), lambda i, j: (i % tpb, 0)),
                  pl.BlockSpec((tn, tn), lambda i, j: (0, 0))],
        out_specs=pl.BlockSpec((tm, tn), lambda i, j: (i, j)),
        out_shape=jax.ShapeDtypeStruct((n, m), BF16),
        compiler_params=_params("parallel", "parallel"),
        name="swa_proj",
    )(hb, w, cos, sin, jnp.asarray(_rope_partner_matrix(tn), BF16))


def _rope_partner_matrix(width):
    p = np.zeros((width, width), np.float32)
    half = ROPE_DIM // 2
    for base in range(0, width, HEAD_DIM):
        for l in range(half):
            p[base + l + half, base + l] = 1.0
            p[base + l, base + l + half] = 1.0
    return p


MLA_A_COLS = MLA_Q_LORA + MLA_KV_LORA + 2 * LANES


def _mla_a_kernel(x_ref, w_ref, gq_ref, gkv_ref, cos_ref, sin_ref, cq_ref, ckv_ref, kr_ref, *, tm, tiles_per_batch):
    y = _dot(x_ref[...], w_ref[...])
    cq = y[:, :MLA_Q_LORA]
    ckv = y[:, MLA_Q_LORA:MLA_Q_LORA + MLA_KV_LORA]
    kr = y[:, MLA_Q_LORA + MLA_KV_LORA:MLA_Q_LORA + MLA_KV_LORA + LANES]
    krs = y[:, MLA_Q_LORA + MLA_KV_LORA + LANES:]
    rms = lambda z, g: z * lax.rsqrt(jnp.mean(z * z, axis=-1, keepdims=True) + RMS_EPS) * g
    cq_ref[...] = rms(cq, gq_ref[...]).astype(cq_ref.dtype)
    ckv_ref[...] = rms(ckv, gkv_ref[...]).astype(ckv_ref.dtype)
    dead = _dead_rows(pl.program_id(0) % tiles_per_batch, tm) * _one_hot_lanes(LANES, LANES, MLA_DEAD_LANE)
    kr_ref[...] = (kr * cos_ref[...] + krs * sin_ref[...] + dead).astype(kr_ref.dtype)


def _mla_a(hb, w, gq, gkv, cos, sin, lp, tm):
    n = hb.shape[0]
    tpb = lp // tm
    return pl.pallas_call(
        functools.partial(_mla_a_kernel, tm=tm, tiles_per_batch=tpb),
        grid=(n // tm,),
        in_specs=[pl.BlockSpec((tm, D_MODEL), lambda i: (i, 0)),
                  pl.BlockSpec((D_MODEL, MLA_A_COLS), lambda i: (0, 0)),
                  pl.BlockSpec((1, MLA_Q_LORA), lambda i: (0, 0)),
                  pl.BlockSpec((1, MLA_KV_LORA), lambda i: (0, 0)),
                  pl.BlockSpec((tm, LANES), lambda i: (i % tpb, 0)),
                  pl.BlockSpec((tm, LANES), lambda i: (i % tpb, 0))],
        out_specs=[pl.BlockSpec((tm, MLA_Q_LORA), lambda i: (i, 0)),
                   pl.BlockSpec((tm, MLA_KV_LORA), lambda i: (i, 0)),
                   pl.BlockSpec((tm, LANES), lambda i: (i, 0))],
        out_shape=[jax.ShapeDtypeStruct((n, MLA_Q_LORA), BF16),
                   jax.ShapeDtypeStruct((n, MLA_KV_LORA), BF16),
                   jax.ShapeDtypeStruct((n, LANES), BF16)],
        compiler_params=_params("parallel"),
        name="mla_a",
    )(hb, w, gq, gkv, cos, sin)


def _mla_q_kernel(x_ref, w_ref, ws_ref, cos_ref, sin_ref, o_ref):
    x = x_ref[...]
    reps = o_ref.shape[1] // LANES
    cos = jnp.concatenate([cos_ref[...]] * reps, axis=1)
    sin = jnp.concatenate([sin_ref[...]] * reps, axis=1)
    y = _dot(x, w_ref[...]) * cos + _dot(x, ws_ref[...]) * sin
    o_ref[...] = (y + _one_hot_lanes(o_ref.shape[1], LANES, MLA_DEAD_LANE)).astype(o_ref.dtype)


def _mla_q(cq, w, ws, cos, sin, lp, tm, tn):
    n = cq.shape[0]
    m = w.shape[1]
    tpb = lp // tm
    return pl.pallas_call(
        _mla_q_kernel,
        grid=(n // tm, m // tn),
        in_specs=[pl.BlockSpec((tm, MLA_Q_LORA), lambda i, j: (i, 0)),
                  pl.BlockSpec((MLA_Q_LORA, tn), lambda i, j: (0, j)),
                  pl.BlockSpec((MLA_Q_LORA, tn), lambda i, j: (0, j)),
                  pl.BlockSpec((tm, LANES), lambda i, j: (i % tpb, 0)),
                  pl.BlockSpec((tm, LANES), lambda i, j: (i % tpb, 0))],
        out_specs=pl.BlockSpec((tm, tn), lambda i, j: (i, j)),
        out_shape=jax.ShapeDtypeStruct((n, m), BF16),
        compiler_params=_params("parallel", "parallel"),
        name="mla_q",
    )(cq, w, ws, cos, sin)


def _oproj_ln_kernel(o_ref, w_ref, h_ref, g_ref, b_ref, hf_ref, hb_ref):
    x = DEEPNORM_ALPHA * h_ref[...] + _dot(o_ref[...], w_ref[...])
    y = _layer_norm(x, g_ref[...], b_ref[...])
    hf_ref[...] = y
    hb_ref[...] = y.astype(hb_ref.dtype)


def _oproj_ln(o, w, h, g, b, tm):
    n = o.shape[0]
    row = lambda i: (i, 0)
    fixed = lambda i: (0, 0)
    return pl.pallas_call(
        _oproj_ln_kernel,
        grid=(n // tm,),
        in_specs=[pl.BlockSpec((tm, D_MODEL), row), pl.BlockSpec((D_MODEL, D_MODEL), fixed),
                  pl.BlockSpec((tm, D_MODEL), row), pl.BlockSpec((1, D_MODEL), fixed),
                  pl.BlockSpec((1, D_MODEL), fixed)],
        out_specs=[pl.BlockSpec((tm, D_MODEL), row), pl.BlockSpec((tm, D_MODEL), row)],
        out_shape=[jax.ShapeDtypeStruct((n, D_MODEL), F32), jax.ShapeDtypeStruct((n, D_MODEL), BF16)],
        compiler_params=_params("parallel"),
        name="oproj_ln",
    )(o, w, h, g, b)


FFN_HALO = BF16_SUBLANES


FFN_CHUNK = 256
FFN_NC = D_FF // FFN_CHUNK
assert FFN_NC * FFN_CHUNK == D_FF and FFN_NC % 2 == 1


def _ffn_kernel(x_ref, halo_ref, h_ref, win_ref, cw_ref, cb_ref, wo_ref, g_ref, b_ref, hf_ref, hb_ref,
                xext_ref, ua_ref, ub_ref, acc_ref, *, tm, tiles_per_batch):
    i = pl.program_id(0)
    nc = FFN_NC
    pos = (i % tiles_per_batch) * tm - FFN_HALO + lax.broadcasted_iota(jnp.int32, (tm + FFN_HALO, 1), 0)
    xe = jnp.concatenate([halo_ref[...], x_ref[...]], axis=0).astype(F32)
    xext_ref[...] = jnp.where(pos >= LEAD, xe, 0.0).astype(BF16)

    cols = lambda idx: slice(idx * FFN_CHUNK, (idx + 1) * FFN_CHUNK)

    def up(u_ref, c):
        u_ref[0] = _dot(xext_ref[...], win_ref[:, cols(c)])
        u_ref[1] = _dot(xext_ref[...], win_ref[:, cols(nc + c)])

    def glu(u_ref, c):
        def conv(part, idx):
            u = u_ref[part]
            delayed = u * cw_ref[0:1, cols(idx)]
            for tap in range(1, CONV_W):
                delayed = u * cw_ref[tap:tap + 1, cols(idx)] + pltpu.roll(delayed, 1, 0)
            return cb_ref[:, cols(idx)] + delayed[FFN_HALO:, :]

        yg = conv(0, c)
        yv = conv(1, nc + c)
        return ((yg / (1.0 + jnp.exp(-yg))) * yv).astype(BF16)

    acc_ref[...] = DEEPNORM_ALPHA * h_ref[...]
    up(ua_ref, 0)

    for c in range(0, nc - 1, 2):
        up(ub_ref, c + 1)
        act_a = glu(ua_ref, c)
        up(ua_ref, c + 2)
        act_b = glu(ub_ref, c + 1)
        acc_ref[...] += _dot(act_a, wo_ref[cols(c), :]) + _dot(act_b, wo_ref[cols(c + 1), :])
    y = _layer_norm(acc_ref[...] + _dot(glu(ua_ref, nc - 1), wo_ref[cols(nc - 1), :]), g_ref[...], b_ref[...])
    hf_ref[...] = y
    hb_ref[...] = y.astype(hb_ref.dtype)


def _ffn(hb, hf, w_in, conv_w, conv_b, w_out, g, b, lp, tm):
    n = hb.shape[0]
    nc, fc = FFN_NC, FFN_CHUNK
    tpb = lp // tm
    halo_blocks = tm // FFN_HALO
    row = lambda i: (i, 0)
    fixed2 = lambda i: (0, 0)
    resident = dict(pipeline_mode=pl.Buffered(1))
    return pl.pallas_call(
        functools.partial(_ffn_kernel, tm=tm, tiles_per_batch=tpb),
        grid=(n // tm,),
        in_specs=[pl.BlockSpec((tm, D_MODEL), row),
                  pl.BlockSpec((FFN_HALO, D_MODEL), lambda i: (jnp.maximum(i * halo_blocks - 1, 0), 0)),
                  pl.BlockSpec((tm, D_MODEL), row),
                  pl.BlockSpec((D_MODEL, 2 * D_FF), fixed2, **resident),
                  pl.BlockSpec((CONV_W, 2 * D_FF), fixed2, **resident),
                  pl.BlockSpec((1, 2 * D_FF), fixed2, **resident),
                  pl.BlockSpec((D_FF, D_MODEL), fixed2, **resident),
                  pl.BlockSpec((1, D_MODEL), fixed2), pl.BlockSpec((1, D_MODEL), fixed2)],
        out_specs=[pl.BlockSpec((tm, D_MODEL), row), pl.BlockSpec((tm, D_MODEL), row)],
        out_shape=[jax.ShapeDtypeStruct((n, D_MODEL), F32), jax.ShapeDtypeStruct((n, D_MODEL), BF16)],
        scratch_shapes=[pltpu.VMEM((tm + FFN_HALO, D_MODEL), BF16),
                        pltpu.VMEM((2, tm + FFN_HALO, fc), F32), pltpu.VMEM((2, tm + FFN_HALO, fc), F32),
                        pltpu.VMEM((tm, D_MODEL), F32)],
        compiler_params=_params("parallel"),
        name="ffn",
    )(hb, hb, hf, w_in, conv_w, conv_b, w_out, g, b)


def _ffn_weights(w_in, conv_w, conv_b, w_out):
    return w_in.astype(BF16), conv_w, conv_b[None, :], w_out.astype(BF16)


def _rope_tables(lp, dim, theta, group, offset):
    pos = (jnp.arange(lp) - LEAD).astype(F32)
    inv = theta ** (-jnp.arange(0, dim, 2, dtype=F32) / dim)
    ang = pos[:, None] * inv[None, :]
    cos, sin = jnp.cos(ang), jnp.sin(ang)
    ones = lambda w: jnp.ones((lp, w), F32)
    zeros = lambda w: jnp.zeros((lp, w), F32)
    rest = group - offset - dim
    cos_g = jnp.concatenate([ones(offset), cos, cos, ones(rest)], axis=1)
    sin_g = jnp.concatenate([zeros(offset), -sin, sin, zeros(rest)], axis=1)
    reps = LANES // group
    return jnp.tile(cos_g, (1, reps)), jnp.tile(sin_g, (1, reps))


def _swap_halves(w, dim):
    return jnp.concatenate([w[..., dim // 2:dim], w[..., :dim // 2]], axis=-1)


def _head_blocks(main, extra):
    src = main if main is not None else extra
    rows, heads = src.shape[0], src.shape[1]
    m = main if main is not None else jnp.zeros((rows, heads, HEAD_DIM), F32)
    e = extra if extra is not None else jnp.zeros((rows, heads, 0), F32)
    pad = jnp.zeros((rows, heads, LANES - HEAD_DIM - e.shape[2]), F32)
    return jnp.concatenate([m, e, pad], axis=2).reshape(rows, heads * LANES)


def _mla_weights(w_a, w_uq, w_ukv):
    d = w_a.shape[0]
    w_kr = w_a[:, MLA_Q_LORA + MLA_KV_LORA:][:, None, :]
    w_a_cat = jnp.concatenate([w_a[:, :MLA_Q_LORA + MLA_KV_LORA], _head_blocks(None, w_kr),
                               _head_blocks(None, _swap_halves(w_kr, MLA_ROPE))], axis=1)
    scale = (MLA_NOPE + MLA_ROPE) ** -0.5 * LOG2E
    wq = (w_uq * scale).reshape(MLA_Q_LORA, HEADS, MLA_NOPE + MLA_ROPE)
    w_main = _head_blocks(wq[..., :MLA_NOPE], wq[..., MLA_NOPE:])
    w_swap = _head_blocks(None, _swap_halves(wq[..., MLA_NOPE:], MLA_ROPE))
    wkv = w_ukv.reshape(MLA_KV_LORA, HEADS, MLA_NOPE + HEAD_DIM)
    w_kn = wkv[..., :MLA_NOPE].reshape(MLA_KV_LORA, HEADS * MLA_NOPE)
    w_v = wkv[..., MLA_NOPE:].reshape(MLA_KV_LORA, HEADS * HEAD_DIM)
    return tuple(w.astype(BF16) for w in (w_a_cat, w_main, w_swap, w_kn)) + (w_v,)


def _swa_weights(w_in):
    qd = HEADS * HEAD_DIM
    kd = SWA_KV_HEADS * HEAD_DIM
    q = w_in[:, :qd] * (HEAD_DIM ** -0.5)
    dup = lambda w: jnp.concatenate([w[:, :HEAD_DIM], w[:, :HEAD_DIM], w[:, HEAD_DIM:], w[:, HEAD_DIM:]], axis=1)
    return jnp.concatenate([q, dup(w_in[:, qd:qd + kd]), dup(w_in[:, qd + kd:])], axis=1).astype(BF16)


def _fox_weights(w_in, b_f):
    hd = HEADS * HEAD_DIM
    d = w_in.shape[0]
    w_q = (w_in[:, :hd] * (HEAD_DIM ** -0.5 * LOG2E)).astype(BF16)
    w_k = w_in[:, hd:2 * hd].astype(BF16)
    w_v = w_in[:, 2 * hd:3 * hd]
    w_gate = w_in[:, 3 * hd:]
    w_fg = jnp.concatenate([w_gate] * GATE_SLOTS + [jnp.zeros((d, LANES - GATE_SLOTS * HEADS), F32)],
                           axis=1).astype(BF16)
    b_fg = jnp.concatenate([b_f] * GATE_SLOTS + [jnp.zeros((LANES - GATE_SLOTS * HEADS,), F32)])[None, :]
    return w_q, w_k, w_v, w_fg, b_fg


def kernel(x, meta_tokens, ln1_g, ln1_b, ln2_g, ln2_b, fox_w_in, fox_b_f, fox_w_o, swa_w_in, swa_sinks, swa_w_o,
           mla_w_a, mla_g_q, mla_g_kv, mla_w_uq, mla_w_ukv, mla_w_o, ffn_w_in, ffn_conv_w, ffn_conv_b, ffn_w_out):
    batch, seq, d = x.shape
    assert d == D_MODEL and seq % 256 == 0
    lp = seq + FIRST_REAL
    n = batch * lp

    tm = _tile(lp, 768, 256)
    tq = _tile(lp, 768, 256)
    tk = 256
    tn = HEADS * HEAD_DIM
    nk = lp // tk

    hf, hb = _embed(x, meta_tokens.astype(x.dtype))

    cos_p, sin_p = _rope_tables(lp, ROPE_DIM, ROPE_THETA, HEAD_DIM, 0)
    cos_m, sin_m = _rope_tables(lp, MLA_ROPE, MLA_ROPE_THETA, LANES, MLA_NOPE)
    b3 = lambda a: a.reshape(batch, lp, -1)

    for i in range(DEPTH):
        kind, j = i % 3, i // 3
        if kind == 0:
            w_q, w_k, w_v, w_fg, b_fg = _fox_weights(fox_w_in[j], fox_b_f[j])
            aq, ak = _fox_gate(hb, w_fg, b_fg, lp, tm)
            qh = _head_proj(hb, w_q, aq, tm, tn)
            kh = _head_proj(hb, w_k, ak, tm, tn)
            vt = _matmul_t(hb, *_value_weights(w_v), tm, V_COLS_TILE, tk).reshape(batch, nk, HEADS * V_ROWS, tk)
            o = _flash_attention(b3(qh), b3(kh), vt, batch, lp, tq, tk)
            w_o = fox_w_o[j]
        elif kind == 1:
            qkv = _swa_proj(hb, _swa_weights(swa_w_in[j]), cos_p, sin_p, lp, tm, 2 * LANES, 5)
            o = _swa_attention(b3(qkv), swa_sinks[j].astype(F32), batch, lp)
            w_o = swa_w_o[j]
        else:
            w_a_cat, w_main, w_swap, w_kn, w_v = _mla_weights(mla_w_a[j], mla_w_uq[j], mla_w_ukv[j])
            cq, ckv, kr = _mla_a(hb, w_a_cat, mla_g_q[j][None, :], mla_g_kv[j][None, :], cos_m, sin_m, lp, tm)
            qh = _mla_q(cq, w_main, w_swap, cos_m, sin_m, lp, tm, tn)
            kh = _head_proj(ckv, w_kn, kr, tm, tn)
            vt = _matmul_t(ckv, *_value_weights(w_v), tm, V_COLS_TILE, tk).reshape(batch, nk, HEADS * V_ROWS, tk)
            o = _flash_attention(b3(qh), b3(kh), vt, batch, lp, tq, tk)
            w_o = mla_w_o[j]
        hf, hb = _oproj_ln(o.reshape(n, d), w_o.astype(BF16), hf, ln1_g[i][None, :], ln1_b[i][None, :], tm)
        hf, hb = _ffn(hb, hf, *_ffn_weights(ffn_w_in[i], ffn_conv_w[i], ffn_conv_b[i], ffn_w_out[i]),
                      ln2_g[i][None, :], ln2_b[i][None, :], lp, tm)
    return hf.reshape(batch, lp, d)[:, FIRST_REAL:]
```

```python
import functools
import math

import numpy as np
import jax
import jax.numpy as jnp
from jax import lax
from jax.experimental import pallas as pl
from jax.experimental.pallas import tpu as pltpu

F32 = jnp.float32
BF16 = jnp.bfloat16

D_MODEL = 1024
DEPTH = 4
N_META = 16
LEAD = 240
FIRST_REAL = LEAD + N_META
NEG = -1e30
DEEPNORM_ALPHA = (2.0 * DEPTH) ** 0.25
LN_EPS = 1e-5
RMS_EPS = 1e-6
HEADS = 16
HEAD_DIM = 64
PAIRS = HEADS // 2
SWA_KV_HEADS = 2
WINDOW = 128
ROPE_THETA = 500000.0
ROPE_DIM = 16
MLA_Q_LORA = 384
MLA_KV_LORA = 256
MLA_NOPE = 64
MLA_ROPE = 32
MLA_ROPE_THETA = 10000.0
D_FF = 2816
CONV_W = 3
LOG2E = math.log2(math.e)

LANES = 128
BF16_SUBLANES = 16
VMEM_LIMIT = 56 * 1024 * 1024

GATE_SLOTS = 3
FOX_DEAD_LANE = HEAD_DIM + 2 * GATE_SLOTS
MLA_DEAD_LANE = MLA_NOPE + MLA_ROPE
M_INIT = -3e38


def _params(*sem):
    return pltpu.CompilerParams(dimension_semantics=sem, vmem_limit_bytes=VMEM_LIMIT)


def _tile(n, pref, mult):
    best = mult
    t = mult
    while t <= min(n, pref):
        if n % t == 0:
            best = t
        t += mult
    assert n % best == 0
    return best


def _dot(a, b):
    return jnp.dot(a, b, preferred_element_type=F32)


def _dot_nt(a, b):
    return lax.dot_general(a, b, (((1,), (1,)), ((), ())), preferred_element_type=F32)


def _layer_norm(x, g, b):
    mu = jnp.mean(x, axis=-1, keepdims=True)
    xc = x - mu
    var = jnp.mean(xc * xc, axis=-1, keepdims=True)
    return xc * lax.rsqrt(var + LN_EPS) * g + b


def _one_hot_lanes(width, period, lane):
    idx = lax.broadcasted_iota(jnp.int32, (1, width), 1)
    return jnp.where((idx & (period - 1)) == lane, 1.0, 0.0)


def _dead_rows(tile_in_batch, tm):
    pos = tile_in_batch * tm + lax.broadcasted_iota(jnp.int32, (tm, 1), 0)
    return jnp.where(pos < LEAD, NEG, 0.0)


def _embed_kernel(x_ref, meta_ref, hf_ref, hb_ref):
    t = pl.program_id(1)

    @pl.when(t == 0)
    def _():
        lead = jnp.concatenate([jnp.zeros((LEAD, D_MODEL), F32), meta_ref[...]], axis=0)
        hf_ref[...] = lead
        hb_ref[...] = lead.astype(hb_ref.dtype)

    @pl.when(t > 0)
    def _():
        hf_ref[...] = x_ref[0]
        hb_ref[...] = x_ref[0].astype(hb_ref.dtype)


def _embed(x, meta):
    batch, seq, d = x.shape
    blocks = seq // FIRST_REAL + 1
    return pl.pallas_call(
        _embed_kernel,
        grid=(batch, blocks),
        in_specs=[pl.BlockSpec((1, FIRST_REAL, d), lambda b, t: (b, jnp.maximum(t - 1, 0), 0)),
                  pl.BlockSpec((N_META, d), lambda b, t: (0, 0))],
        out_specs=[pl.BlockSpec((FIRST_REAL, d), lambda b, t: (b * blocks + t, 0)),
                   pl.BlockSpec((FIRST_REAL, d), lambda b, t: (b * blocks + t, 0))],
        out_shape=[jax.ShapeDtypeStruct((batch * blocks * FIRST_REAL, d), F32),
                   jax.ShapeDtypeStruct((batch * blocks * FIRST_REAL, d), BF16)],
        compiler_params=_params("parallel", "parallel"),
        name="embed",
    )(x, meta)


def _head_proj_kernel(x_ref, w_ref, e_ref, o_ref):
    y = _dot(x_ref[...], w_ref[...])
    lo = lax.broadcasted_iota(jnp.int32, (1, LANES), 1) < HEAD_DIM
    shared = e_ref.shape[1] == LANES
    for pair in range(y.shape[1] // LANES):
        y_pair = y[:, pair * LANES:(pair + 1) * LANES]
        for a, feats in enumerate((y_pair, pltpu.roll(y_pair, HEAD_DIM, 1))):
            h = 2 * pair + a
            extra = e_ref[...] if shared else e_ref[:, h * LANES:(h + 1) * LANES]
            o_ref[:, h * LANES:(h + 1) * LANES] = jnp.where(lo, feats, extra.astype(F32)).astype(o_ref.dtype)


def _head_proj(x, w, e, tm, tn):
    n, k = x.shape
    m = w.shape[1]
    if e.shape[1] == LANES:
        e_spec = pl.BlockSpec((tm, LANES), lambda i, j: (i, 0))
    else:
        assert e.shape[1] == 2 * m
        e_spec = pl.BlockSpec((tm, 2 * tn), lambda i, j: (i, j))
    return pl.pallas_call(
        _head_proj_kernel,
        grid=(n // tm, m // tn),
        in_specs=[pl.BlockSpec((tm, k), lambda i, j: (i, 0)),
                  pl.BlockSpec((k, tn), lambda i, j: (0, j)),
                  e_spec],
        out_specs=pl.BlockSpec((tm, 2 * tn), lambda i, j: (i, j)),
        out_shape=jax.ShapeDtypeStruct((n, 2 * m), BF16),
        compiler_params=_params("parallel", "parallel"),
        name="head_proj",
    )(x, w, e)


def _mm_t_kernel(x_ref, w_ref, b_ref, o_ref, *, tk):
    y = _dot(x_ref[...], w_ref[...]) + b_ref[...]
    for c in range(o_ref.shape[0]):
        o_ref[c] = y[c * tk:(c + 1) * tk, :].T.astype(o_ref.dtype)


def _matmul_t(x, w, bias, tm, tn, tk):
    n, k = x.shape
    m = w.shape[1]
    r = tm // tk
    return pl.pallas_call(
        functools.partial(_mm_t_kernel, tk=tk),
        grid=(n // tm, m // tn),
        in_specs=[pl.BlockSpec((tm, k), lambda i, j: (i, 0)),
                  pl.BlockSpec((k, tn), lambda i, j: (0, j)),
                  pl.BlockSpec((1, tn), lambda i, j: (0, j))],
        out_specs=pl.BlockSpec((r, tn, tk), lambda i, j: (i, j, 0)),
        out_shape=jax.ShapeDtypeStruct((n // tk, m, tk), BF16),
        compiler_params=_params("parallel", "parallel"),
        name="matmul_t",
    )(x, w, bias)


V_ROWS = HEAD_DIM + BF16_SUBLANES
V_COLS_TILE = HEADS * V_ROWS


def _value_weights(w_v):
    rows = w_v.shape[0]
    w = jnp.concatenate([w_v.reshape(rows, HEADS, HEAD_DIM),
                         jnp.zeros((rows, HEADS, V_ROWS - HEAD_DIM), w_v.dtype)], axis=2)
    bias = np.zeros((HEADS, V_ROWS), np.float32)
    bias[:, HEAD_DIM] = 1.0
    return w.reshape(rows, HEADS * V_ROWS).astype(BF16), jnp.asarray(bias.reshape(1, HEADS * V_ROWS))


def _gate_placement():
    wide = HEADS * LANES
    sq = np.zeros((LANES, wide), np.float32)
    sk = np.zeros_like(sq)
    oq = np.zeros((1, wide), np.float32)
    ok = np.zeros_like(oq)
    for h in range(HEADS):
        base = h * LANES + HEAD_DIM
        for part in range(GATE_SLOTS):
            sq[part * HEADS + h, base + part] = 1.0
            sk[part * HEADS + h, base + GATE_SLOTS + part] = -1.0
            oq[0, base + GATE_SLOTS + part] = 1.0
            ok[0, base + part] = 1.0
        oq[0, h * LANES + FOX_DEAD_LANE] = 1.0
    return sq, sk, oq, ok


def _split3(x):
    hi = x.astype(BF16)
    r1 = x - hi.astype(F32)
    mid = r1.astype(BF16)
    lo = (r1 - mid.astype(F32)).astype(BF16)
    return hi, mid, lo


def _fox_gate_kernel(x_ref, w_ref, b_ref, sq_ref, sk_ref, oq_ref, ok_ref, aq_ref, ak_ref, carry_ref,
                     *, tm, tiles_per_batch):
    i = pl.program_id(0)

    @pl.when(i % tiles_per_batch == 0)
    def _():
        carry_ref[...] = jnp.zeros_like(carry_ref)

    fg = _dot(x_ref[...], w_ref[...]) + b_ref[...]
    logf = jnp.minimum(fg, 0.0) - jnp.log(1.0 + jnp.exp(-jnp.abs(fg)))
    sub = FIRST_REAL
    row = lax.broadcasted_iota(jnp.int32, (sub, sub), 0)
    col = lax.broadcasted_iota(jnp.int32, (sub, sub), 1)
    tri = jnp.where(col <= row, 1.0, 0.0).astype(BF16)
    hi, mid, lo = _split3(logf)
    carry = carry_ref[...]
    pieces = []
    for r0 in range(0, tm, sub):
        rows = slice(r0, r0 + sub)
        piece = _dot(tri, hi[rows]) + _dot(tri, mid[rows]) + _dot(tri, lo[rows]) + carry
        carry = piece[sub - 1:sub, :]
        pieces.append(piece)
    cs = jnp.concatenate(pieces, axis=0)
    carry_ref[...] = carry
    lane = lax.broadcasted_iota(jnp.int32, (1, LANES), 1)
    hi, mid, lo = _split3(cs * LOG2E)
    parts = jnp.where(lane < HEADS, hi.astype(F32),
                      jnp.where(lane < 2 * HEADS, mid.astype(F32), lo.astype(F32))).astype(BF16)
    wide = aq_ref.shape[1]
    dead = _dead_rows(i % tiles_per_batch, tm) * _one_hot_lanes(wide, LANES, FOX_DEAD_LANE)
    aq_ref[...] = (_dot(parts, sq_ref[...]) + oq_ref[...]).astype(aq_ref.dtype)
    ak_ref[...] = (_dot(parts, sk_ref[...]) + ok_ref[...] + dead).astype(ak_ref.dtype)


def _fox_gate(hb, w_fg, b_fg, lp, tm):
    n = hb.shape[0]
    tpb = lp // tm
    sq, sk, oq, ok = _gate_placement()
    fixed = lambda i: (0, 0)
    wide = HEADS * LANES
    return pl.pallas_call(
        functools.partial(_fox_gate_kernel, tm=tm, tiles_per_batch=tpb),
        grid=(n // tm,),
        in_specs=[pl.BlockSpec((tm, D_MODEL), lambda i: (i, 0)),
                  pl.BlockSpec((D_MODEL, LANES), fixed),
                  pl.BlockSpec((1, LANES), fixed),
                  pl.BlockSpec((LANES, wide), fixed),
                  pl.BlockSpec((LANES, wide), fixed),
                  pl.BlockSpec((1, wide), fixed),
                  pl.BlockSpec((1, wide), fixed)],
        out_specs=[pl.BlockSpec((tm, wide), lambda i: (i, 0)),
                   pl.BlockSpec((tm, wide), lambda i: (i, 0))],
        out_shape=[jax.ShapeDtypeStruct((n, wide), BF16), jax.ShapeDtypeStruct((n, wide), BF16)],
        scratch_shapes=[pltpu.VMEM((1, LANES), F32)],
        compiler_params=_params("arbitrary"),
        name="fox_gate",
    )(hb, w_fg, b_fg, jnp.asarray(sq, BF16), jnp.asarray(sk, BF16), jnp.asarray(oq), jnp.asarray(ok))


def _flash_kernel(q_ref, k_ref, vt_ref, o_ref, sa_ref, sb_ref, xa_ref, xb_ref, m_ref, acc_ref, *, tq, tk):
    i = pl.program_id(2)
    r = tq // tk
    m_ref[...] = jnp.full_like(m_ref, M_INIT)
    acc_ref[...] = jnp.zeros_like(acc_ref)

    def diag_mask(s):
        keep = lax.broadcasted_iota(jnp.int32, s.shape, 0) <= lax.broadcasted_iota(jnp.int32, s.shape, 1)
        return jnp.where(keep, s, NEG)

    def scores(s_ref, x_ref, j, a, qs, diagonal, blk=None):
        off = pl.multiple_of(j * tk, tk)
        q0 = pl.multiple_of((i if blk is None else blk) * tq + qs, tk)
        s = _dot_nt(k_ref[0, pl.ds(off, tk), a * LANES:(a + 1) * LANES],
                    q_ref[0, pl.ds(q0, tq - qs), a * LANES:(a + 1) * LANES])
        if diagonal:
            s = diag_mask(s)
        s_ref[a, :, qs:] = s
        x_ref[a, :, qs:] = jnp.max(s, axis=0, keepdims=True)

    def consume(s_ref, x_ref, j, a, qs, mask_now):
        s = s_ref[a, :, qs:]
        if mask_now:
            s = diag_mask(s)
            smax = jnp.max(s, axis=0, keepdims=True)
        else:
            smax = x_ref[a, :, qs:]
        m_old = m_ref[a, :, qs:]
        m_new = jnp.maximum(m_old, smax)
        alpha = jnp.exp2(m_old - m_new)
        p = jnp.exp2(s - m_new)
        m_ref[a, :, qs:] = m_new
        pv = _dot(vt_ref[0, j, a * V_ROWS:(a + 1) * V_ROWS, :], p.astype(BF16))
        acc_ref[a, :, qs:] = alpha * acc_ref[a, :, qs:] + pv

    buf_a, buf_b = (sa_ref, xa_ref), (sb_ref, xb_ref)
    jdiag = i * r
    odd = jdiag & 1

    @pl.when(i == 0)
    def _():
        for a in range(2):
            scores(*buf_a, 0, a, 0, False)

    @pl.when(odd == 1)
    def _():
        for a in range(2):
            scores(*buf_b, 1, a, 0, False)
            consume(*buf_a, 0, a, 0, False)
        sa_ref[...] = sb_ref[...]
        xa_ref[...] = xb_ref[...]

    def pair(j):
        for a in range(2):
            scores(*buf_b, j + 1, a, 0, False)
            consume(*buf_a, j, a, 0, False)
        for a in range(2):
            scores(*buf_a, j + 2, a, 0, False)
            consume(*buf_b, j + 1, a, 0, False)

    pairs = (jdiag - odd) // 2
    one = pairs & 1
    two = (pairs >> 1) & 1

    @pl.when(one == 1)
    def _():
        pair(odd)

    @pl.when(two == 1)
    def _():
        j = odd + 2 * one
        pair(j)
        pair(j + 2)

    def octo_body(t, c):
        j = odd + 2 * one + 4 * two + 8 * t
        for u in range(4):
            pair(j + 2 * u)
        return c

    lax.fori_loop(0, pairs >> 2, octo_body, 0)

    bufs = (buf_a, buf_b)
    for d in range(r):
        qs = d * tk
        for a in range(2):
            if d + 1 < r:
                scores(*bufs[(d + 1) & 1], jdiag + d + 1, a, qs + tk, True)
            consume(*bufs[d & 1], jdiag + d, a, qs, d == 0)

    for a in range(2):
        scores(*buf_a, 0, a, 0, False, blk=jnp.minimum(i + 1, pl.num_programs(2) - 1))

    ot = jnp.concatenate([acc_ref[a, :HEAD_DIM, :] * (1.0 / acc_ref[a, HEAD_DIM:HEAD_DIM + 1, :]) for a in range(2)],
                         axis=0)
    o_ref[0] = ot.T.astype(o_ref.dtype)


def _flash_attention(qh, kh, vt, batch, lp, tq, tk):
    nk = lp // tk
    return pl.pallas_call(
        functools.partial(_flash_kernel, tq=tq, tk=tk),
        grid=(batch, PAIRS, lp // tq),
        in_specs=[pl.BlockSpec((1, lp, 2 * LANES), lambda b, p, i: (b, 0, p)),
                  pl.BlockSpec((1, lp, 2 * LANES), lambda b, p, i: (b, 0, p)),
                  pl.BlockSpec((1, nk, 2 * V_ROWS, tk), lambda b, p, i: (b, 0, p, 0))],
        out_specs=pl.BlockSpec((1, tq, 2 * HEAD_DIM), lambda b, p, i: (b, i, p)),
        out_shape=jax.ShapeDtypeStruct((batch, lp, HEADS * HEAD_DIM), BF16),
        scratch_shapes=[pltpu.VMEM((2, tk, tq), F32), pltpu.VMEM((2, tk, tq), F32),
                        pltpu.VMEM((2, 1, tq), F32), pltpu.VMEM((2, 1, tq), F32),
                        pltpu.VMEM((2, 1, tq), F32), pltpu.VMEM((2, V_ROWS, tq), F32)],
        compiler_params=_params("parallel", "parallel", "arbitrary"),
        name="flash_attention",
    )(qh, kh, vt)


SWA_TQ = 128


def _swa_attn_kernel(sink_ref, q_ref, km_ref, kp_ref, kc_ref, vm_ref, vp_ref, vc_ref, o_ref):
    i = pl.program_id(1)
    t = SWA_TQ
    lo = lax.broadcasted_iota(jnp.int32, (1, LANES), 1) < HEAD_DIM
    row = lax.broadcasted_iota(jnp.int32, (t, 3 * t), 0)
    col = lax.broadcasted_iota(jnp.int32, (t, 3 * t), 1)
    qpos = i * t + row
    kpos = jnp.where(col < t, t + col, (i - 2) * t + col)
    d = qpos - kpos
    valid = (d >= 0) & (((col < t) & (kpos >= LEAD)) |
                        ((col >= t) & (d < WINDOW) & (kpos >= FIRST_REAL)))
    kcat, v_lo, v_hi = [], [], []
    for g in range(SWA_KV_HEADS):
        sl = slice(g * LANES, (g + 1) * LANES)
        kcat.append(jnp.concatenate([km_ref[0, :, sl], kp_ref[0, :, sl], kc_ref[0, :, sl]], axis=0))
        vf = jnp.concatenate([vm_ref[0, :, sl], vp_ref[0, :, sl], vc_ref[0, :, sl]], axis=0).astype(F32)
        v_lo.append(jnp.where(lo, vf, 0.0).astype(BF16))
        v_hi.append(jnp.where(lo, 0.0, vf).astype(BF16))
    pairs_per_group = PAIRS // SWA_KV_HEADS

    def logits(p):
        qf = q_ref[0, :, p * LANES:(p + 1) * LANES].astype(F32)
        q_pair = (jnp.where(lo, qf, 0.0).astype(BF16), jnp.where(lo, 0.0, qf).astype(BF16))
        return [_dot_nt(q_pair[a], kcat[p // pairs_per_group]) for a in range(2)]

    def finish(p, s_pair):
        g = p // pairs_per_group
        ps, inv = [], []
        for a in range(2):
            sink = sink_ref[2 * p + a]
            s = jnp.where(valid, s_pair[a], NEG)
            m = jnp.maximum(jnp.max(s, axis=1, keepdims=True), sink)
            e = jnp.exp(s - m)
            den = jnp.sum(e, axis=1, keepdims=True) + jnp.exp(sink - m)
            ps.append(e.astype(BF16))
            inv.append(1.0 / den)
        o = (_dot(ps[0], v_lo[g]) + _dot(ps[1], v_hi[g])) * jnp.where(lo, inv[0], inv[1])
        o_ref[0, :, p * LANES:(p + 1) * LANES] = o.astype(o_ref.dtype)

    s_next = logits(0)
    for p in range(PAIRS):
        s_cur = s_next
        if p + 1 < PAIRS:
            s_next = logits(p + 1)
        finish(p, s_cur)


def _swa_attention(qkv, sinks, batch, lp):
    t = SWA_TQ
    kblk, vblk = 4, 5
    kv_spec = lambda col, row_of: pl.BlockSpec((1, t, 2 * LANES), lambda b, i: (b, row_of(i), col))
    meta = lambda i: 1
    prev = lambda i: jnp.maximum(i - 1, 0)
    cur = lambda i: i
    return pl.pallas_call(
        _swa_attn_kernel,
        grid=(batch, lp // t),
        in_specs=[pl.BlockSpec(memory_space=pltpu.SMEM),
                  pl.BlockSpec((1, t, HEADS * HEAD_DIM), lambda b, i: (b, i, 0)),
                  kv_spec(kblk, meta), kv_spec(kblk, prev), kv_spec(kblk, cur),
                  kv_spec(vblk, meta), kv_spec(vblk, prev), kv_spec(vblk, cur)],
        out_specs=pl.BlockSpec((1, t, HEADS * HEAD_DIM), lambda b, i: (b, i, 0)),
        out_shape=jax.ShapeDtypeStruct((batch, lp, HEADS * HEAD_DIM), BF16),
        compiler_params=_params("parallel", "parallel"),
        name="swa_attention",
    )(sinks, qkv, qkv, qkv, qkv, qkv, qkv, qkv)


def _swa_proj_kernel(x_ref, w_ref, cos_ref, sin_ref, perm_ref, o_ref, *, tn, n_rope_blocks):
    j = pl.program_id(1)
    y = _dot(x_ref[...], w_ref[...])

    @pl.when(j < n_rope_blocks)
    def _():
        reps = tn // LANES
        cos = jnp.concatenate([cos_ref[...]] * reps, axis=1)
        sin = jnp.concatenate([sin_ref[...]] * reps, axis=1)
        partner = _dot(y.astype(BF16), perm_ref[...])
        o_ref[...] = (y * cos + partner * sin).astype(o_ref.dtype)

    @pl.when(j >= n_rope_blocks)
    def _():
        o_ref[...] = y.astype(o_ref.dtype)


def _swa_proj(hb, w, cos, sin, lp, tm, tn, n_rope_blocks):
    n = hb.shape[0]
    m = w.shape[1]
    tpb = lp // tm
    return pl.pallas_call(
        functools.partial(_swa_proj_kernel, tn=tn, n_rope_blocks=n_rope_blocks),
        grid=(n // tm, m // tn),
        in_specs=[pl.BlockSpec((tm, D_MODEL), lambda i, j: (i, 0)),
                  pl.BlockSpec((D_MODEL, tn), lambda i, j: (0, j)),
                  pl.BlockSpec((tm, LANES), lambda i, j: (i % tpb, 0)),
                  pl.BlockSpec((tm, LANES), lambda i, j: (i % tpb, 0)),
                  pl.BlockSpec((tn, tn), lambda i, j: (0, 0))],
        out_specs=pl.BlockSpec((tm, tn), lambda i, j: (i, j)),
        out_shape=jax.ShapeDtypeStruct((n, m), BF16),
        compiler_params=_params("parallel", "parallel"),
        name="swa_proj",
    )(hb, w, cos, sin, jnp.asarray(_rope_partner_matrix(tn), BF16))


def _rope_partner_matrix(width):
    p = np.zeros((width, width), np.float32)
    half = ROPE_DIM // 2
    for base in range(0, width, HEAD_DIM):
        for l in range(half):
            p[base + l + half, base + l] = 1.0
            p[base + l, base + l + half] = 1.0
    return p


MLA_A_COLS = MLA_Q_LORA + MLA_KV_LORA + 2 * LANES


def _mla_a_kernel(x_ref, w_ref, gq_ref, gkv_ref, cos_ref, sin_ref, cq_ref, ckv_ref, kr_ref, *, tm, tiles_per_batch):
    y = _dot(x_ref[...], w_ref[...])
    cq = y[:, :MLA_Q_LORA]
    ckv = y[:, MLA_Q_LORA:MLA_Q_LORA + MLA_KV_LORA]
    kr = y[:, MLA_Q_LORA + MLA_KV_LORA:MLA_Q_LORA + MLA_KV_LORA + LANES]
    krs = y[:, MLA_Q_LORA + MLA_KV_LORA + LANES:]
    rms = lambda z, g: z * lax.rsqrt(jnp.mean(z * z, axis=-1, keepdims=True) + RMS_EPS) * g
    cq_ref[...] = rms(cq, gq_ref[...]).astype(cq_ref.dtype)
    ckv_ref[...] = rms(ckv, gkv_ref[...]).astype(ckv_ref.dtype)
    dead = _dead_rows(pl.program_id(0) % tiles_per_batch, tm) * _one_hot_lanes(LANES, LANES, MLA_DEAD_LANE)
    kr_ref[...] = (kr * cos_ref[...] + krs * sin_ref[...] + dead).astype(kr_ref.dtype)


def _mla_a(hb, w, gq, gkv, cos, sin, lp, tm):
    n = hb.shape[0]
    tpb = lp // tm
    return pl.pallas_call(
        functools.partial(_mla_a_kernel, tm=tm, tiles_per_batch=tpb),
        grid=(n // tm,),
        in_specs=[pl.BlockSpec((tm, D_MODEL), lambda i: (i, 0)),
                  pl.BlockSpec((D_MODEL, MLA_A_COLS), lambda i: (0, 0)),
                  pl.BlockSpec((1, MLA_Q_LORA), lambda i: (0, 0)),
                  pl.BlockSpec((1, MLA_KV_LORA), lambda i: (0, 0)),
                  pl.BlockSpec((tm, LANES), lambda i: (i % tpb, 0)),
                  pl.BlockSpec((tm, LANES), lambda i: (i % tpb, 0))],
        out_specs=[pl.BlockSpec((tm, MLA_Q_LORA), lambda i: (i, 0)),
                   pl.BlockSpec((tm, MLA_KV_LORA), lambda i: (i, 0)),
                   pl.BlockSpec((tm, LANES), lambda i: (i, 0))],
        out_shape=[jax.ShapeDtypeStruct((n, MLA_Q_LORA), BF16),
                   jax.ShapeDtypeStruct((n, MLA_KV_LORA), BF16),
                   jax.ShapeDtypeStruct((n, LANES), BF16)],
        compiler_params=_params("parallel"),
        name="mla_a",
    )(hb, w, gq, gkv, cos, sin)


def _mla_q_kernel(x_ref, w_ref, ws_ref, cos_ref, sin_ref, o_ref):
    x = x_ref[...]
    reps = o_ref.shape[1] // LANES
    cos = jnp.concatenate([cos_ref[...]] * reps, axis=1)
    sin = jnp.concatenate([sin_ref[...]] * reps, axis=1)
    y = _dot(x, w_ref[...]) * cos + _dot(x, ws_ref[...]) * sin
    o_ref[...] = (y + _one_hot_lanes(o_ref.shape[1], LANES, MLA_DEAD_LANE)).astype(o_ref.dtype)


def _mla_q(cq, w, ws, cos, sin, lp, tm, tn):
    n = cq.shape[0]
    m = w.shape[1]
    tpb = lp // tm
    return pl.pallas_call(
        _mla_q_kernel,
        grid=(n // tm, m // tn),
        in_specs=[pl.BlockSpec((tm, MLA_Q_LORA), lambda i, j: (i, 0)),
                  pl.BlockSpec((MLA_Q_LORA, tn), lambda i, j: (0, j)),
                  pl.BlockSpec((MLA_Q_LORA, tn), lambda i, j: (0, j)),
                  pl.BlockSpec((tm, LANES), lambda i, j: (i % tpb, 0)),
                  pl.BlockSpec((tm, LANES), lambda i, j: (i % tpb, 0))],
        out_specs=pl.BlockSpec((tm, tn), lambda i, j: (i, j)),
        out_shape=jax.ShapeDtypeStruct((n, m), BF16),
        compiler_params=_params("parallel", "parallel"),
        name="mla_q",
    )(cq, w, ws, cos, sin)


def _oproj_ln_kernel(o_ref, w_ref, h_ref, g_ref, b_ref, hf_ref, hb_ref):
    x = DEEPNORM_ALPHA * h_ref[...] + _dot(o_ref[...], w_ref[...])
    y = _layer_norm(x, g_ref[...], b_ref[...])
    hf_ref[...] = y
    hb_ref[...] = y.astype(hb_ref.dtype)


def _oproj_ln(o, w, h, g, b, tm):
    n = o.shape[0]
    row = lambda i: (i, 0)
    fixed = lambda i: (0, 0)
    return pl.pallas_call(
        _oproj_ln_kernel,
        grid=(n // tm,),
        in_specs=[pl.BlockSpec((tm, D_MODEL), row), pl.BlockSpec((D_MODEL, D_MODEL), fixed),
                  pl.BlockSpec((tm, D_MODEL), row), pl.BlockSpec((1, D_MODEL), fixed),
                  pl.BlockSpec((1, D_MODEL), fixed)],
        out_specs=[pl.BlockSpec((tm, D_MODEL), row), pl.BlockSpec((tm, D_MODEL), row)],
        out_shape=[jax.ShapeDtypeStruct((n, D_MODEL), F32), jax.ShapeDtypeStruct((n, D_MODEL), BF16)],
        compiler_params=_params("parallel"),
        name="oproj_ln",
    )(o, w, h, g, b)


FFN_HALO = BF16_SUBLANES


FFN_CHUNK = 256
FFN_NC = D_FF // FFN_CHUNK
assert FFN_NC * FFN_CHUNK == D_FF and FFN_NC % 2 == 1


def _ffn_kernel(x_ref, halo_ref, h_ref, win_ref, cw_ref, cb_ref, wo_ref, g_ref, b_ref, hf_ref, hb_ref,
                xext_ref, ua_ref, ub_ref, acc_ref, *, tm, tiles_per_batch):
    i = pl.program_id(0)
    nc = FFN_NC
    pos = (i % tiles_per_batch) * tm - FFN_HALO + lax.broadcasted_iota(jnp.int32, (tm + FFN_HALO, 1), 0)
    xe = jnp.concatenate([halo_ref[...], x_ref[...]], axis=0).astype(F32)
    xext_ref[...] = jnp.where(pos >= LEAD, xe, 0.0).astype(BF16)

    cols = lambda idx: slice(idx * FFN_CHUNK, (idx + 1) * FFN_CHUNK)

    def up(u_ref, c):
        u_ref[0] = _dot(xext_ref[...], win_ref[:, cols(c)])
        u_ref[1] = _dot(xext_ref[...], win_ref[:, cols(nc + c)])

    def glu(u_ref, c):
        def conv(part, idx):
            u = u_ref[part]
            delayed = u * cw_ref[0:1, cols(idx)]
            for tap in range(1, CONV_W):
                delayed = u * cw_ref[tap:tap + 1, cols(idx)] + pltpu.roll(delayed, 1, 0)
            return cb_ref[:, cols(idx)] + delayed[FFN_HALO:, :]

        yg = conv(0, c)
        yv = conv(1, nc + c)
        return ((yg / (1.0 + jnp.exp(-yg))) * yv).astype(BF16)

    acc_ref[...] = DEEPNORM_ALPHA * h_ref[...]
    up(ua_ref, 0)

    for c in range(0, nc - 1, 2):
        up(ub_ref, c + 1)
        act_a = glu(ua_ref, c)
        up(ua_ref, c + 2)
        act_b = glu(ub_ref, c + 1)
        acc_ref[...] += _dot(act_a, wo_ref[cols(c), :]) + _dot(act_b, wo_ref[cols(c + 1), :])
    y = _layer_norm(acc_ref[...] + _dot(glu(ua_ref, nc - 1), wo_ref[cols(nc - 1), :]), g_ref[...], b_ref[...])
    hf_ref[...] = y
    hb_ref[...] = y.astype(hb_ref.dtype)


def _ffn(hb, hf, w_in, conv_w, conv_b, w_out, g, b, lp, tm):
    n = hb.shape[0]
    nc, fc = FFN_NC, FFN_CHUNK
    tpb = lp // tm
    halo_blocks = tm // FFN_HALO
    row = lambda i: (i, 0)
    fixed2 = lambda i: (0, 0)
    resident = dict(pipeline_mode=pl.Buffered(1))
    return pl.pallas_call(
        functools.partial(_ffn_kernel, tm=tm, tiles_per_batch=tpb),
        grid=(n // tm,),
        in_specs=[pl.BlockSpec((tm, D_MODEL), row),
                  pl.BlockSpec((FFN_HALO, D_MODEL), lambda i: (jnp.maximum(i * halo_blocks - 1, 0), 0)),
                  pl.BlockSpec((tm, D_MODEL), row),
                  pl.BlockSpec((D_MODEL, 2 * D_FF), fixed2, **resident),
                  pl.BlockSpec((CONV_W, 2 * D_FF), fixed2, **resident),
                  pl.BlockSpec((1, 2 * D_FF), fixed2, **resident),
                  pl.BlockSpec((D_FF, D_MODEL), fixed2, **resident),
                  pl.BlockSpec((1, D_MODEL), fixed2), pl.BlockSpec((1, D_MODEL), fixed2)],
        out_specs=[pl.BlockSpec((tm, D_MODEL), row), pl.BlockSpec((tm, D_MODEL), row)],
        out_shape=[jax.ShapeDtypeStruct((n, D_MODEL), F32), jax.ShapeDtypeStruct((n, D_MODEL), BF16)],
        scratch_shapes=[pltpu.VMEM((tm + FFN_HALO, D_MODEL), BF16),
                        pltpu.VMEM((2, tm + FFN_HALO, fc), F32), pltpu.VMEM((2, tm + FFN_HALO, fc), F32),
                        pltpu.VMEM((tm, D_MODEL), F32)],
        compiler_params=_params("parallel"),
        name="ffn",
    )(hb, hb, hf, w_in, conv_w, conv_b, w_out, g, b)


def _ffn_weights(w_in, conv_w, conv_b, w_out):
    return w_in.astype(BF16), conv_w, conv_b[None, :], w_out.astype(BF16)


def _rope_tables(lp, dim, theta, group, offset):
    pos = (jnp.arange(lp) - LEAD).astype(F32)
    inv = theta ** (-jnp.arange(0, dim, 2, dtype=F32) / dim)
    ang = pos[:, None] * inv[None, :]
    cos, sin = jnp.cos(ang), jnp.sin(ang)
    ones = lambda w: jnp.ones((lp, w), F32)
    zeros = lambda w: jnp.zeros((lp, w), F32)
    rest = group - offset - dim
    cos_g = jnp.concatenate([ones(offset), cos, cos, ones(rest)], axis=1)
    sin_g = jnp.concatenate([zeros(offset), -sin, sin, zeros(rest)], axis=1)
    reps = LANES // group
    return jnp.tile(cos_g, (1, reps)), jnp.tile(sin_g, (1, reps))


def _swap_halves(w, dim):
    return jnp.concatenate([w[..., dim // 2:dim], w[..., :dim // 2]], axis=-1)


def _head_blocks(main, extra):
    src = main if main is not None else extra
    rows, heads = src.shape[0], src.shape[1]
    m = main if main is not None else jnp.zeros((rows, heads, HEAD_DIM), F32)
    e = extra if extra is not None else jnp.zeros((rows, heads, 0), F32)
    pad = jnp.zeros((rows, heads, LANES - HEAD_DIM - e.shape[2]), F32)
    return jnp.concatenate([m, e, pad], axis=2).reshape(rows, heads * LANES)


def _mla_weights(w_a, w_uq, w_ukv):
    d = w_a.shape[0]
    w_kr = w_a[:, MLA_Q_LORA + MLA_KV_LORA:][:, None, :]
    w_a_cat = jnp.concatenate([w_a[:, :MLA_Q_LORA + MLA_KV_LORA], _head_blocks(None, w_kr),
                               _head_blocks(None, _swap_halves(w_kr, MLA_ROPE))], axis=1)
    scale = (MLA_NOPE + MLA_ROPE) ** -0.5 * LOG2E
    wq = (w_uq * scale).reshape(MLA_Q_LORA, HEADS, MLA_NOPE + MLA_ROPE)
    w_main = _head_blocks(wq[..., :MLA_NOPE], wq[..., MLA_NOPE:])
    w_swap = _head_blocks(None, _swap_halves(wq[..., MLA_NOPE:], MLA_ROPE))
    wkv = w_ukv.reshape(MLA_KV_LORA, HEADS, MLA_NOPE + HEAD_DIM)
    w_kn = wkv[..., :MLA_NOPE].reshape(MLA_KV_LORA, HEADS * MLA_NOPE)
    w_v = wkv[..., MLA_NOPE:].reshape(MLA_KV_LORA, HEADS * HEAD_DIM)
    return tuple(w.astype(BF16) for w in (w_a_cat, w_main, w_swap, w_kn)) + (w_v,)


def _swa_weights(w_in):
    qd = HEADS * HEAD_DIM
    kd = SWA_KV_HEADS * HEAD_DIM
    q = w_in[:, :qd] * (HEAD_DIM ** -0.5)
    dup = lambda w: jnp.concatenate([w[:, :HEAD_DIM], w[:, :HEAD_DIM], w[:, HEAD_DIM:], w[:, HEAD_DIM:]], axis=1)
    return jnp.concatenate([q, dup(w_in[:, qd:qd + kd]), dup(w_in[:, qd + kd:])], axis=1).astype(BF16)


def _fox_weights(w_in, b_f):
    hd = HEADS * HEAD_DIM
    d = w_in.shape[0]
    w_q = (w_in[:, :hd] * (HEAD_DIM ** -0.5 * LOG2E)).astype(BF16)
    w_k = w_in[:, hd:2 * hd].astype(BF16)
    w_v = w_in[:, 2 * hd:3 * hd]
    w_gate = w_in[:, 3 * hd:]
    w_fg = jnp.concatenate([w_gate] * GATE_SLOTS + [jnp.zeros((d, LANES - GATE_SLOTS * HEADS), F32)],
                           axis=1).astype(BF16)
    b_fg = jnp.concatenate([b_f] * GATE_SLOTS + [jnp.zeros((LANES - GATE_SLOTS * HEADS,), F32)])[None, :]
    return w_q, w_k, w_v, w_fg, b_fg


def kernel(x, meta_tokens, ln1_g, ln1_b, ln2_g, ln2_b, fox_w_in, fox_b_f, fox_w_o, swa_w_in, swa_sinks, swa_w_o,
           mla_w_a, mla_g_q, mla_g_kv, mla_w_uq, mla_w_ukv, mla_w_o, ffn_w_in, ffn_conv_w, ffn_conv_b, ffn_w_out):
    batch, seq, d = x.shape
    assert d == D_MODEL and seq % 256 == 0
    lp = seq + FIRST_REAL
    n = batch * lp

    tm = _tile(lp, 768, 256)
    tq = _tile(lp, 768, 256)
    tk = 256
    tn = HEADS * HEAD_DIM
    nk = lp // tk

    hf, hb = _embed(x, meta_tokens.astype(x.dtype))

    cos_p, sin_p = _rope_tables(lp, ROPE_DIM, ROPE_THETA, HEAD_DIM, 0)
    cos_m, sin_m = _rope_tables(lp, MLA_ROPE, MLA_ROPE_THETA, LANES, MLA_NOPE)
    b3 = lambda a: a.reshape(batch, lp, -1)

    for i in range(DEPTH):
        kind, j = i % 3, i // 3
        if kind == 0:
            w_q, w_k, w_v, w_fg, b_fg = _fox_weights(fox_w_in[j], fox_b_f[j])
            aq, ak = _fox_gate(hb, w_fg, b_fg, lp, tm)
            qh = _head_proj(hb, w_q, aq, tm, tn)
            kh = _head_proj(hb, w_k, ak, tm, tn)
            vt = _matmul_t(hb, *_value_weights(w_v), tm, V_COLS_TILE, tk).reshape(batch, nk, HEADS * V_ROWS, tk)
            o = _flash_attention(b3(qh), b3(kh), vt, batch, lp, tq, tk)
            w_o = fox_w_o[j]
        elif kind == 1:
            qkv = _swa_proj(hb, _swa_weights(swa_w_in[j]), cos_p, sin_p, lp, tm, 2 * LANES, 5)
            o = _swa_attention(b3(qkv), swa_sinks[j].astype(F32), batch, lp)
            w_o = swa_w_o[j]
        else:
            w_a_cat, w_main, w_swap, w_kn, w_v = _mla_weights(mla_w_a[j], mla_w_uq[j], mla_w_ukv[j])
            cq, ckv, kr = _mla_a(hb, w_a_cat, mla_g_q[j][None, :], mla_g_kv[j][None, :], cos_m, sin_m, lp, tm)
            qh = _mla_q(cq, w_main, w_swap, cos_m, sin_m, lp, tm, tn)
            kh = _head_proj(ckv, w_kn, kr, tm, tn)
            vt = _matmul_t(ckv, *_value_weights(w_v), tm, V_COLS_TILE, tk).reshape(batch, nk, HEADS * V_ROWS, tk)
            o = _flash_attention(b3(qh), b3(kh), vt, batch, lp, tq, tk)
            w_o = mla_w_o[j]
        hf, hb = _oproj_ln(o.reshape(n, d), w_o.astype(BF16), hf, ln1_g[i][None, :], ln1_b[i][None, :], tm)
        hf, hb = _ffn(hb, hf, *_ffn_weights(ffn_w_in[i], ffn_conv_w[i], ffn_conv_b[i], ffn_w_out[i]),
                      ln2_g[i][None, :], ln2_b[i][None, :], lp, tm)
    return hf.reshape(batch, lp, d)[:, FIRST_REAL:]
```

```python
import functools
import math

import numpy as np
import jax
import jax.numpy as jnp
from jax import lax
from jax.experimental import pallas as pl
from jax.experimental.pallas import tpu as pltpu

F32 = jnp.float32
BF16 = jnp.bfloat16

D_MODEL = 1024
DEPTH = 4
N_META = 16
LEAD = 240
FIRST_REAL = LEAD + N_META
NEG = -1e30
DEEPNORM_ALPHA = (2.0 * DEPTH) ** 0.25
LN_EPS = 1e-5
RMS_EPS = 1e-6
HEADS = 16
HEAD_DIM = 64
PAIRS = HEADS // 2
SWA_KV_HEADS = 2
WINDOW = 128
ROPE_THETA = 500000.0
ROPE_DIM = 16
MLA_Q_LORA = 384
MLA_KV_LORA = 256
MLA_NOPE = 64
MLA_ROPE = 32
MLA_ROPE_THETA = 10000.0
D_FF = 2816
CONV_W = 3
LOG2E = math.log2(math.e)

LANES = 128
BF16_SUBLANES = 16
VMEM_LIMIT = 56 * 1024 * 1024

GATE_SLOTS = 3
FOX_DEAD_LANE = HEAD_DIM + 2 * GATE_SLOTS
MLA_DEAD_LANE = MLA_NOPE + MLA_ROPE
M_INIT = -3e38


def _params(*sem):
    return pltpu.CompilerParams(dimension_semantics=sem, vmem_limit_bytes=VMEM_LIMIT)


def _tile(n, pref, mult):
    best = mult
    t = mult
    while t <= min(n, pref):
        if n % t == 0:
            best = t
        t += mult
    assert n % best == 0
    return best


def _dot(a, b):
    return jnp.dot(a, b, preferred_element_type=F32)


def _dot_nt(a, b):
    return lax.dot_general(a, b, (((1,), (1,)), ((), ())), preferred_element_type=F32)


def _layer_norm(x, g, b):
    mu = jnp.mean(x, axis=-1, keepdims=True)
    xc = x - mu
    var = jnp.mean(xc * xc, axis=-1, keepdims=True)
    return xc * lax.rsqrt(var + LN_EPS) * g + b


def _one_hot_lanes(width, period, lane):
    idx = lax.broadcasted_iota(jnp.int32, (1, width), 1)
    return jnp.where((idx & (period - 1)) == lane, 1.0, 0.0)


def _dead_rows(tile_in_batch, tm):
    pos = tile_in_batch * tm + lax.broadcasted_iota(jnp.int32, (tm, 1), 0)
    return jnp.where(pos < LEAD, NEG, 0.0)


def _embed_kernel(x_ref, meta_ref, hf_ref, hb_ref):
    t = pl.program_id(1)

    @pl.when(t == 0)
    def _():
        lead = jnp.concatenate([jnp.zeros((LEAD, D_MODEL), F32), meta_ref[...]], axis=0)
        hf_ref[...] = lead
        hb_ref[...] = lead.astype(hb_ref.dtype)

    @pl.when(t > 0)
    def _():
        hf_ref[...] = x_ref[0]
        hb_ref[...] = x_ref[0].astype(hb_ref.dtype)


def _embed(x, meta):
    batch, seq, d = x.shape
    blocks = seq // FIRST_REAL + 1
    return pl.pallas_call(
        _embed_kernel,
        grid=(batch, blocks),
        in_specs=[pl.BlockSpec((1, FIRST_REAL, d), lambda b, t: (b, jnp.maximum(t - 1, 0), 0)),
                  pl.BlockSpec((N_META, d), lambda b, t: (0, 0))],
        out_specs=[pl.BlockSpec((FIRST_REAL, d), lambda b, t: (b * blocks + t, 0)),
                   pl.BlockSpec((FIRST_REAL, d), lambda b, t: (b * blocks + t, 0))],
        out_shape=[jax.ShapeDtypeStruct((batch * blocks * FIRST_REAL, d), F32),
                   jax.ShapeDtypeStruct((batch * blocks * FIRST_REAL, d), BF16)],
        compiler_params=_params("parallel", "parallel"),
        name="embed",
    )(x, meta)


def _head_proj_kernel(x_ref, w_ref, e_ref, o_ref):
    y = _dot(x_ref[...], w_ref[...])
    lo = lax.broadcasted_iota(jnp.int32, (1, LANES), 1) < HEAD_DIM
    shared = e_ref.shape[1] == LANES
    for pair in range(y.shape[1] // LANES):
        y_pair = y[:, pair * LANES:(pair + 1) * LANES]
        for a, feats in enumerate((y_pair, pltpu.roll(y_pair, HEAD_DIM, 1))):
            h = 2 * pair + a
            extra = e_ref[...] if shared else e_ref[:, h * LANES:(h + 1) * LANES]
            o_ref[:, h * LANES:(h + 1) * LANES] = jnp.where(lo, feats, extra.astype(F32)).astype(o_ref.dtype)


def _head_proj(x, w, e, tm, tn):
    n, k = x.shape
    m = w.shape[1]
    if e.shape[1] == LANES:
        e_spec = pl.BlockSpec((tm, LANES), lambda i, j: (i, 0))
    else:
        assert e.shape[1] == 2 * m
        e_spec = pl.BlockSpec((tm, 2 * tn), lambda i, j: (i, j))
    return pl.pallas_call(
        _head_proj_kernel,
        grid=(n // tm, m // tn),
        in_specs=[pl.BlockSpec((tm, k), lambda i, j: (i, 0)),
                  pl.BlockSpec((k, tn), lambda i, j: (0, j)),
                  e_spec],
        out_specs=pl.BlockSpec((tm, 2 * tn), lambda i, j: (i, j)),
        out_shape=jax.ShapeDtypeStruct((n, 2 * m), BF16),
        compiler_params=_params("parallel", "parallel"),
        name="head_proj",
    )(x, w, e)


def _mm_t_kernel(x_ref, w_ref, b_ref, o_ref, *, tk):
    y = _dot(x_ref[...], w_ref[...]) + b_ref[...]
    for c in range(o_ref.shape[0]):
        o_ref[c] = y[c * tk:(c + 1) * tk, :].T.astype(o_ref.dtype)


def _matmul_t(x, w, bias, tm, tn, tk):
    n, k = x.shape
    m = w.shape[1]
    r = tm // tk
    return pl.pallas_call(
        functools.partial(_mm_t_kernel, tk=tk),
        grid=(n // tm, m // tn),
        in_specs=[pl.BlockSpec((tm, k), lambda i, j: (i, 0)),
                  pl.BlockSpec((k, tn), lambda i, j: (0, j)),
                  pl.BlockSpec((1, tn), lambda i, j: (0, j))],
        out_specs=pl.BlockSpec((r, tn, tk), lambda i, j: (i, j, 0)),
        out_shape=jax.ShapeDtypeStruct((n // tk, m, tk), BF16),
        compiler_params=_params("parallel", "parallel"),
        name="matmul_t",
    )(x, w, bias)


V_ROWS = HEAD_DIM + BF16_SUBLANES
V_COLS_TILE = HEADS * V_ROWS


def _value_weights(w_v):
    rows = w_v.shape[0]
    w = jnp.concatenate([w_v.reshape(rows, HEADS, HEAD_DIM),
                         jnp.zeros((rows, HEADS, V_ROWS - HEAD_DIM), w_v.dtype)], axis=2)
    bias = np.zeros((HEADS, V_ROWS), np.float32)
    bias[:, HEAD_DIM] = 1.0
    return w.reshape(rows, HEADS * V_ROWS).astype(BF16), jnp.asarray(bias.reshape(1, HEADS * V_ROWS))


def _gate_placement():
    wide = HEADS * LANES
    sq = np.zeros((LANES, wide), np.float32)
    sk = np.zeros_like(sq)
    oq = np.zeros((1, wide), np.float32)
    ok = np.zeros_like(oq)
    for h in range(HEADS):
        base = h * LANES + HEAD_DIM
        for part in range(GATE_SLOTS):
            sq[part * HEADS + h, base + part] = 1.0
            sk[part * HEADS + h, base + GATE_SLOTS + part] = -1.0
            oq[0, base + GATE_SLOTS + part] = 1.0
            ok[0, base + part] = 1.0
        oq[0, h * LANES + FOX_DEAD_LANE] = 1.0
    return sq, sk, oq, ok


def _split3(x):
    hi = x.astype(BF16)
    r1 = x - hi.astype(F32)
    mid = r1.astype(BF16)
    lo = (r1 - mid.astype(F32)).astype(BF16)
    return hi, mid, lo


def _fox_gate_kernel(x_ref, w_ref, b_ref, sq_ref, sk_ref, oq_ref, ok_ref, aq_ref, ak_ref, carry_ref,
                     *, tm, tiles_per_batch):
    i = pl.program_id(0)

    @pl.when(i % tiles_per_batch == 0)
    def _():
        carry_ref[...] = jnp.zeros_like(carry_ref)

    fg = _dot(x_ref[...], w_ref[...]) + b_ref[...]
    logf = jnp.minimum(fg, 0.0) - jnp.log(1.0 + jnp.exp(-jnp.abs(fg)))
    sub = FIRST_REAL
    row = lax.broadcasted_iota(jnp.int32, (sub, sub), 0)
    col = lax.broadcasted_iota(jnp.int32, (sub, sub), 1)
    tri = jnp.where(col <= row, 1.0, 0.0).astype(BF16)
    hi, mid, lo = _split3(logf)
    carry = carry_ref[...]
    pieces = []
    for r0 in range(0, tm, sub):
        rows = slice(r0, r0 + sub)
        piece = _dot(tri, hi[rows]) + _dot(tri, mid[rows]) + _dot(tri, lo[rows]) + carry
        carry = piece[sub - 1:sub, :]
        pieces.append(piece)
    cs = jnp.concatenate(pieces, axis=0)
    carry_ref[...] = carry
    lane = lax.broadcasted_iota(jnp.int32, (1, LANES), 1)
    hi, mid, lo = _split3(cs * LOG2E)
    parts = jnp.where(lane < HEADS, hi.astype(F32),
                      jnp.where(lane < 2 * HEADS, mid.astype(F32), lo.astype(F32))).astype(BF16)
    wide = aq_ref.shape[1]
    dead = _dead_rows(i % tiles_per_batch, tm) * _one_hot_lanes(wide, LANES, FOX_DEAD_LANE)
    aq_ref[...] = (_dot(parts, sq_ref[...]) + oq_ref[...]).astype(aq_ref.dtype)
    ak_ref[...] = (_dot(parts, sk_ref[...]) + ok_ref[...] + dead).astype(ak_ref.dtype)


def _fox_gate(hb, w_fg, b_fg, lp, tm):
    n = hb.shape[0]
    tpb = lp // tm
    sq, sk, oq, ok = _gate_placement()
    fixed = lambda i: (0, 0)
    wide = HEADS * LANES
    return pl.pallas_call(
        functools.partial(_fox_gate_kernel, tm=tm, tiles_per_batch=tpb),
        grid=(n // tm,),
        in_specs=[pl.BlockSpec((tm, D_MODEL), lambda i: (i, 0)),
                  pl.BlockSpec((D_MODEL, LANES), fixed),
                  pl.BlockSpec((1, LANES), fixed),
                  pl.BlockSpec((LANES, wide), fixed),
                  pl.BlockSpec((LANES, wide), fixed),
                  pl.BlockSpec((1, wide), fixed),
                  pl.BlockSpec((1, wide), fixed)],
        out_specs=[pl.BlockSpec((tm, wide), lambda i: (i, 0)),
                   pl.BlockSpec((tm, wide), lambda i: (i, 0))],
        out_shape=[jax.ShapeDtypeStruct((n, wide), BF16), jax.ShapeDtypeStruct((n, wide), BF16)],
        scratch_shapes=[pltpu.VMEM((1, LANES), F32)],
        compiler_params=_params("arbitrary"),
        name="fox_gate",
    )(hb, w_fg, b_fg, jnp.asarray(sq, BF16), jnp.asarray(sk, BF16), jnp.asarray(oq), jnp.asarray(ok))


def _flash_kernel(q_ref, k_ref, vt_ref, o_ref, sa_ref, sb_ref, xa_ref, xb_ref, m_ref, acc_ref, *, tq, tk):
    i = pl.program_id(2)
    r = tq // tk
    m_ref[...] = jnp.full_like(m_ref, M_INIT)
    acc_ref[...] = jnp.zeros_like(acc_ref)

    def diag_mask(s):
        keep = lax.broadcasted_iota(jnp.int32, s.shape, 0) <= lax.broadcasted_iota(jnp.int32, s.shape, 1)
        return jnp.where(keep, s, NEG)

    def scores(s_ref, x_ref, j, a, qs, diagonal, blk=None):
        off = pl.multiple_of(j * tk, tk)
        q0 = pl.multiple_of((i if blk is None else blk) * tq + qs, tk)
        s = _dot_nt(k_ref[0, pl.ds(off, tk), a * LANES:(a + 1) * LANES],
                    q_ref[0, pl.ds(q0, tq - qs), a * LANES:(a + 1) * LANES])
        if diagonal:
            s = diag_mask(s)
        s_ref[a, :, qs:] = s
        x_ref[a, :, qs:] = jnp.max(s, axis=0, keepdims=True)

    def consume(s_ref, x_ref, j, a, qs, mask_now):
        s = s_ref[a, :, qs:]
        if mask_now:
            s = diag_mask(s)
            smax = jnp.max(s, axis=0, keepdims=True)
        else:
            smax = x_ref[a, :, qs:]
        m_old = m_ref[a, :, qs:]
        m_new = jnp.maximum(m_old, smax)
        alpha = jnp.exp2(m_old - m_new)
        p = jnp.exp2(s - m_new)
        m_ref[a, :, qs:] = m_new
        pv = _dot(vt_ref[0, j, a * V_ROWS:(a + 1) * V_ROWS, :], p.astype(BF16))
        acc_ref[a, :, qs:] = alpha * acc_ref[a, :, qs:] + pv

    buf_a, buf_b = (sa_ref, xa_ref), (sb_ref, xb_ref)
    jdiag = i * r
    odd = jdiag & 1

    @pl.when(i == 0)
    def _():
        for a in range(2):
            scores(*buf_a, 0, a, 0, False)

    @pl.when(odd == 1)
    def _():
        for a in range(2):
            scores(*buf_b, 1, a, 0, False)
            consume(*buf_a, 0, a, 0, False)
        sa_ref[...] = sb_ref[...]
        xa_ref[...] = xb_ref[...]

    def pair(j):
        for a in range(2):
            scores(*buf_b, j + 1, a, 0, False)
            consume(*buf_a, j, a, 0, False)
        for a in range(2):
            scores(*buf_a, j + 2, a, 0, False)
            consume(*buf_b, j + 1, a, 0, False)

    pairs = (jdiag - odd) // 2
    one = pairs & 1
    two = (pairs >> 1) & 1

    @pl.when(one == 1)
    def _():
        pair(odd)

    @pl.when(two == 1)
    def _():
        j = odd + 2 * one
        pair(j)
        pair(j + 2)

    def octo_body(t, c):
        j = odd + 2 * one + 4 * two + 8 * t
        for u in range(4):
            pair(j + 2 * u)
        return c

    lax.fori_loop(0, pairs >> 2, octo_body, 0)

    bufs = (buf_a, buf_b)
    for d in range(r):
        qs = d * tk
        for a in range(2):
            if d + 1 < r:
                scores(*bufs[(d + 1) & 1], jdiag + d + 1, a, qs + tk, True)
            consume(*bufs[d & 1], jdiag + d, a, qs, d == 0)

    for a in range(2):
        scores(*buf_a, 0, a, 0, False, blk=jnp.minimum(i + 1, pl.num_programs(2) - 1))

    ot = jnp.concatenate([acc_ref[a, :HEAD_DIM, :] * (1.0 / acc_ref[a, HEAD_DIM:HEAD_DIM + 1, :]) for a in range(2)],
                         axis=0)
    o_ref[0] = ot.T.astype(o_ref.dtype)


def _flash_attention(qh, kh, vt, batch, lp, tq, tk):
    nk = lp // tk
    return pl.pallas_call(
        functools.partial(_flash_kernel, tq=tq, tk=tk),
        grid=(batch, PAIRS, lp // tq),
        in_specs=[pl.BlockSpec((1, lp, 2 * LANES), lambda b, p, i: (b, 0, p)),
                  pl.BlockSpec((1, lp, 2 * LANES), lambda b, p, i: (b, 0, p)),
                  pl.BlockSpec((1, nk, 2 * V_ROWS, tk), lambda b, p, i: (b, 0, p, 0))],
        out_specs=pl.BlockSpec((1, tq, 2 * HEAD_DIM), lambda b, p, i: (b, i, p)),
        out_shape=jax.ShapeDtypeStruct((batch, lp, HEADS * HEAD_DIM), BF16),
        scratch_shapes=[pltpu.VMEM((2, tk, tq), F32), pltpu.VMEM((2, tk, tq), F32),
                        pltpu.VMEM((2, 1, tq), F32), pltpu.VMEM((2, 1, tq), F32),
                        pltpu.VMEM((2, 1, tq), F32), pltpu.VMEM((2, V_ROWS, tq), F32)],
        compiler_params=_params("parallel", "parallel", "arbitrary"),
        name="flash_attention",
    )(qh, kh, vt)


SWA_TQ = 128


def _swa_attn_kernel(sink_ref, q_ref, km_ref, kp_ref, kc_ref, vm_ref, vp_ref, vc_ref, o_ref):
    i = pl.program_id(1)
    t = SWA_TQ
    lo = lax.broadcasted_iota(jnp.int32, (1, LANES), 1) < HEAD_DIM
    row = lax.broadcasted_iota(jnp.int32, (t, 3 * t), 0)
    col = lax.broadcasted_iota(jnp.int32, (t, 3 * t), 1)
    qpos = i * t + row
    kpos = jnp.where(col < t, t + col, (i - 2) * t + col)
    d = qpos - kpos
    valid = (d >= 0) & (((col < t) & (kpos >= LEAD)) |
                        ((col >= t) & (d < WINDOW) & (kpos >= FIRST_REAL)))
    kcat, v_lo, v_hi = [], [], []
    for g in range(SWA_KV_HEADS):
        sl = slice(g * LANES, (g + 1) * LANES)
        kcat.append(jnp.concatenate([km_ref[0, :, sl], kp_ref[0, :, sl], kc_ref[0, :, sl]], axis=0))
        vf = jnp.concatenate([vm_ref[0, :, sl], vp_ref[0, :, sl], vc_ref[0, :, sl]], axis=0).astype(F32)
        v_lo.append(jnp.where(lo, vf, 0.0).astype(BF16))
        v_hi.append(jnp.where(lo, 0.0, vf).astype(BF16))
    pairs_per_group = PAIRS // SWA_KV_HEADS

    def logits(p):
        qf = q_ref[0, :, p * LANES:(p + 1) * LANES].astype(F32)
        q_pair = (jnp.where(lo, qf, 0.0).astype(BF16), jnp.where(lo, 0.0, qf).astype(BF16))
        return [_dot_nt(q_pair[a], kcat[p // pairs_per_group]) for a in range(2)]

    def finish(p, s_pair):
        g = p // pairs_per_group
        ps, inv = [], []
        for a in range(2):
            sink = sink_ref[2 * p + a]
            s = jnp.where(valid, s_pair[a], NEG)
            m = jnp.maximum(jnp.max(s, axis=1, keepdims=True), sink)
            e = jnp.exp(s - m)
            den = jnp.sum(e, axis=1, keepdims=True) + jnp.exp(sink - m)
            ps.append(e.astype(BF16))
            inv.append(1.0 / den)
        o = (_dot(ps[0], v_lo[g]) + _dot(ps[1], v_hi[g])) * jnp.where(lo, inv[0], inv[1])
        o_ref[0, :, p * LANES:(p + 1) * LANES] = o.astype(o_ref.dtype)

    s_next = logits(0)
    for p in range(PAIRS):
        s_cur = s_next
        if p + 1 < PAIRS:
            s_next = logits(p + 1)
        finish(p, s_cur)


def _swa_attention(qkv, sinks, batch, lp):
    t = SWA_TQ
    kblk, vblk = 4, 5
    kv_spec = lambda col, row_of: pl.BlockSpec((1, t, 2 * LANES), lambda b, i: (b, row_of(i), col))
    meta = lambda i: 1
    prev = lambda i: jnp.maximum(i - 1, 0)
    cur = lambda i: i
    return pl.pallas_call(
        _swa_attn_kernel,
        grid=(batch, lp // t),
        in_specs=[pl.BlockSpec(memory_space=pltpu.SMEM),
                  pl.BlockSpec((1, t, HEADS * HEAD_DIM), lambda b, i: (b, i, 0)),
                  kv_spec(kblk, meta), kv_spec(kblk, prev), kv_spec(kblk, cur),
                  kv_spec(vblk, meta), kv_spec(vblk, prev), kv_spec(vblk, cur)],
        out_specs=pl.BlockSpec((1, t, HEADS * HEAD_DIM), lambda b, i: (b, i, 0)),
        out_shape=jax.ShapeDtypeStruct((batch, lp, HEADS * HEAD_DIM), BF16),
        compiler_params=_params("parallel", "parallel"),
        name="swa_attention",
    )(sinks, qkv, qkv, qkv, qkv, qkv, qkv, qkv)


def _swa_proj_kernel(x_ref, w_ref, cos_ref, sin_ref, perm_ref, o_ref, *, tn, n_rope_blocks):
    y = _dot(x_ref[...], w_ref[...])
    reps = tn // LANES
    cos = jnp.concatenate([cos_ref[...]] * reps, axis=1)
    sin = jnp.concatenate([sin_ref[...]] * reps, axis=1)
    for blk in range(y.shape[1] // tn):
        yb = y[:, blk * tn:(blk + 1) * tn]
        if blk < n_rope_blocks:
            yb = yb * cos + _dot(yb.astype(BF16), perm_ref[...]) * sin
        o_ref[:, blk * tn:(blk + 1) * tn] = yb.astype(o_ref.dtype)


def _swa_proj(hb, w, cos, sin, lp, tm, tn, n_rope_blocks):
    n = hb.shape[0]
    m = w.shape[1]
    tpb = lp // tm
    return pl.pallas_call(
        functools.partial(_swa_proj_kernel, tn=tn, n_rope_blocks=n_rope_blocks),
        grid=(n // tm,),
        in_specs=[pl.BlockSpec((tm, D_MODEL), lambda i: (i, 0)),
                  pl.BlockSpec((D_MODEL, m), lambda i: (0, 0)),
                  pl.BlockSpec((tm, LANES), lambda i: (i % tpb, 0)),
                  pl.BlockSpec((tm, LANES), lambda i: (i % tpb, 0)),
                  pl.BlockSpec((tn, tn), lambda i: (0, 0))],
        out_specs=pl.BlockSpec((tm, m), lambda i: (i, 0)),
        out_shape=jax.ShapeDtypeStruct((n, m), BF16),
        compiler_params=_params("parallel"),
        name="swa_proj",
    )(hb, w, cos, sin, jnp.asarray(_rope_partner_matrix(tn), BF16))


def _rope_partner_matrix(width):
    p = np.zeros((width, width), np.float32)
    half = ROPE_DIM // 2
    for base in range(0, width, HEAD_DIM):
        for l in range(half):
            p[base + l + half, base + l] = 1.0
            p[base + l, base + l + half] = 1.0
    return p


MLA_A_COLS = MLA_Q_LORA + MLA_KV_LORA + 2 * LANES


def _mla_a_kernel(x_ref, w_ref, gq_ref, gkv_ref, cos_ref, sin_ref, cq_ref, ckv_ref, kr_ref, *, tm, tiles_per_batch):
    y = _dot(x_ref[...], w_ref[...])
    cq = y[:, :MLA_Q_LORA]
    ckv = y[:, MLA_Q_LORA:MLA_Q_LORA + MLA_KV_LORA]
    kr = y[:, MLA_Q_LORA + MLA_KV_LORA:MLA_Q_LORA + MLA_KV_LORA + LANES]
    krs = y[:, MLA_Q_LORA + MLA_KV_LORA + LANES:]
    rms = lambda z, g: z * lax.rsqrt(jnp.mean(z * z, axis=-1, keepdims=True) + RMS_EPS) * g
    cq_ref[...] = rms(cq, gq_ref[...]).astype(cq_ref.dtype)
    ckv_ref[...] = rms(ckv, gkv_ref[...]).astype(ckv_ref.dtype)
    dead = _dead_rows(pl.program_id(0) % tiles_per_batch, tm) * _one_hot_lanes(LANES, LANES, MLA_DEAD_LANE)
    kr_ref[...] = (kr * cos_ref[...] + krs * sin_ref[...] + dead).astype(kr_ref.dtype)


def _mla_a(hb, w, gq, gkv, cos, sin, lp, tm):
    n = hb.shape[0]
    tpb = lp // tm
    return pl.pallas_call(
        functools.partial(_mla_a_kernel, tm=tm, tiles_per_batch=tpb),
        grid=(n // tm,),
        in_specs=[pl.BlockSpec((tm, D_MODEL), lambda i: (i, 0)),
                  pl.BlockSpec((D_MODEL, MLA_A_COLS), lambda i: (0, 0)),
                  pl.BlockSpec((1, MLA_Q_LORA), lambda i: (0, 0)),
                  pl.BlockSpec((1, MLA_KV_LORA), lambda i: (0, 0)),
                  pl.BlockSpec((tm, LANES), lambda i: (i % tpb, 0)),
                  pl.BlockSpec((tm, LANES), lambda i: (i % tpb, 0))],
        out_specs=[pl.BlockSpec((tm, MLA_Q_LORA), lambda i: (i, 0)),
                   pl.BlockSpec((tm, MLA_KV_LORA), lambda i: (i, 0)),
                   pl.BlockSpec((tm, LANES), lambda i: (i, 0))],
        out_shape=[jax.ShapeDtypeStruct((n, MLA_Q_LORA), BF16),
                   jax.ShapeDtypeStruct((n, MLA_KV_LORA), BF16),
                   jax.ShapeDtypeStruct((n, LANES), BF16)],
        compiler_params=_params("parallel"),
        name="mla_a",
    )(hb, w, gq, gkv, cos, sin)


def _mla_q_kernel(x_ref, w_ref, ws_ref, cos_ref, sin_ref, o_ref):
    x = x_ref[...]
    reps = o_ref.shape[1] // LANES
    cos = jnp.concatenate([cos_ref[...]] * reps, axis=1)
    sin = jnp.concatenate([sin_ref[...]] * reps, axis=1)
    y = _dot(x, w_ref[...]) * cos + _dot(x, ws_ref[...]) * sin
    o_ref[...] = (y + _one_hot_lanes(o_ref.shape[1], LANES, MLA_DEAD_LANE)).astype(o_ref.dtype)


def _mla_q(cq, w, ws, cos, sin, lp, tm, tn):
    n = cq.shape[0]
    m = w.shape[1]
    tpb = lp // tm
    return pl.pallas_call(
        _mla_q_kernel,
        grid=(n // tm, m // tn),
        in_specs=[pl.BlockSpec((tm, MLA_Q_LORA), lambda i, j: (i, 0)),
                  pl.BlockSpec((MLA_Q_LORA, tn), lambda i, j: (0, j)),
                  pl.BlockSpec((MLA_Q_LORA, tn), lambda i, j: (0, j)),
                  pl.BlockSpec((tm, LANES), lambda i, j: (i % tpb, 0)),
                  pl.BlockSpec((tm, LANES), lambda i, j: (i % tpb, 0))],
        out_specs=pl.BlockSpec((tm, tn), lambda i, j: (i, j)),
        out_shape=jax.ShapeDtypeStruct((n, m), BF16),
        compiler_params=_params("parallel", "parallel"),
        name="mla_q",
    )(cq, w, ws, cos, sin)


def _oproj_ln_kernel(o_ref, w_ref, h_ref, g_ref, b_ref, hf_ref, hb_ref):
    x = DEEPNORM_ALPHA * h_ref[...] + _dot(o_ref[...], w_ref[...])
    y = _layer_norm(x, g_ref[...], b_ref[...])
    hf_ref[...] = y
    hb_ref[...] = y.astype(hb_ref.dtype)


def _oproj_ln(o, w, h, g, b, tm):
    n = o.shape[0]
    row = lambda i: (i, 0)
    fixed = lambda i: (0, 0)
    return pl.pallas_call(
        _oproj_ln_kernel,
        grid=(n // tm,),
        in_specs=[pl.BlockSpec((tm, D_MODEL), row), pl.BlockSpec((D_MODEL, D_MODEL), fixed),
                  pl.BlockSpec((tm, D_MODEL), row), pl.BlockSpec((1, D_MODEL), fixed),
                  pl.BlockSpec((1, D_MODEL), fixed)],
        out_specs=[pl.BlockSpec((tm, D_MODEL), row), pl.BlockSpec((tm, D_MODEL), row)],
        out_shape=[jax.ShapeDtypeStruct((n, D_MODEL), F32), jax.ShapeDtypeStruct((n, D_MODEL), BF16)],
        compiler_params=_params("parallel"),
        name="oproj_ln",
    )(o, w, h, g, b)


FFN_HALO = BF16_SUBLANES


FFN_CHUNK = 256
FFN_NC = D_FF // FFN_CHUNK
assert FFN_NC * FFN_CHUNK == D_FF and FFN_NC % 2 == 1


def _ffn_kernel(x_ref, halo_ref, h_ref, win_ref, cw_ref, cb_ref, wo_ref, g_ref, b_ref, hf_ref, hb_ref,
                xext_ref, ua_ref, ub_ref, acc_ref, *, tm, tiles_per_batch):
    i = pl.program_id(0)
    nc = FFN_NC
    pos = (i % tiles_per_batch) * tm - FFN_HALO + lax.broadcasted_iota(jnp.int32, (tm + FFN_HALO, 1), 0)
    xe = jnp.concatenate([halo_ref[...], x_ref[...]], axis=0).astype(F32)
    xext_ref[...] = jnp.where(pos >= LEAD, xe, 0.0).astype(BF16)

    cols = lambda idx: slice(idx * FFN_CHUNK, (idx + 1) * FFN_CHUNK)

    def up(u_ref, c):
        u_ref[0] = _dot(xext_ref[...], win_ref[:, cols(c)])
        u_ref[1] = _dot(xext_ref[...], win_ref[:, cols(nc + c)])

    def glu(u_ref, c):
        def conv(part, idx):
            u = u_ref[part]
            delayed = u * cw_ref[0:1, cols(idx)]
            for tap in range(1, CONV_W):
                delayed = u * cw_ref[tap:tap + 1, cols(idx)] + pltpu.roll(delayed, 1, 0)
            return cb_ref[:, cols(idx)] + delayed[FFN_HALO:, :]

        yg = conv(0, c)
        yv = conv(1, nc + c)
        return ((yg / (1.0 + jnp.exp(-yg))) * yv).astype(BF16)

    acc_ref[...] = DEEPNORM_ALPHA * h_ref[...]
    up(ua_ref, 0)

    for c in range(0, nc - 1, 2):
        up(ub_ref, c + 1)
        act_a = glu(ua_ref, c)
        up(ua_ref, c + 2)
        act_b = glu(ub_ref, c + 1)
        acc_ref[...] += _dot(act_a, wo_ref[cols(c), :]) + _dot(act_b, wo_ref[cols(c + 1), :])
    y = _layer_norm(acc_ref[...] + _dot(glu(ua_ref, nc - 1), wo_ref[cols(nc - 1), :]), g_ref[...], b_ref[...])
    hf_ref[...] = y
    hb_ref[...] = y.astype(hb_ref.dtype)


def _ffn(hb, hf, w_in, conv_w, conv_b, w_out, g, b, lp, tm):
    n = hb.shape[0]
    nc, fc = FFN_NC, FFN_CHUNK
    tpb = lp // tm
    halo_blocks = tm // FFN_HALO
    row = lambda i: (i, 0)
    fixed2 = lambda i: (0, 0)
    resident = dict(pipeline_mode=pl.Buffered(1))
    return pl.pallas_call(
        functools.partial(_ffn_kernel, tm=tm, tiles_per_batch=tpb),
        grid=(n // tm,),
        in_specs=[pl.BlockSpec((tm, D_MODEL), row),
                  pl.BlockSpec((FFN_HALO, D_MODEL), lambda i: (jnp.maximum(i * halo_blocks - 1, 0), 0)),
                  pl.BlockSpec((tm, D_MODEL), row),
                  pl.BlockSpec((D_MODEL, 2 * D_FF), fixed2, **resident),
                  pl.BlockSpec((CONV_W, 2 * D_FF), fixed2, **resident),
                  pl.BlockSpec((1, 2 * D_FF), fixed2, **resident),
                  pl.BlockSpec((D_FF, D_MODEL), fixed2, **resident),
                  pl.BlockSpec((1, D_MODEL), fixed2), pl.BlockSpec((1, D_MODEL), fixed2)],
        out_specs=[pl.BlockSpec((tm, D_MODEL), row), pl.BlockSpec((tm, D_MODEL), row)],
        out_shape=[jax.ShapeDtypeStruct((n, D_MODEL), F32), jax.ShapeDtypeStruct((n, D_MODEL), BF16)],
        scratch_shapes=[pltpu.VMEM((tm + FFN_HALO, D_MODEL), BF16),
                        pltpu.VMEM((2, tm + FFN_HALO, fc), F32), pltpu.VMEM((2, tm + FFN_HALO, fc), F32),
                        pltpu.VMEM((tm, D_MODEL), F32)],
        compiler_params=_params("parallel"),
        name="ffn",
    )(hb, hb, hf, w_in, conv_w, conv_b, w_out, g, b)


def _ffn_weights(w_in, conv_w, conv_b, w_out):
    return w_in.astype(BF16), conv_w, conv_b[None, :], w_out.astype(BF16)


def _rope_tables(lp, dim, theta, group, offset):
    pos = (jnp.arange(lp) - LEAD).astype(F32)
    inv = theta ** (-jnp.arange(0, dim, 2, dtype=F32) / dim)
    ang = pos[:, None] * inv[None, :]
    cos, sin = jnp.cos(ang), jnp.sin(ang)
    ones = lambda w: jnp.ones((lp, w), F32)
    zeros = lambda w: jnp.zeros((lp, w), F32)
    rest = group - offset - dim
    cos_g = jnp.concatenate([ones(offset), cos, cos, ones(rest)], axis=1)
    sin_g = jnp.concatenate([zeros(offset), -sin, sin, zeros(rest)], axis=1)
    reps = LANES // group
    return jnp.tile(cos_g, (1, reps)), jnp.tile(sin_g, (1, reps))


def _swap_halves(w, dim):
    return jnp.concatenate([w[..., dim // 2:dim], w[..., :dim // 2]], axis=-1)


def _head_blocks(main, extra):
    src = main if main is not None else extra
    rows, heads = src.shape[0], src.shape[1]
    m = main if main is not None else jnp.zeros((rows, heads, HEAD_DIM), F32)
    e = extra if extra is not None else jnp.zeros((rows, heads, 0), F32)
    pad = jnp.zeros((rows, heads, LANES - HEAD_DIM - e.shape[2]), F32)
    return jnp.concatenate([m, e, pad], axis=2).reshape(rows, heads * LANES)


def _mla_weights(w_a, w_uq, w_ukv):
    d = w_a.shape[0]
    w_kr = w_a[:, MLA_Q_LORA + MLA_KV_LORA:][:, None, :]
    w_a_cat = jnp.concatenate([w_a[:, :MLA_Q_LORA + MLA_KV_LORA], _head_blocks(None, w_kr),
                               _head_blocks(None, _swap_halves(w_kr, MLA_ROPE))], axis=1)
    scale = (MLA_NOPE + MLA_ROPE) ** -0.5 * LOG2E
    wq = (w_uq * scale).reshape(MLA_Q_LORA, HEADS, MLA_NOPE + MLA_ROPE)
    w_main = _head_blocks(wq[..., :MLA_NOPE], wq[..., MLA_NOPE:])
    w_swap = _head_blocks(None, _swap_halves(wq[..., MLA_NOPE:], MLA_ROPE))
    wkv = w_ukv.reshape(MLA_KV_LORA, HEADS, MLA_NOPE + HEAD_DIM)
    w_kn = wkv[..., :MLA_NOPE].reshape(MLA_KV_LORA, HEADS * MLA_NOPE)
    w_v = wkv[..., MLA_NOPE:].reshape(MLA_KV_LORA, HEADS * HEAD_DIM)
    return tuple(w.astype(BF16) for w in (w_a_cat, w_main, w_swap, w_kn)) + (w_v,)


def _swa_weights(w_in):
    qd = HEADS * HEAD_DIM
    kd = SWA_KV_HEADS * HEAD_DIM
    q = w_in[:, :qd] * (HEAD_DIM ** -0.5)
    dup = lambda w: jnp.concatenate([w[:, :HEAD_DIM], w[:, :HEAD_DIM], w[:, HEAD_DIM:], w[:, HEAD_DIM:]], axis=1)
    return jnp.concatenate([q, dup(w_in[:, qd:qd + kd]), dup(w_in[:, qd + kd:])], axis=1).astype(BF16)


def _fox_weights(w_in, b_f):
    hd = HEADS * HEAD_DIM
    d = w_in.shape[0]
    w_q = (w_in[:, :hd] * (HEAD_DIM ** -0.5 * LOG2E)).astype(BF16)
    w_k = w_in[:, hd:2 * hd].astype(BF16)
    w_v = w_in[:, 2 * hd:3 * hd]
    w_gate = w_in[:, 3 * hd:]
    w_fg = jnp.concatenate([w_gate] * GATE_SLOTS + [jnp.zeros((d, LANES - GATE_SLOTS * HEADS), F32)],
                           axis=1).astype(BF16)
    b_fg = jnp.concatenate([b_f] * GATE_SLOTS + [jnp.zeros((LANES - GATE_SLOTS * HEADS,), F32)])[None, :]
    return w_q, w_k, w_v, w_fg, b_fg


def kernel(x, meta_tokens, ln1_g, ln1_b, ln2_g, ln2_b, fox_w_in, fox_b_f, fox_w_o, swa_w_in, swa_sinks, swa_w_o,
           mla_w_a, mla_g_q, mla_g_kv, mla_w_uq, mla_w_ukv, mla_w_o, ffn_w_in, ffn_conv_w, ffn_conv_b, ffn_w_out):
    batch, seq, d = x.shape
    assert d == D_MODEL and seq % 256 == 0
    lp = seq + FIRST_REAL
    n = batch * lp

    tm = _tile(lp, 768, 256)
    tq = _tile(lp, 768, 256)
    tk = 256
    tn = HEADS * HEAD_DIM
    nk = lp // tk

    hf, hb = _embed(x, meta_tokens.astype(x.dtype))

    cos_p, sin_p = _rope_tables(lp, ROPE_DIM, ROPE_THETA, HEAD_DIM, 0)
    cos_m, sin_m = _rope_tables(lp, MLA_ROPE, MLA_ROPE_THETA, LANES, MLA_NOPE)
    b3 = lambda a: a.reshape(batch, lp, -1)

    for i in range(DEPTH):
        kind, j = i % 3, i // 3
        if kind == 0:
            w_q, w_k, w_v, w_fg, b_fg = _fox_weights(fox_w_in[j], fox_b_f[j])
            aq, ak = _fox_gate(hb, w_fg, b_fg, lp, tm)
            qh = _head_proj(hb, w_q, aq, tm, tn)
            kh = _head_proj(hb, w_k, ak, tm, tn)
            vt = _matmul_t(hb, *_value_weights(w_v), tm, V_COLS_TILE, tk).reshape(batch, nk, HEADS * V_ROWS, tk)
            o = _flash_attention(b3(qh), b3(kh), vt, batch, lp, tq, tk)
            w_o = fox_w_o[j]
        elif kind == 1:
            qkv = _swa_proj(hb, _swa_weights(swa_w_in[j]), cos_p, sin_p, lp, tm, 2 * LANES, 5)
            o = _swa_attention(b3(qkv), swa_sinks[j].astype(F32), batch, lp)
            w_o = swa_w_o[j]
        else:
            w_a_cat, w_main, w_swap, w_kn, w_v = _mla_weights(mla_w_a[j], mla_w_uq[j], mla_w_ukv[j])
            cq, ckv, kr = _mla_a(hb, w_a_cat, mla_g_q[j][None, :], mla_g_kv[j][None, :], cos_m, sin_m, lp, tm)
            qh = _mla_q(cq, w_main, w_swap, cos_m, sin_m, lp, tm, tn)
            kh = _head_proj(ckv, w_kn, kr, tm, tn)
            vt = _matmul_t(ckv, *_value_weights(w_v), tm, V_COLS_TILE, tk).reshape(batch, nk, HEADS * V_ROWS, tk)
            o = _flash_attention(b3(qh), b3(kh), vt, batch, lp, tq, tk)
            w_o = mla_w_o[j]
        hf, hb = _oproj_ln(o.reshape(n, d), w_o.astype(BF16), hf, ln1_g[i][None, :], ln1_b[i][None, :], tm)
        hf, hb = _ffn(hb, hf, *_ffn_weights(ffn_w_in[i], ffn_conv_w[i], ffn_conv_b[i], ffn_w_out[i]),
                      ln2_g[i][None, :], ln2_b[i][None, :], lp, tm)
    return hf.reshape(batch, lp, d)[:, FIRST_REAL:]
```

```python
import functools
import math

import numpy as np
import jax
import jax.numpy as jnp
from jax import lax
from jax.experimental import pallas as pl
from jax.experimental.pallas import tpu as pltpu

F32 = jnp.float32
BF16 = jnp.bfloat16

D_MODEL = 1024
DEPTH = 4
N_META = 16
LEAD = 240
FIRST_REAL = LEAD + N_META
NEG = -1e30
DEEPNORM_ALPHA = (2.0 * DEPTH) ** 0.25
LN_EPS = 1e-5
RMS_EPS = 1e-6
HEADS = 16
HEAD_DIM = 64
PAIRS = HEADS // 2
SWA_KV_HEADS = 2
WINDOW = 128
ROPE_THETA = 500000.0
ROPE_DIM = 16
MLA_Q_LORA = 384
MLA_KV_LORA = 256
MLA_NOPE = 64
MLA_ROPE = 32
MLA_ROPE_THETA = 10000.0
D_FF = 2816
CONV_W = 3
LOG2E = math.log2(math.e)

LANES = 128
BF16_SUBLANES = 16
VMEM_LIMIT = 56 * 1024 * 1024

GATE_SLOTS = 3
FOX_DEAD_LANE = HEAD_DIM + 2 * GATE_SLOTS
MLA_DEAD_LANE = MLA_NOPE + MLA_ROPE
M_INIT = -3e38


def _params(*sem):
    return pltpu.CompilerParams(dimension_semantics=sem, vmem_limit_bytes=VMEM_LIMIT)


def _tile(n, pref, mult):
    best = mult
    t = mult
    while t <= min(n, pref):
        if n % t == 0:
            best = t
        t += mult
    assert n % best == 0
    return best


def _dot(a, b):
    return jnp.dot(a, b, preferred_element_type=F32)


def _dot_nt(a, b):
    return lax.dot_general(a, b, (((1,), (1,)), ((), ())), preferred_element_type=F32)


def _layer_norm(x, g, b):
    mu = jnp.mean(x, axis=-1, keepdims=True)
    xc = x - mu
    var = jnp.mean(xc * xc, axis=-1, keepdims=True)
    return xc * lax.rsqrt(var + LN_EPS) * g + b


def _one_hot_lanes(width, period, lane):
    idx = lax.broadcasted_iota(jnp.int32, (1, width), 1)
    return jnp.where((idx & (period - 1)) == lane, 1.0, 0.0)


def _dead_rows(tile_in_batch, tm):
    pos = tile_in_batch * tm + lax.broadcasted_iota(jnp.int32, (tm, 1), 0)
    return jnp.where(pos < LEAD, NEG, 0.0)


def _embed_kernel(x_ref, meta_ref, hf_ref, hb_ref):
    t = pl.program_id(1)

    @pl.when(t == 0)
    def _():
        lead = jnp.concatenate([jnp.zeros((LEAD, D_MODEL), F32), meta_ref[...]], axis=0)
        hf_ref[...] = lead
        hb_ref[...] = lead.astype(hb_ref.dtype)

    @pl.when(t > 0)
    def _():
        hf_ref[...] = x_ref[0]
        hb_ref[...] = x_ref[0].astype(hb_ref.dtype)


def _embed(x, meta):
    batch, seq, d = x.shape
    blocks = seq // FIRST_REAL + 1
    return pl.pallas_call(
        _embed_kernel,
        grid=(batch, blocks),
        in_specs=[pl.BlockSpec((1, FIRST_REAL, d), lambda b, t: (b, jnp.maximum(t - 1, 0), 0)),
                  pl.BlockSpec((N_META, d), lambda b, t: (0, 0))],
        out_specs=[pl.BlockSpec((FIRST_REAL, d), lambda b, t: (b * blocks + t, 0)),
                   pl.BlockSpec((FIRST_REAL, d), lambda b, t: (b * blocks + t, 0))],
        out_shape=[jax.ShapeDtypeStruct((batch * blocks * FIRST_REAL, d), F32),
                   jax.ShapeDtypeStruct((batch * blocks * FIRST_REAL, d), BF16)],
        compiler_params=_params("parallel", "parallel"),
        name="embed",
    )(x, meta)


def _head_proj_kernel(x_ref, w_ref, e_ref, o_ref):
    y = _dot(x_ref[...], w_ref[...])
    lo = lax.broadcasted_iota(jnp.int32, (1, LANES), 1) < HEAD_DIM
    shared = e_ref.shape[1] == LANES
    for pair in range(y.shape[1] // LANES):
        y_pair = y[:, pair * LANES:(pair + 1) * LANES]
        for a, feats in enumerate((y_pair, pltpu.roll(y_pair, HEAD_DIM, 1))):
            h = 2 * pair + a
            extra = e_ref[...] if shared else e_ref[:, h * LANES:(h + 1) * LANES]
            o_ref[:, h * LANES:(h + 1) * LANES] = jnp.where(lo, feats, extra.astype(F32)).astype(o_ref.dtype)


def _head_proj(x, w, e, tm, tn):
    n, k = x.shape
    m = w.shape[1]
    if e.shape[1] == LANES:
        e_spec = pl.BlockSpec((tm, LANES), lambda i, j: (i, 0))
    else:
        assert e.shape[1] == 2 * m
        e_spec = pl.BlockSpec((tm, 2 * tn), lambda i, j: (i, j))
    return pl.pallas_call(
        _head_proj_kernel,
        grid=(n // tm, m // tn),
        in_specs=[pl.BlockSpec((tm, k), lambda i, j: (i, 0)),
                  pl.BlockSpec((k, tn), lambda i, j: (0, j)),
                  e_spec],
        out_specs=pl.BlockSpec((tm, 2 * tn), lambda i, j: (i, j)),
        out_shape=jax.ShapeDtypeStruct((n, 2 * m), BF16),
        compiler_params=_params("parallel", "parallel"),
        name="head_proj",
    )(x, w, e)


def _mm_t_kernel(x_ref, w_ref, b_ref, o_ref, *, tk):
    y = _dot(x_ref[...], w_ref[...]) + b_ref[...]
    for c in range(o_ref.shape[0]):
        o_ref[c] = y[c * tk:(c + 1) * tk, :].T.astype(o_ref.dtype)


def _matmul_t(x, w, bias, tm, tn, tk):
    n, k = x.shape
    m = w.shape[1]
    r = tm // tk
    return pl.pallas_call(
        functools.partial(_mm_t_kernel, tk=tk),
        grid=(n // tm, m // tn),
        in_specs=[pl.BlockSpec((tm, k), lambda i, j: (i, 0)),
                  pl.BlockSpec((k, tn), lambda i, j: (0, j)),
                  pl.BlockSpec((1, tn), lambda i, j: (0, j))],
        out_specs=pl.BlockSpec((r, tn, tk), lambda i, j: (i, j, 0)),
        out_shape=jax.ShapeDtypeStruct((n // tk, m, tk), BF16),
        compiler_params=_params("parallel", "parallel"),
        name="matmul_t",
    )(x, w, bias)


V_ROWS = HEAD_DIM + BF16_SUBLANES
V_COLS_TILE = HEADS * V_ROWS


def _value_weights(w_v):
    rows = w_v.shape[0]
    w = jnp.concatenate([w_v.reshape(rows, HEADS, HEAD_DIM),
                         jnp.zeros((rows, HEADS, V_ROWS - HEAD_DIM), w_v.dtype)], axis=2)
    bias = np.zeros((HEADS, V_ROWS), np.float32)
    bias[:, HEAD_DIM] = 1.0
    return w.reshape(rows, HEADS * V_ROWS).astype(BF16), jnp.asarray(bias.reshape(1, HEADS * V_ROWS))


def _gate_placement():
    wide = HEADS * LANES
    sq = np.zeros((LANES, wide), np.float32)
    sk = np.zeros_like(sq)
    oq = np.zeros((1, wide), np.float32)
    ok = np.zeros_like(oq)
    for h in range(HEADS):
        base = h * LANES + HEAD_DIM
        for part in range(GATE_SLOTS):
            sq[part * HEADS + h, base + part] = 1.0
            sk[part * HEADS + h, base + GATE_SLOTS + part] = -1.0
            oq[0, base + GATE_SLOTS + part] = 1.0
            ok[0, base + part] = 1.0
        oq[0, h * LANES + FOX_DEAD_LANE] = 1.0
    return sq, sk, oq, ok


def _split3(x):
    hi = x.astype(BF16)
    r1 = x - hi.astype(F32)
    mid = r1.astype(BF16)
    lo = (r1 - mid.astype(F32)).astype(BF16)
    return hi, mid, lo


def _fox_gate_kernel(x_ref, w_ref, b_ref, sq_ref, sk_ref, oq_ref, ok_ref, aq_ref, ak_ref, carry_ref,
                     *, tm, tiles_per_batch):
    i = pl.program_id(0)

    @pl.when(i % tiles_per_batch == 0)
    def _():
        carry_ref[...] = jnp.zeros_like(carry_ref)

    fg = _dot(x_ref[...], w_ref[...]) + b_ref[...]
    logf = jnp.minimum(fg, 0.0) - jnp.log(1.0 + jnp.exp(-jnp.abs(fg)))
    sub = FIRST_REAL
    row = lax.broadcasted_iota(jnp.int32, (sub, sub), 0)
    col = lax.broadcasted_iota(jnp.int32, (sub, sub), 1)
    tri = jnp.where(col <= row, 1.0, 0.0).astype(BF16)
    hi, mid, lo = _split3(logf)
    carry = carry_ref[...]
    pieces = []
    for r0 in range(0, tm, sub):
        rows = slice(r0, r0 + sub)
        piece = _dot(tri, hi[rows]) + _dot(tri, mid[rows]) + _dot(tri, lo[rows]) + carry
        carry = piece[sub - 1:sub, :]
        pieces.append(piece)
    cs = jnp.concatenate(pieces, axis=0)
    carry_ref[...] = carry
    lane = lax.broadcasted_iota(jnp.int32, (1, LANES), 1)
    hi, mid, lo = _split3(cs * LOG2E)
    parts = jnp.where(lane < HEADS, hi.astype(F32),
                      jnp.where(lane < 2 * HEADS, mid.astype(F32), lo.astype(F32))).astype(BF16)
    wide = aq_ref.shape[1]
    dead = _dead_rows(i % tiles_per_batch, tm) * _one_hot_lanes(wide, LANES, FOX_DEAD_LANE)
    aq_ref[...] = (_dot(parts, sq_ref[...]) + oq_ref[...]).astype(aq_ref.dtype)
    ak_ref[...] = (_dot(parts, sk_ref[...]) + ok_ref[...] + dead).astype(ak_ref.dtype)


def _fox_gate(hb, w_fg, b_fg, lp, tm):
    n = hb.shape[0]
    tpb = lp // tm
    sq, sk, oq, ok = _gate_placement()
    fixed = lambda i: (0, 0)
    wide = HEADS * LANES
    return pl.pallas_call(
        functools.partial(_fox_gate_kernel, tm=tm, tiles_per_batch=tpb),
        grid=(n // tm,),
        in_specs=[pl.BlockSpec((tm, D_MODEL), lambda i: (i, 0)),
                  pl.BlockSpec((D_MODEL, LANES), fixed),
                  pl.BlockSpec((1, LANES), fixed),
                  pl.BlockSpec((LANES, wide), fixed),
                  pl.BlockSpec((LANES, wide), fixed),
                  pl.BlockSpec((1, wide), fixed),
                  pl.BlockSpec((1, wide), fixed)],
        out_specs=[pl.BlockSpec((tm, wide), lambda i: (i, 0)),
                   pl.BlockSpec((tm, wide), lambda i: (i, 0))],
        out_shape=[jax.ShapeDtypeStruct((n, wide), BF16), jax.ShapeDtypeStruct((n, wide), BF16)],
        scratch_shapes=[pltpu.VMEM((1, LANES), F32)],
        compiler_params=_params("arbitrary"),
        name="fox_gate",
    )(hb, w_fg, b_fg, jnp.asarray(sq, BF16), jnp.asarray(sk, BF16), jnp.asarray(oq), jnp.asarray(ok))


def _flash_kernel(q_ref, k_ref, vt_ref, o_ref, sa_ref, sb_ref, xa_ref, xb_ref, m_ref, acc_ref, *, tq, tk):
    i = pl.program_id(2)
    r = tq // tk
    m_ref[...] = jnp.full_like(m_ref, M_INIT)
    acc_ref[...] = jnp.zeros_like(acc_ref)

    def diag_mask(s):
        keep = lax.broadcasted_iota(jnp.int32, s.shape, 0) <= lax.broadcasted_iota(jnp.int32, s.shape, 1)
        return jnp.where(keep, s, NEG)

    def scores(s_ref, x_ref, j, a, qs, diagonal, blk=None):
        off = pl.multiple_of(j * tk, tk)
        q0 = pl.multiple_of((i if blk is None else blk) * tq + qs, tk)
        s = _dot_nt(k_ref[0, pl.ds(off, tk), a * LANES:(a + 1) * LANES],
                    q_ref[0, pl.ds(q0, tq - qs), a * LANES:(a + 1) * LANES])
        if diagonal:
            s = diag_mask(s)
        s_ref[a, :, qs:] = s
        x_ref[a, :, qs:] = jnp.max(s, axis=0, keepdims=True)

    def consume(s_ref, x_ref, j, a, qs, mask_now):
        s = s_ref[a, :, qs:]
        if mask_now:
            s = diag_mask(s)
            smax = jnp.max(s, axis=0, keepdims=True)
        else:
            smax = x_ref[a, :, qs:]
        m_old = m_ref[a, :, qs:]
        m_new = jnp.maximum(m_old, smax)
        alpha = jnp.exp2(m_old - m_new)
        p = jnp.exp2(s - m_new)
        m_ref[a, :, qs:] = m_new
        pv = _dot(vt_ref[0, j, a * V_ROWS:(a + 1) * V_ROWS, :], p.astype(BF16))
        acc_ref[a, :, qs:] = alpha * acc_ref[a, :, qs:] + pv

    buf_a, buf_b = (sa_ref, xa_ref), (sb_ref, xb_ref)
    jdiag = i * r
    odd = jdiag & 1

    @pl.when(i == 0)
    def _():
        for a in range(2):
            scores(*buf_a, 0, a, 0, False)

    @pl.when(odd == 1)
    def _():
        for a in range(2):
            scores(*buf_b, 1, a, 0, False)
            consume(*buf_a, 0, a, 0, False)
        sa_ref[...] = sb_ref[...]
        xa_ref[...] = xb_ref[...]

    def pair(j):
        for a in range(2):
            scores(*buf_b, j + 1, a, 0, False)
            consume(*buf_a, j, a, 0, False)
        for a in range(2):
            scores(*buf_a, j + 2, a, 0, False)
            consume(*buf_b, j + 1, a, 0, False)

    pairs = (jdiag - odd) // 2
    one = pairs & 1
    two = (pairs >> 1) & 1

    @pl.when(one == 1)
    def _():
        pair(odd)

    @pl.when(two == 1)
    def _():
        j = odd + 2 * one
        pair(j)
        pair(j + 2)

    def octo_body(t, c):
        j = odd + 2 * one + 4 * two + 8 * t
        for u in range(4):
            pair(j + 2 * u)
        return c

    lax.fori_loop(0, pairs >> 2, octo_body, 0)

    bufs = (buf_a, buf_b)
    for d in range(r):
        qs = d * tk
        for a in range(2):
            if d + 1 < r:
                scores(*bufs[(d + 1) & 1], jdiag + d + 1, a, qs + tk, True)
            consume(*bufs[d & 1], jdiag + d, a, qs, d == 0)

    for a in range(2):
        scores(*buf_a, 0, a, 0, False, blk=jnp.minimum(i + 1, pl.num_programs(2) - 1))

    ot = jnp.concatenate([acc_ref[a, :HEAD_DIM, :] * (1.0 / acc_ref[a, HEAD_DIM:HEAD_DIM + 1, :]) for a in range(2)],
                         axis=0)
    o_ref[0] = ot.T.astype(o_ref.dtype)


def _flash_attention(qh, kh, vt, batch, lp, tq, tk):
    nk = lp // tk
    return pl.pallas_call(
        functools.partial(_flash_kernel, tq=tq, tk=tk),
        grid=(batch, PAIRS, lp // tq),
        in_specs=[pl.BlockSpec((1, lp, 2 * LANES), lambda b, p, i: (b, 0, p)),
                  pl.BlockSpec((1, lp, 2 * LANES), lambda b, p, i: (b, 0, p)),
                  pl.BlockSpec((1, nk, 2 * V_ROWS, tk), lambda b, p, i: (b, 0, p, 0))],
        out_specs=pl.BlockSpec((1, tq, 2 * HEAD_DIM), lambda b, p, i: (b, i, p)),
        out_shape=jax.ShapeDtypeStruct((batch, lp, HEADS * HEAD_DIM), BF16),
        scratch_shapes=[pltpu.VMEM((2, tk, tq), F32), pltpu.VMEM((2, tk, tq), F32),
                        pltpu.VMEM((2, 1, tq), F32), pltpu.VMEM((2, 1, tq), F32),
                        pltpu.VMEM((2, 1, tq), F32), pltpu.VMEM((2, V_ROWS, tq), F32)],
        compiler_params=_params("parallel", "parallel", "arbitrary"),
        name="flash_attention",
    )(qh, kh, vt)


SWA_TQ = 128


def _swa_attn_kernel(sink_ref, q_ref, km_ref, kp_ref, kc_ref, vm_ref, vp_ref, vc_ref, o_ref):
    i = pl.program_id(1)
    t = SWA_TQ
    lo = lax.broadcasted_iota(jnp.int32, (1, LANES), 1) < HEAD_DIM
    row = lax.broadcasted_iota(jnp.int32, (t, 3 * t), 0)
    col = lax.broadcasted_iota(jnp.int32, (t, 3 * t), 1)
    qpos = i * t + row
    kpos = jnp.where(col < t, t + col, (i - 2) * t + col)
    d = qpos - kpos
    valid = (d >= 0) & (((col < t) & (kpos >= LEAD)) |
                        ((col >= t) & (d < WINDOW) & (kpos >= FIRST_REAL)))
    kcat, v_lo, v_hi = [], [], []
    for g in range(SWA_KV_HEADS):
        sl = slice(g * LANES, (g + 1) * LANES)
        kcat.append(jnp.concatenate([km_ref[0, :, sl], kp_ref[0, :, sl], kc_ref[0, :, sl]], axis=0))
        vf = jnp.concatenate([vm_ref[0, :, sl], vp_ref[0, :, sl], vc_ref[0, :, sl]], axis=0).astype(F32)
        v_lo.append(jnp.where(lo, vf, 0.0).astype(BF16))
        v_hi.append(jnp.where(lo, 0.0, vf).astype(BF16))
    pairs_per_group = PAIRS // SWA_KV_HEADS

    def logits(p):
        qf = q_ref[0, :, p * LANES:(p + 1) * LANES].astype(F32)
        q_pair = (jnp.where(lo, qf, 0.0).astype(BF16), jnp.where(lo, 0.0, qf).astype(BF16))
        return [_dot_nt(q_pair[a], kcat[p // pairs_per_group]) for a in range(2)]

    def finish(p, s_pair):
        g = p // pairs_per_group
        ps, inv = [], []
        for a in range(2):
            sink = sink_ref[2 * p + a]
            s = jnp.where(valid, s_pair[a], NEG)
            m = jnp.maximum(jnp.max(s, axis=1, keepdims=True), sink)
            e = jnp.exp(s - m)
            den = jnp.sum(e, axis=1, keepdims=True) + jnp.exp(sink - m)
            ps.append(e.astype(BF16))
            inv.append(1.0 / den)
        o = (_dot(ps[0], v_lo[g]) + _dot(ps[1], v_hi[g])) * jnp.where(lo, inv[0], inv[1])
        o_ref[0, :, p * LANES:(p + 1) * LANES] = o.astype(o_ref.dtype)

    s_next = logits(0)
    for p in range(PAIRS):
        s_cur = s_next
        if p + 1 < PAIRS:
            s_next = logits(p + 1)
        finish(p, s_cur)


def _swa_attention(qkv, sinks, batch, lp):
    t = SWA_TQ
    kblk, vblk = 4, 5
    kv_spec = lambda col, row_of: pl.BlockSpec((1, t, 2 * LANES), lambda b, i: (b, row_of(i), col))
    meta = lambda i: 1
    prev = lambda i: jnp.maximum(i - 1, 0)
    cur = lambda i: i
    return pl.pallas_call(
        _swa_attn_kernel,
        grid=(batch, lp // t),
        in_specs=[pl.BlockSpec(memory_space=pltpu.SMEM),
                  pl.BlockSpec((1, t, HEADS * HEAD_DIM), lambda b, i: (b, i, 0)),
                  kv_spec(kblk, meta), kv_spec(kblk, prev), kv_spec(kblk, cur),
                  kv_spec(vblk, meta), kv_spec(vblk, prev), kv_spec(vblk, cur)],
        out_specs=pl.BlockSpec((1, t, HEADS * HEAD_DIM), lambda b, i: (b, i, 0)),
        out_shape=jax.ShapeDtypeStruct((batch, lp, HEADS * HEAD_DIM), BF16),
        compiler_params=_params("parallel", "parallel"),
        name="swa_attention",
    )(sinks, qkv, qkv, qkv, qkv, qkv, qkv, qkv)


def _swa_proj_kernel(x_ref, w_ref, cos_ref, sin_ref, perm_ref, o_ref, *, tn, n_rope_blocks):
    y = _dot(x_ref[...], w_ref[...])
    reps = tn // LANES
    cos = jnp.concatenate([cos_ref[...]] * reps, axis=1)
    sin = jnp.concatenate([sin_ref[...]] * reps, axis=1)
    for blk in range(y.shape[1] // tn):
        yb = y[:, blk * tn:(blk + 1) * tn]
        if blk < n_rope_blocks:
            yb = yb * cos + _dot(yb.astype(BF16), perm_ref[...]) * sin
        o_ref[:, blk * tn:(blk + 1) * tn] = yb.astype(o_ref.dtype)


def _swa_proj(hb, w, cos, sin, lp, tm, tn, n_rope_blocks):
    n = hb.shape[0]
    m = w.shape[1]
    tpb = lp // tm
    return pl.pallas_call(
        functools.partial(_swa_proj_kernel, tn=tn, n_rope_blocks=n_rope_blocks),
        grid=(n // tm,),
        in_specs=[pl.BlockSpec((tm, D_MODEL), lambda i: (i, 0)),
                  pl.BlockSpec((D_MODEL, m), lambda i: (0, 0)),
                  pl.BlockSpec((tm, LANES), lambda i: (i % tpb, 0)),
                  pl.BlockSpec((tm, LANES), lambda i: (i % tpb, 0)),
                  pl.BlockSpec((tn, tn), lambda i: (0, 0))],
        out_specs=pl.BlockSpec((tm, m), lambda i: (i, 0)),
        out_shape=jax.ShapeDtypeStruct((n, m), BF16),
        compiler_params=_params("parallel"),
        name="swa_proj",
    )(hb, w, cos, sin, jnp.asarray(_rope_partner_matrix(tn), BF16))


def _rope_partner_matrix(width):
    p = np.zeros((width, width), np.float32)
    half = ROPE_DIM // 2
    for base in range(0, width, HEAD_DIM):
        for l in range(half):
            p[base + l + half, base + l] = 1.0
            p[base + l, base + l + half] = 1.0
    return p


MLA_A_COLS = MLA_Q_LORA + MLA_KV_LORA + 2 * LANES


def _mla_a_kernel(x_ref, w_ref, gq_ref, gkv_ref, cos_ref, sin_ref, cq_ref, ckv_ref, kr_ref, *, tm, tiles_per_batch):
    y = _dot(x_ref[...], w_ref[...])
    cq = y[:, :MLA_Q_LORA]
    ckv = y[:, MLA_Q_LORA:MLA_Q_LORA + MLA_KV_LORA]
    kr = y[:, MLA_Q_LORA + MLA_KV_LORA:MLA_Q_LORA + MLA_KV_LORA + LANES]
    krs = y[:, MLA_Q_LORA + MLA_KV_LORA + LANES:]
    rms = lambda z, g: z * lax.rsqrt(jnp.mean(z * z, axis=-1, keepdims=True) + RMS_EPS) * g
    cq_ref[...] = rms(cq, gq_ref[...]).astype(cq_ref.dtype)
    ckv_ref[...] = rms(ckv, gkv_ref[...]).astype(ckv_ref.dtype)
    dead = _dead_rows(pl.program_id(0) % tiles_per_batch, tm) * _one_hot_lanes(LANES, LANES, MLA_DEAD_LANE)
    kr_ref[...] = (kr * cos_ref[...] + krs * sin_ref[...] + dead).astype(kr_ref.dtype)


def _mla_a(hb, w, gq, gkv, cos, sin, lp, tm):
    n = hb.shape[0]
    tpb = lp // tm
    return pl.pallas_call(
        functools.partial(_mla_a_kernel, tm=tm, tiles_per_batch=tpb),
        grid=(n // tm,),
        in_specs=[pl.BlockSpec((tm, D_MODEL), lambda i: (i, 0)),
                  pl.BlockSpec((D_MODEL, MLA_A_COLS), lambda i: (0, 0)),
                  pl.BlockSpec((1, MLA_Q_LORA), lambda i: (0, 0)),
                  pl.BlockSpec((1, MLA_KV_LORA), lambda i: (0, 0)),
                  pl.BlockSpec((tm, LANES), lambda i: (i % tpb, 0)),
                  pl.BlockSpec((tm, LANES), lambda i: (i % tpb, 0))],
        out_specs=[pl.BlockSpec((tm, MLA_Q_LORA), lambda i: (i, 0)),
                   pl.BlockSpec((tm, MLA_KV_LORA), lambda i: (i, 0)),
                   pl.BlockSpec((tm, LANES), lambda i: (i, 0))],
        out_shape=[jax.ShapeDtypeStruct((n, MLA_Q_LORA), BF16),
                   jax.ShapeDtypeStruct((n, MLA_KV_LORA), BF16),
                   jax.ShapeDtypeStruct((n, LANES), BF16)],
        compiler_params=_params("parallel"),
        name="mla_a",
    )(hb, w, gq, gkv, cos, sin)


def _mla_q_kernel(x_ref, w_ref, ws_ref, cos_ref, sin_ref, o_ref):
    x = x_ref[...]
    reps = o_ref.shape[1] // LANES
    cos = jnp.concatenate([cos_ref[...]] * reps, axis=1)
    sin = jnp.concatenate([sin_ref[...]] * reps, axis=1)
    y = _dot(x, w_ref[...]) * cos + _dot(x, ws_ref[...]) * sin
    o_ref[...] = (y + _one_hot_lanes(o_ref.shape[1], LANES, MLA_DEAD_LANE)).astype(o_ref.dtype)


def _mla_q(cq, w, ws, cos, sin, lp, tm, tn):
    n = cq.shape[0]
    m = w.shape[1]
    tpb = lp // tm
    return pl.pallas_call(
        _mla_q_kernel,
        grid=(n // tm, m // tn),
        in_specs=[pl.BlockSpec((tm, MLA_Q_LORA), lambda i, j: (i, 0)),
                  pl.BlockSpec((MLA_Q_LORA, tn), lambda i, j: (0, j)),
                  pl.BlockSpec((MLA_Q_LORA, tn), lambda i, j: (0, j)),
                  pl.BlockSpec((tm, LANES), lambda i, j: (i % tpb, 0)),
                  pl.BlockSpec((tm, LANES), lambda i, j: (i % tpb, 0))],
        out_specs=pl.BlockSpec((tm, tn), lambda i, j: (i, j)),
        out_shape=jax.ShapeDtypeStruct((n, m), BF16),
        compiler_params=_params("parallel", "parallel"),
        name="mla_q",
    )(cq, w, ws, cos, sin)


def _oproj_ln_kernel(o_ref, w_ref, h_ref, g_ref, b_ref, hf_ref, hb_ref):
    x = DEEPNORM_ALPHA * h_ref[...] + _dot(o_ref[...], w_ref[...])
    y = _layer_norm(x, g_ref[...], b_ref[...])
    hf_ref[...] = y
    hb_ref[...] = y.astype(hb_ref.dtype)


def _oproj_ln(o, w, h, g, b, tm):
    n = o.shape[0]
    row = lambda i: (i, 0)
    fixed = lambda i: (0, 0)
    return pl.pallas_call(
        _oproj_ln_kernel,
        grid=(n // tm,),
        in_specs=[pl.BlockSpec((tm, D_MODEL), row), pl.BlockSpec((D_MODEL, D_MODEL), fixed),
                  pl.BlockSpec((tm, D_MODEL), row), pl.BlockSpec((1, D_MODEL), fixed),
                  pl.BlockSpec((1, D_MODEL), fixed)],
        out_specs=[pl.BlockSpec((tm, D_MODEL), row), pl.BlockSpec((tm, D_MODEL), row)],
        out_shape=[jax.ShapeDtypeStruct((n, D_MODEL), F32), jax.ShapeDtypeStruct((n, D_MODEL), BF16)],
        compiler_params=_params("parallel"),
        name="oproj_ln",
    )(o, w, h, g, b)


FFN_HALO = BF16_SUBLANES


FFN_CHUNK = 256
FFN_NC = D_FF // FFN_CHUNK
assert FFN_NC * FFN_CHUNK == D_FF and FFN_NC % 2 == 1


def _ffn_kernel(x_ref, halo_ref, h_ref, win_ref, cw_ref, cb_ref, wo_ref, g_ref, b_ref, hf_ref, hb_ref,
                xext_ref, ua_ref, ub_ref, acc_ref, *, tm, tiles_per_batch):
    i = pl.program_id(0)
    nc = FFN_NC
    pos = (i % tiles_per_batch) * tm - FFN_HALO + lax.broadcasted_iota(jnp.int32, (tm + FFN_HALO, 1), 0)
    xe = jnp.concatenate([halo_ref[...], x_ref[...]], axis=0).astype(F32)
    xext_ref[...] = jnp.where(pos >= LEAD, xe, 0.0).astype(BF16)

    cols = lambda idx: slice(idx * FFN_CHUNK, (idx + 1) * FFN_CHUNK)

    def up(u_ref, c):
        u_ref[0] = _dot(xext_ref[...], win_ref[:, cols(c)])
        u_ref[1] = _dot(xext_ref[...], win_ref[:, cols(nc + c)])

    def glu(u_ref, c):
        def conv(part, idx):
            u = u_ref[part]
            delayed = u * cw_ref[0:1, cols(idx)]
            for tap in range(1, CONV_W):
                delayed = u * cw_ref[tap:tap + 1, cols(idx)] + pltpu.roll(delayed, 1, 0)
            return cb_ref[:, cols(idx)] + delayed[FFN_HALO:, :]

        yg = conv(0, c)
        yv = conv(1, nc + c)
        return ((yg / (1.0 + jnp.exp(-yg))) * yv).astype(BF16)

    acc_ref[...] = DEEPNORM_ALPHA * h_ref[...]
    up(ua_ref, 0)

    for c in range(0, nc - 1, 2):
        up(ub_ref, c + 1)
        act_a = glu(ua_ref, c)
        up(ua_ref, c + 2)
        act_b = glu(ub_ref, c + 1)
        acc_ref[...] += _dot(act_a, wo_ref[cols(c), :]) + _dot(act_b, wo_ref[cols(c + 1), :])
    y = _layer_norm(acc_ref[...] + _dot(glu(ua_ref, nc - 1), wo_ref[cols(nc - 1), :]), g_ref[...], b_ref[...])
    hf_ref[...] = y
    hb_ref[...] = y.astype(hb_ref.dtype)


def _ffn(hb, hf, w_in, conv_w, conv_b, w_out, g, b, lp, tm):
    n = hb.shape[0]
    nc, fc = FFN_NC, FFN_CHUNK
    tpb = lp // tm
    halo_blocks = tm // FFN_HALO
    row = lambda i: (i, 0)
    fixed2 = lambda i: (0, 0)
    resident = dict(pipeline_mode=pl.Buffered(1))
    return pl.pallas_call(
        functools.partial(_ffn_kernel, tm=tm, tiles_per_batch=tpb),
        grid=(n // tm,),
        in_specs=[pl.BlockSpec((tm, D_MODEL), row),
                  pl.BlockSpec((FFN_HALO, D_MODEL), lambda i: (jnp.maximum(i * halo_blocks - 1, 0), 0)),
                  pl.BlockSpec((tm, D_MODEL), row),
                  pl.BlockSpec((D_MODEL, 2 * D_FF), fixed2, **resident),
                  pl.BlockSpec((CONV_W, 2 * D_FF), fixed2, **resident),
                  pl.BlockSpec((1, 2 * D_FF), fixed2, **resident),
                  pl.BlockSpec((D_FF, D_MODEL), fixed2, **resident),
                  pl.BlockSpec((1, D_MODEL), fixed2), pl.BlockSpec((1, D_MODEL), fixed2)],
        out_specs=[pl.BlockSpec((tm, D_MODEL), row), pl.BlockSpec((tm, D_MODEL), row)],
        out_shape=[jax.ShapeDtypeStruct((n, D_MODEL), F32), jax.ShapeDtypeStruct((n, D_MODEL), BF16)],
        scratch_shapes=[pltpu.VMEM((tm + FFN_HALO, D_MODEL), BF16),
                        pltpu.VMEM((2, tm + FFN_HALO, fc), F32), pltpu.VMEM((2, tm + FFN_HALO, fc), F32),
                        pltpu.VMEM((tm, D_MODEL), F32)],
        compiler_params=_params("parallel"),
        name="ffn",
    )(hb, hb, hf, w_in, conv_w, conv_b, w_out, g, b)


def _ffn_weights(w_in, conv_w, conv_b, w_out):
    return w_in.astype(BF16), conv_w, conv_b[None, :], w_out.astype(BF16)


def _rope_tables(lp, dim, theta, group, offset):
    pos = (jnp.arange(lp) - LEAD).astype(F32)
    inv = theta ** (-jnp.arange(0, dim, 2, dtype=F32) / dim)
    ang = pos[:, None] * inv[None, :]
    cos, sin = jnp.cos(ang), jnp.sin(ang)
    ones = lambda w: jnp.ones((lp, w), F32)
    zeros = lambda w: jnp.zeros((lp, w), F32)
    rest = group - offset - dim
    cos_g = jnp.concatenate([ones(offset), cos, cos, ones(rest)], axis=1)
    sin_g = jnp.concatenate([zeros(offset), -sin, sin, zeros(rest)], axis=1)
    reps = LANES // group
    return jnp.tile(cos_g, (1, reps)), jnp.tile(sin_g, (1, reps))


def _swap_halves(w, dim):
    return jnp.concatenate([w[..., dim // 2:dim], w[..., :dim // 2]], axis=-1)


def _head_blocks(main, extra):
    src = main if main is not None else extra
    rows, heads = src.shape[0], src.shape[1]
    m = main if main is not None else jnp.zeros((rows, heads, HEAD_DIM), F32)
    e = extra if extra is not None else jnp.zeros((rows, heads, 0), F32)
    pad = jnp.zeros((rows, heads, LANES - HEAD_DIM - e.shape[2]), F32)
    return jnp.concatenate([m, e, pad], axis=2).reshape(rows, heads * LANES)


def _mla_weights(w_a, w_uq, w_ukv):
    d = w_a.shape[0]
    w_kr = w_a[:, MLA_Q_LORA + MLA_KV_LORA:][:, None, :]
    w_a_cat = jnp.concatenate([w_a[:, :MLA_Q_LORA + MLA_KV_LORA], _head_blocks(None, w_kr),
                               _head_blocks(None, _swap_halves(w_kr, MLA_ROPE))], axis=1)
    scale = (MLA_NOPE + MLA_ROPE) ** -0.5 * LOG2E
    wq = (w_uq * scale).reshape(MLA_Q_LORA, HEADS, MLA_NOPE + MLA_ROPE)
    w_main = _head_blocks(wq[..., :MLA_NOPE], wq[..., MLA_NOPE:])
    w_swap = _head_blocks(None, _swap_halves(wq[..., MLA_NOPE:], MLA_ROPE))
    wkv = w_ukv.reshape(MLA_KV_LORA, HEADS, MLA_NOPE + HEAD_DIM)
    w_kn = wkv[..., :MLA_NOPE].reshape(MLA_KV_LORA, HEADS * MLA_NOPE)
    w_v = wkv[..., MLA_NOPE:].reshape(MLA_KV_LORA, HEADS * HEAD_DIM)
    return tuple(w.astype(BF16) for w in (w_a_cat, w_main, w_swap, w_kn)) + (w_v,)


def _swa_weights(w_in):
    qd = HEADS * HEAD_DIM
    kd = SWA_KV_HEADS * HEAD_DIM
    q = w_in[:, :qd] * (HEAD_DIM ** -0.5)
    dup = lambda w: jnp.concatenate([w[:, :HEAD_DIM], w[:, :HEAD_DIM], w[:, HEAD_DIM:], w[:, HEAD_DIM:]], axis=1)
    return jnp.concatenate([q, dup(w_in[:, qd:qd + kd]), dup(w_in[:, qd + kd:])], axis=1).astype(BF16)


def _fox_weights(w_in, b_f):
    hd = HEADS * HEAD_DIM
    d = w_in.shape[0]
    w_q = (w_in[:, :hd] * (HEAD_DIM ** -0.5 * LOG2E)).astype(BF16)
    w_k = w_in[:, hd:2 * hd].astype(BF16)
    w_v = w_in[:, 2 * hd:3 * hd]
    w_gate = w_in[:, 3 * hd:]
    w_fg = jnp.concatenate([w_gate] * GATE_SLOTS + [jnp.zeros((d, LANES - GATE_SLOTS * HEADS), F32)],
                           axis=1).astype(BF16)
    b_fg = jnp.concatenate([b_f] * GATE_SLOTS + [jnp.zeros((LANES - GATE_SLOTS * HEADS,), F32)])[None, :]
    return w_q, w_k, w_v, w_fg, b_fg


def kernel(x, meta_tokens, ln1_g, ln1_b, ln2_g, ln2_b, fox_w_in, fox_b_f, fox_w_o, swa_w_in, swa_sinks, swa_w_o,
           mla_w_a, mla_g_q, mla_g_kv, mla_w_uq, mla_w_ukv, mla_w_o, ffn_w_in, ffn_conv_w, ffn_conv_b, ffn_w_out):
    batch, seq, d = x.shape
    assert d == D_MODEL and seq % 256 == 0
    lp = seq + FIRST_REAL
    n = batch * lp

    tm = _tile(lp, 768, 256)
    tq = _tile(lp, 768, 256)
    tk = 256
    tn = HEADS * HEAD_DIM
    nk = lp // tk

    hf, hb = _embed(x, meta_tokens.astype(x.dtype))

    cos_p, sin_p = _rope_tables(lp, ROPE_DIM, ROPE_THETA, HEAD_DIM, 0)
    cos_m, sin_m = _rope_tables(lp, MLA_ROPE, MLA_ROPE_THETA, LANES, MLA_NOPE)
    b3 = lambda a: a.reshape(batch, lp, -1)

    for i in range(DEPTH):
        kind, j = i % 3, i // 3
        if kind == 0:
            w_q, w_k, w_v, w_fg, b_fg = _fox_weights(fox_w_in[j], fox_b_f[j])
            aq, ak = _fox_gate(hb, w_fg, b_fg, lp, tm)
            qh = _head_proj(hb, w_q, aq, tm, tn)
            kh = _head_proj(hb, w_k, ak, tm, tn)
            vt = _matmul_t(hb, *_value_weights(w_v), tm, V_COLS_TILE, tk).reshape(batch, nk, HEADS * V_ROWS, tk)
            o = _flash_attention(b3(qh), b3(kh), vt, batch, lp, tq, tk)
            w_o = fox_w_o[j]
        elif kind == 1:
            qkv = _swa_proj(hb, _swa_weights(swa_w_in[j]), cos_p, sin_p, lp, tm, 2 * LANES, 5)
            o = _swa_attention(b3(qkv), swa_sinks[j].astype(F32), batch, lp)
            w_o = swa_w_o[j]
        else:
            w_a_cat, w_main, w_swap, w_kn, w_v = _mla_weights(mla_w_a[j], mla_w_uq[j], mla_w_ukv[j])
            cq, ckv, kr = _mla_a(hb, w_a_cat, mla_g_q[j][None, :], mla_g_kv[j][None, :], cos_m, sin_m, lp, tm)
            qh = _mla_q(cq, w_main, w_swap, cos_m, sin_m, lp, tm, HEADS * LANES)
            kh = _head_proj(ckv, w_kn, kr, tm, tn)
            vt = _matmul_t(ckv, *_value_weights(w_v), tm, V_COLS_TILE, tk).reshape(batch, nk, HEADS * V_ROWS, tk)
            o = _flash_attention(b3(qh), b3(kh), vt, batch, lp, tq, tk)
            w_o = mla_w_o[j]
        hf, hb = _oproj_ln(o.reshape(n, d), w_o.astype(BF16), hf, ln1_g[i][None, :], ln1_b[i][None, :], tm)
        hf, hb = _ffn(hb, hf, *_ffn_weights(ffn_w_in[i], ffn_conv_w[i], ffn_conv_b[i], ffn_w_out[i]),
                      ln2_g[i][None, :], ln2_b[i][None, :], lp, tm)
    return hf.reshape(batch, lp, d)[:, FIRST_REAL:]
```
